```python
import math
import jax, jax.numpy as jnp
from jax import lax
import numpy as np

D_MODEL = 1024
BATCH = 8
SEQ = 2048
DEPTH = 1
DEC_BATCH = 128
DEC_SEQ = 4
PAST_LEN = 16384
PAGE_SIZE = 128

N_META = 16
POOL_WIDTH = D_MODEL // 2
POOL_WINDOWS = (2, 4, 8, 16)
POOL_GROUPS = len(POOL_WINDOWS)
POOL_GROUP_DIM = POOL_WIDTH // POOL_GROUPS
POOL_BUF = max(POOL_WINDOWS) - 1
GLA_HEADS = 4
GLA_WIDTH = D_MODEL // 2
GLA_DV = GLA_WIDTH // GLA_HEADS
GLA_DK = GLA_DV // 2
GLA_KW = GLA_HEADS * GLA_DK
GLA_GATE_RANK = 16
GLA_TAU = 16.0
GLA_CHUNK = 64
D_FF = -(-8 * D_MODEL // (3 * 256)) * 256
EPS = 1e-6

_IN_SIZES = (POOL_WIDTH, GLA_KW, GLA_KW, GLA_WIDTH, GLA_WIDTH, GLA_GATE_RANK, D_MODEL, D_MODEL)
IN_DIM = sum(_IN_SIZES)
IN_SPLIT_IDX = tuple(int(s) for s in np.cumsum(_IN_SIZES)[:-1])

kernel_name = 'hybrid_pool_gla_gated_decoder_step'


def _rmsnorm(x, g):
    xf = x.astype(jnp.float32)
    y = xf * lax.rsqrt(jnp.mean(xf * xf, axis=-1, keepdims=True) + EPS)
    return (y * g.astype(jnp.float32)).astype(x.dtype)


def _pool_mix(u, buf, start_pos, w_group, scale):
    T = u.shape[1]
    ext = jnp.concatenate([buf.astype(u.dtype), u], axis=1)
    c = jnp.cumsum(ext.astype(jnp.float32), axis=1)
    c = jnp.pad(c, ((0, 0), (1, 0), (0, 0)))
    pos = start_pos + jnp.arange(T)
    uf = u.astype(jnp.float32)
    outs = []
    for gi, w in enumerate(POOL_WINDOWS):
        sl = slice(gi * POOL_GROUP_DIM, (gi + 1) * POOL_GROUP_DIM)
        hi = c[:, POOL_BUF + 1:POOL_BUF + 1 + T, sl]
        lo = c[:, POOL_BUF + 1 - w:POOL_BUF + 1 - w + T, sl]
        cnt = jnp.minimum(w, pos + 1).astype(jnp.float32)[None, :, None]
        outs.append((hi - lo) / cnt - uf[:, :, sl])
    pooled = jnp.stack(outs, axis=2).astype(u.dtype)
    mixed = jnp.einsum('btgc,gcd->btgd', pooled, w_group)
    out = mixed.reshape(u.shape) * scale
    return out, ext[:, -POOL_BUF:]


def _gla_segment(q, k, v, g, S, chunk):
    B, T, H, _ = q.shape
    n = T // chunk

    def to_chunks(a):
        return a.reshape(B, n, chunk, H, a.shape[-1]).transpose(1, 0, 3, 2, 4)

    mask = jnp.tril(jnp.ones((chunk, chunk), dtype=bool))

    def step(S, inp):
        qc, kc, vc, gc = inp
        b = jnp.cumsum(gc, axis=2)
        qe = qc * jnp.exp(b)
        ke = kc * jnp.exp(-b)
        att = jnp.where(mask, jnp.einsum('bhik,bhjk->bhij', qe, ke), 0.0)
        o = jnp.einsum('bhik,bhkv->bhiv', qe, S) + jnp.einsum('bhij,bhjv->bhiv', att, vc)
        b_last = b[:, :, -1:, :]
        S = jnp.exp(b_last[:, :, 0, :])[..., None] * S + jnp.einsum(
            'bhjk,bhjv->bhkv', kc * jnp.exp(b_last - b), vc)
        return S, o

    S, o = lax.scan(step, S, (to_chunks(q), to_chunks(k), to_chunks(v), to_chunks(g)))
    o = o.transpose(1, 0, 3, 2, 4).reshape(B, T, H, v.shape[-1])
    return o, S


def _gla(q, k, v, g, S, seg_lens):
    outs = []
    start = 0
    for L in seg_lens:
        chunk = math.gcd(L, GLA_CHUNK)
        sl = slice(start, start + L)
        o, S = _gla_segment(q[:, sl], k[:, sl], v[:, sl], g[:, sl], S, chunk)
        outs.append(o)
        start += L
    return jnp.concatenate(outs, axis=1), S


def _layer(x, pool_buf, gla_S, start_pos, seg_lens, g_mix, w_in, w_gk_up, b_gk,
           w_pool_group, pool_scale, w_pool_proj, g_gla_norm, w_gla_proj, w_out,
           g_ffn, w_ffn_in, w_ffn_out):
    B, T, _ = x.shape
    h = _rmsnorm(x, g_mix)
    u, q, k, v, og, zr, ga, gb = jnp.split(h @ w_in, IN_SPLIT_IDX, axis=-1)
    pool_out, new_buf = _pool_mix(u, pool_buf, start_pos, w_pool_group, pool_scale)
    y_a = pool_out @ w_pool_proj
    f32 = jnp.float32
    qh = q.reshape(B, T, GLA_HEADS, GLA_DK).astype(f32) * (GLA_DK ** -0.5)
    kh = k.reshape(B, T, GLA_HEADS, GLA_DK).astype(f32)
    vh = v.reshape(B, T, GLA_HEADS, GLA_DV).astype(f32)
    loga = jax.nn.log_sigmoid((zr @ w_gk_up + b_gk).astype(f32)) / GLA_TAU
    loga = loga.reshape(B, T, GLA_HEADS, GLA_DK)
    o, new_S = _gla(qh, kh, vh, loga, gla_S.astype(f32), seg_lens)
    o = o * lax.rsqrt(jnp.mean(o * o, axis=-1, keepdims=True) + EPS) * g_gla_norm.astype(f32)
    o = o * jax.nn.silu(og.reshape(B, T, GLA_HEADS, GLA_DV).astype(f32))
    y_b = o.reshape(B, T, GLA_WIDTH).astype(x.dtype) @ w_gla_proj
    merged = jax.nn.sigmoid(ga) * y_a + jax.nn.sigmoid(gb) * y_b
    x = x + merged @ w_out
    h2 = _rmsnorm(x, g_ffn)
    gate, up = jnp.split(h2 @ w_ffn_in, 2, axis=-1)
    x = x + (jax.nn.silu(gate) * up) @ w_ffn_out
    return x, new_buf, new_S.astype(x.dtype)


def setup_inputs(seed: int = 0) -> dict:
    key = jax.random.key(seed)
    ks = jax.random.split(key, 20)
    f32 = jnp.float32
    nrm = lambda k, s, sc: jax.random.normal(k, s, f32) * sc
    return {
        'x_prompt': nrm(ks[0], (BATCH, SEQ, D_MODEL), 1.0),
        'x_sample': nrm(ks[1], (DEC_BATCH, DEC_SEQ, D_MODEL), 1.0),
        'state_pool': nrm(ks[2], (DEPTH, DEC_BATCH, POOL_BUF, POOL_WIDTH), 1.0),
        'state_gla': nrm(ks[3], (DEPTH, DEC_BATCH, GLA_HEADS, GLA_DK, GLA_DV), 1.0),
        'meta_tokens': nrm(ks[4], (N_META, D_MODEL), 1.0),
        'g_mix': 1.0 + nrm(ks[5], (DEPTH, D_MODEL), 0.05),
        'w_in': nrm(ks[6], (DEPTH, D_MODEL, IN_DIM), D_MODEL ** -0.5),
        'w_gk_up': nrm(ks[7], (DEPTH, GLA_GATE_RANK, GLA_KW), GLA_GATE_RANK ** -0.5),
        'b_gk': nrm(ks[8], (DEPTH, GLA_KW), 0.01),
        'w_pool_group': nrm(ks[9], (DEPTH, POOL_GROUPS, POOL_GROUP_DIM, POOL_GROUP_DIM), POOL_GROUP_DIM ** -0.5),
        'pool_scale': 1.0 + nrm(ks[10], (DEPTH, POOL_WIDTH), 0.1),
        'w_pool_proj': nrm(ks[11], (DEPTH, POOL_WIDTH, D_MODEL), POOL_WIDTH ** -0.5),
        'g_gla_norm': 1.0 + nrm(ks[12], (DEPTH, GLA_DV), 0.05),
        'w_gla_proj': nrm(ks[13], (DEPTH, GLA_WIDTH, D_MODEL), GLA_WIDTH ** -0.5),
        'w_out': nrm(ks[14], (DEPTH, D_MODEL, D_MODEL), D_MODEL ** -0.5),
        'g_ffn': 1.0 + nrm(ks[15], (DEPTH, D_MODEL), 0.05),
        'w_ffn_in': nrm(ks[16], (DEPTH, D_MODEL, 2 * D_FF), D_MODEL ** -0.5),
        'w_ffn_out': nrm(ks[17], (DEPTH, D_FF, D_MODEL), D_FF ** -0.5),
        'g_final': 1.0 + nrm(ks[18], (D_MODEL,), 0.05),
    }


def reference(x_prompt, x_sample, state_pool, state_gla, meta_tokens, g_mix, w_in, w_gk_up,
              b_gk, w_pool_group, pool_scale, w_pool_proj, g_gla_norm, w_gla_proj, w_out,
              g_ffn, w_ffn_in, w_ffn_out, g_final):
    Bp, Tp, D = x_prompt.shape
    xp = jnp.concatenate(
        [jnp.broadcast_to(meta_tokens.astype(x_prompt.dtype)[None], (Bp, N_META, D)), x_prompt], axis=1)
    xs = x_sample
    zero_buf = jnp.zeros((Bp, POOL_BUF, POOL_WIDTH), x_prompt.dtype)
    zero_S = jnp.zeros((Bp, GLA_HEADS, GLA_DK, GLA_DV), jnp.float32)
    pool_p, gla_p, pool_s, gla_s = [], [], [], []
    for l in range(DEPTH):
        params = (g_mix[l], w_in[l], w_gk_up[l], b_gk[l], w_pool_group[l], pool_scale[l],
                  w_pool_proj[l], g_gla_norm[l], w_gla_proj[l], w_out[l], g_ffn[l],
                  w_ffn_in[l], w_ffn_out[l])
        xp, bp, sp = _layer(xp, zero_buf, zero_S, 0, (N_META, Tp), *params)
        xs, bs, ss = _layer(xs, state_pool[l], state_gla[l], PAST_LEN, (xs.shape[1],), *params)
        pool_p.append(bp)
        gla_p.append(sp)
        pool_s.append(bs)
        gla_s.append(ss)
    y_prompt = _rmsnorm(xp[:, N_META:], g_final)
    y_sample = _rmsnorm(xs, g_final)
    return (y_prompt, y_sample, jnp.stack(pool_p), jnp.stack(gla_p), jnp.stack(pool_s), jnp.stack(gla_s))
```

```python
import functools

import jax
import jax.numpy as jnp
from jax import lax
from jax.experimental import pallas as pl
from jax.experimental.pallas import tpu as pltpu

F32 = jnp.float32
BF16 = jnp.bfloat16

D_MODEL = 1024
N_META = 16
POOL_WIDTH = 512
POOL_WINDOWS = (2, 4, 8, 16)
POOL_GROUP_DIM = 128
POOL_BUF = 15
GLA_HEADS = 4
GLA_DV = 128
GLA_DK = 64
GLA_KW = GLA_HEADS * GLA_DK
GLA_VW = GLA_HEADS * GLA_DV
GLA_GATE_RANK = 16
GLA_TAU = 16.0
GLA_CHUNK = 64
D_FF = 2816
EPS = 1e-6

LANES = 128
SUBLANES = 8
MAIN_W = POOL_WIDTH + 2 * GLA_KW + 2 * GLA_VW
TAIL_ROWS = 16

PROMPT_TILE = 512
FFN_TILE = 512
FFN_CHUNK = 256
SAMPLE_BATCH_BLOCK = 16
SAMPLE_ROWS = 8
VMEM_LIMIT = 60 * 1024 * 1024


def _dot(a, b):
    return jnp.dot(a, b, preferred_element_type=F32)


def _dot_nt(a, b):
    return lax.dot_general(a, b, (((1,), (1,)), ((), ())), preferred_element_type=F32)


def _dot_tn(a, b):
    return lax.dot_general(a, b, (((0,), (0,)), ((), ())), preferred_element_type=F32)


def _rms(x, g):
    return x * lax.rsqrt(jnp.mean(x * x, axis=-1, keepdims=True) + EPS) * g


def _sigmoid(x):
    return 1.0 / (1.0 + jnp.exp(-x))


def _log_sigmoid(x):
    return jnp.minimum(x, 0.0) - jnp.log(1.0 + jnp.exp(-jnp.abs(x)))


def _split_bf16(x):
    hi = x.astype(BF16)
    lo = (x - hi.astype(F32)).astype(BF16)
    return hi, lo


def _in_proj(x, gmix_ref, wmain_ref, wzr_ref, wgab_ref, wgk_ref, bgk_ref):
    h = _rms(x, gmix_ref[...]).astype(BF16)
    pm = _dot(h, wmain_ref[...])
    u = pm[:, 0:POOL_WIDTH]
    q = pm[:, POOL_WIDTH:POOL_WIDTH + GLA_KW] * (GLA_DK ** -0.5)
    k = pm[:, POOL_WIDTH + GLA_KW:POOL_WIDTH + 2 * GLA_KW]
    v = pm[:, POOL_WIDTH + 2 * GLA_KW:POOL_WIDTH + 2 * GLA_KW + GLA_VW]
    og = pm[:, POOL_WIDTH + 2 * GLA_KW + GLA_VW:MAIN_W]
    zr = _dot(h, wzr_ref[...])
    z = _dot(zr.astype(BF16), wgk_ref[...]) + bgk_ref[...]
    gab = _dot(h, wgab_ref[...])
    return u, q, k, v, og, z, gab[:, :D_MODEL], gab[:, D_MODEL:]


def _chunk_cumsum(g, chunk):
    m = g.shape[0]
    r = lax.broadcasted_iota(jnp.int32, (m, m), 0)
    c = lax.broadcasted_iota(jnp.int32, (m, m), 1)
    tri = jnp.where((r // chunk == c // chunk) & (c <= r), 1.0, 0.0).astype(BF16)
    hi, lo = _split_bf16(g)
    return _dot(tri, hi) + _dot(tri, lo)


def _head_lane_mask(width, per_head):
    lane = lax.broadcasted_iota(jnp.int32, (1, width), 1)
    return [(lane // per_head) == h for h in range(GLA_HEADS)]


def _block_diag_rows(x_bf, per_head):
    r = x_bf.shape[0]
    zero = jnp.zeros((r, per_head), x_bf.dtype)
    rows = []
    for h in range(GLA_HEADS):
        rows.append(jnp.concatenate(
            [x_bf[:, h * per_head:(h + 1) * per_head] if hh == h else zero for hh in range(GLA_HEADS)], axis=1))
    return jnp.concatenate(rows, axis=0)


def _gla_post(o, og, gnorm_ref, wgp_ref):
    parts = []
    for h in range(GLA_HEADS):
        oh = o[:, h * GLA_DV:(h + 1) * GLA_DV]
        parts.append(oh * lax.rsqrt(jnp.mean(oh * oh, axis=-1, keepdims=True) + EPS) * gnorm_ref[...])
    on = jnp.concatenate(parts, axis=1)
    on = on * (og * _sigmoid(og))
    return _dot(on.astype(BF16), wgp_ref[...])


def _pool_post(pooled, wpg_ref, pscale_ref, wpp_ref):
    pb = pooled.astype(BF16)
    mixed = jnp.concatenate(
        [_dot(pb[:, g * POOL_GROUP_DIM:(g + 1) * POOL_GROUP_DIM], wpg_ref[g]) for g in range(len(POOL_WINDOWS))],
        axis=1)
    return _dot((mixed * pscale_ref[...]).astype(BF16), wpp_ref[...])


def _merge(x, y_a, y_b, ga, gb, wout_ref):
    merged = _sigmoid(ga) * y_a + _sigmoid(gb) * y_b
    return x + _dot(merged.astype(BF16), wout_ref[...])


def _decay_columns(decay_row):
    return jnp.transpose(jnp.broadcast_to(decay_row, (LANES, decay_row.shape[1])))


def _state_update(s_heads, kd_bf, v_bf, decay_row):
    upd = _dot_tn(kd_bf, v_bf)
    dcol = _decay_columns(decay_row)
    out = []
    for h in range(GLA_HEADS):
        rows = slice(h * GLA_DK, (h + 1) * GLA_DK)
        out.append(dcol[rows] * s_heads[h] + upd[rows, h * GLA_DV:(h + 1) * GLA_DV])
    return out


def _prompt_mixer_kernel(x_ref, meta_ref, gmix_ref, wmain_ref, wzr_ref, wgab_ref, wgk_ref, bgk_ref,
                         wpg_ref, pscale_ref, wpp_ref, gnorm_ref, wgp_ref, wout_ref,
                         x2_ref, pbuf_ref, sout_ref,
                         ext_ref, s_ref, meta_tail_ref, meta_s_ref):
    b_idx = pl.program_id(0)
    t_idx = pl.program_id(1)
    tile = x_ref.shape[1]
    n_chunks = tile // GLA_CHUNK
    proj_refs = (gmix_ref, wmain_ref, wzr_ref, wgab_ref, wgk_ref, bgk_ref)

    @pl.when((b_idx == 0) & (t_idx == 0))
    def _():
        u, _, k, v, _, z, _, _ = _in_proj(meta_ref[...], *proj_refs)
        meta_tail_ref[...] = u
        g = _log_sigmoid(z) * (1.0 / GLA_TAU)
        b = _chunk_cumsum(g, N_META)
        b_last = b[N_META - 1:N_META, :]
        kd = k * jnp.exp(b_last - b)
        zero_s = [jnp.zeros((GLA_DK, GLA_DV), F32)] * GLA_HEADS
        s_new = _state_update(zero_s, kd.astype(BF16), v.astype(BF16), jnp.exp(b_last))
        for h in range(GLA_HEADS):
            meta_s_ref[h] = s_new[h]

    @pl.when(t_idx == 0)
    def _():
        ext_ref[0:TAIL_ROWS, :] = meta_tail_ref[...]
        s_ref[...] = meta_s_ref[...]

    x = x_ref[0]
    u, q, k, v, og, z, ga, gb = _in_proj(x, *proj_refs)

    ext_ref[TAIL_ROWS:TAIL_ROWS + tile, :] = u
    pooled = []
    for gi, w in enumerate(POOL_WINDOWS):
        cols = slice(gi * POOL_GROUP_DIM, (gi + 1) * POOL_GROUP_DIM)
        acc = u[:, cols]
        for n in range(1, w):
            acc = acc + ext_ref[TAIL_ROWS - n:TAIL_ROWS - n + tile, cols]
        pooled.append(acc * (1.0 / w) - u[:, cols])
    y_a = _pool_post(jnp.concatenate(pooled, axis=1), wpg_ref, pscale_ref, wpp_ref)

    @pl.when(t_idx == pl.num_programs(1) - 1)
    def _():
        pbuf_ref[0, 0] = ext_ref[tile + TAIL_ROWS - POOL_BUF:tile + TAIL_ROWS, :]

    ext_ref[0:TAIL_ROWS, :] = ext_ref[tile:tile + TAIL_ROWS, :]

    g = _log_sigmoid(z) * (1.0 / GLA_TAU)
    b = _chunk_cumsum(g, GLA_CHUNK)
    b3 = b.reshape(n_chunks, GLA_CHUNK, GLA_KW)
    b_last = jnp.broadcast_to(b3[:, GLA_CHUNK - 1:GLA_CHUNK, :], b3.shape).reshape(tile, GLA_KW)
    qe = (q * jnp.exp(b)).astype(BF16)
    ke = k * jnp.exp(-b)
    kd = (k * jnp.exp(b_last - b)).astype(BF16)
    decay = jnp.exp(b_last)
    v_bf = v.astype(BF16)

    k_masks = _head_lane_mask(GLA_KW, GLA_DK)
    row_i = lax.broadcasted_iota(jnp.int32, (GLA_CHUNK, GLA_KW), 0)
    col_j = lax.broadcasted_iota(jnp.int32, (GLA_CHUNK, GLA_KW), 1) % GLA_CHUNK
    causal = col_j <= row_i

    s_heads = [s_ref[h] for h in range(GLA_HEADS)]
    o_chunks = []
    for c in range(n_chunks):
        rows = slice(c * GLA_CHUNK, (c + 1) * GLA_CHUNK)
        ke_c = ke[rows]
        ke_bd = jnp.concatenate([jnp.where(k_masks[h], ke_c, 0.0) for h in range(GLA_HEADS)], axis=0).astype(BF16)
        att = jnp.where(causal, _dot_nt(qe[rows], ke_bd), 0.0)
        s_rows = []
        for h in range(GLA_HEADS):
            zero = jnp.zeros((GLA_DK, GLA_DV), BF16)
            s_rows.append(jnp.concatenate(
                [s_heads[h].astype(BF16) if hh == h else zero for hh in range(GLA_HEADS)], axis=1))
        rhs = jnp.concatenate(s_rows + [_block_diag_rows(v_bf[rows], GLA_DV)], axis=0)
        lhs = jnp.concatenate([qe[rows], att.astype(BF16)], axis=1)
        o_chunks.append(_dot(lhs, rhs))
        s_heads = _state_update(s_heads, kd[rows], v_bf[rows], decay[c * GLA_CHUNK:c * GLA_CHUNK + 1, :])
    for h in range(GLA_HEADS):
        s_ref[h] = s_heads[h]

    @pl.when(t_idx == pl.num_programs(1) - 1)
    def _():
        sout_ref[0, 0] = s_ref[...]

    y_b = _gla_post(jnp.concatenate(o_chunks, axis=0), og, gnorm_ref, wgp_ref)
    x2_ref[0] = _merge(x, y_a, y_b, ga, gb, wout_ref)


def _sample_mixer_kernel(x_ref, pool_ref, sin_ref, gmix_ref, wmain_ref, wzr_ref, wgab_ref, wgk_ref, bgk_ref,
                         wpg_ref, pscale_ref, wpp_ref, gnorm_ref, wgp_ref, wout_ref,
                         x2_ref, pbuf_ref, sout_ref,
                         xs_ref, ext_ref, qm_ref, kdx_ref, rhs_ref, oi_ref):
    nb, seq, _ = x_ref.shape
    rows_pb = SAMPLE_ROWS
    m = nb * rows_pb
    proj_refs = (gmix_ref, wmain_ref, wzr_ref, wgab_ref, wgk_ref, bgk_ref)

    xs_ref[...] = jnp.zeros(xs_ref.shape, F32)
    xs_ref[:, 0:seq, :] = x_ref[...]
    x = xs_ref[...].reshape(m, D_MODEL)
    u, q, k, v, og, z, ga, gb = _in_proj(x, *proj_refs)

    ext_ref[:, 0:1, :] = jnp.zeros((nb, 1, POOL_WIDTH), F32)
    ext_ref[:, 1:TAIL_ROWS, :] = pool_ref[0]
    ext_ref[:, TAIL_ROWS:TAIL_ROWS + rows_pb, :] = u.reshape(nb, rows_pb, POOL_WIDTH)
    pooled = []
    for gi, w in enumerate(POOL_WINDOWS):
        cols = slice(gi * POOL_GROUP_DIM, (gi + 1) * POOL_GROUP_DIM)
        acc = ext_ref[:, TAIL_ROWS:TAIL_ROWS + rows_pb, cols]
        cur = acc
        for n in range(1, w):
            acc = acc + ext_ref[:, TAIL_ROWS - n:TAIL_ROWS - n + rows_pb, cols]
        pooled.append(acc * (1.0 / w) - cur)
    pooled = jnp.concatenate(pooled, axis=2).reshape(m, POOL_WIDTH)
    y_a = _pool_post(pooled, wpg_ref, pscale_ref, wpp_ref)
    pbuf_ref[0] = ext_ref[:, seq + 1:seq + 1 + POOL_BUF, :]

    r8 = lax.broadcasted_iota(jnp.int32, (m, 1), 0) % rows_pb
    g = jnp.where(r8 < seq, _log_sigmoid(z) * (1.0 / GLA_TAU), 0.0)
    b = _chunk_cumsum(g, rows_pb)
    b3 = b.reshape(nb, rows_pb, GLA_KW)
    b_last = jnp.broadcast_to(b3[:, seq - 1:seq, :], b3.shape).reshape(m, GLA_KW)
    qe = q * jnp.exp(b)
    ke = k * jnp.exp(-b)
    kd = k * jnp.exp(b_last - b)
    decay = jnp.exp(b_last)
    v_bf = v.astype(BF16)

    k_masks = _head_lane_mask(GLA_KW, GLA_DK)
    ke_bd = jnp.concatenate([jnp.where(k_masks[h], ke, 0.0) for h in range(GLA_HEADS)], axis=0).astype(BF16)
    row_i = lax.broadcasted_iota(jnp.int32, (m, GLA_HEADS * m), 0)
    col_j = lax.broadcasted_iota(jnp.int32, (m, GLA_HEADS * m), 1) % m
    keep = (row_i // rows_pb == col_j // rows_pb) & (col_j <= row_i)
    att = jnp.where(keep, _dot_nt(qe.astype(BF16), ke_bd), 0.0).astype(BF16)
    o_intra = _dot(att, _block_diag_rows(v_bf, GLA_DV))

    qe3 = qe.reshape(nb, rows_pb, GLA_KW)
    qm_ref[...] = jnp.concatenate([jnp.where(k_masks[h], qe3, 0.0) for h in range(GLA_HEADS)], axis=1).astype(BF16)
    d_hi = decay.astype(BF16).astype(F32)
    d_lo = decay - d_hi
    kdx = jnp.where(r8 == seq, d_hi, jnp.where(r8 == seq + 1, d_lo, kd))
    kdx_ref[...] = kdx.reshape(nb, rows_pb, GLA_KW)
    ones_rows = jnp.where((r8 == seq) | (r8 == seq + 1), 1.0, 0.0) + jnp.zeros((m, GLA_DV), F32)
    rhs = jnp.concatenate(
        [piece for h in range(GLA_HEADS) for piece in (v[:, h * GLA_DV:(h + 1) * GLA_DV], ones_rows)], axis=1)
    rhs_ref[...] = rhs.reshape(nb, rows_pb, 2 * GLA_VW)

    def per_batch(i, carry):
        s_all = sin_ref[0, i]
        s_flat = s_all.reshape(GLA_KW, GLA_DV).astype(BF16)
        oi_ref[i] = _dot(qm_ref[i], s_flat)
        kdt = jnp.transpose(kdx_ref[i]).astype(BF16)
        rhs_i = rhs_ref[i].astype(BF16)
        for h in range(GLA_HEADS):
            r = _dot(kdt[h * GLA_DK:(h + 1) * GLA_DK], rhs_i[:, h * 2 * GLA_DV:(h + 1) * 2 * GLA_DV])
            sout_ref[0, i, h] = r[:, GLA_DV:] * s_all[h] + r[:, :GLA_DV]
        return carry

    lax.fori_loop(0, nb, per_batch, 0)

    oi = oi_ref[...]
    o_inter = jnp.concatenate([oi[:, h * rows_pb:(h + 1) * rows_pb, :] for h in range(GLA_HEADS)], axis=2)
    o = o_intra + o_inter.reshape(m, GLA_VW)
    y_b = _gla_post(o, og, gnorm_ref, wgp_ref)
    x2 = _merge(x, y_a, y_b, ga, gb, wout_ref)
    x2_ref[...] = x2.reshape(nb, rows_pb, D_MODEL)[:, 0:seq, :]


def _ffn_kernel(x_ref, gffn_ref, wi_ref, wo_ref, gfin_ref, y_ref):
    x = x_ref[0]
    h = _rms(x, gffn_ref[...]).astype(BF16)
    acc = x
    for c in range(D_FF // FFN_CHUNK):
        lo = c * FFN_CHUNK
        gate = _dot(h, wi_ref[:, lo:lo + FFN_CHUNK])
        up = _dot(h, wi_ref[:, D_FF + lo:D_FF + lo + FFN_CHUNK])
        act = (gate * _sigmoid(gate) * up).astype(BF16)
        acc = acc + _dot(act, wo_ref[lo:lo + FFN_CHUNK, :])
    y_ref[0] = _rms(acc, gfin_ref[...])


def _const_spec(shape):
    zeros = (0,) * len(shape)
    return pl.BlockSpec(shape, lambda *_: zeros, pipeline_mode=pl.Buffered(1))


def _ffn_call(x, gffn, wi, wo, gfin):
    nb, t, d = x.shape
    tile = min(FFN_TILE, t)
    weights = (gffn, wi, wo, gfin)
    return pl.pallas_call(
        _ffn_kernel,
        grid=(nb, t // tile),
        in_specs=[pl.BlockSpec((1, tile, d), lambda b, i: (b, i, 0))] + [_const_spec(w.shape) for w in weights],
        out_specs=pl.BlockSpec((1, tile, d), lambda b, i: (b, i, 0)),
        out_shape=jax.ShapeDtypeStruct(x.shape, F32),
        compiler_params=pltpu.CompilerParams(
            dimension_semantics=("arbitrary", "arbitrary"), vmem_limit_bytes=VMEM_LIMIT),
        name="ffn",
    )(x, *weights)


def kernel(x_prompt, x_sample, state_pool, state_gla, meta_tokens, g_mix, w_in, w_gk_up, b_gk, w_pool_group,
           pool_scale, w_pool_proj, g_gla_norm, w_gla_proj, w_out, g_ffn, w_ffn_in, w_ffn_out, g_final):
    depth = w_in.shape[0]
    assert depth == 1, "single-layer trunk only"
    bp, tp, d = x_prompt.shape
    bs, ts, _ = x_sample.shape
    assert d == D_MODEL and tp % PROMPT_TILE == 0 and bs % SAMPLE_BATCH_BLOCK == 0 and ts + 2 <= SAMPLE_ROWS
    assert meta_tokens.shape == (N_META, D_MODEL)

    w_in0 = w_in[0]
    zr_lo = MAIN_W
    gab_lo = MAIN_W + GLA_GATE_RANK
    w_main = w_in0[:, :MAIN_W].astype(BF16)
    w_zr = jnp.pad(w_in0[:, zr_lo:gab_lo], ((0, 0), (0, LANES - GLA_GATE_RANK))).astype(BF16)
    w_gab = w_in0[:, gab_lo:].astype(BF16)
    w_gk = jnp.pad(w_gk_up[0], ((0, LANES - GLA_GATE_RANK), (0, 0))).astype(BF16)
    mixer_weights = (
        g_mix[0].reshape(1, D_MODEL), w_main, w_zr, w_gab, w_gk, b_gk[0].reshape(1, GLA_KW),
        w_pool_group[0].astype(BF16), pool_scale[0].reshape(1, POOL_WIDTH), w_pool_proj[0].astype(BF16),
        g_gla_norm[0].reshape(1, GLA_DV), w_gla_proj[0].astype(BF16), w_out[0].astype(BF16))
    weight_specs = [_const_spec(w.shape) for w in mixer_weights]
    ffn_weights = (g_ffn[0].reshape(1, D_MODEL), w_ffn_in[0].astype(BF16), w_ffn_out[0].astype(BF16),
                   g_final.reshape(1, D_MODEL))

    n_t = tp // PROMPT_TILE
    x2_p, pool_p, gla_p = pl.pallas_call(
        _prompt_mixer_kernel,
        grid=(bp, n_t),
        in_specs=[pl.BlockSpec((1, PROMPT_TILE, d), lambda b, t: (b, t, 0)),
                  _const_spec(meta_tokens.shape)] + weight_specs,
        out_specs=[pl.BlockSpec((1, PROMPT_TILE, d), lambda b, t: (b, t, 0)),
                   pl.BlockSpec((1, 1, POOL_BUF, POOL_WIDTH), lambda b, t: (0, b, 0, 0)),
                   pl.BlockSpec((1, 1, GLA_HEADS, GLA_DK, GLA_DV), lambda b, t: (0, b, 0, 0, 0))],
        out_shape=[jax.ShapeDtypeStruct(x_prompt.shape, F32),
                   jax.ShapeDtypeStruct((1, bp, POOL_BUF, POOL_WIDTH), F32),
                   jax.ShapeDtypeStruct((1, bp, GLA_HEADS, GLA_DK, GLA_DV), F32)],
        scratch_shapes=[pltpu.VMEM((TAIL_ROWS + PROMPT_TILE, POOL_WIDTH), F32),
                        pltpu.VMEM((GLA_HEADS, GLA_DK, GLA_DV), F32),
                        pltpu.VMEM((N_META, POOL_WIDTH), F32),
                        pltpu.VMEM((GLA_HEADS, GLA_DK, GLA_DV), F32)],
        compiler_params=pltpu.CompilerParams(
            dimension_semantics=("arbitrary", "arbitrary"), vmem_limit_bytes=VMEM_LIMIT),
        name="prompt_mixer",
    )(x_prompt, meta_tokens, *mixer_weights)

    nbb = SAMPLE_BATCH_BLOCK
    x2_s, pool_s, gla_s = pl.pallas_call(
        _sample_mixer_kernel,
        grid=(bs // nbb,),
        in_specs=[pl.BlockSpec((nbb, ts, d), lambda i: (i, 0, 0)),
                  pl.BlockSpec((1, nbb, POOL_BUF, POOL_WIDTH), lambda i: (0, i, 0, 0)),
                  pl.BlockSpec((1, nbb, GLA_HEADS, GLA_DK, GLA_DV), lambda i: (0, i, 0, 0, 0))] + weight_specs,
        out_specs=[pl.BlockSpec((nbb, ts, d), lambda i: (i, 0, 0)),
                   pl.BlockSpec((1, nbb, POOL_BUF, POOL_WIDTH), lambda i: (0, i, 0, 0)),
                   pl.BlockSpec((1, nbb, GLA_HEADS, GLA_DK, GLA_DV), lambda i: (0, i, 0, 0, 0))],
        out_shape=[jax.ShapeDtypeStruct(x_sample.shape, F32),
                   jax.ShapeDtypeStruct(state_pool.shape, F32),
                   jax.ShapeDtypeStruct(state_gla.shape, F32)],
        scratch_shapes=[pltpu.VMEM((nbb, SAMPLE_ROWS, D_MODEL), F32),
                        pltpu.VMEM((nbb, TAIL_ROWS + SAMPLE_ROWS, POOL_WIDTH), F32),
                        pltpu.VMEM((nbb, GLA_HEADS * SAMPLE_ROWS, GLA_KW), BF16),
                        pltpu.VMEM((nbb, SAMPLE_ROWS, GLA_KW), F32),
                        pltpu.VMEM((nbb, SAMPLE_ROWS, 2 * GLA_VW), F32),
                        pltpu.VMEM((nbb, GLA_HEADS * SAMPLE_ROWS, GLA_DV), F32)],
        compiler_params=pltpu.CompilerParams(
            dimension_semantics=("arbitrary",), vmem_limit_bytes=VMEM_LIMIT),
        name="sample_mixer",
    )(x_sample, state_pool, state_gla, *mixer_weights)

    y_prompt = _ffn_call(x2_p, *ffn_weights)
    y_sample = _ffn_call(x2_s.reshape(1, bs * ts, d), *ffn_weights).reshape(bs, ts, d)
    return y_prompt, y_sample, pool_p, gla_p, pool_s, gla_s
```

```python
import functools

import jax
import jax.numpy as jnp
from jax import lax
from jax.experimental import pallas as pl
from jax.experimental.pallas import tpu as pltpu

F32 = jnp.float32
BF16 = jnp.bfloat16

D_MODEL = 1024
N_META = 16
POOL_WIDTH = 512
POOL_WINDOWS = (2, 4, 8, 16)
POOL_GROUP_DIM = 128
POOL_BUF = 15
GLA_HEADS = 4
GLA_DV = 128
GLA_DK = 64
GLA_KW = GLA_HEADS * GLA_DK
GLA_VW = GLA_HEADS * GLA_DV
GLA_GATE_RANK = 16
GLA_TAU = 16.0
GLA_CHUNK = 64
D_FF = 2816
EPS = 1e-6

LANES = 128
SUBLANES = 8
MAIN_W = POOL_WIDTH + 2 * GLA_KW + 2 * GLA_VW
TAIL_ROWS = 16
POOL_PAD = 8

PROMPT_TILE = 512
FFN_TILE = 512
FFN_CHUNK = 256
SAMPLE_BATCH_BLOCK = 16
SAMPLE_ROWS = 8
VMEM_LIMIT = 60 * 1024 * 1024


def _dot(a, b):
    return jnp.dot(a, b, preferred_element_type=F32)


def _dot_nt(a, b):
    return lax.dot_general(a, b, (((1,), (1,)), ((), ())), preferred_element_type=F32)


def _dot_tn(a, b):
    return lax.dot_general(a, b, (((0,), (0,)), ((), ())), preferred_element_type=F32)


def _rms(x, g):
    return x * lax.rsqrt(jnp.mean(x * x, axis=-1, keepdims=True) + EPS) * g


def _sigmoid(x):
    return 0.5 * jnp.tanh(0.5 * x) + 0.5


def _silu(x):
    half = 0.5 * x
    return half * jnp.tanh(half) + half


def _log_sigmoid(x):
    return jnp.minimum(x, 0.0) - jnp.log(1.0 + jnp.exp(-jnp.abs(x)))


def _split_bf16(x):
    hi = x.astype(BF16)
    lo = (x - hi.astype(F32)).astype(BF16)
    return hi, lo


def _in_proj(x, gmix_ref, wmain_ref, wzr_ref, wgab_ref, wgk_ref, bgk_ref):
    h = _rms(x, gmix_ref[...]).astype(BF16)
    zr = _dot(h, wzr_ref[...])
    z = _dot(zr.astype(BF16), wgk_ref[...]) + bgk_ref[...]
    c_q, c_k, c_v, c_og = POOL_WIDTH, POOL_WIDTH + GLA_KW, POOL_WIDTH + 2 * GLA_KW, POOL_WIDTH + 2 * GLA_KW + GLA_VW
    qk = _dot(h, wmain_ref[:, c_q:c_v])
    q = qk[:, :GLA_KW] * (GLA_DK ** -0.5)
    k = qk[:, GLA_KW:]
    u = _dot(h, wmain_ref[:, 0:c_q])
    gab = _dot(h, wgab_ref[...])
    vog = _dot(h, wmain_ref[:, c_v:MAIN_W])
    v = vog[:, :GLA_VW]
    og = vog[:, GLA_VW:]
    return u, q, k, v, og, z, gab[:, :D_MODEL], gab[:, D_MODEL:]


def _chunk_cumsum(g, chunk):
    m = g.shape[0]
    r = lax.broadcasted_iota(jnp.int32, (m, m), 0)
    c = lax.broadcasted_iota(jnp.int32, (m, m), 1)
    tri = jnp.where((r // chunk == c // chunk) & (c <= r), 1.0, 0.0).astype(BF16)
    hi, lo = _split_bf16(g)
    return _dot(tri, hi) + _dot(tri, lo)


def _head_lane_mask(width, per_head):
    lane = lax.broadcasted_iota(jnp.int32, (1, width), 1)
    return [(lane // per_head) == h for h in range(GLA_HEADS)]


def _block_diag_rows(x_bf, per_head):
    r = x_bf.shape[0]
    zero = jnp.zeros((r, per_head), x_bf.dtype)
    rows = []
    for h in range(GLA_HEADS):
        rows.append(jnp.concatenate(
            [x_bf[:, h * per_head:(h + 1) * per_head] if hh == h else zero for hh in range(GLA_HEADS)], axis=1))
    return jnp.concatenate(rows, axis=0)


def _gla_post(o, og, gnorm_ref, wgp_ref):
    parts = []
    for h in range(GLA_HEADS):
        oh = o[:, h * GLA_DV:(h + 1) * GLA_DV]
        parts.append(oh * lax.rsqrt(jnp.mean(oh * oh, axis=-1, keepdims=True) + EPS) * gnorm_ref[...])
    on = jnp.concatenate(parts, axis=1)
    on = on * _silu(og)
    return _dot(on.astype(BF16), wgp_ref[...])


def _pool_post(pooled, wpg_ref, pscale_ref, wpp_ref):
    pb = pooled.astype(BF16)
    mixed = jnp.concatenate(
        [_dot(pb[:, g * POOL_GROUP_DIM:(g + 1) * POOL_GROUP_DIM], wpg_ref[g]) for g in range(len(POOL_WINDOWS))],
        axis=1)
    return _dot((mixed * pscale_ref[...]).astype(BF16), wpp_ref[...])


def _merge(x, y_a, y_b, ga, gb, wout_ref):
    merged = _sigmoid(ga) * y_a + _sigmoid(gb) * y_b
    return x + _dot(merged.astype(BF16), wout_ref[...])


def _decay_columns(decay_row):
    return jnp.transpose(jnp.broadcast_to(decay_row, (LANES, decay_row.shape[1])))


def _state_update(s_heads, kd_bf, v_bf, decay_row):
    upd = _dot_tn(kd_bf, v_bf)
    dcol = _decay_columns(decay_row)
    out = []
    for h in range(GLA_HEADS):
        rows = slice(h * GLA_DK, (h + 1) * GLA_DK)
        out.append(dcol[rows] * s_heads[h] + upd[rows, h * GLA_DV:(h + 1) * GLA_DV])
    return out


def _prompt_mixer_kernel(x_ref, meta_ref, gmix_ref, wmain_ref, wzr_ref, wgab_ref, wgk_ref, bgk_ref,
                         wpg_ref, pscale_ref, wpp_ref, gnorm_ref, wgp_ref, wout_ref,
                         x2_ref, pbuf_ref, sout_ref,
                         ext_ref, lvl_ref, s_ref, meta_tail_ref, meta_s_ref):
    b_idx = pl.program_id(0)
    t_idx = pl.program_id(1)
    tile = x_ref.shape[1]
    n_chunks = tile // GLA_CHUNK
    proj_refs = (gmix_ref, wmain_ref, wzr_ref, wgab_ref, wgk_ref, bgk_ref)

    @pl.when((b_idx == 0) & (t_idx == 0))
    def _():
        u, _, k, v, _, z, _, _ = _in_proj(meta_ref[...], *proj_refs)
        meta_tail_ref[...] = u
        g = _log_sigmoid(z) * (1.0 / GLA_TAU)
        b = _chunk_cumsum(g, N_META)
        b_last = b[N_META - 1:N_META, :]
        kd = k * jnp.exp(b_last - b)
        zero_s = [jnp.zeros((GLA_DK, GLA_DV), F32)] * GLA_HEADS
        s_new = _state_update(zero_s, kd.astype(BF16), v.astype(BF16), jnp.exp(b_last))
        for h in range(GLA_HEADS):
            meta_s_ref[h] = s_new[h]

    @pl.when(t_idx == 0)
    def _():
        ext_ref[0:POOL_PAD, :] = jnp.zeros((POOL_PAD, POOL_WIDTH), F32)
        lvl_ref[:, 0:POOL_PAD, :] = jnp.zeros((lvl_ref.shape[0], POOL_PAD, POOL_WIDTH), F32)
        ext_ref[POOL_PAD:POOL_PAD + TAIL_ROWS, :] = meta_tail_ref[...]
        s_ref[...] = meta_s_ref[...]

    x = x_ref[0]
    u, q, k, v, og, z, ga, gb = _in_proj(x, *proj_refs)

    base = POOL_PAD + TAIL_ROWS
    span = TAIL_ROWS + tile
    ext_ref[base:base + tile, :] = u
    cur = ext_ref[POOL_PAD:POOL_PAD + span, :]
    pooled = []
    for gi, w in enumerate(POOL_WINDOWS):
        shift = w // 2
        lo = gi * POOL_GROUP_DIM
        prev_ref = ext_ref if gi == 0 else lvl_ref.at[gi - 1]
        cur = cur[:, (POOL_GROUP_DIM if gi else 0):] + prev_ref[POOL_PAD - shift:POOL_PAD - shift + span, lo:]
        pooled.append(cur[TAIL_ROWS:, 0:POOL_GROUP_DIM] * (1.0 / w) - u[:, lo:lo + POOL_GROUP_DIM])
        if gi + 1 < len(POOL_WINDOWS):
            lvl_ref[gi, POOL_PAD:POOL_PAD + span, lo:] = cur
    y_a = _pool_post(jnp.concatenate(pooled, axis=1), wpg_ref, pscale_ref, wpp_ref)

    @pl.when(t_idx == pl.num_programs(1) - 1)
    def _():
        pbuf_ref[0, 0] = ext_ref[base + tile - POOL_BUF:base + tile, :]

    ext_ref[POOL_PAD:base, :] = ext_ref[POOL_PAD + tile:base + tile, :]

    g = _log_sigmoid(z) * (1.0 / GLA_TAU)
    b = _chunk_cumsum(g, GLA_CHUNK)
    b3 = b.reshape(n_chunks, GLA_CHUNK, GLA_KW)
    b_last = jnp.broadcast_to(b3[:, GLA_CHUNK - 1:GLA_CHUNK, :], b3.shape).reshape(tile, GLA_KW)
    qe = (q * jnp.exp(b)).astype(BF16)
    ke = k * jnp.exp(-b)
    kd = (k * jnp.exp(b_last - b)).astype(BF16)
    decay = jnp.exp(b_last)
    v_bf = v.astype(BF16)

    k_masks = _head_lane_mask(GLA_KW, GLA_DK)
    row_i = lax.broadcasted_iota(jnp.int32, (GLA_CHUNK, GLA_KW), 0)
    col_j = lax.broadcasted_iota(jnp.int32, (GLA_CHUNK, GLA_KW), 1) % GLA_CHUNK
    causal = col_j <= row_i

    s_heads = [s_ref[h] for h in range(GLA_HEADS)]
    o_chunks = []
    for c in range(n_chunks):
        rows = slice(c * GLA_CHUNK, (c + 1) * GLA_CHUNK)
        ke_c = ke[rows]
        ke_bd = jnp.concatenate([jnp.where(k_masks[h], ke_c, 0.0) for h in range(GLA_HEADS)], axis=0).astype(BF16)
        att = jnp.where(causal, _dot_nt(qe[rows], ke_bd), 0.0)
        s_rows = []
        for h in range(GLA_HEADS):
            zero = jnp.zeros((GLA_DK, GLA_DV), BF16)
            s_rows.append(jnp.concatenate(
                [s_heads[h].astype(BF16) if hh == h else zero for hh in range(GLA_HEADS)], axis=1))
        rhs = jnp.concatenate(s_rows + [_block_diag_rows(v_bf[rows], GLA_DV)], axis=0)
        lhs = jnp.concatenate([qe[rows], att.astype(BF16)], axis=1)
        o_chunks.append(_dot(lhs, rhs))
        s_heads = _state_update(s_heads, kd[rows], v_bf[rows], decay[c * GLA_CHUNK:c * GLA_CHUNK + 1, :])
    for h in range(GLA_HEADS):
        s_ref[h] = s_heads[h]

    @pl.when(t_idx == pl.num_programs(1) - 1)
    def _():
        sout_ref[0, 0] = s_ref[...]

    y_b = _gla_post(jnp.concatenate(o_chunks, axis=0), og, gnorm_ref, wgp_ref)
    x2_ref[0] = _merge(x, y_a, y_b, ga, gb, wout_ref)


def _sample_mixer_kernel(x_ref, pool_ref, sin_ref, gmix_ref, wmain_ref, wzr_ref, wgab_ref, wgk_ref, bgk_ref,
                         wpg_ref, pscale_ref, wpp_ref, gnorm_ref, wgp_ref, wout_ref,
                         x2_ref, pbuf_ref, sout_ref,
                         xs_ref, ext_ref, qm_ref, kdx_ref, rhs_ref, oi_ref):
    nb, seq, _ = x_ref.shape
    rows_pb = SAMPLE_ROWS
    m = nb * rows_pb
    proj_refs = (gmix_ref, wmain_ref, wzr_ref, wgab_ref, wgk_ref, bgk_ref)

    xs_ref[...] = jnp.zeros(xs_ref.shape, F32)
    xs_ref[:, 0:seq, :] = x_ref[...]
    x = xs_ref[...].reshape(m, D_MODEL)
    u, q, k, v, og, z, ga, gb = _in_proj(x, *proj_refs)

    ext_ref[:, 0:1, :] = jnp.zeros((nb, 1, POOL_WIDTH), F32)
    ext_ref[:, 1:TAIL_ROWS, :] = pool_ref[0]
    ext_ref[:, TAIL_ROWS:TAIL_ROWS + rows_pb, :] = u.reshape(nb, rows_pb, POOL_WIDTH)
    pooled = []
    for gi, w in enumerate(POOL_WINDOWS):
        cols = slice(gi * POOL_GROUP_DIM, (gi + 1) * POOL_GROUP_DIM)
        acc = ext_ref[:, TAIL_ROWS:TAIL_ROWS + rows_pb, cols]
        cur = acc
        for n in range(1, w):
            acc = acc + ext_ref[:, TAIL_ROWS - n:TAIL_ROWS - n + rows_pb, cols]
        pooled.append(acc * (1.0 / w) - cur)
    pooled = jnp.concatenate(pooled, axis=2).reshape(m, POOL_WIDTH)
    y_a = _pool_post(pooled, wpg_ref, pscale_ref, wpp_ref)
    pbuf_ref[0] = ext_ref[:, seq + 1:seq + 1 + POOL_BUF, :]

    r8 = lax.broadcasted_iota(jnp.int32, (m, 1), 0) % rows_pb
    g = jnp.where(r8 < seq, _log_sigmoid(z) * (1.0 / GLA_TAU), 0.0)
    b = _chunk_cumsum(g, rows_pb)
    b3 = b.reshape(nb, rows_pb, GLA_KW)
    b_last = jnp.broadcast_to(b3[:, seq - 1:seq, :], b3.shape).reshape(m, GLA_KW)
    qe = q * jnp.exp(b)
    ke = k * jnp.exp(-b)
    kd = k * jnp.exp(b_last - b)
    decay = jnp.exp(b_last)
    v_bf = v.astype(BF16)

    k_masks = _head_lane_mask(GLA_KW, GLA_DK)
    ke_bd = jnp.concatenate([jnp.where(k_masks[h], ke, 0.0) for h in range(GLA_HEADS)], axis=0).astype(BF16)
    row_i = lax.broadcasted_iota(jnp.int32, (m, GLA_HEADS * m), 0)
    col_j = lax.broadcasted_iota(jnp.int32, (m, GLA_HEADS * m), 1) % m
    keep = (row_i // rows_pb == col_j // rows_pb) & (col_j <= row_i)
    att = jnp.where(keep, _dot_nt(qe.astype(BF16), ke_bd), 0.0).astype(BF16)
    o_intra = _dot(att, _block_diag_rows(v_bf, GLA_DV))

    qe3 = qe.reshape(nb, rows_pb, GLA_KW)
    qm_ref[...] = jnp.concatenate([jnp.where(k_masks[h], qe3, 0.0) for h in range(GLA_HEADS)], axis=1).astype(BF16)
    d_hi = decay.astype(BF16).astype(F32)
    d_lo = decay - d_hi
    kdx = jnp.where(r8 == seq, d_hi, jnp.where(r8 == seq + 1, d_lo, kd))
    kdx_ref[...] = kdx.reshape(nb, rows_pb, GLA_KW)
    ones_rows = jnp.where((r8 == seq) | (r8 == seq + 1), 1.0, 0.0) + jnp.zeros((m, GLA_DV), F32)
    rhs = jnp.concatenate(
        [piece for h in range(GLA_HEADS) for piece in (v[:, h * GLA_DV:(h + 1) * GLA_DV], ones_rows)], axis=1)
    rhs_ref[...] = rhs.reshape(nb, rows_pb, 2 * GLA_VW)

    def per_batch(i, carry):
        s_all = sin_ref[0, i]
        s_flat = s_all.reshape(GLA_KW, GLA_DV).astype(BF16)
        oi_ref[i] = _dot(qm_ref[i], s_flat)
        kdt = jnp.transpose(kdx_ref[i]).astype(BF16)
        rhs_i = rhs_ref[i].astype(BF16)
        for h in range(GLA_HEADS):
            r = _dot(kdt[h * GLA_DK:(h + 1) * GLA_DK], rhs_i[:, h * 2 * GLA_DV:(h + 1) * 2 * GLA_DV])
            sout_ref[0, i, h] = r[:, GLA_DV:] * s_all[h] + r[:, :GLA_DV]
        return carry

    lax.fori_loop(0, nb, per_batch, 0, unroll=4)

    oi = oi_ref[...]
    o_inter = jnp.concatenate([oi[:, h * rows_pb:(h + 1) * rows_pb, :] for h in range(GLA_HEADS)], axis=2)
    o = o_intra + o_inter.reshape(m, GLA_VW)
    y_b = _gla_post(o, og, gnorm_ref, wgp_ref)
    x2 = _merge(x, y_a, y_b, ga, gb, wout_ref)
    x2_ref[...] = x2.reshape(nb, rows_pb, D_MODEL)[:, 0:seq, :]


def _ffn_kernel(x_ref, gffn_ref, wi_ref, wo_ref, gfin_ref, y_ref):
    x = x_ref[0]
    h = _rms(x, gffn_ref[...]).astype(BF16)
    acc = x
    for c in range(D_FF // FFN_CHUNK):
        lo = c * FFN_CHUNK
        gate = _dot(h, wi_ref[:, lo:lo + FFN_CHUNK])
        up = _dot(h, wi_ref[:, D_FF + lo:D_FF + lo + FFN_CHUNK])
        act = (_silu(gate) * up).astype(BF16)
        acc = acc + _dot(act, wo_ref[lo:lo + FFN_CHUNK, :])
    y_ref[0] = _rms(acc, gfin_ref[...])


def _const_spec(shape):
    zeros = (0,) * len(shape)
    return pl.BlockSpec(shape, lambda *_: zeros, pipeline_mode=pl.Buffered(1))


def _ffn_call(x, gffn, wi, wo, gfin):
    nb, t, d = x.shape
    tile = min(FFN_TILE, t)
    weights = (gffn, wi, wo, gfin)
    return pl.pallas_call(
        _ffn_kernel,
        grid=(nb, t // tile),
        in_specs=[pl.BlockSpec((1, tile, d), lambda b, i: (b, i, 0))] + [_const_spec(w.shape) for w in weights],
        out_specs=pl.BlockSpec((1, tile, d), lambda b, i: (b, i, 0)),
        out_shape=jax.ShapeDtypeStruct(x.shape, F32),
        compiler_params=pltpu.CompilerParams(
            dimension_semantics=("arbitrary", "arbitrary"), vmem_limit_bytes=VMEM_LIMIT),
        name="ffn",
    )(x, *weights)


def kernel(x_prompt, x_sample, state_pool, state_gla, meta_tokens, g_mix, w_in, w_gk_up, b_gk, w_pool_group,
           pool_scale, w_pool_proj, g_gla_norm, w_gla_proj, w_out, g_ffn, w_ffn_in, w_ffn_out, g_final):
    depth = w_in.shape[0]
    assert depth == 1, "single-layer trunk only"
    bp, tp, d = x_prompt.shape
    bs, ts, _ = x_sample.shape
    assert d == D_MODEL and tp % PROMPT_TILE == 0 and bs % SAMPLE_BATCH_BLOCK == 0 and ts + 2 <= SAMPLE_ROWS
    assert meta_tokens.shape == (N_META, D_MODEL)

    w_in0 = w_in[0]
    zr_lo = MAIN_W
    gab_lo = MAIN_W + GLA_GATE_RANK
    w_main = w_in0[:, :MAIN_W].astype(BF16)
    w_zr = jnp.pad(w_in0[:, zr_lo:gab_lo], ((0, 0), (0, LANES - GLA_GATE_RANK))).astype(BF16)
    w_gab = w_in0[:, gab_lo:].astype(BF16)
    w_gk = jnp.pad(w_gk_up[0], ((0, LANES - GLA_GATE_RANK), (0, 0))).astype(BF16)
    mixer_weights = (
        g_mix[0].reshape(1, D_MODEL), w_main, w_zr, w_gab, w_gk, b_gk[0].reshape(1, GLA_KW),
        w_pool_group[0].astype(BF16), pool_scale[0].reshape(1, POOL_WIDTH), w_pool_proj[0].astype(BF16),
        g_gla_norm[0].reshape(1, GLA_DV), w_gla_proj[0].astype(BF16), w_out[0].astype(BF16))
    weight_specs = [_const_spec(w.shape) for w in mixer_weights]
    ffn_weights = (g_ffn[0].reshape(1, D_MODEL), w_ffn_in[0].astype(BF16), w_ffn_out[0].astype(BF16),
                   g_final.reshape(1, D_MODEL))

    n_t = tp // PROMPT_TILE
    x2_p, pool_p, gla_p = pl.pallas_call(
        _prompt_mixer_kernel,
        grid=(bp, n_t),
        in_specs=[pl.BlockSpec((1, PROMPT_TILE, d), lambda b, t: (b, t, 0)),
                  _const_spec(meta_tokens.shape)] + weight_specs,
        out_specs=[pl.BlockSpec((1, PROMPT_TILE, d), lambda b, t: (b, t, 0)),
                   pl.BlockSpec((1, 1, POOL_BUF, POOL_WIDTH), lambda b, t: (0, b, 0, 0)),
                   pl.BlockSpec((1, 1, GLA_HEADS, GLA_DK, GLA_DV), lambda b, t: (0, b, 0, 0, 0))],
        out_shape=[jax.ShapeDtypeStruct(x_prompt.shape, F32),
                   jax.ShapeDtypeStruct((1, bp, POOL_BUF, POOL_WIDTH), F32),
                   jax.ShapeDtypeStruct((1, bp, GLA_HEADS, GLA_DK, GLA_DV), F32)],
        scratch_shapes=[pltpu.VMEM((POOL_PAD + TAIL_ROWS + PROMPT_TILE, POOL_WIDTH), F32),
                        pltpu.VMEM((len(POOL_WINDOWS) - 1, POOL_PAD + TAIL_ROWS + PROMPT_TILE, POOL_WIDTH), F32),
                        pltpu.VMEM((GLA_HEADS, GLA_DK, GLA_DV), F32),
                        pltpu.VMEM((N_META, POOL_WIDTH), F32),
                        pltpu.VMEM((GLA_HEADS, GLA_DK, GLA_DV), F32)],
        compiler_params=pltpu.CompilerParams(
            dimension_semantics=("arbitrary", "arbitrary"), vmem_limit_bytes=VMEM_LIMIT),
        name="prompt_mixer",
    )(x_prompt, meta_tokens, *mixer_weights)

    nbb = SAMPLE_BATCH_BLOCK
    x2_s, pool_s, gla_s = pl.pallas_call(
        _sample_mixer_kernel,
        grid=(bs // nbb,),
        in_specs=[pl.BlockSpec((nbb, ts, d), lambda i: (i, 0, 0)),
                  pl.BlockSpec((1, nbb, POOL_BUF, POOL_WIDTH), lambda i: (0, i, 0, 0)),
                  pl.BlockSpec((1, nbb, GLA_HEADS, GLA_DK, GLA_DV), lambda i: (0, i, 0, 0, 0))] + weight_specs,
        out_specs=[pl.BlockSpec((nbb, ts, d), lambda i: (i, 0, 0)),
                   pl.BlockSpec((1, nbb, POOL_BUF, POOL_WIDTH), lambda i: (0, i, 0, 0)),
                   pl.BlockSpec((1, nbb, GLA_HEADS, GLA_DK, GLA_DV), lambda i: (0, i, 0, 0, 0))],
        out_shape=[jax.ShapeDtypeStruct(x_sample.shape, F32),
                   jax.ShapeDtypeStruct(state_pool.shape, F32),
                   jax.ShapeDtypeStruct(state_gla.shape, F32)],
        scratch_shapes=[pltpu.VMEM((nbb, SAMPLE_ROWS, D_MODEL), F32),
                        pltpu.VMEM((nbb, TAIL_ROWS + SAMPLE_ROWS, POOL_WIDTH), F32),
                        pltpu.VMEM((nbb, GLA_HEADS * SAMPLE_ROWS, GLA_KW), BF16),
                        pltpu.VMEM((nbb, SAMPLE_ROWS, GLA_KW), F32),
                        pltpu.VMEM((nbb, SAMPLE_ROWS, 2 * GLA_VW), F32),
                        pltpu.VMEM((nbb, GLA_HEADS * SAMPLE_ROWS, GLA_DV), F32)],
        compiler_params=pltpu.CompilerParams(
            dimension_semantics=("arbitrary",), vmem_limit_bytes=VMEM_LIMIT),
        name="sample_mixer",
    )(x_sample, state_pool, state_gla, *mixer_weights)

    y_prompt = _ffn_call(x2_p, *ffn_weights)
    y_sample = _ffn_call(x2_s.reshape(1, bs * ts, d), *ffn_weights).reshape(bs, ts, d)
    return y_prompt, y_sample, pool_p, gla_p, pool_s, gla_s
```

```python
import functools

import jax
import jax.numpy as jnp
from jax import lax
from jax.experimental import pallas as pl
from jax.experimental.pallas import tpu as pltpu

F32 = jnp.float32
BF16 = jnp.bfloat16

D_MODEL = 1024
N_META = 16
POOL_WIDTH = 512
POOL_WINDOWS = (2, 4, 8, 16)
POOL_GROUP_DIM = 128
POOL_BUF = 15
GLA_HEADS = 4
GLA_DV = 128
GLA_DK = 64
GLA_KW = GLA_HEADS * GLA_DK
GLA_VW = GLA_HEADS * GLA_DV
GLA_GATE_RANK = 16
GLA_TAU = 16.0
GLA_CHUNK = 64
D_FF = 2816
EPS = 1e-6

LANES = 128
SUBLANES = 8
MAIN_W = POOL_WIDTH + 2 * GLA_KW + 2 * GLA_VW
TAIL_ROWS = 16
POOL_PAD = 8

PROMPT_TILE = 512
FFN_TILE = 512
FFN_CHUNK = 256
SAMPLE_BATCH_BLOCK = 16
SAMPLE_ROWS = 8
VMEM_LIMIT = 60 * 1024 * 1024


def _dot(a, b):
    return jnp.dot(a, b, preferred_element_type=F32)


def _dot_nt(a, b):
    return lax.dot_general(a, b, (((1,), (1,)), ((), ())), preferred_element_type=F32)


def _dot_tn(a, b):
    return lax.dot_general(a, b, (((0,), (0,)), ((), ())), preferred_element_type=F32)


def _rms(x, g):
    return x * lax.rsqrt(jnp.mean(x * x, axis=-1, keepdims=True) + EPS) * g


def _sigmoid(x):
    return 0.5 * jnp.tanh(0.5 * x) + 0.5


def _silu(x):
    half = 0.5 * x
    return half * jnp.tanh(half) + half


def _log_sigmoid(x):
    return jnp.minimum(x, 0.0) - jnp.log(1.0 + jnp.exp(-jnp.abs(x)))


def _split_bf16(x):
    hi = x.astype(BF16)
    lo = (x - hi.astype(F32)).astype(BF16)
    return hi, lo


def _in_proj(x, gmix_ref, wmain_ref, wzr_ref, wgab_ref, wgk_ref, bgk_ref):
    h = _rms(x, gmix_ref[...]).astype(BF16)
    zr = _dot(h, wzr_ref[...])
    z = _dot(zr.astype(BF16), wgk_ref[...]) + bgk_ref[...]
    c_q, c_k, c_v, c_og = POOL_WIDTH, POOL_WIDTH + GLA_KW, POOL_WIDTH + 2 * GLA_KW, POOL_WIDTH + 2 * GLA_KW + GLA_VW
    qk = _dot(h, wmain_ref[:, c_q:c_v])
    q = qk[:, :GLA_KW] * (GLA_DK ** -0.5)
    k = qk[:, GLA_KW:]
    u = _dot(h, wmain_ref[:, 0:c_q])
    vog = _dot(h, wmain_ref[:, c_v:MAIN_W])
    v = vog[:, :GLA_VW]
    og = vog[:, GLA_VW:]
    return h, u, q, k, v, og, z


def _gate_proj(h, wgab_ref, lo, hi):
    return _dot(h, wgab_ref[:, lo:hi])


def _chunk_cumsum(g, chunk):
    m = g.shape[0]
    r = lax.broadcasted_iota(jnp.int32, (m, m), 0)
    c = lax.broadcasted_iota(jnp.int32, (m, m), 1)
    tri = jnp.where((r // chunk == c // chunk) & (c <= r), 1.0, 0.0).astype(BF16)
    hi, lo = _split_bf16(g)
    return _dot(tri, hi) + _dot(tri, lo)


def _head_lane_mask(width, per_head):
    lane = lax.broadcasted_iota(jnp.int32, (1, width), 1)
    return [(lane // per_head) == h for h in range(GLA_HEADS)]


def _block_diag_rows(x_bf, per_head):
    r = x_bf.shape[0]
    zero = jnp.zeros((r, per_head), x_bf.dtype)
    rows = []
    for h in range(GLA_HEADS):
        rows.append(jnp.concatenate(
            [x_bf[:, h * per_head:(h + 1) * per_head] if hh == h else zero for hh in range(GLA_HEADS)], axis=1))
    return jnp.concatenate(rows, axis=0)


def _gla_post(o, og, gnorm_ref, wgp_ref):
    parts = []
    for h in range(GLA_HEADS):
        oh = o[:, h * GLA_DV:(h + 1) * GLA_DV]
        parts.append(oh * lax.rsqrt(jnp.mean(oh * oh, axis=-1, keepdims=True) + EPS) * gnorm_ref[...])
    on = jnp.concatenate(parts, axis=1)
    on = on * _silu(og)
    return _dot(on.astype(BF16), wgp_ref[...])


def _pool_post(pooled, wpg_ref, pscale_ref, wpp_ref):
    pb = pooled.astype(BF16)
    mixed = jnp.concatenate(
        [_dot(pb[:, g * POOL_GROUP_DIM:(g + 1) * POOL_GROUP_DIM], wpg_ref[g]) for g in range(len(POOL_WINDOWS))],
        axis=1)
    return _dot((mixed * pscale_ref[...]).astype(BF16), wpp_ref[...])


def _merge(x, y_a, y_b, sa, sb, wout_ref):
    merged = sa * y_a + sb * y_b
    return x + _dot(merged.astype(BF16), wout_ref[...])


def _decay_columns(decay_row):
    return jnp.transpose(jnp.broadcast_to(decay_row, (LANES, decay_row.shape[1])))


def _state_update(s_heads, kd_bf, v_bf, decay_row):
    upd = _dot_tn(kd_bf, v_bf)
    dcol = _decay_columns(decay_row)
    out = []
    for h in range(GLA_HEADS):
        rows = slice(h * GLA_DK, (h + 1) * GLA_DK)
        out.append(dcol[rows] * s_heads[h] + upd[rows, h * GLA_DV:(h + 1) * GLA_DV])
    return out


def _prompt_mixer_kernel(x_ref, meta_ref, gmix_ref, wmain_ref, wzr_ref, wgab_ref, wgk_ref, bgk_ref,
                         wpg_ref, pscale_ref, wpp_ref, gnorm_ref, wgp_ref, wout_ref,
                         x2_ref, pbuf_ref, sout_ref,
                         ext_ref, lvl_ref, s_ref, meta_tail_ref, meta_s_ref):
    b_idx = pl.program_id(0)
    t_idx = pl.program_id(1)
    tile = x_ref.shape[1]
    n_chunks = tile // GLA_CHUNK
    proj_refs = (gmix_ref, wmain_ref, wzr_ref, wgab_ref, wgk_ref, bgk_ref)

    @pl.when((b_idx == 0) & (t_idx == 0))
    def _():
        _, u, _, k, v, _, z = _in_proj(meta_ref[...], *proj_refs)
        meta_tail_ref[...] = u
        g = _log_sigmoid(z) * (1.0 / GLA_TAU)
        b = _chunk_cumsum(g, N_META)
        b_last = b[N_META - 1:N_META, :]
        kd = k * jnp.exp(b_last - b)
        zero_s = [jnp.zeros((GLA_DK, GLA_DV), F32)] * GLA_HEADS
        s_new = _state_update(zero_s, kd.astype(BF16), v.astype(BF16), jnp.exp(b_last))
        for h in range(GLA_HEADS):
            meta_s_ref[h] = s_new[h]

    @pl.when(t_idx == 0)
    def _():
        ext_ref[0:POOL_PAD, :] = jnp.zeros((POOL_PAD, POOL_WIDTH), F32)
        lvl_ref[:, 0:POOL_PAD, :] = jnp.zeros((lvl_ref.shape[0], POOL_PAD, POOL_WIDTH), F32)
        ext_ref[POOL_PAD:POOL_PAD + TAIL_ROWS, :] = meta_tail_ref[...]
        s_ref[...] = meta_s_ref[...]

    x = x_ref[0]
    xn, u, q, k, v, og, z = _in_proj(x, *proj_refs)

    base = POOL_PAD + TAIL_ROWS
    span = TAIL_ROWS + tile
    ext_ref[base:base + tile, :] = u
    cur = ext_ref[POOL_PAD:POOL_PAD + span, :]
    pooled = []
    for gi, w in enumerate(POOL_WINDOWS):
        shift = w // 2
        lo = gi * POOL_GROUP_DIM
        prev_ref = ext_ref if gi == 0 else lvl_ref.at[gi - 1]
        cur = cur[:, (POOL_GROUP_DIM if gi else 0):] + prev_ref[POOL_PAD - shift:POOL_PAD - shift + span, lo:]
        pooled.append(cur[TAIL_ROWS:, 0:POOL_GROUP_DIM] * (1.0 / w) - u[:, lo:lo + POOL_GROUP_DIM])
        if gi + 1 < len(POOL_WINDOWS):
            lvl_ref[gi, POOL_PAD:POOL_PAD + span, lo:] = cur
    y_a = _pool_post(jnp.concatenate(pooled, axis=1), wpg_ref, pscale_ref, wpp_ref)

    ext_ref[POOL_PAD:base, :] = ext_ref[POOL_PAD + tile:base + tile, :]

    g = _log_sigmoid(z) * (1.0 / GLA_TAU)
    b = _chunk_cumsum(g, GLA_CHUNK)
    b3 = b.reshape(n_chunks, GLA_CHUNK, GLA_KW)
    b_last = jnp.broadcast_to(b3[:, GLA_CHUNK - 1:GLA_CHUNK, :], b3.shape).reshape(tile, GLA_KW)
    qe = (q * jnp.exp(b)).astype(BF16)
    ke = k * jnp.exp(-b)
    kd = (k * jnp.exp(b_last - b)).astype(BF16)
    decay = jnp.exp(b_last)
    v_bf = v.astype(BF16)

    k_masks = _head_lane_mask(GLA_KW, GLA_DK)
    row_i = lax.broadcasted_iota(jnp.int32, (GLA_CHUNK, GLA_KW), 0)
    col_j = lax.broadcasted_iota(jnp.int32, (GLA_CHUNK, GLA_KW), 1) % GLA_CHUNK
    causal = col_j <= row_i

    gate_cols = 2 * D_MODEL // n_chunks
    gate_parts = []
    s_heads = [s_ref[hd] for hd in range(GLA_HEADS)]
    o_chunks = []
    for c in range(n_chunks):
        gate_parts.append(_sigmoid(_gate_proj(xn, wgab_ref, c * gate_cols, (c + 1) * gate_cols)))
        rows = slice(c * GLA_CHUNK, (c + 1) * GLA_CHUNK)
        ke_c = ke[rows]
        ke_bd = jnp.concatenate([jnp.where(k_masks[h], ke_c, 0.0) for h in range(GLA_HEADS)], axis=0).astype(BF16)
        att = jnp.where(causal, _dot_nt(qe[rows], ke_bd), 0.0)
        s_rows = []
        for h in range(GLA_HEADS):
            zero = jnp.zeros((GLA_DK, GLA_DV), BF16)
            s_rows.append(jnp.concatenate(
                [s_heads[h].astype(BF16) if hh == h else zero for hh in range(GLA_HEADS)], axis=1))
        rhs = jnp.concatenate(s_rows + [_block_diag_rows(v_bf[rows], GLA_DV)], axis=0)
        lhs = jnp.concatenate([qe[rows], att.astype(BF16)], axis=1)
        o_chunks.append(_dot(lhs, rhs))
        s_heads = _state_update(s_heads, kd[rows], v_bf[rows], decay[c * GLA_CHUNK:c * GLA_CHUNK + 1, :])
    for h in range(GLA_HEADS):
        s_ref[h] = s_heads[h]

    y_b = _gla_post(jnp.concatenate(o_chunks, axis=0), og, gnorm_ref, wgp_ref)
    sg = jnp.concatenate(gate_parts, axis=1)
    x2_ref[0] = _merge(x, y_a, y_b, sg[:, :D_MODEL], sg[:, D_MODEL:], wout_ref)

    @pl.when(t_idx == pl.num_programs(1) - 1)
    def _():
        pbuf_ref[0, 0] = ext_ref[base - POOL_BUF:base, :]
        sout_ref[0, 0] = s_ref[...]


def _sample_mixer_kernel(x_ref, pool_ref, sin_ref, gmix_ref, wmain_ref, wzr_ref, wgab_ref, wgk_ref, bgk_ref,
                         wpg_ref, pscale_ref, wpp_ref, gnorm_ref, wgp_ref, wout_ref,
                         x2_ref, pbuf_ref, sout_ref,
                         xs_ref, ext_ref, qm_ref, kdx_ref, rhs_ref, oi_ref):
    nb, seq, _ = x_ref.shape
    rows_pb = SAMPLE_ROWS
    m = nb * rows_pb
    proj_refs = (gmix_ref, wmain_ref, wzr_ref, wgab_ref, wgk_ref, bgk_ref)

    xs_ref[...] = jnp.zeros(xs_ref.shape, F32)
    xs_ref[:, 0:seq, :] = x_ref[...]
    x = xs_ref[...].reshape(m, D_MODEL)
    xn, u, q, k, v, og, z = _in_proj(x, *proj_refs)

    ext_ref[:, 0:1, :] = jnp.zeros((nb, 1, POOL_WIDTH), F32)
    ext_ref[:, 1:TAIL_ROWS, :] = pool_ref[0]
    ext_ref[:, TAIL_ROWS:TAIL_ROWS + rows_pb, :] = u.reshape(nb, rows_pb, POOL_WIDTH)
    pooled = []
    for gi, w in enumerate(POOL_WINDOWS):
        cols = slice(gi * POOL_GROUP_DIM, (gi + 1) * POOL_GROUP_DIM)
        acc = ext_ref[:, TAIL_ROWS:TAIL_ROWS + rows_pb, cols]
        cur = acc
        for n in range(1, w):
            acc = acc + ext_ref[:, TAIL_ROWS - n:TAIL_ROWS - n + rows_pb, cols]
        pooled.append(acc * (1.0 / w) - cur)
    pooled = jnp.concatenate(pooled, axis=2).reshape(m, POOL_WIDTH)
    y_a = _pool_post(pooled, wpg_ref, pscale_ref, wpp_ref)
    pbuf_ref[0] = ext_ref[:, seq + 1:seq + 1 + POOL_BUF, :]

    r8 = lax.broadcasted_iota(jnp.int32, (m, 1), 0) % rows_pb
    g = jnp.where(r8 < seq, _log_sigmoid(z) * (1.0 / GLA_TAU), 0.0)
    b = _chunk_cumsum(g, rows_pb)
    b3 = b.reshape(nb, rows_pb, GLA_KW)
    b_last = jnp.broadcast_to(b3[:, seq - 1:seq, :], b3.shape).reshape(m, GLA_KW)
    qe = q * jnp.exp(b)
    ke = k * jnp.exp(-b)
    kd = k * jnp.exp(b_last - b)
    decay = jnp.exp(b_last)
    v_bf = v.astype(BF16)

    k_masks = _head_lane_mask(GLA_KW, GLA_DK)
    ke_bd = jnp.concatenate([jnp.where(k_masks[h], ke, 0.0) for h in range(GLA_HEADS)], axis=0).astype(BF16)
    row_i = lax.broadcasted_iota(jnp.int32, (m, GLA_HEADS * m), 0)
    col_j = lax.broadcasted_iota(jnp.int32, (m, GLA_HEADS * m), 1) % m
    keep = (row_i // rows_pb == col_j // rows_pb) & (col_j <= row_i)
    att = jnp.where(keep, _dot_nt(qe.astype(BF16), ke_bd), 0.0).astype(BF16)
    o_intra = _dot(att, _block_diag_rows(v_bf, GLA_DV))

    qe3 = qe.reshape(nb, rows_pb, GLA_KW)
    qm_ref[...] = jnp.concatenate([jnp.where(k_masks[h], qe3, 0.0) for h in range(GLA_HEADS)], axis=1).astype(BF16)
    d_hi = decay.astype(BF16).astype(F32)
    d_lo = decay - d_hi
    kdx = jnp.where(r8 == seq, d_hi, jnp.where(r8 == seq + 1, d_lo, kd))
    kdx_ref[...] = kdx.reshape(nb, rows_pb, GLA_KW)
    ones_rows = jnp.where((r8 == seq) | (r8 == seq + 1), 1.0, 0.0) + jnp.zeros((m, GLA_DV), F32)
    rhs = jnp.concatenate(
        [piece for h in range(GLA_HEADS) for piece in (v[:, h * GLA_DV:(h + 1) * GLA_DV], ones_rows)], axis=1)
    rhs_ref[...] = rhs.reshape(nb, rows_pb, 2 * GLA_VW)

    def per_batch(i, carry):
        s_all = sin_ref[0, i]
        s_flat = s_all.reshape(GLA_KW, GLA_DV).astype(BF16)
        oi_ref[i] = _dot(qm_ref[i], s_flat)
        kdt = jnp.transpose(kdx_ref[i]).astype(BF16)
        rhs_i = rhs_ref[i].astype(BF16)
        for h in range(GLA_HEADS):
            r = _dot(kdt[h * GLA_DK:(h + 1) * GLA_DK], rhs_i[:, h * 2 * GLA_DV:(h + 1) * 2 * GLA_DV])
            sout_ref[0, i, h] = r[:, GLA_DV:] * s_all[h] + r[:, :GLA_DV]
        return carry

    lax.fori_loop(0, nb, per_batch, 0, unroll=4)

    oi = oi_ref[...]
    o_inter = jnp.concatenate([oi[:, h * rows_pb:(h + 1) * rows_pb, :] for h in range(GLA_HEADS)], axis=2)
    o = o_intra + o_inter.reshape(m, GLA_VW)
    y_b = _gla_post(o, og, gnorm_ref, wgp_ref)
    sg = _sigmoid(_gate_proj(xn, wgab_ref, 0, 2 * D_MODEL))
    x2 = _merge(x, y_a, y_b, sg[:, :D_MODEL], sg[:, D_MODEL:], wout_ref)
    x2_ref[...] = x2.reshape(nb, rows_pb, D_MODEL)[:, 0:seq, :]


def _ffn_kernel(x_ref, gffn_ref, wi_ref, wo_ref, gfin_ref, y_ref):
    x = x_ref[0]
    h = _rms(x, gffn_ref[...]).astype(BF16)
    acc = x
    for c in range(D_FF // FFN_CHUNK):
        lo = c * FFN_CHUNK
        gate = _dot(h, wi_ref[:, lo:lo + FFN_CHUNK])
        up = _dot(h, wi_ref[:, D_FF + lo:D_FF + lo + FFN_CHUNK])
        act = (_silu(gate) * up).astype(BF16)
        acc = acc + _dot(act, wo_ref[lo:lo + FFN_CHUNK, :])
    y_ref[0] = _rms(acc, gfin_ref[...])


def _const_spec(shape):
    zeros = (0,) * len(shape)
    return pl.BlockSpec(shape, lambda *_: zeros, pipeline_mode=pl.Buffered(1))


def _ffn_call(x, gffn, wi, wo, gfin):
    nb, t, d = x.shape
    tile = min(FFN_TILE, t)
    weights = (gffn, wi, wo, gfin)
    return pl.pallas_call(
        _ffn_kernel,
        grid=(nb, t // tile),
        in_specs=[pl.BlockSpec((1, tile, d), lambda b, i: (b, i, 0))] + [_const_spec(w.shape) for w in weights],
        out_specs=pl.BlockSpec((1, tile, d), lambda b, i: (b, i, 0)),
        out_shape=jax.ShapeDtypeStruct(x.shape, F32),
        compiler_params=pltpu.CompilerParams(
            dimension_semantics=("arbitrary", "arbitrary"), vmem_limit_bytes=VMEM_LIMIT),
        name="ffn",
    )(x, *weights)


def kernel(x_prompt, x_sample, state_pool, state_gla, meta_tokens, g_mix, w_in, w_gk_up, b_gk, w_pool_group,
           pool_scale, w_pool_proj, g_gla_norm, w_gla_proj, w_out, g_ffn, w_ffn_in, w_ffn_out, g_final):
    depth = w_in.shape[0]
    assert depth == 1, "single-layer trunk only"
    bp, tp, d = x_prompt.shape
    bs, ts, _ = x_sample.shape
    assert d == D_MODEL and tp % PROMPT_TILE == 0 and bs % SAMPLE_BATCH_BLOCK == 0 and ts + 2 <= SAMPLE_ROWS
    assert meta_tokens.shape == (N_META, D_MODEL)

    w_in0 = w_in[0]
    zr_lo = MAIN_W
    gab_lo = MAIN_W + GLA_GATE_RANK
    w_main = w_in0[:, :MAIN_W].astype(BF16)
    w_zr = jnp.pad(w_in0[:, zr_lo:gab_lo], ((0, 0), (0, LANES - GLA_GATE_RANK))).astype(BF16)
    w_gab = w_in0[:, gab_lo:].astype(BF16)
    w_gk = jnp.pad(w_gk_up[0], ((0, LANES - GLA_GATE_RANK), (0, 0))).astype(BF16)
    mixer_weights = (
        g_mix[0].reshape(1, D_MODEL), w_main, w_zr, w_gab, w_gk, b_gk[0].reshape(1, GLA_KW),
        w_pool_group[0].astype(BF16), pool_scale[0].reshape(1, POOL_WIDTH), w_pool_proj[0].astype(BF16),
        g_gla_norm[0].reshape(1, GLA_DV), w_gla_proj[0].astype(BF16), w_out[0].astype(BF16))
    weight_specs = [_const_spec(w.shape) for w in mixer_weights]
    ffn_weights = (g_ffn[0].reshape(1, D_MODEL), w_ffn_in[0].astype(BF16), w_ffn_out[0].astype(BF16),
                   g_final.reshape(1, D_MODEL))

    n_t = tp // PROMPT_TILE
    x2_p, pool_p, gla_p = pl.pallas_call(
        _prompt_mixer_kernel,
        grid=(bp, n_t),
        in_specs=[pl.BlockSpec((1, PROMPT_TILE, d), lambda b, t: (b, t, 0)),
                  _const_spec(meta_tokens.shape)] + weight_specs,
        out_specs=[pl.BlockSpec((1, PROMPT_TILE, d), lambda b, t: (b, t, 0)),
                   pl.BlockSpec((1, 1, POOL_BUF, POOL_WIDTH), lambda b, t: (0, b, 0, 0)),
                   pl.BlockSpec((1, 1, GLA_HEADS, GLA_DK, GLA_DV), lambda b, t: (0, b, 0, 0, 0))],
        out_shape=[jax.ShapeDtypeStruct(x_prompt.shape, F32),
                   jax.ShapeDtypeStruct((1, bp, POOL_BUF, POOL_WIDTH), F32),
                   jax.ShapeDtypeStruct((1, bp, GLA_HEADS, GLA_DK, GLA_DV), F32)],
        scratch_shapes=[pltpu.VMEM((POOL_PAD + TAIL_ROWS + PROMPT_TILE, POOL_WIDTH), F32),
                        pltpu.VMEM((len(POOL_WINDOWS) - 1, POOL_PAD + TAIL_ROWS + PROMPT_TILE, POOL_WIDTH), F32),
                        pltpu.VMEM((GLA_HEADS, GLA_DK, GLA_DV), F32),
                        pltpu.VMEM((N_META, POOL_WIDTH), F32),
                        pltpu.VMEM((GLA_HEADS, GLA_DK, GLA_DV), F32)],
        compiler_params=pltpu.CompilerParams(
            dimension_semantics=("arbitrary", "arbitrary"), vmem_limit_bytes=VMEM_LIMIT),
        name="prompt_mixer",
    )(x_prompt, meta_tokens, *mixer_weights)

    nbb = SAMPLE_BATCH_BLOCK
    x2_s, pool_s, gla_s = pl.pallas_call(
        _sample_mixer_kernel,
        grid=(bs // nbb,),
        in_specs=[pl.BlockSpec((nbb, ts, d), lambda i: (i, 0, 0)),
                  pl.BlockSpec((1, nbb, POOL_BUF, POOL_WIDTH), lambda i: (0, i, 0, 0)),
                  pl.BlockSpec((1, nbb, GLA_HEADS, GLA_DK, GLA_DV), lambda i: (0, i, 0, 0, 0))] + weight_specs,
        out_specs=[pl.BlockSpec((nbb, ts, d), lambda i: (i, 0, 0)),
                   pl.BlockSpec((1, nbb, POOL_BUF, POOL_WIDTH), lambda i: (0, i, 0, 0)),
                   pl.BlockSpec((1, nbb, GLA_HEADS, GLA_DK, GLA_DV), lambda i: (0, i, 0, 0, 0))],
        out_shape=[jax.ShapeDtypeStruct(x_sample.shape, F32),
                   jax.ShapeDtypeStruct(state_pool.shape, F32),
                   jax.ShapeDtypeStruct(state_gla.shape, F32)],
        scratch_shapes=[pltpu.VMEM((nbb, SAMPLE_ROWS, D_MODEL), F32),
                        pltpu.VMEM((nbb, TAIL_ROWS + SAMPLE_ROWS, POOL_WIDTH), F32),
                        pltpu.VMEM((nbb, GLA_HEADS * SAMPLE_ROWS, GLA_KW), BF16),
                        pltpu.VMEM((nbb, SAMPLE_ROWS, GLA_KW), F32),
                        pltpu.VMEM((nbb, SAMPLE_ROWS, 2 * GLA_VW), F32),
                        pltpu.VMEM((nbb, GLA_HEADS * SAMPLE_ROWS, GLA_DV), F32)],
        compiler_params=pltpu.CompilerParams(
            dimension_semantics=("arbitrary",), vmem_limit_bytes=VMEM_LIMIT),
        name="sample_mixer",
    )(x_sample, state_pool, state_gla, *mixer_weights)

    y_prompt = _ffn_call(x2_p, *ffn_weights)
    y_sample = _ffn_call(x2_s.reshape(1, bs * ts, d), *ffn_weights).reshape(bs, ts, d)
    return y_prompt, y_sample, pool_p, gla_p, pool_s, gla_s
```

```python
import jax
import jax.numpy as jnp
from jax import lax
from jax.experimental import pallas as pl
from jax.experimental.pallas import tpu as pltpu

F32 = jnp.float32
BF16 = jnp.bfloat16

D_MODEL = 1024
N_META = 16
POOL_WIDTH = 512
POOL_WINDOWS = (2, 4, 8, 16)
POOL_GROUP_DIM = 128
POOL_BUF = 15
GLA_HEADS = 4
GLA_DV = 128
GLA_DK = 64
GLA_KW = GLA_HEADS * GLA_DK
GLA_VW = GLA_HEADS * GLA_DV
GLA_GATE_RANK = 16
GLA_TAU = 16.0
GLA_CHUNK = 64
D_FF = 2816
EPS = 1e-6

LANES = 128
SUBLANES = 8
MAIN_W = POOL_WIDTH + 2 * GLA_KW + 2 * GLA_VW
GAB_LO = MAIN_W + GLA_GATE_RANK
IN_DIM = GAB_LO + 2 * D_MODEL
TAIL_ROWS = 16
POOL_PAD = 8

PROMPT_TILE = 512
FFN_TILE = 512
FFN_CHUNK = 256
SAMPLE_BATCH_BLOCK = 16
SAMPLE_ROWS = 8
IN_STAGE_ROWS = 64
SQ_STAGE_ROWS = 256
FFN_IN_CAST_ROWS = 32
FFN_OUT_CAST_ROWS = 128
VMEM_LIMIT = 60 * 1024 * 1024


def _dot(a, b):
    return jnp.dot(a, b, preferred_element_type=F32)


def _dot_nt(a, b):
    return lax.dot_general(a, b, (((1,), (1,)), ((), ())), preferred_element_type=F32)


def _dot_tn(a, b):
    return lax.dot_general(a, b, (((0,), (0,)), ((), ())), preferred_element_type=F32)


def _rms(x, g):
    return x * lax.rsqrt(jnp.mean(x * x, axis=-1, keepdims=True) + EPS) * g


def _sigmoid(x):
    return 0.5 * jnp.tanh(0.5 * x) + 0.5


def _silu(x):
    half = 0.5 * x
    return half * jnp.tanh(half) + half


def _log_sigmoid(x):
    return jnp.minimum(x, 0.0) - jnp.log(1.0 + jnp.exp(-jnp.abs(x)))


def _split_bf16(x):
    hi = x.astype(BF16)
    lo = (x - hi.astype(F32)).astype(BF16)
    return hi, lo


class _Weights:
    def __init__(self, gmix, bgk, pscale, gnorm, wpg, wmain, wzr, wgab, wgk, wpp, wgp, wout):
        self.gmix, self.bgk, self.pscale, self.gnorm, self.wpg = gmix, bgk, pscale, gnorm, wpg
        self.wmain, self.wzr, self.wgab, self.wgk = wmain, wzr, wgab, wgk
        self.wpp, self.wgp, self.wout = wpp, wgp, wout


def _in_proj(x, w):
    h = _rms(x, w.gmix[...]).astype(BF16)
    zr = _dot(h, w.wzr[...])
    z = _dot(zr.astype(BF16), w.wgk[...]) + w.bgk[...]
    c_q, c_v = POOL_WIDTH, POOL_WIDTH + 2 * GLA_KW
    qk = _dot(h, w.wmain[:, c_q:c_v])
    q = qk[:, :GLA_KW] * (GLA_DK ** -0.5)
    k = qk[:, GLA_KW:]
    u = _dot(h, w.wmain[:, 0:c_q])
    vog = _dot(h, w.wmain[:, c_v:MAIN_W])
    v = vog[:, :GLA_VW]
    og = vog[:, GLA_VW:]
    return h, u, q, k, v, og, z


def _gate_proj(h, w, lo, hi):
    return _dot(h, w.wgab[:, lo:hi])


def _chunk_cumsum(g, chunk):
    m = g.shape[0]
    r = lax.broadcasted_iota(jnp.int32, (m, m), 0)
    c = lax.broadcasted_iota(jnp.int32, (m, m), 1)
    tri = jnp.where((r // chunk == c // chunk) & (c <= r), 1.0, 0.0).astype(BF16)
    hi, lo = _split_bf16(g)
    return _dot(tri, hi) + _dot(tri, lo)


def _head_lane_mask(width, per_head):
    lane = lax.broadcasted_iota(jnp.int32, (1, width), 1)
    return [(lane // per_head) == h for h in range(GLA_HEADS)]


def _block_diag_rows(x_bf, per_head):
    r = x_bf.shape[0]
    zero = jnp.zeros((r, per_head), x_bf.dtype)
    rows = []
    for h in range(GLA_HEADS):
        rows.append(jnp.concatenate(
            [x_bf[:, h * per_head:(h + 1) * per_head] if hh == h else zero for hh in range(GLA_HEADS)], axis=1))
    return jnp.concatenate(rows, axis=0)


def _gla_post(o, og, w):
    parts = []
    for h in range(GLA_HEADS):
        oh = o[:, h * GLA_DV:(h + 1) * GLA_DV]
        parts.append(oh * lax.rsqrt(jnp.mean(oh * oh, axis=-1, keepdims=True) + EPS) * w.gnorm[...])
    on = jnp.concatenate(parts, axis=1)
    on = on * _silu(og)
    return _dot(on.astype(BF16), w.wgp[...])


def _pool_post(pooled, w):
    pb = pooled.astype(BF16)
    mixed = jnp.concatenate(
        [_dot(pb[:, g * POOL_GROUP_DIM:(g + 1) * POOL_GROUP_DIM], w.wpg[0, g].astype(BF16))
         for g in range(len(POOL_WINDOWS))], axis=1)
    return _dot((mixed * w.pscale[...]).astype(BF16), w.wpp[...])


def _merge(x, y_a, y_b, sa, sb, w):
    merged = sa * y_a + sb * y_b
    return x + _dot(merged.astype(BF16), w.wout[...])


def _decay_columns(decay_row):
    return jnp.transpose(jnp.broadcast_to(decay_row, (LANES, decay_row.shape[1])))


def _state_update(s_heads, kd_bf, v_bf, decay_row):
    upd = _dot_tn(kd_bf, v_bf)
    dcol = _decay_columns(decay_row)
    out = []
    for h in range(GLA_HEADS):
        rows = slice(h * GLA_DK, (h + 1) * GLA_DK)
        out.append(dcol[rows] * s_heads[h] + upd[rows, h * GLA_DV:(h + 1) * GLA_DV])
    return out


def _stage_weights(win_hbm, wpp_hbm, wgp_hbm, wout_hbm, wgk_ref, w, stage_in, stage_sq, sem_in, sem_sq):
    n_in = D_MODEL // IN_STAGE_ROWS
    sq_plan = ([(wpp_hbm, w.wpp, r) for r in range(0, POOL_WIDTH, SQ_STAGE_ROWS)]
               + [(wgp_hbm, w.wgp, r) for r in range(0, GLA_VW, SQ_STAGE_ROWS)]
               + [(wout_hbm, w.wout, r) for r in range(0, D_MODEL, SQ_STAGE_ROWS)])

    def in_copy(i):
        return pltpu.make_async_copy(
            win_hbm.at[0, pl.ds(i * IN_STAGE_ROWS, IN_STAGE_ROWS), :], stage_in.at[i % 2], sem_in.at[i % 2])

    def sq_copy(j):
        src, _, r0 = sq_plan[j]
        return pltpu.make_async_copy(src.at[0, pl.ds(r0, SQ_STAGE_ROWS), :], stage_sq.at[j % 2], sem_sq.at[j % 2])

    for i in range(2):
        in_copy(i).start()
        sq_copy(i).start()

    w.wgk[...] = jnp.zeros(w.wgk.shape, BF16)
    w.wgk[0:GLA_GATE_RANK, :] = wgk_ref[0].astype(BF16)

    gate_lane = lax.broadcasted_iota(jnp.int32, (1, LANES), 1) < GLA_GATE_RANK
    for i in range(n_in):
        in_copy(i).wait()
        rows = slice(i * IN_STAGE_ROWS, (i + 1) * IN_STAGE_ROWS)
        slot = i % 2
        w.wmain[rows, :] = stage_in[slot, :, 0:MAIN_W].astype(BF16)
        w.wzr[rows, :] = jnp.where(gate_lane, stage_in[slot, :, MAIN_W:MAIN_W + LANES], 0.0).astype(BF16)
        w.wgab[rows, :] = stage_in[slot, :, GAB_LO:IN_DIM].astype(BF16)
        if i + 2 < n_in:
            in_copy(i + 2).start()
    for j in range(len(sq_plan)):
        sq_copy(j).wait()
        _, dst, r0 = sq_plan[j]
        dst[r0:r0 + SQ_STAGE_ROWS, :] = stage_sq[j % 2].astype(BF16)
        if j + 2 < len(sq_plan):
            sq_copy(j + 2).start()


def _prompt_tile(b_first, t_idx, n_t, x_ref, meta_ref, w, x2_ref, pbuf_ref, sout_ref,
                 ext_ref, lvl_ref, s_ref, meta_tail_ref, meta_s_ref):
    tile = x_ref.shape[1]
    n_chunks = tile // GLA_CHUNK

    @pl.when(b_first)
    def _():
        _, u, _, k, v, _, z = _in_proj(meta_ref[...], w)
        meta_tail_ref[...] = u
        g = _log_sigmoid(z) * (1.0 / GLA_TAU)
        b = _chunk_cumsum(g, N_META)
        b_last = b[N_META - 1:N_META, :]
        kd = k * jnp.exp(b_last - b)
        zero_s = [jnp.zeros((GLA_DK, GLA_DV), F32)] * GLA_HEADS
        s_new = _state_update(zero_s, kd.astype(BF16), v.astype(BF16), jnp.exp(b_last))
        for hd in range(GLA_HEADS):
            meta_s_ref[hd] = s_new[hd]

    @pl.when(t_idx == 0)
    def _():
        ext_ref[0:POOL_PAD, :] = jnp.zeros((POOL_PAD, POOL_WIDTH), F32)
        lvl_ref[:, 0:POOL_PAD, :] = jnp.zeros((lvl_ref.shape[0], POOL_PAD, POOL_WIDTH), F32)
        ext_ref[POOL_PAD:POOL_PAD + TAIL_ROWS, :] = meta_tail_ref[...]
        s_ref[...] = meta_s_ref[...]

    x = x_ref[0]
    xn, u, q, k, v, og, z = _in_proj(x, w)

    base = POOL_PAD + TAIL_ROWS
    span = TAIL_ROWS + tile
    ext_ref[base:base + tile, :] = u
    cur = ext_ref[POOL_PAD:POOL_PAD + span, :]
    pooled = []
    for gi, win in enumerate(POOL_WINDOWS):
        shift = win // 2
        lo = gi * POOL_GROUP_DIM
        prev_ref = ext_ref if gi == 0 else lvl_ref.at[gi - 1]
        cur = cur[:, (POOL_GROUP_DIM if gi else 0):] + prev_ref[POOL_PAD - shift:POOL_PAD - shift + span, lo:]
        pooled.append(cur[TAIL_ROWS:, 0:POOL_GROUP_DIM] * (1.0 / win) - u[:, lo:lo + POOL_GROUP_DIM])
        if gi + 1 < len(POOL_WINDOWS):
            lvl_ref[gi, POOL_PAD:POOL_PAD + span, lo:] = cur
    y_a = _pool_post(jnp.concatenate(pooled, axis=1), w)
    ext_ref[POOL_PAD:base, :] = ext_ref[POOL_PAD + tile:base + tile, :]

    g = _log_sigmoid(z) * (1.0 / GLA_TAU)
    b = _chunk_cumsum(g, GLA_CHUNK)
    b3 = b.reshape(n_chunks, GLA_CHUNK, GLA_KW)
    b_last = jnp.broadcast_to(b3[:, GLA_CHUNK - 1:GLA_CHUNK, :], b3.shape).reshape(tile, GLA_KW)
    qe = (q * jnp.exp(b)).astype(BF16)
    ke = k * jnp.exp(-b)
    kd = (k * jnp.exp(b_last - b)).astype(BF16)
    decay = jnp.exp(b_last)
    v_bf = v.astype(BF16)

    k_masks = _head_lane_mask(GLA_KW, GLA_DK)
    row_i = lax.broadcasted_iota(jnp.int32, (GLA_CHUNK, GLA_KW), 0)
    col_j = lax.broadcasted_iota(jnp.int32, (GLA_CHUNK, GLA_KW), 1) % GLA_CHUNK
    causal = col_j <= row_i

    gate_cols = 2 * D_MODEL // n_chunks
    gate_parts = []
    s_heads = [s_ref[hd] for hd in range(GLA_HEADS)]
    o_chunks = []
    for c in range(n_chunks):
        gate_parts.append(_sigmoid(_gate_proj(xn, w, c * gate_cols, (c + 1) * gate_cols)))
        rows = slice(c * GLA_CHUNK, (c + 1) * GLA_CHUNK)
        ke_c = ke[rows]
        ke_bd = jnp.concatenate([jnp.where(k_masks[hd], ke_c, 0.0) for hd in range(GLA_HEADS)], axis=0).astype(BF16)
        att = jnp.where(causal, _dot_nt(qe[rows], ke_bd), 0.0)
        s_rows = []
        for hd in range(GLA_HEADS):
            zero = jnp.zeros((GLA_DK, GLA_DV), BF16)
            s_rows.append(jnp.concatenate(
                [s_heads[hd].astype(BF16) if hh == hd else zero for hh in range(GLA_HEADS)], axis=1))
        rhs = jnp.concatenate(s_rows + [_block_diag_rows(v_bf[rows], GLA_DV)], axis=0)
        lhs = jnp.concatenate([qe[rows], att.astype(BF16)], axis=1)
        o_chunks.append(_dot(lhs, rhs))
        s_heads = _state_update(s_heads, kd[rows], v_bf[rows], decay[c * GLA_CHUNK:c * GLA_CHUNK + 1, :])
    for hd in range(GLA_HEADS):
        s_ref[hd] = s_heads[hd]

    y_b = _gla_post(jnp.concatenate(o_chunks, axis=0), og, w)
    sg = jnp.concatenate(gate_parts, axis=1)
    x2_ref[0] = _merge(x, y_a, y_b, sg[:, :D_MODEL], sg[:, D_MODEL:], w)

    @pl.when(t_idx == n_t - 1)
    def _():
        pbuf_ref[0, 0] = ext_ref[base - POOL_BUF:base, :]
        sout_ref[0, 0] = s_ref[...]


def _sample_block(x_ref, pool_ref, sin_ref, w, x2_ref, pbuf_ref, sout_ref,
                  xs_ref, ext_ref, qm_ref, kdx_ref, rhs_ref, oi_ref):
    nb, seq, _ = x_ref.shape
    rows_pb = SAMPLE_ROWS
    m = nb * rows_pb

    xs_ref[...] = jnp.zeros(xs_ref.shape, F32)
    xs_ref[:, 0:seq, :] = x_ref[...]
    x = xs_ref[...].reshape(m, D_MODEL)
    xn, u, q, k, v, og, z = _in_proj(x, w)

    ext_ref[:, 0:1, :] = jnp.zeros((nb, 1, POOL_WIDTH), F32)
    ext_ref[:, 1:TAIL_ROWS, :] = pool_ref[0]
    ext_ref[:, TAIL_ROWS:TAIL_ROWS + rows_pb, :] = u.reshape(nb, rows_pb, POOL_WIDTH)
    pooled = []
    for gi, win in enumerate(POOL_WINDOWS):
        cols = slice(gi * POOL_GROUP_DIM, (gi + 1) * POOL_GROUP_DIM)
        acc = ext_ref[:, TAIL_ROWS:TAIL_ROWS + rows_pb, cols]
        cur = acc
        for n in range(1, win):
            acc = acc + ext_ref[:, TAIL_ROWS - n:TAIL_ROWS - n + rows_pb, cols]
        pooled.append(acc * (1.0 / win) - cur)
    pooled = jnp.concatenate(pooled, axis=2).reshape(m, POOL_WIDTH)
    y_a = _pool_post(pooled, w)
    pbuf_ref[0] = ext_ref[:, seq + 1:seq + 1 + POOL_BUF, :]

    r8 = lax.broadcasted_iota(jnp.int32, (m, 1), 0) % rows_pb
    g = jnp.where(r8 < seq, _log_sigmoid(z) * (1.0 / GLA_TAU), 0.0)
    b = _chunk_cumsum(g, rows_pb)
    b3 = b.reshape(nb, rows_pb, GLA_KW)
    b_last = jnp.broadcast_to(b3[:, seq - 1:seq, :], b3.shape).reshape(m, GLA_KW)
    qe = q * jnp.exp(b)
    ke = k * jnp.exp(-b)
    kd = k * jnp.exp(b_last - b)
    decay = jnp.exp(b_last)
    v_bf = v.astype(BF16)

    k_masks = _head_lane_mask(GLA_KW, GLA_DK)
    ke_bd = jnp.concatenate([jnp.where(k_masks[hd], ke, 0.0) for hd in range(GLA_HEADS)], axis=0).astype(BF16)
    row_i = lax.broadcasted_iota(jnp.int32, (m, GLA_HEADS * m), 0)
    col_j = lax.broadcasted_iota(jnp.int32, (m, GLA_HEADS * m), 1) % m
    keep = (row_i // rows_pb == col_j // rows_pb) & (col_j <= row_i)
    att = jnp.where(keep, _dot_nt(qe.astype(BF16), ke_bd), 0.0).astype(BF16)
    o_intra = _dot(att, _block_diag_rows(v_bf, GLA_DV))

    qe3 = qe.reshape(nb, rows_pb, GLA_KW)
    qm_ref[...] = jnp.concatenate([jnp.where(k_masks[hd], qe3, 0.0) for hd in range(GLA_HEADS)], axis=1).astype(BF16)
    d_hi = decay.astype(BF16).astype(F32)
    d_lo = decay - d_hi
    kdx = jnp.where(r8 == seq, d_hi, jnp.where(r8 == seq + 1, d_lo, kd))
    kdx_ref[...] = kdx.reshape(nb, rows_pb, GLA_KW)
    ones_rows = jnp.where((r8 == seq) | (r8 == seq + 1), 1.0, 0.0) + jnp.zeros((m, GLA_DV), F32)
    rhs = jnp.concatenate(
        [piece for hd in range(GLA_HEADS) for piece in (v[:, hd * GLA_DV:(hd + 1) * GLA_DV], ones_rows)], axis=1)
    rhs_ref[...] = rhs.reshape(nb, rows_pb, 2 * GLA_VW)

    def per_batch(i, carry):
        s_all = sin_ref[0, i]
        s_flat = s_all.reshape(GLA_KW, GLA_DV).astype(BF16)
        oi_ref[i] = _dot(qm_ref[i], s_flat)
        kdt = jnp.transpose(kdx_ref[i]).astype(BF16)
        rhs_i = rhs_ref[i].astype(BF16)
        for hd in range(GLA_HEADS):
            r = _dot(kdt[hd * GLA_DK:(hd + 1) * GLA_DK], rhs_i[:, hd * 2 * GLA_DV:(hd + 1) * 2 * GLA_DV])
            sout_ref[0, i, hd] = r[:, GLA_DV:] * s_all[hd] + r[:, :GLA_DV]
        return carry

    lax.fori_loop(0, nb, per_batch, 0, unroll=4)

    oi = oi_ref[...]
    o_inter = jnp.concatenate([oi[:, hd * rows_pb:(hd + 1) * rows_pb, :] for hd in range(GLA_HEADS)], axis=2)
    o = o_intra + o_inter.reshape(m, GLA_VW)
    y_b = _gla_post(o, og, w)
    sg = _sigmoid(_gate_proj(xn, w, 0, 2 * D_MODEL))
    x2 = _merge(x, y_a, y_b, sg[:, :D_MODEL], sg[:, D_MODEL:], w).reshape(nb, rows_pb, D_MODEL)
    for bi in range(nb):
        x2_ref[bi * seq:(bi + 1) * seq, :] = x2[bi, 0:seq, :]


def _mixer_kernel(n_s, n_t,
                  xs_in, pool_in, s_in, xp_in, meta_ref, gmix_ref, bgk_ref, pscale_ref, gnorm_ref, wgk_ref, wpg_ref,
                  win_hbm, wpp_hbm, wgp_hbm, wout_hbm, wfi_in, wfo_in,
                  x2s_out, pools_out, ss_out, x2p_out, poolp_out, sp_out, wfi_out, wfo_out,
                  wmain_s, wzr_s, wgab_s, wgk_s, wpp_s, wgp_s, wout_s, stage_in, stage_sq, sem_in, sem_sq,
                  ext_ref, lvl_ref, s_ref, meta_tail_ref, meta_s_ref,
                  xs_ref, exts_ref, qm_ref, kdx_ref, rhs_ref, oi_ref):
    i = pl.program_id(0)
    w = _Weights(gmix_ref, bgk_ref, pscale_ref, gnorm_ref, wpg_ref,
                 wmain_s, wzr_s, wgab_s, wgk_s, wpp_s, wgp_s, wout_s)

    @pl.when(i == 0)
    def _():
        _stage_weights(win_hbm, wpp_hbm, wgp_hbm, wout_hbm, wgk_ref, w, stage_in, stage_sq, sem_in, sem_sq)

    wfi_out[...] = wfi_in[0].astype(BF16)
    wfo_out[...] = wfo_in[0].astype(BF16)

    @pl.when(i < n_s)
    def _():
        _sample_block(xs_in, pool_in, s_in, w, x2s_out, pools_out, ss_out,
                      xs_ref, exts_ref, qm_ref, kdx_ref, rhs_ref, oi_ref)

    @pl.when(i >= n_s)
    def _():
        t_idx = (i - n_s) % n_t
        _prompt_tile(i == n_s, t_idx, n_t, xp_in, meta_ref, w, x2p_out, poolp_out, sp_out,
                     ext_ref, lvl_ref, s_ref, meta_tail_ref, meta_s_ref)


def _ffn_kernel(n_p, seq, xp_ref, xs_ref, gffn_ref, wi_ref, wo_ref, gfin_ref, yp_ref, ys_ref):
    i = pl.program_id(0)
    x = jnp.where(i < n_p, xp_ref[0], xs_ref[...])
    h = _rms(x, gffn_ref[...]).astype(BF16)
    acc = x
    for c in range(D_FF // FFN_CHUNK):
        lo = c * FFN_CHUNK
        gate = _dot(h, wi_ref[:, lo:lo + FFN_CHUNK])
        up = _dot(h, wi_ref[:, D_FF + lo:D_FF + lo + FFN_CHUNK])
        act = (_silu(gate) * up).astype(BF16)
        acc = acc + _dot(act, wo_ref[lo:lo + FFN_CHUNK, :])
    y = _rms(acc, gfin_ref[...])

    @pl.when(i < n_p)
    def _():
        yp_ref[0] = y

    @pl.when(i == n_p)
    def _():
        for bi in range(ys_ref.shape[0]):
            ys_ref[bi] = y[bi * seq:(bi + 1) * seq, :]


def _const_spec(shape):
    zeros = (0,) * len(shape)
    return pl.BlockSpec(shape, lambda *_: zeros, pipeline_mode=pl.Buffered(1))


def kernel(x_prompt, x_sample, state_pool, state_gla, meta_tokens, g_mix, w_in, w_gk_up, b_gk, w_pool_group,
           pool_scale, w_pool_proj, g_gla_norm, w_gla_proj, w_out, g_ffn, w_ffn_in, w_ffn_out, g_final):
    depth = w_in.shape[0]
    assert depth == 1, "single-layer trunk only"
    bp, tp, d = x_prompt.shape
    bs, ts, _ = x_sample.shape
    nbb = SAMPLE_BATCH_BLOCK
    assert d == D_MODEL and w_in.shape == (1, D_MODEL, IN_DIM) and meta_tokens.shape == (N_META, D_MODEL)
    assert tp % PROMPT_TILE == 0 and bs % nbb == 0 and ts + 2 <= SAMPLE_ROWS and bs * ts == FFN_TILE
    n_t = tp // PROMPT_TILE
    n_p = bp * n_t
    n_s = bs // nbb
    n_fi = D_MODEL // FFN_IN_CAST_ROWS
    n_fo = D_FF // FFN_OUT_CAST_ROWS
    assert n_fi <= n_p and n_fo <= n_p

    def s_idx(i):
        return jnp.minimum(i, n_s - 1)

    def p_idx(i):
        return jnp.maximum(i - n_s, 0)

    small = (meta_tokens, g_mix, b_gk, pool_scale, g_gla_norm, w_gk_up, w_pool_group)
    hbm = pl.BlockSpec(memory_space=pl.ANY)
    in_specs = (
        [pl.BlockSpec((nbb, ts, d), lambda i: (s_idx(i), 0, 0)),
         pl.BlockSpec((1, nbb, POOL_BUF, POOL_WIDTH), lambda i: (0, s_idx(i), 0, 0)),
         pl.BlockSpec((1, nbb, GLA_HEADS, GLA_DK, GLA_DV), lambda i: (0, s_idx(i), 0, 0, 0)),
         pl.BlockSpec((1, PROMPT_TILE, d), lambda i: (p_idx(i) // n_t, p_idx(i) % n_t, 0))]
        + [_const_spec(a.shape) for a in small]
        + [hbm, hbm, hbm, hbm,
           pl.BlockSpec((1, FFN_IN_CAST_ROWS, 2 * D_FF), lambda i: (0, jnp.minimum(p_idx(i), n_fi - 1), 0)),
           pl.BlockSpec((1, FFN_OUT_CAST_ROWS, d), lambda i: (0, jnp.minimum(p_idx(i), n_fo - 1), 0))])
    out_specs = [
        pl.BlockSpec((nbb * ts, d), lambda i: (s_idx(i), 0)),
        pl.BlockSpec((1, nbb, POOL_BUF, POOL_WIDTH), lambda i: (0, s_idx(i), 0, 0)),
        pl.BlockSpec((1, nbb, GLA_HEADS, GLA_DK, GLA_DV), lambda i: (0, s_idx(i), 0, 0, 0)),
        pl.BlockSpec((1, PROMPT_TILE, d), lambda i: (p_idx(i) // n_t, p_idx(i) % n_t, 0)),
        pl.BlockSpec((1, 1, POOL_BUF, POOL_WIDTH), lambda i: (0, p_idx(i) // n_t, 0, 0)),
        pl.BlockSpec((1, 1, GLA_HEADS, GLA_DK, GLA_DV), lambda i: (0, p_idx(i) // n_t, 0, 0, 0)),
        pl.BlockSpec((FFN_IN_CAST_ROWS, 2 * D_FF), lambda i: (jnp.minimum(p_idx(i), n_fi - 1), 0)),
        pl.BlockSpec((FFN_OUT_CAST_ROWS, d), lambda i: (jnp.minimum(p_idx(i), n_fo - 1), 0))]
    out_shape = [
        jax.ShapeDtypeStruct((bs * ts, d), F32),
        jax.ShapeDtypeStruct(state_pool.shape, F32),
        jax.ShapeDtypeStruct(state_gla.shape, F32),
        jax.ShapeDtypeStruct(x_prompt.shape, F32),
        jax.ShapeDtypeStruct((1, bp, POOL_BUF, POOL_WIDTH), F32),
        jax.ShapeDtypeStruct((1, bp, GLA_HEADS, GLA_DK, GLA_DV), F32),
        jax.ShapeDtypeStruct((D_MODEL, 2 * D_FF), BF16),
        jax.ShapeDtypeStruct((D_FF, D_MODEL), BF16)]
    scratch_shapes = [
        pltpu.VMEM((D_MODEL, MAIN_W), BF16), pltpu.VMEM((D_MODEL, LANES), BF16),
        pltpu.VMEM((D_MODEL, 2 * D_MODEL), BF16), pltpu.VMEM((LANES, GLA_KW), BF16),
        pltpu.VMEM((POOL_WIDTH, D_MODEL), BF16), pltpu.VMEM((GLA_VW, D_MODEL), BF16),
        pltpu.VMEM((D_MODEL, D_MODEL), BF16),
        pltpu.VMEM((2, IN_STAGE_ROWS, IN_DIM), F32), pltpu.VMEM((2, SQ_STAGE_ROWS, D_MODEL), F32),
        pltpu.SemaphoreType.DMA((2,)), pltpu.SemaphoreType.DMA((2,)),
        pltpu.VMEM((POOL_PAD + TAIL_ROWS + PROMPT_TILE, POOL_WIDTH), F32),
        pltpu.VMEM((len(POOL_WINDOWS) - 1, POOL_PAD + TAIL_ROWS + PROMPT_TILE, POOL_WIDTH), F32),
        pltpu.VMEM((GLA_HEADS, GLA_DK, GLA_DV), F32),
        pltpu.VMEM((N_META, POOL_WIDTH), F32),
        pltpu.VMEM((GLA_HEADS, GLA_DK, GLA_DV), F32),
        pltpu.VMEM((nbb, SAMPLE_ROWS, D_MODEL), F32),
        pltpu.VMEM((nbb, TAIL_ROWS + SAMPLE_ROWS, POOL_WIDTH), F32),
        pltpu.VMEM((nbb, GLA_HEADS * SAMPLE_ROWS, GLA_KW), BF16),
        pltpu.VMEM((nbb, SAMPLE_ROWS, GLA_KW), F32),
        pltpu.VMEM((nbb, SAMPLE_ROWS, 2 * GLA_VW), F32),
        pltpu.VMEM((nbb, GLA_HEADS * SAMPLE_ROWS, GLA_DV), F32)]

    def mixer(*refs):
        _mixer_kernel(n_s, n_t, *refs)

    x2_s, pool_s, gla_s, x2_p, pool_p, gla_p, wfi_bf, wfo_bf = pl.pallas_call(
        mixer,
        grid=(n_s + n_p,),
        in_specs=in_specs,
        out_specs=out_specs,
        out_shape=out_shape,
        scratch_shapes=scratch_shapes,
        compiler_params=pltpu.CompilerParams(dimension_semantics=("arbitrary",), vmem_limit_bytes=VMEM_LIMIT),
        name="mixer",
    )(x_sample, state_pool, state_gla, x_prompt, *small, w_in, w_pool_proj, w_gla_proj, w_out, w_ffn_in, w_ffn_out)

    def ffn(*refs):
        _ffn_kernel(n_p, ts, *refs)

    def fp_idx(i):
        return jnp.minimum(i, n_p - 1)

    ffn_small = (g_ffn, wfi_bf, wfo_bf, g_final.reshape(1, D_MODEL))
    y_prompt, y_sample = pl.pallas_call(
        ffn,
        grid=(n_p + 1,),
        in_specs=[pl.BlockSpec((1, FFN_TILE, d), lambda i: (fp_idx(i) // n_t, fp_idx(i) % n_t, 0)),
                  _const_spec(x2_s.shape)] + [_const_spec(a.shape) for a in ffn_small],
        out_specs=[pl.BlockSpec((1, FFN_TILE, d), lambda i: (fp_idx(i) // n_t, fp_idx(i) % n_t, 0)),
                   pl.BlockSpec(x_sample.shape, lambda i: (0, 0, 0))],
        out_shape=[jax.ShapeDtypeStruct(x_prompt.shape, F32), jax.ShapeDtypeStruct(x_sample.shape, F32)],
        compiler_params=pltpu.CompilerParams(dimension_semantics=("arbitrary",), vmem_limit_bytes=VMEM_LIMIT),
        name="ffn",
    )(x2_p, x2_s, *ffn_small)
    return y_prompt, y_sample, pool_p, gla_p, pool_s, gla_s
```

```python
import jax
import jax.numpy as jnp
from jax import lax
from jax.experimental import pallas as pl
from jax.experimental.pallas import tpu as pltpu

F32 = jnp.float32
BF16 = jnp.bfloat16

D_MODEL = 1024
N_META = 16
POOL_WIDTH = 512
POOL_WINDOWS = (2, 4, 8, 16)
POOL_GROUP_DIM = 128
POOL_BUF = 15
GLA_HEADS = 4
GLA_DV = 128
GLA_DK = 64
GLA_KW = GLA_HEADS * GLA_DK
GLA_VW = GLA_HEADS * GLA_DV
GLA_GATE_RANK = 16
GLA_TAU = 16.0
GLA_CHUNK = 64
D_FF = 2816
EPS = 1e-6

LANES = 128
SUBLANES = 8
MAIN_W = POOL_WIDTH + 2 * GLA_KW + 2 * GLA_VW
GAB_LO = MAIN_W + GLA_GATE_RANK
IN_DIM = GAB_LO + 2 * D_MODEL
TAIL_ROWS = 16
POOL_PAD = 8

PROMPT_TILE = 512
FFN_TILE = 512
FFN_CHUNK = 256
SAMPLE_BATCH_BLOCK = 16
SAMPLE_ROWS = 8
STAGE_ROWS = 256
FFN_IN_CAST_ROWS = 32
FFN_OUT_CAST_ROWS = 128
VMEM_LIMIT = 60 * 1024 * 1024


def _dot(a, b):
    return jnp.dot(a, b, preferred_element_type=F32)


def _dot_nt(a, b):
    return lax.dot_general(a, b, (((1,), (1,)), ((), ())), preferred_element_type=F32)


def _dot_tn(a, b):
    return lax.dot_general(a, b, (((0,), (0,)), ((), ())), preferred_element_type=F32)


def _rms(x, g):
    return x * lax.rsqrt(jnp.mean(x * x, axis=-1, keepdims=True) + EPS) * g


def _sigmoid(x):
    return 0.5 * jnp.tanh(0.5 * x) + 0.5


def _silu(x):
    half = 0.5 * x
    return half * jnp.tanh(half) + half


def _log_sigmoid(x):
    return jnp.minimum(x, 0.0) - jnp.log(1.0 + jnp.exp(-jnp.abs(x)))


def _split_bf16(x):
    hi = x.astype(BF16)
    lo = (x - hi.astype(F32)).astype(BF16)
    return hi, lo


class _Weights:
    def __init__(self, gmix, bgk, pscale, gnorm, wpg, wmain, wzr, wgab, wgk, wpp, wgp, wout):
        self.gmix, self.bgk, self.pscale, self.gnorm, self.wpg = gmix, bgk, pscale, gnorm, wpg
        self.wmain, self.wzr, self.wgab, self.wgk = wmain, wzr, wgab, wgk
        self.wpp, self.wgp, self.wout = wpp, wgp, wout


def _in_proj(x, w):
    h = _rms(x, w.gmix[...]).astype(BF16)
    zr = _dot(h, w.wzr[...])
    z = _dot(zr.astype(BF16), w.wgk[...]) + w.bgk[...]
    c_q, c_v = POOL_WIDTH, POOL_WIDTH + 2 * GLA_KW
    qk = _dot(h, w.wmain[:, c_q:c_v])
    q = qk[:, :GLA_KW] * (GLA_DK ** -0.5)
    k = qk[:, GLA_KW:]
    u = _dot(h, w.wmain[:, 0:c_q])
    vog = _dot(h, w.wmain[:, c_v:MAIN_W])
    v = vog[:, :GLA_VW]
    og = vog[:, GLA_VW:]
    return h, u, q, k, v, og, z


def _gate_proj(h, w, lo, hi):
    return _dot(h, w.wgab[:, lo:hi])


def _chunk_cumsum(g, chunk):
    m = g.shape[0]
    r = lax.broadcasted_iota(jnp.int32, (m, m), 0)
    c = lax.broadcasted_iota(jnp.int32, (m, m), 1)
    tri = jnp.where((r // chunk == c // chunk) & (c <= r), 1.0, 0.0).astype(BF16)
    hi, lo = _split_bf16(g)
    return _dot(tri, hi) + _dot(tri, lo)


def _head_lane_mask(width, per_head):
    lane = lax.broadcasted_iota(jnp.int32, (1, width), 1)
    return [(lane // per_head) == h for h in range(GLA_HEADS)]


def _block_diag_rows(x_bf, per_head):
    r = x_bf.shape[0]
    zero = jnp.zeros((r, per_head), x_bf.dtype)
    rows = []
    for h in range(GLA_HEADS):
        rows.append(jnp.concatenate(
            [x_bf[:, h * per_head:(h + 1) * per_head] if hh == h else zero for hh in range(GLA_HEADS)], axis=1))
    return jnp.concatenate(rows, axis=0)


def _gla_post(o, og, w):
    parts = []
    for h in range(GLA_HEADS):
        oh = o[:, h * GLA_DV:(h + 1) * GLA_DV]
        parts.append(oh * lax.rsqrt(jnp.mean(oh * oh, axis=-1, keepdims=True) + EPS) * w.gnorm[...])
    on = jnp.concatenate(parts, axis=1)
    on = on * _silu(og)
    return _dot(on.astype(BF16), w.wgp[...])


def _pool_post(pooled, w):
    pb = pooled.astype(BF16)
    mixed = jnp.concatenate(
        [_dot(pb[:, g * POOL_GROUP_DIM:(g + 1) * POOL_GROUP_DIM], w.wpg[0, g].astype(BF16))
         for g in range(len(POOL_WINDOWS))], axis=1)
    return _dot((mixed * w.pscale[...]).astype(BF16), w.wpp[...])


def _merge(x, y_a, y_b, sa, sb, w):
    merged = sa * y_a + sb * y_b
    return x + _dot(merged.astype(BF16), w.wout[...])


def _decay_columns(decay_row):
    return jnp.transpose(jnp.broadcast_to(decay_row, (LANES, decay_row.shape[1])))


def _state_update(s_heads, kd_bf, v_bf, decay_row):
    upd = _dot_tn(kd_bf, v_bf)
    dcol = _decay_columns(decay_row)
    out = []
    for h in range(GLA_HEADS):
        rows = slice(h * GLA_DK, (h + 1) * GLA_DK)
        out.append(dcol[rows] * s_heads[h] + upd[rows, h * GLA_DV:(h + 1) * GLA_DV])
    return out


def _stage_weights(wint_hbm, wpp_hbm, wgp_hbm, wout_hbm, wgk_ref, w, stage, zr_stage, sem, zr_sem):
    plan = []
    for r in range(0, MAIN_W, STAGE_ROWS):
        plan.append((wint_hbm, r, w.wmain, r, True))
    for r in range(0, 2 * D_MODEL, STAGE_ROWS):
        plan.append((wint_hbm, GAB_LO + r, w.wgab, r, True))
    for src, dst, n_rows in ((wpp_hbm, w.wpp, POOL_WIDTH), (wgp_hbm, w.wgp, GLA_VW), (wout_hbm, w.wout, D_MODEL)):
        for r in range(0, n_rows, STAGE_ROWS):
            plan.append((src, r, dst, r, False))

    def copy(j):
        src, r0 = plan[j][0], plan[j][1]
        return pltpu.make_async_copy(src.at[0, pl.ds(r0, STAGE_ROWS), :], stage.at[j % 2], sem.at[j % 2])

    def zr_copy():
        return pltpu.make_async_copy(
            wint_hbm.at[0, pl.ds(MAIN_W, GLA_GATE_RANK), :], zr_stage.at[pl.ds(0, GLA_GATE_RANK), :], zr_sem.at[0])

    copy(0).start()
    copy(1).start()
    zr_copy().start()

    w.wgk[...] = jnp.zeros(w.wgk.shape, BF16)
    w.wgk[0:GLA_GATE_RANK, :] = wgk_ref[0].astype(BF16)
    zr_stage[GLA_GATE_RANK:, :] = jnp.zeros((LANES - GLA_GATE_RANK, D_MODEL), F32)

    for j in range(len(plan)):
        copy(j).wait()
        _, _, dst, d0, transposed = plan[j]
        slab = stage[j % 2]
        if transposed:
            dst[:, d0:d0 + STAGE_ROWS] = jnp.transpose(slab).astype(BF16)
        else:
            dst[d0:d0 + STAGE_ROWS, :] = slab.astype(BF16)
        if j + 2 < len(plan):
            copy(j + 2).start()
    zr_copy().wait()
    w.wzr[...] = jnp.transpose(zr_stage[...]).astype(BF16)


def _prompt_tile(b_first, b_idx, t_idx, n_t, x_ref, meta_ref, w, x2_ref, pbuf_ref, sout_ref,
                 ext_ref, lvl_ref, s_ref, meta_tail_ref, meta_s_ref):
    tile = x_ref.shape[1]
    n_chunks = tile // GLA_CHUNK

    @pl.when(b_first)
    def _():
        _, u, _, k, v, _, z = _in_proj(meta_ref[...], w)
        meta_tail_ref[...] = u
        g = _log_sigmoid(z) * (1.0 / GLA_TAU)
        b = _chunk_cumsum(g, N_META)
        b_last = b[N_META - 1:N_META, :]
        kd = k * jnp.exp(b_last - b)
        zero_s = [jnp.zeros((GLA_DK, GLA_DV), F32)] * GLA_HEADS
        s_new = _state_update(zero_s, kd.astype(BF16), v.astype(BF16), jnp.exp(b_last))
        for hd in range(GLA_HEADS):
            meta_s_ref[hd] = s_new[hd]

    @pl.when(t_idx == 0)
    def _():
        ext_ref[0:POOL_PAD, :] = jnp.zeros((POOL_PAD, POOL_WIDTH), F32)
        lvl_ref[:, 0:POOL_PAD, :] = jnp.zeros((lvl_ref.shape[0], POOL_PAD, POOL_WIDTH), F32)
        ext_ref[POOL_PAD:POOL_PAD + TAIL_ROWS, :] = meta_tail_ref[...]
        s_ref[...] = meta_s_ref[...]

    x = x_ref[0]
    xn, u, q, k, v, og, z = _in_proj(x, w)

    base = POOL_PAD + TAIL_ROWS
    span = TAIL_ROWS + tile
    ext_ref[base:base + tile, :] = u
    cur = ext_ref[POOL_PAD:POOL_PAD + span, :]
    pooled = []
    for gi, win in enumerate(POOL_WINDOWS):
        shift = win // 2
        lo = gi * POOL_GROUP_DIM
        prev_ref = ext_ref if gi == 0 else lvl_ref.at[gi - 1]
        cur = cur[:, (POOL_GROUP_DIM if gi else 0):] + prev_ref[POOL_PAD - shift:POOL_PAD - shift + span, lo:]
        pooled.append(cur[TAIL_ROWS:, 0:POOL_GROUP_DIM] * (1.0 / win) - u[:, lo:lo + POOL_GROUP_DIM])
        if gi + 1 < len(POOL_WINDOWS):
            lvl_ref[gi, POOL_PAD:POOL_PAD + span, lo:] = cur
    y_a = _pool_post(jnp.concatenate(pooled, axis=1), w)
    ext_ref[POOL_PAD:base, :] = ext_ref[POOL_PAD + tile:base + tile, :]

    g = _log_sigmoid(z) * (1.0 / GLA_TAU)
    b = _chunk_cumsum(g, GLA_CHUNK)
    b3 = b.reshape(n_chunks, GLA_CHUNK, GLA_KW)
    b_last = jnp.broadcast_to(b3[:, GLA_CHUNK - 1:GLA_CHUNK, :], b3.shape).reshape(tile, GLA_KW)
    qe = (q * jnp.exp(b)).astype(BF16)
    ke = k * jnp.exp(-b)
    kd = (k * jnp.exp(b_last - b)).astype(BF16)
    decay = jnp.exp(b_last)
    v_bf = v.astype(BF16)

    k_masks = _head_lane_mask(GLA_KW, GLA_DK)
    row_i = lax.broadcasted_iota(jnp.int32, (GLA_CHUNK, GLA_KW), 0)
    col_j = lax.broadcasted_iota(jnp.int32, (GLA_CHUNK, GLA_KW), 1) % GLA_CHUNK
    causal = col_j <= row_i

    gate_cols = 2 * D_MODEL // n_chunks
    gate_parts = []
    s_heads = [s_ref[hd] for hd in range(GLA_HEADS)]
    o_chunks = []
    for c in range(n_chunks):
        gate_parts.append(_sigmoid(_gate_proj(xn, w, c * gate_cols, (c + 1) * gate_cols)))
        rows = slice(c * GLA_CHUNK, (c + 1) * GLA_CHUNK)
        ke_c = ke[rows]
        ke_bd = jnp.concatenate([jnp.where(k_masks[hd], ke_c, 0.0) for hd in range(GLA_HEADS)], axis=0).astype(BF16)
        att = jnp.where(causal, _dot_nt(qe[rows], ke_bd), 0.0)
        s_rows = []
        for hd in range(GLA_HEADS):
            zero = jnp.zeros((GLA_DK, GLA_DV), BF16)
            s_rows.append(jnp.concatenate(
                [s_heads[hd].astype(BF16) if hh == hd else zero for hh in range(GLA_HEADS)], axis=1))
        rhs = jnp.concatenate(s_rows + [_block_diag_rows(v_bf[rows], GLA_DV)], axis=0)
        lhs = jnp.concatenate([qe[rows], att.astype(BF16)], axis=1)
        o_chunks.append(_dot(lhs, rhs))
        s_heads = _state_update(s_heads, kd[rows], v_bf[rows], decay[c * GLA_CHUNK:c * GLA_CHUNK + 1, :])
    for hd in range(GLA_HEADS):
        s_ref[hd] = s_heads[hd]

    y_b = _gla_post(jnp.concatenate(o_chunks, axis=0), og, w)
    sg = jnp.concatenate(gate_parts, axis=1)
    x2_ref[0] = _merge(x, y_a, y_b, sg[:, :D_MODEL], sg[:, D_MODEL:], w)

    @pl.when(t_idx == n_t - 1)
    def _():
        sout_ref[0, 0] = s_ref[...]

    for bb in range(pbuf_ref.shape[2]):
        @pl.when((t_idx == n_t - 1) & (b_idx == bb))
        def _():
            for r in range(POOL_BUF):
                row = base - POOL_BUF + r
                pbuf_ref[0, r, bb:bb + 1, :] = ext_ref[row:row + 1, :]


def _sample_block(x_ref, pool_ref, sin_ref, w, x2_ref, pbuf_ref, sout_ref,
                  xs_ref, us_ref, pooled_ref, qm_ref, kdx_ref, rhs_ref, oi_ref):
    nb, seq, _ = x_ref.shape
    rows_pb = SAMPLE_ROWS
    m = nb * rows_pb

    xs_ref[...] = jnp.zeros(xs_ref.shape, F32)
    xs_ref[:, 0:seq, :] = x_ref[...]
    x = xs_ref[...].reshape(m, D_MODEL)
    xn, u, q, k, v, og, z = _in_proj(x, w)

    pooled_ref[...] = jnp.zeros(pooled_ref.shape, F32)
    for gi, win in enumerate(POOL_WINDOWS):
        cols = slice(gi * POOL_GROUP_DIM, (gi + 1) * POOL_GROUP_DIM)
        us_ref[gi] = u[:, cols]
        tok = [us_ref[gi, pl.ds(t, nb, stride=rows_pb), :] for t in range(seq)]
        hist = [pool_ref[0, r, :, cols] for r in range(POOL_BUF)] + tok
        for t in range(seq):
            acc = tok[t]
            for n in range(1, win):
                acc = acc + hist[POOL_BUF + t - n]
            pooled_ref[gi, pl.ds(t, nb, stride=rows_pb), :] = acc * (1.0 / win) - tok[t]
        for r in range(POOL_BUF):
            pbuf_ref[0, r, :, cols] = hist[seq + r]
    y_a = _pool_post(jnp.concatenate([pooled_ref[gi] for gi in range(len(POOL_WINDOWS))], axis=1), w)

    r8 = lax.broadcasted_iota(jnp.int32, (m, 1), 0) % rows_pb
    g = jnp.where(r8 < seq, _log_sigmoid(z) * (1.0 / GLA_TAU), 0.0)
    b = _chunk_cumsum(g, rows_pb)
    b3 = b.reshape(nb, rows_pb, GLA_KW)
    b_last = jnp.broadcast_to(b3[:, seq - 1:seq, :], b3.shape).reshape(m, GLA_KW)
    qe = q * jnp.exp(b)
    ke = k * jnp.exp(-b)
    kd = k * jnp.exp(b_last - b)
    decay = jnp.exp(b_last)
    v_bf = v.astype(BF16)

    k_masks = _head_lane_mask(GLA_KW, GLA_DK)
    ke_bd = jnp.concatenate([jnp.where(k_masks[hd], ke, 0.0) for hd in range(GLA_HEADS)], axis=0).astype(BF16)
    row_i = lax.broadcasted_iota(jnp.int32, (m, GLA_HEADS * m), 0)
    col_j = lax.broadcasted_iota(jnp.int32, (m, GLA_HEADS * m), 1) % m
    keep = (row_i // rows_pb == col_j // rows_pb) & (col_j <= row_i)
    att = jnp.where(keep, _dot_nt(qe.astype(BF16), ke_bd), 0.0).astype(BF16)
    o_intra = _dot(att, _block_diag_rows(v_bf, GLA_DV))

    qe3 = qe.reshape(nb, rows_pb, GLA_KW)
    qm_ref[...] = jnp.concatenate([jnp.where(k_masks[hd], qe3, 0.0) for hd in range(GLA_HEADS)], axis=1).astype(BF16)
    d_hi = decay.astype(BF16).astype(F32)
    d_lo = decay - d_hi
    kdx = jnp.where(r8 == seq, d_hi, jnp.where(r8 == seq + 1, d_lo, kd))
    kdx_ref[...] = kdx.reshape(nb, rows_pb, GLA_KW)
    ones_rows = jnp.where((r8 == seq) | (r8 == seq + 1), 1.0, 0.0) + jnp.zeros((m, GLA_DV), F32)
    rhs = jnp.concatenate(
        [piece for hd in range(GLA_HEADS) for piece in (v[:, hd * GLA_DV:(hd + 1) * GLA_DV], ones_rows)], axis=1)
    rhs_ref[...] = rhs.reshape(nb, rows_pb, 2 * GLA_VW)

    def per_batch(i, carry):
        s_all = sin_ref[0, i]
        s_flat = s_all.reshape(GLA_KW, GLA_DV).astype(BF16)
        oi_ref[i] = _dot(qm_ref[i], s_flat)
        kdt = jnp.transpose(kdx_ref[i]).astype(BF16)
        rhs_i = rhs_ref[i].astype(BF16)
        for hd in range(GLA_HEADS):
            r = _dot(kdt[hd * GLA_DK:(hd + 1) * GLA_DK], rhs_i[:, hd * 2 * GLA_DV:(hd + 1) * 2 * GLA_DV])
            sout_ref[0, i, hd] = r[:, GLA_DV:] * s_all[hd] + r[:, :GLA_DV]
        return carry

    lax.fori_loop(0, nb, per_batch, 0, unroll=4)

    oi = oi_ref[...]
    o_inter = jnp.concatenate([oi[:, hd * rows_pb:(hd + 1) * rows_pb, :] for hd in range(GLA_HEADS)], axis=2)
    o = o_intra + o_inter.reshape(m, GLA_VW)
    y_b = _gla_post(o, og, w)
    sg = _sigmoid(_gate_proj(xn, w, 0, 2 * D_MODEL))
    x2 = _merge(x, y_a, y_b, sg[:, :D_MODEL], sg[:, D_MODEL:], w).reshape(nb, rows_pb, D_MODEL)
    for bi in range(nb):
        x2_ref[bi * seq:(bi + 1) * seq, :] = x2[bi, 0:seq, :]


def _mixer_kernel(n_s, n_t,
                  xs_in, pool_in, s_in, xp_in, meta_ref, gmix_ref, bgk_ref, pscale_ref, gnorm_ref, wgk_ref, wpg_ref,
                  wint_hbm, wpp_hbm, wgp_hbm, wout_hbm, wfi_in, wfo_in,
                  x2s_out, pools_out, ss_out, x2p_out, poolp_out, sp_out, wfi_out, wfo_out,
                  wmain_s, wzr_s, wgab_s, wgk_s, wpp_s, wgp_s, wout_s, stage, zr_stage, sem, zr_sem,
                  ext_ref, lvl_ref, s_ref, meta_tail_ref, meta_s_ref,
                  xs_ref, us_ref, pooled_ref, qm_ref, kdx_ref, rhs_ref, oi_ref):
    i = pl.program_id(0)
    w = _Weights(gmix_ref, bgk_ref, pscale_ref, gnorm_ref, wpg_ref,
                 wmain_s, wzr_s, wgab_s, wgk_s, wpp_s, wgp_s, wout_s)

    @pl.when(i == 0)
    def _():
        _stage_weights(wint_hbm, wpp_hbm, wgp_hbm, wout_hbm, wgk_ref, w, stage, zr_stage, sem, zr_sem)

    wfi_out[...] = wfi_in[0].astype(BF16)
    wfo_out[...] = wfo_in[0].astype(BF16)

    @pl.when(i < n_s)
    def _():
        _sample_block(xs_in, pool_in, s_in, w, x2s_out, pools_out, ss_out,
                      xs_ref, us_ref, pooled_ref, qm_ref, kdx_ref, rhs_ref, oi_ref)

    @pl.when(i >= n_s)
    def _():
        t_idx = (i - n_s) % n_t
        _prompt_tile(i == n_s, (i - n_s) // n_t, t_idx, n_t, xp_in, meta_ref, w, x2p_out, poolp_out, sp_out,
                     ext_ref, lvl_ref, s_ref, meta_tail_ref, meta_s_ref)


def _ffn_kernel(n_p, seq, xp_ref, xs_ref, gffn_ref, wi_ref, wo_ref, gfin_ref, yp_ref, ys_ref):
    i = pl.program_id(0)
    x = jnp.where(i < n_p, xp_ref[0], xs_ref[...])
    h = _rms(x, gffn_ref[...]).astype(BF16)
    acc = x
    for c in range(D_FF // FFN_CHUNK):
        lo = c * FFN_CHUNK
        gate = _dot(h, wi_ref[:, lo:lo + FFN_CHUNK])
        up = _dot(h, wi_ref[:, D_FF + lo:D_FF + lo + FFN_CHUNK])
        act = (_silu(gate) * up).astype(BF16)
        acc = acc + _dot(act, wo_ref[lo:lo + FFN_CHUNK, :])
    y = _rms(acc, gfin_ref[...])

    @pl.when(i < n_p)
    def _():
        yp_ref[0] = y

    @pl.when(i == n_p)
    def _():
        for bi in range(ys_ref.shape[0]):
            ys_ref[bi] = y[bi * seq:(bi + 1) * seq, :]


def _const_spec(shape):
    zeros = (0,) * len(shape)
    return pl.BlockSpec(shape, lambda *_: zeros, pipeline_mode=pl.Buffered(1))


def kernel(x_prompt, x_sample, state_pool, state_gla, meta_tokens, g_mix, w_in, w_gk_up, b_gk, w_pool_group,
           pool_scale, w_pool_proj, g_gla_norm, w_gla_proj, w_out, g_ffn, w_ffn_in, w_ffn_out, g_final):
    depth = w_in.shape[0]
    assert depth == 1, "single-layer trunk only"
    bp, tp, d = x_prompt.shape
    bs, ts, _ = x_sample.shape
    nbb = SAMPLE_BATCH_BLOCK
    assert d == D_MODEL and w_in.shape == (1, D_MODEL, IN_DIM) and meta_tokens.shape == (N_META, D_MODEL)
    assert tp % PROMPT_TILE == 0 and bs % nbb == 0 and ts + 2 <= SAMPLE_ROWS and bs * ts == FFN_TILE
    n_t = tp // PROMPT_TILE
    n_p = bp * n_t
    n_s = bs // nbb
    n_fi = D_MODEL // FFN_IN_CAST_ROWS
    n_fo = D_FF // FFN_OUT_CAST_ROWS
    assert n_fi <= n_p and n_fo <= n_p

    def s_idx(i):
        return jnp.minimum(i, n_s - 1)

    def p_idx(i):
        return jnp.maximum(i - n_s, 0)

    w_in_t = jnp.transpose(w_in, (0, 2, 1))
    pool_hist = jnp.transpose(state_pool, (0, 2, 1, 3))
    small = (meta_tokens, g_mix, b_gk, pool_scale, g_gla_norm, w_gk_up, w_pool_group)
    hbm = pl.BlockSpec(memory_space=pl.ANY)
    in_specs = (
        [pl.BlockSpec((nbb, ts, d), lambda i: (s_idx(i), 0, 0)),
         pl.BlockSpec((1, POOL_BUF, nbb, POOL_WIDTH), lambda i: (0, 0, s_idx(i), 0)),
         pl.BlockSpec((1, nbb, GLA_HEADS, GLA_DK, GLA_DV), lambda i: (0, s_idx(i), 0, 0, 0)),
         pl.BlockSpec((1, PROMPT_TILE, d), lambda i: (p_idx(i) // n_t, p_idx(i) % n_t, 0))]
        + [_const_spec(a.shape) for a in small]
        + [hbm, hbm, hbm, hbm,
           pl.BlockSpec((1, FFN_IN_CAST_ROWS, 2 * D_FF), lambda i: (0, jnp.minimum(p_idx(i), n_fi - 1), 0)),
           pl.BlockSpec((1, FFN_OUT_CAST_ROWS, d), lambda i: (0, jnp.minimum(p_idx(i), n_fo - 1), 0))])
    out_specs = [
        pl.BlockSpec((nbb * ts, d), lambda i: (s_idx(i), 0)),
        pl.BlockSpec((1, POOL_BUF, nbb, POOL_WIDTH), lambda i: (0, 0, s_idx(i), 0)),
        pl.BlockSpec((1, nbb, GLA_HEADS, GLA_DK, GLA_DV), lambda i: (0, s_idx(i), 0, 0, 0)),
        pl.BlockSpec((1, PROMPT_TILE, d), lambda i: (p_idx(i) // n_t, p_idx(i) % n_t, 0)),
        pl.BlockSpec((1, POOL_BUF, bp, POOL_WIDTH), lambda i: (0, 0, 0, 0)),
        pl.BlockSpec((1, 1, GLA_HEADS, GLA_DK, GLA_DV), lambda i: (0, p_idx(i) // n_t, 0, 0, 0)),
        pl.BlockSpec((FFN_IN_CAST_ROWS, 2 * D_FF), lambda i: (jnp.minimum(p_idx(i), n_fi - 1), 0)),
        pl.BlockSpec((FFN_OUT_CAST_ROWS, d), lambda i: (jnp.minimum(p_idx(i), n_fo - 1), 0))]
    out_shape = [
        jax.ShapeDtypeStruct((bs * ts, d), F32),
        jax.ShapeDtypeStruct((1, POOL_BUF, bs, POOL_WIDTH), F32),
        jax.ShapeDtypeStruct(state_gla.shape, F32),
        jax.ShapeDtypeStruct(x_prompt.shape, F32),
        jax.ShapeDtypeStruct((1, POOL_BUF, bp, POOL_WIDTH), F32),
        jax.ShapeDtypeStruct((1, bp, GLA_HEADS, GLA_DK, GLA_DV), F32),
        jax.ShapeDtypeStruct((D_MODEL, 2 * D_FF), BF16),
        jax.ShapeDtypeStruct((D_FF, D_MODEL), BF16)]
    scratch_shapes = [
        pltpu.VMEM((D_MODEL, MAIN_W), BF16), pltpu.VMEM((D_MODEL, LANES), BF16),
        pltpu.VMEM((D_MODEL, 2 * D_MODEL), BF16), pltpu.VMEM((LANES, GLA_KW), BF16),
        pltpu.VMEM((POOL_WIDTH, D_MODEL), BF16), pltpu.VMEM((GLA_VW, D_MODEL), BF16),
        pltpu.VMEM((D_MODEL, D_MODEL), BF16),
        pltpu.VMEM((2, STAGE_ROWS, D_MODEL), F32), pltpu.VMEM((LANES, D_MODEL), F32),
        pltpu.SemaphoreType.DMA((2,)), pltpu.SemaphoreType.DMA((1,)),
        pltpu.VMEM((POOL_PAD + TAIL_ROWS + PROMPT_TILE, POOL_WIDTH), F32),
        pltpu.VMEM((len(POOL_WINDOWS) - 1, POOL_PAD + TAIL_ROWS + PROMPT_TILE, POOL_WIDTH), F32),
        pltpu.VMEM((GLA_HEADS, GLA_DK, GLA_DV), F32),
        pltpu.VMEM((N_META, POOL_WIDTH), F32),
        pltpu.VMEM((GLA_HEADS, GLA_DK, GLA_DV), F32),
        pltpu.VMEM((nbb, SAMPLE_ROWS, D_MODEL), F32),
        pltpu.VMEM((len(POOL_WINDOWS), nbb * SAMPLE_ROWS, POOL_GROUP_DIM), F32),
        pltpu.VMEM((len(POOL_WINDOWS), nbb * SAMPLE_ROWS, POOL_GROUP_DIM), F32),
        pltpu.VMEM((nbb, GLA_HEADS * SAMPLE_ROWS, GLA_KW), BF16),
        pltpu.VMEM((nbb, SAMPLE_ROWS, GLA_KW), F32),
        pltpu.VMEM((nbb, SAMPLE_ROWS, 2 * GLA_VW), F32),
        pltpu.VMEM((nbb, GLA_HEADS * SAMPLE_ROWS, GLA_DV), F32)]

    def mixer(*refs):
        _mixer_kernel(n_s, n_t, *refs)

    x2_s, pool_s, gla_s, x2_p, pool_p, gla_p, wfi_bf, wfo_bf = pl.pallas_call(
        mixer,
        grid=(n_s + n_p,),
        in_specs=in_specs,
        out_specs=out_specs,
        out_shape=out_shape,
        scratch_shapes=scratch_shapes,
        compiler_params=pltpu.CompilerParams(dimension_semantics=("arbitrary",), vmem_limit_bytes=VMEM_LIMIT),
        name="mixer",
    )(x_sample, pool_hist, state_gla, x_prompt, *small, w_in_t, w_pool_proj, w_gla_proj, w_out, w_ffn_in, w_ffn_out)

    def ffn(*refs):
        _ffn_kernel(n_p, ts, *refs)

    def fp_idx(i):
        return jnp.minimum(i, n_p - 1)

    ffn_small = (g_ffn, wfi_bf, wfo_bf, g_final.reshape(1, D_MODEL))
    y_prompt, y_sample = pl.pallas_call(
        ffn,
        grid=(n_p + 1,),
        in_specs=[pl.BlockSpec((1, FFN_TILE, d), lambda i: (fp_idx(i) // n_t, fp_idx(i) % n_t, 0)),
                  _const_spec(x2_s.shape)] + [_const_spec(a.shape) for a in ffn_small],
        out_specs=[pl.BlockSpec((1, FFN_TILE, d), lambda i: (fp_idx(i) // n_t, fp_idx(i) % n_t, 0)),
                   pl.BlockSpec(x_sample.shape, lambda i: (0, 0, 0))],
        out_shape=[jax.ShapeDtypeStruct(x_prompt.shape, F32), jax.ShapeDtypeStruct(x_sample.shape, F32)],
        compiler_params=pltpu.CompilerParams(dimension_semantics=("arbitrary",), vmem_limit_bytes=VMEM_LIMIT),
        name="ffn",
    )(x2_p, x2_s, *ffn_small)
    pool_p = jnp.transpose(pool_p, (0, 2, 1, 3))
    pool_s = jnp.transpose(pool_s, (0, 2, 1, 3))
    return y_prompt, y_sample, pool_p, gla_p, pool_s, gla_s
```

```python
import jax
import jax.numpy as jnp
from jax import lax
from jax.experimental import pallas as pl
from jax.experimental.pallas import tpu as pltpu

F32 = jnp.float32
BF16 = jnp.bfloat16

D_MODEL = 1024
N_META = 16
POOL_WIDTH = 512
POOL_WINDOWS = (2, 4, 8, 16)
POOL_GROUP_DIM = 128
POOL_BUF = 15
GLA_HEADS = 4
GLA_DV = 128
GLA_DK = 64
GLA_KW = GLA_HEADS * GLA_DK
GLA_VW = GLA_HEADS * GLA_DV
GLA_GATE_RANK = 16
GLA_TAU = 16.0
GLA_CHUNK = 64
D_FF = 2816
EPS = 1e-6

LANES = 128
SUBLANES = 8
MAIN_W = POOL_WIDTH + 2 * GLA_KW + 2 * GLA_VW
GAB_LO = MAIN_W + GLA_GATE_RANK
IN_DIM = GAB_LO + 2 * D_MODEL
TAIL_ROWS = 16
POOL_PAD = 8

PROMPT_TILE = 512
FFN_TILE = 512
FFN_CHUNK = 256
SAMPLE_BATCH_BLOCK = 16
SAMPLE_ROWS = 8
STAGE_ROWS = 256
FFN_IN_CAST_ROWS = 32
FFN_OUT_CAST_ROWS = 128
VMEM_LIMIT = 60 * 1024 * 1024


def _dot(a, b):
    return jnp.dot(a, b, preferred_element_type=F32)


def _dot_nt(a, b):
    return lax.dot_general(a, b, (((1,), (1,)), ((), ())), preferred_element_type=F32)


def _dot_tn(a, b):
    return lax.dot_general(a, b, (((0,), (0,)), ((), ())), preferred_element_type=F32)


def _rms(x, g):
    return x * lax.rsqrt(jnp.mean(x * x, axis=-1, keepdims=True) + EPS) * g


def _sigmoid(x):
    return 0.5 * jnp.tanh(0.5 * x) + 0.5


def _silu(x):
    half = 0.5 * x
    return half * jnp.tanh(half) + half


def _log_sigmoid(x):
    return jnp.minimum(x, 0.0) - jnp.log(1.0 + jnp.exp(-jnp.abs(x)))


def _split_bf16(x):
    hi = x.astype(BF16)
    lo = (x - hi.astype(F32)).astype(BF16)
    return hi, lo


class _Weights:
    def __init__(self, gmix, bgk, pscale, gnorm, wpg, wmain, wzr, wgab, wgk, wpp, wgp, wout):
        self.gmix, self.bgk, self.pscale, self.gnorm, self.wpg = gmix, bgk, pscale, gnorm, wpg
        self.wmain, self.wzr, self.wgab, self.wgk = wmain, wzr, wgab, wgk
        self.wpp, self.wgp, self.wout = wpp, wgp, wout


def _in_proj(x, w):
    h = _rms(x, w.gmix[...]).astype(BF16)
    zr = _dot(h, w.wzr[...])
    z = _dot(zr.astype(BF16), w.wgk[...]) + w.bgk[...]
    c_q, c_v = POOL_WIDTH, POOL_WIDTH + 2 * GLA_KW
    qk = _dot(h, w.wmain[:, c_q:c_v])
    q = qk[:, :GLA_KW] * (GLA_DK ** -0.5)
    k = qk[:, GLA_KW:]
    u = _dot(h, w.wmain[:, 0:c_q])
    vog = _dot(h, w.wmain[:, c_v:MAIN_W])
    v = vog[:, :GLA_VW]
    og = vog[:, GLA_VW:]
    return h, u, q, k, v, og, z


def _gate_proj(h, w, lo, hi):
    return _dot(h, w.wgab[:, lo:hi])


def _chunk_cumsum(g, chunk):
    m = g.shape[0]
    r = lax.broadcasted_iota(jnp.int32, (m, m), 0)
    c = lax.broadcasted_iota(jnp.int32, (m, m), 1)
    tri = jnp.where((r // chunk == c // chunk) & (c <= r), 1.0, 0.0).astype(BF16)
    hi, lo = _split_bf16(g)
    return _dot(tri, hi) + _dot(tri, lo)


def _head_lane_mask(width, per_head):
    lane = lax.broadcasted_iota(jnp.int32, (1, width), 1)
    return [(lane // per_head) == h for h in range(GLA_HEADS)]


def _block_diag_rows(x_bf, per_head):
    r = x_bf.shape[0]
    zero = jnp.zeros((r, per_head), x_bf.dtype)
    rows = []
    for h in range(GLA_HEADS):
        rows.append(jnp.concatenate(
            [x_bf[:, h * per_head:(h + 1) * per_head] if hh == h else zero for hh in range(GLA_HEADS)], axis=1))
    return jnp.concatenate(rows, axis=0)


def _gla_post(o, og, w):
    parts = []
    for h in range(GLA_HEADS):
        oh = o[:, h * GLA_DV:(h + 1) * GLA_DV]
        parts.append(oh * lax.rsqrt(jnp.mean(oh * oh, axis=-1, keepdims=True) + EPS) * w.gnorm[...])
    on = jnp.concatenate(parts, axis=1)
    on = on * _silu(og)
    return _dot(on.astype(BF16), w.wgp[...])


def _pool_post(pooled, w):
    pb = pooled.astype(BF16)
    mixed = jnp.concatenate(
        [_dot(pb[:, g * POOL_GROUP_DIM:(g + 1) * POOL_GROUP_DIM], w.wpg[0, g].astype(BF16))
         for g in range(len(POOL_WINDOWS))], axis=1)
    return _dot((mixed * w.pscale[...]).astype(BF16), w.wpp[...])


def _merge(x, y_a, y_b, sa, sb, w):
    merged = sa * y_a + sb * y_b
    return x + _dot(merged.astype(BF16), w.wout[...])


def _decay_columns(decay_row):
    return jnp.transpose(jnp.broadcast_to(decay_row, (LANES, decay_row.shape[1])))


def _state_update(s_heads, kd_bf, v_bf, decay_row):
    upd = _dot_tn(kd_bf, v_bf)
    dcol = _decay_columns(decay_row)
    out = []
    for h in range(GLA_HEADS):
        rows = slice(h * GLA_DK, (h + 1) * GLA_DK)
        out.append(dcol[rows] * s_heads[h] + upd[rows, h * GLA_DV:(h + 1) * GLA_DV])
    return out


def _stage_weights(wint_hbm, wpp_hbm, wgp_hbm, wout_hbm, wgk_ref, w, stage, zr_stage, sem, zr_sem):
    plan = []
    for r in range(0, MAIN_W, STAGE_ROWS):
        plan.append((wint_hbm, r, w.wmain, r, True))
    for r in range(0, 2 * D_MODEL, STAGE_ROWS):
        plan.append((wint_hbm, GAB_LO + r, w.wgab, r, True))
    for src, dst, n_rows in ((wpp_hbm, w.wpp, POOL_WIDTH), (wgp_hbm, w.wgp, GLA_VW), (wout_hbm, w.wout, D_MODEL)):
        for r in range(0, n_rows, STAGE_ROWS):
            plan.append((src, r, dst, r, False))

    def copy(j):
        src, r0 = plan[j][0], plan[j][1]
        return pltpu.make_async_copy(src.at[0, pl.ds(r0, STAGE_ROWS), :], stage.at[j % 2], sem.at[j % 2])

    def zr_copy():
        return pltpu.make_async_copy(
            wint_hbm.at[0, pl.ds(MAIN_W, GLA_GATE_RANK), :], zr_stage.at[pl.ds(0, GLA_GATE_RANK), :], zr_sem.at[0])

    copy(0).start()
    copy(1).start()
    zr_copy().start()

    w.wgk[...] = jnp.zeros(w.wgk.shape, BF16)
    w.wgk[0:GLA_GATE_RANK, :] = wgk_ref[0].astype(BF16)
    zr_stage[GLA_GATE_RANK:, :] = jnp.zeros((LANES - GLA_GATE_RANK, D_MODEL), F32)

    for j in range(len(plan)):
        copy(j).wait()
        _, _, dst, d0, transposed = plan[j]
        slab = stage[j % 2]
        if transposed:
            dst[:, d0:d0 + STAGE_ROWS] = jnp.transpose(slab).astype(BF16)
        else:
            dst[d0:d0 + STAGE_ROWS, :] = slab.astype(BF16)
        if j + 2 < len(plan):
            copy(j + 2).start()
    zr_copy().wait()
    w.wzr[...] = jnp.transpose(zr_stage[...]).astype(BF16)


def _prompt_tile(b_first, b_idx, t_idx, n_t, x_ref, meta_ref, w, x2_ref, pbuf_ref, sout_ref,
                 ext_ref, lvl_ref, s_ref, meta_tail_ref, meta_s_ref):
    tile = x_ref.shape[1]
    n_chunks = tile // GLA_CHUNK

    @pl.when(b_first)
    def _():
        _, u, _, k, v, _, z = _in_proj(meta_ref[...], w)
        meta_tail_ref[...] = u
        g = _log_sigmoid(z) * (1.0 / GLA_TAU)
        b = _chunk_cumsum(g, N_META)
        b_last = b[N_META - 1:N_META, :]
        kd = k * jnp.exp(b_last - b)
        zero_s = [jnp.zeros((GLA_DK, GLA_DV), F32)] * GLA_HEADS
        s_new = _state_update(zero_s, kd.astype(BF16), v.astype(BF16), jnp.exp(b_last))
        for hd in range(GLA_HEADS):
            meta_s_ref[hd] = s_new[hd]

    @pl.when(t_idx == 0)
    def _():
        ext_ref[0:POOL_PAD, :] = jnp.zeros((POOL_PAD, POOL_WIDTH), F32)
        lvl_ref[:, 0:POOL_PAD, :] = jnp.zeros((lvl_ref.shape[0], POOL_PAD, POOL_WIDTH), F32)
        ext_ref[POOL_PAD:POOL_PAD + TAIL_ROWS, :] = meta_tail_ref[...]
        s_ref[...] = meta_s_ref[...]

    x = x_ref[0]
    xn, u, q, k, v, og, z = _in_proj(x, w)

    base = POOL_PAD + TAIL_ROWS
    span = TAIL_ROWS + tile
    ext_ref[base:base + tile, :] = u
    cur = ext_ref[POOL_PAD:POOL_PAD + span, :]
    pooled = []
    for gi, win in enumerate(POOL_WINDOWS):
        shift = win // 2
        lo = gi * POOL_GROUP_DIM
        prev_ref = ext_ref if gi == 0 else lvl_ref.at[gi - 1]
        cur = cur[:, (POOL_GROUP_DIM if gi else 0):] + prev_ref[POOL_PAD - shift:POOL_PAD - shift + span, lo:]
        pooled.append(cur[TAIL_ROWS:, 0:POOL_GROUP_DIM] * (1.0 / win) - u[:, lo:lo + POOL_GROUP_DIM])
        if gi + 1 < len(POOL_WINDOWS):
            lvl_ref[gi, POOL_PAD:POOL_PAD + span, lo:] = cur
    y_a = _pool_post(jnp.concatenate(pooled, axis=1), w)
    ext_ref[POOL_PAD:base, :] = ext_ref[POOL_PAD + tile:base + tile, :]

    g = _log_sigmoid(z) * (1.0 / GLA_TAU)
    b = _chunk_cumsum(g, GLA_CHUNK)
    b3 = b.reshape(n_chunks, GLA_CHUNK, GLA_KW)
    b_last = jnp.broadcast_to(b3[:, GLA_CHUNK - 1:GLA_CHUNK, :], b3.shape).reshape(tile, GLA_KW)
    qe = (q * jnp.exp(b)).astype(BF16)
    ke = k * jnp.exp(-b)
    kd = (k * jnp.exp(b_last - b)).astype(BF16)
    decay = jnp.exp(b_last)
    v_bf = v.astype(BF16)

    k_masks = _head_lane_mask(GLA_KW, GLA_DK)
    row_i = lax.broadcasted_iota(jnp.int32, (GLA_CHUNK, GLA_KW), 0)
    col_j = lax.broadcasted_iota(jnp.int32, (GLA_CHUNK, GLA_KW), 1) % GLA_CHUNK
    causal = col_j <= row_i

    gate_cols = 2 * D_MODEL // n_chunks
    gate_parts = []
    s_heads = [s_ref[hd] for hd in range(GLA_HEADS)]
    o_chunks = []
    for c in range(n_chunks):
        gate_parts.append(_sigmoid(_gate_proj(xn, w, c * gate_cols, (c + 1) * gate_cols)))
        rows = slice(c * GLA_CHUNK, (c + 1) * GLA_CHUNK)
        ke_c = ke[rows]
        ke_bd = jnp.concatenate([jnp.where(k_masks[hd], ke_c, 0.0) for hd in range(GLA_HEADS)], axis=0).astype(BF16)
        att = jnp.where(causal, _dot_nt(qe[rows], ke_bd), 0.0)
        s_rows = []
        for hd in range(GLA_HEADS):
            zero = jnp.zeros((GLA_DK, GLA_DV), BF16)
            s_rows.append(jnp.concatenate(
                [s_heads[hd].astype(BF16) if hh == hd else zero for hh in range(GLA_HEADS)], axis=1))
        rhs = jnp.concatenate(s_rows + [_block_diag_rows(v_bf[rows], GLA_DV)], axis=0)
        lhs = jnp.concatenate([qe[rows], att.astype(BF16)], axis=1)
        o_chunks.append(_dot(lhs, rhs))
        s_heads = _state_update(s_heads, kd[rows], v_bf[rows], decay[c * GLA_CHUNK:c * GLA_CHUNK + 1, :])
    for hd in range(GLA_HEADS):
        s_ref[hd] = s_heads[hd]

    y_b = _gla_post(jnp.concatenate(o_chunks, axis=0), og, w)
    sg = jnp.concatenate(gate_parts, axis=1)
    x2_ref[0] = _merge(x, y_a, y_b, sg[:, :D_MODEL], sg[:, D_MODEL:], w)

    @pl.when(t_idx == n_t - 1)
    def _():
        sout_ref[0, 0] = s_ref[...]

    for bb in range(pbuf_ref.shape[2]):
        @pl.when((t_idx == n_t - 1) & (b_idx == bb))
        def _():
            for r in range(POOL_BUF):
                row = base - POOL_BUF + r
                pbuf_ref[0, r, bb:bb + 1, :] = ext_ref[row:row + 1, :]


def _sample_block(x_ref, pool_ref, sin_ref, w, x2_ref, pbuf_ref, sout_ref,
                  xs_ref, us_ref, pooled_ref, qm_ref, kdx_ref, rhs_ref, oi_ref):
    nb, seq, _ = x_ref.shape
    rows_pb = SAMPLE_ROWS
    m = nb * rows_pb

    xs_ref[...] = jnp.zeros(xs_ref.shape, F32)
    xs_ref[:, 0:seq, :] = x_ref[...]
    x = xs_ref[...].reshape(m, D_MODEL)
    xn, u, q, k, v, og, z = _in_proj(x, w)

    pooled_ref[...] = jnp.zeros(pooled_ref.shape, F32)
    for gi, win in enumerate(POOL_WINDOWS):
        cols = slice(gi * POOL_GROUP_DIM, (gi + 1) * POOL_GROUP_DIM)
        us_ref[gi] = u[:, cols]
        tok = [us_ref[gi, pl.ds(t, nb, stride=rows_pb), :] for t in range(seq)]
        hist = [pool_ref[0, r, :, cols] for r in range(POOL_BUF)] + tok
        for t in range(seq):
            acc = tok[t]
            for n in range(1, win):
                acc = acc + hist[POOL_BUF + t - n]
            pooled_ref[gi, pl.ds(t, nb, stride=rows_pb), :] = acc * (1.0 / win) - tok[t]
        for r in range(POOL_BUF):
            pbuf_ref[0, r, :, cols] = hist[seq + r]
    y_a = _pool_post(jnp.concatenate([pooled_ref[gi] for gi in range(len(POOL_WINDOWS))], axis=1), w)

    r8 = lax.broadcasted_iota(jnp.int32, (m, 1), 0) % rows_pb
    g = jnp.where(r8 < seq, _log_sigmoid(z) * (1.0 / GLA_TAU), 0.0)
    b = _chunk_cumsum(g, rows_pb)
    b3 = b.reshape(nb, rows_pb, GLA_KW)
    b_last = jnp.broadcast_to(b3[:, seq - 1:seq, :], b3.shape).reshape(m, GLA_KW)
    qe = q * jnp.exp(b)
    ke = k * jnp.exp(-b)
    kd = k * jnp.exp(b_last - b)
    decay = jnp.exp(b_last)
    v_bf = v.astype(BF16)

    k_masks = _head_lane_mask(GLA_KW, GLA_DK)
    ke_bd = jnp.concatenate([jnp.where(k_masks[hd], ke, 0.0) for hd in range(GLA_HEADS)], axis=0).astype(BF16)
    row_i = lax.broadcasted_iota(jnp.int32, (m, GLA_HEADS * m), 0)
    col_j = lax.broadcasted_iota(jnp.int32, (m, GLA_HEADS * m), 1) % m
    keep = (row_i // rows_pb == col_j // rows_pb) & (col_j <= row_i)
    att = jnp.where(keep, _dot_nt(qe.astype(BF16), ke_bd), 0.0).astype(BF16)
    o_intra = _dot(att, _block_diag_rows(v_bf, GLA_DV))

    qe3 = qe.reshape(nb, rows_pb, GLA_KW)
    qm_ref[...] = jnp.concatenate([jnp.where(k_masks[hd], qe3, 0.0) for hd in range(GLA_HEADS)], axis=1).astype(BF16)
    d_hi = decay.astype(BF16).astype(F32)
    d_lo = decay - d_hi
    kdx = jnp.where(r8 == seq, d_hi, jnp.where(r8 == seq + 1, d_lo, kd))
    kdx_ref[...] = kdx.reshape(nb, rows_pb, GLA_KW)
    ones_rows = jnp.where((r8 == seq) | (r8 == seq + 1), 1.0, 0.0) + jnp.zeros((m, GLA_DV), F32)
    rhs = jnp.concatenate(
        [piece for hd in range(GLA_HEADS) for piece in (v[:, hd * GLA_DV:(hd + 1) * GLA_DV], ones_rows)], axis=1)
    rhs_ref[...] = rhs.reshape(nb, rows_pb, 2 * GLA_VW)

    def per_batch(i, carry):
        s_all = sin_ref[0, i]
        s_flat = s_all.reshape(GLA_KW, GLA_DV).astype(BF16)
        oi_ref[i] = _dot(qm_ref[i], s_flat)
        kdt = jnp.transpose(kdx_ref[i]).astype(BF16)
        rhs_i = rhs_ref[i].astype(BF16)
        for hd in range(GLA_HEADS):
            r = _dot(kdt[hd * GLA_DK:(hd + 1) * GLA_DK], rhs_i[:, hd * 2 * GLA_DV:(hd + 1) * 2 * GLA_DV])
            sout_ref[0, i, hd] = r[:, GLA_DV:] * s_all[hd] + r[:, :GLA_DV]
        return carry

    lax.fori_loop(0, nb, per_batch, 0, unroll=4)

    oi = oi_ref[...]
    o_inter = jnp.concatenate([oi[:, hd * rows_pb:(hd + 1) * rows_pb, :] for hd in range(GLA_HEADS)], axis=2)
    o = o_intra + o_inter.reshape(m, GLA_VW)
    y_b = _gla_post(o, og, w)
    sg = _sigmoid(_gate_proj(xn, w, 0, 2 * D_MODEL))
    x2 = _merge(x, y_a, y_b, sg[:, :D_MODEL], sg[:, D_MODEL:], w).reshape(nb, rows_pb, D_MODEL)
    for bi in range(nb):
        x2_ref[bi * seq:(bi + 1) * seq, :] = x2[bi, 0:seq, :]


def _mixer_kernel(n_s, n_t,
                  xs_in, pool_in, s_in, xp_in, meta_ref, gmix_ref, bgk_ref, pscale_ref, gnorm_ref, wgk_ref, wpg_ref,
                  wint_hbm, wpp_hbm, wgp_hbm, wout_hbm, wfi_in, wfo_in,
                  x2s_out, pools_out, ss_out, x2p_out, poolp_out, sp_out, wfi_out, wfo_out,
                  wmain_s, wzr_s, wgab_s, wgk_s, wpp_s, wgp_s, wout_s, stage, zr_stage, sem, zr_sem,
                  ext_ref, lvl_ref, s_ref, meta_tail_ref, meta_s_ref,
                  xs_ref, us_ref, pooled_ref, qm_ref, kdx_ref, rhs_ref, oi_ref):
    i = pl.program_id(0)
    w = _Weights(gmix_ref, bgk_ref, pscale_ref, gnorm_ref, wpg_ref,
                 wmain_s, wzr_s, wgab_s, wgk_s, wpp_s, wgp_s, wout_s)

    @pl.when(i == 0)
    def _():
        _stage_weights(wint_hbm, wpp_hbm, wgp_hbm, wout_hbm, wgk_ref, w, stage, zr_stage, sem, zr_sem)

    wfi_out[...] = wfi_in[0].astype(BF16)
    wfo_out[...] = wfo_in[0].astype(BF16)

    @pl.when(i < n_s)
    def _():
        _sample_block(xs_in, pool_in, s_in, w, x2s_out, pools_out, ss_out,
                      xs_ref, us_ref, pooled_ref, qm_ref, kdx_ref, rhs_ref, oi_ref)

    @pl.when(i >= n_s)
    def _():
        t_idx = (i - n_s) % n_t
        _prompt_tile(i == n_s, (i - n_s) // n_t, t_idx, n_t, xp_in, meta_ref, w, x2p_out, poolp_out, sp_out,
                     ext_ref, lvl_ref, s_ref, meta_tail_ref, meta_s_ref)


def _ffn_kernel(n_p, seq, xc_ref, xn_ref, xs_ref, gffn_ref, wi_ref, wo_ref, gfin_ref, yp_ref, ys_ref,
                h_ref, acc_ref):
    i = pl.program_id(0)
    n_tiles = n_p + 1
    n_chunks = D_FF // FFN_CHUNK

    def tile_input(ref, idx):
        return jnp.where(idx < n_p, ref[0], xs_ref[...])

    @pl.when(i == 0)
    def _():
        h_ref[0] = _rms(tile_input(xc_ref, i), gffn_ref[...]).astype(BF16)
        acc_ref[...] = jnp.zeros(acc_ref.shape, F32)

    @pl.when(i < n_tiles)
    def _():
        x = tile_input(xc_ref, i)
        h = h_ref[i % 2]
        never = i < 0
        acc_prev = acc_ref[...]
        acc = x
        for c in range(n_chunks):
            lo = c * FFN_CHUNK
            gate = _dot(h, wi_ref[:, lo:lo + FFN_CHUNK])
            up = _dot(h, wi_ref[:, D_FF + lo:D_FF + lo + FFN_CHUNK])
            act = (_silu(gate) * up).astype(BF16)
            acc = acc + _dot(act, wo_ref[lo:lo + FFN_CHUNK, :])
            if c == 1:
                y_prev = _rms(acc_prev, gfin_ref[...])
                yp_ref[0] = y_prev
                acc = jnp.where(never, y_prev, acc)
            if c == n_chunks // 2:
                h_next = _rms(tile_input(xn_ref, i + 1), gffn_ref[...]).astype(BF16)
                h_ref[(i + 1) % 2] = h_next
                acc = jnp.where(never, h_next.astype(F32), acc)
        acc_ref[...] = acc

    @pl.when(i == n_tiles)
    def _():
        y = _rms(acc_ref[...], gfin_ref[...])
        for bi in range(ys_ref.shape[0]):
            ys_ref[bi] = y[bi * seq:(bi + 1) * seq, :]


def _const_spec(shape):
    zeros = (0,) * len(shape)
    return pl.BlockSpec(shape, lambda *_: zeros, pipeline_mode=pl.Buffered(1))


def kernel(x_prompt, x_sample, state_pool, state_gla, meta_tokens, g_mix, w_in, w_gk_up, b_gk, w_pool_group,
           pool_scale, w_pool_proj, g_gla_norm, w_gla_proj, w_out, g_ffn, w_ffn_in, w_ffn_out, g_final):
    depth = w_in.shape[0]
    assert depth == 1, "single-layer trunk only"
    bp, tp, d = x_prompt.shape
    bs, ts, _ = x_sample.shape
    nbb = SAMPLE_BATCH_BLOCK
    assert d == D_MODEL and w_in.shape == (1, D_MODEL, IN_DIM) and meta_tokens.shape == (N_META, D_MODEL)
    assert tp % PROMPT_TILE == 0 and bs % nbb == 0 and ts + 2 <= SAMPLE_ROWS and bs * ts == FFN_TILE
    n_t = tp // PROMPT_TILE
    n_p = bp * n_t
    n_s = bs // nbb
    n_fi = D_MODEL // FFN_IN_CAST_ROWS
    n_fo = D_FF // FFN_OUT_CAST_ROWS
    assert n_fi <= n_p and n_fo <= n_p

    def s_idx(i):
        return jnp.minimum(i, n_s - 1)

    def p_idx(i):
        return jnp.maximum(i - n_s, 0)

    w_in_t = jnp.transpose(w_in, (0, 2, 1))
    pool_hist = jnp.transpose(state_pool, (0, 2, 1, 3))
    small = (meta_tokens, g_mix, b_gk, pool_scale, g_gla_norm, w_gk_up, w_pool_group)
    hbm = pl.BlockSpec(memory_space=pl.ANY)
    in_specs = (
        [pl.BlockSpec((nbb, ts, d), lambda i: (s_idx(i), 0, 0)),
         pl.BlockSpec((1, POOL_BUF, nbb, POOL_WIDTH), lambda i: (0, 0, s_idx(i), 0)),
         pl.BlockSpec((1, nbb, GLA_HEADS, GLA_DK, GLA_DV), lambda i: (0, s_idx(i), 0, 0, 0)),
         pl.BlockSpec((1, PROMPT_TILE, d), lambda i: (p_idx(i) // n_t, p_idx(i) % n_t, 0))]
        + [_const_spec(a.shape) for a in small]
        + [hbm, hbm, hbm, hbm,
           pl.BlockSpec((1, FFN_IN_CAST_ROWS, 2 * D_FF), lambda i: (0, jnp.minimum(p_idx(i), n_fi - 1), 0)),
           pl.BlockSpec((1, FFN_OUT_CAST_ROWS, d), lambda i: (0, jnp.minimum(p_idx(i), n_fo - 1), 0))])
    out_specs = [
        pl.BlockSpec((nbb * ts, d), lambda i: (s_idx(i), 0)),
        pl.BlockSpec((1, POOL_BUF, nbb, POOL_WIDTH), lambda i: (0, 0, s_idx(i), 0)),
        pl.BlockSpec((1, nbb, GLA_HEADS, GLA_DK, GLA_DV), lambda i: (0, s_idx(i), 0, 0, 0)),
        pl.BlockSpec((1, PROMPT_TILE, d), lambda i: (p_idx(i) // n_t, p_idx(i) % n_t, 0)),
        pl.BlockSpec((1, POOL_BUF, bp, POOL_WIDTH), lambda i: (0, 0, 0, 0)),
        pl.BlockSpec((1, 1, GLA_HEADS, GLA_DK, GLA_DV), lambda i: (0, p_idx(i) // n_t, 0, 0, 0)),
        pl.BlockSpec((FFN_IN_CAST_ROWS, 2 * D_FF), lambda i: (jnp.minimum(p_idx(i), n_fi - 1), 0)),
        pl.BlockSpec((FFN_OUT_CAST_ROWS, d), lambda i: (jnp.minimum(p_idx(i), n_fo - 1), 0))]
    out_shape = [
        jax.ShapeDtypeStruct((bs * ts, d), F32),
        jax.ShapeDtypeStruct((1, POOL_BUF, bs, POOL_WIDTH), F32),
        jax.ShapeDtypeStruct(state_gla.shape, F32),
        jax.ShapeDtypeStruct(x_prompt.shape, F32),
        jax.ShapeDtypeStruct((1, POOL_BUF, bp, POOL_WIDTH), F32),
        jax.ShapeDtypeStruct((1, bp, GLA_HEADS, GLA_DK, GLA_DV), F32),
        jax.ShapeDtypeStruct((D_MODEL, 2 * D_FF), BF16),
        jax.ShapeDtypeStruct((D_FF, D_MODEL), BF16)]
    scratch_shapes = [
        pltpu.VMEM((D_MODEL, MAIN_W), BF16), pltpu.VMEM((D_MODEL, LANES), BF16),
        pltpu.VMEM((D_MODEL, 2 * D_MODEL), BF16), pltpu.VMEM((LANES, GLA_KW), BF16),
        pltpu.VMEM((POOL_WIDTH, D_MODEL), BF16), pltpu.VMEM((GLA_VW, D_MODEL), BF16),
        pltpu.VMEM((D_MODEL, D_MODEL), BF16),
        pltpu.VMEM((2, STAGE_ROWS, D_MODEL), F32), pltpu.VMEM((LANES, D_MODEL), F32),
        pltpu.SemaphoreType.DMA((2,)), pltpu.SemaphoreType.DMA((1,)),
        pltpu.VMEM((POOL_PAD + TAIL_ROWS + PROMPT_TILE, POOL_WIDTH), F32),
        pltpu.VMEM((len(POOL_WINDOWS) - 1, POOL_PAD + TAIL_ROWS + PROMPT_TILE, POOL_WIDTH), F32),
        pltpu.VMEM((GLA_HEADS, GLA_DK, GLA_DV), F32),
        pltpu.VMEM((N_META, POOL_WIDTH), F32),
        pltpu.VMEM((GLA_HEADS, GLA_DK, GLA_DV), F32),
        pltpu.VMEM((nbb, SAMPLE_ROWS, D_MODEL), F32),
        pltpu.VMEM((len(POOL_WINDOWS), nbb * SAMPLE_ROWS, POOL_GROUP_DIM), F32),
        pltpu.VMEM((len(POOL_WINDOWS), nbb * SAMPLE_ROWS, POOL_GROUP_DIM), F32),
        pltpu.VMEM((nbb, GLA_HEADS * SAMPLE_ROWS, GLA_KW), BF16),
        pltpu.VMEM((nbb, SAMPLE_ROWS, GLA_KW), F32),
        pltpu.VMEM((nbb, SAMPLE_ROWS, 2 * GLA_VW), F32),
        pltpu.VMEM((nbb, GLA_HEADS * SAMPLE_ROWS, GLA_DV), F32)]

    def mixer(*refs):
        _mixer_kernel(n_s, n_t, *refs)

    x2_s, pool_s, gla_s, x2_p, pool_p, gla_p, wfi_bf, wfo_bf = pl.pallas_call(
        mixer,
        grid=(n_s + n_p,),
        in_specs=in_specs,
        out_specs=out_specs,
        out_shape=out_shape,
        scratch_shapes=scratch_shapes,
        compiler_params=pltpu.CompilerParams(dimension_semantics=("arbitrary",), vmem_limit_bytes=VMEM_LIMIT),
        name="mixer",
    )(x_sample, pool_hist, state_gla, x_prompt, *small, w_in_t, w_pool_proj, w_gla_proj, w_out, w_ffn_in, w_ffn_out)

    def ffn(*refs):
        _ffn_kernel(n_p, ts, *refs)

    def tile_spec(offset):
        def index(i):
            tile_id = jnp.clip(i + offset, 0, n_p - 1)
            return (tile_id // n_t, tile_id % n_t, 0)
        return pl.BlockSpec((1, FFN_TILE, d), index)

    ffn_small = (g_ffn, wfi_bf, wfo_bf, g_final.reshape(1, D_MODEL))
    y_prompt, y_sample = pl.pallas_call(
        ffn,
        grid=(n_p + 2,),
        in_specs=[tile_spec(0), tile_spec(1), _const_spec(x2_s.shape)] + [_const_spec(a.shape) for a in ffn_small],
        out_specs=[tile_spec(-1), pl.BlockSpec(x_sample.shape, lambda i: (0, 0, 0))],
        out_shape=[jax.ShapeDtypeStruct(x_prompt.shape, F32), jax.ShapeDtypeStruct(x_sample.shape, F32)],
        scratch_shapes=[pltpu.VMEM((2, FFN_TILE, d), BF16), pltpu.VMEM((FFN_TILE, d), F32)],
        compiler_params=pltpu.CompilerParams(dimension_semantics=("arbitrary",), vmem_limit_bytes=VMEM_LIMIT),
        name="ffn",
    )(x2_p, x2_p, x2_s, *ffn_small)
    pool_p = jnp.transpose(pool_p, (0, 2, 1, 3))
    pool_s = jnp.transpose(pool_s, (0, 2, 1, 3))
    return y_prompt, y_sample, pool_p, gla_p, pool_s, gla_s
```

```python
import jax
import jax.numpy as jnp
from jax import lax
from jax.experimental import pallas as pl
from jax.experimental.pallas import tpu as pltpu

F32 = jnp.float32
BF16 = jnp.bfloat16

D_MODEL = 1024
N_META = 16
POOL_WIDTH = 512
POOL_WINDOWS = (2, 4, 8, 16)
POOL_GROUP_DIM = 128
POOL_BUF = 15
GLA_HEADS = 4
GLA_DV = 128
GLA_DK = 64
GLA_KW = GLA_HEADS * GLA_DK
GLA_VW = GLA_HEADS * GLA_DV
GLA_GATE_RANK = 16
GLA_TAU = 16.0
GLA_CHUNK = 64
D_FF = 2816
EPS = 1e-6

LANES = 128
SUBLANES = 8
MAIN_W = POOL_WIDTH + 2 * GLA_KW + 2 * GLA_VW
U_COL, VOG_COL, QK_COL, ZR_COL = 0, POOL_WIDTH, POOL_WIDTH + 2 * GLA_VW, MAIN_W
MAIN_COLS = MAIN_W + 128
GAB_LO = MAIN_W + GLA_GATE_RANK
IN_DIM = GAB_LO + 2 * D_MODEL
TAIL_ROWS = 16
POOL_PAD = 8

PROMPT_TILE = 512
FFN_TILE = 512
FFN_CHUNK = 256
SAMPLE_BATCH_BLOCK = 16
SAMPLE_ROWS = 8
STAGE_ROWS = 256
FFN_IN_CAST_ROWS = 32
FFN_OUT_CAST_ROWS = 128
VMEM_LIMIT = 60 * 1024 * 1024


def _dot(a, b):
    return jnp.dot(a, b, preferred_element_type=F32)


def _dot_nt(a, b):
    return lax.dot_general(a, b, (((1,), (1,)), ((), ())), preferred_element_type=F32)


def _dot_tn(a, b):
    return lax.dot_general(a, b, (((0,), (0,)), ((), ())), preferred_element_type=F32)


def _rms(x, g):
    return x * lax.rsqrt(jnp.mean(x * x, axis=-1, keepdims=True) + EPS) * g


def _sigmoid(x):
    return 0.5 * jnp.tanh(0.5 * x) + 0.5


def _silu(x):
    half = 0.5 * x
    return half * jnp.tanh(half) + half


def _log_sigmoid(x):
    return jnp.minimum(x, 0.0) - jnp.log(1.0 + jnp.exp(-jnp.abs(x)))


def _split_bf16(x):
    hi = x.astype(BF16)
    lo = (x - hi.astype(F32)).astype(BF16)
    return hi, lo


class _Weights:
    def __init__(self, gmix, bgk, pscale, gnorm, wpg, wmain, wgab, wgk, wpp, wgp, wout):
        self.gmix, self.bgk, self.pscale, self.gnorm, self.wpg = gmix, bgk, pscale, gnorm, wpg
        self.wmain, self.wgab, self.wgk = wmain, wgab, wgk
        self.wpp, self.wgp, self.wout = wpp, wgp, wout


def _in_proj(x, w):
    h = _rms(x, w.gmix[...]).astype(BF16)
    qkz = _dot(h, w.wmain[:, QK_COL:MAIN_COLS])
    q = qkz[:, :GLA_KW] * (GLA_DK ** -0.5)
    k = qkz[:, GLA_KW:2 * GLA_KW]
    zr = qkz[:, 2 * GLA_KW:]
    z = _dot(zr.astype(BF16), w.wgk[...]) + w.bgk[...]
    u = _dot(h, w.wmain[:, U_COL:VOG_COL])
    vog = _dot(h, w.wmain[:, VOG_COL:QK_COL])
    v = vog[:, :GLA_VW]
    og = vog[:, GLA_VW:]
    return h, u, q, k, v, og, z


def _gate_proj(h, w, lo, hi):
    return _dot(h, w.wgab[:, lo:hi])


def _chunk_cumsum_wide(g, chunk):
    n = g.shape[0] // chunk
    r = lax.broadcasted_iota(jnp.int32, (chunk, chunk), 0)
    c = lax.broadcasted_iota(jnp.int32, (chunk, chunk), 1)
    tri = jnp.where(c <= r, 1.0, 0.0).astype(BF16)
    hi, lo = _split_bf16(jnp.concatenate([g[j * chunk:(j + 1) * chunk] for j in range(n)], axis=1))
    wide = _dot(tri, hi) + _dot(tri, lo)
    width = g.shape[1]
    parts = [wide[:, j * width:(j + 1) * width] for j in range(n)]
    return jnp.concatenate(parts, axis=0), [p[chunk - 1:chunk, :] for p in parts]


def _chunk_cumsum(g, chunk):
    m = g.shape[0]
    r = lax.broadcasted_iota(jnp.int32, (m, m), 0)
    c = lax.broadcasted_iota(jnp.int32, (m, m), 1)
    tri = jnp.where((r // chunk == c // chunk) & (c <= r), 1.0, 0.0).astype(BF16)
    hi, lo = _split_bf16(g)
    return _dot(tri, hi) + _dot(tri, lo)


def _head_lane_mask(width, per_head):
    lane = lax.broadcasted_iota(jnp.int32, (1, width), 1)
    return [(lane // per_head) == h for h in range(GLA_HEADS)]


def _block_diag_rows(x_bf, per_head):
    r = x_bf.shape[0]
    zero = jnp.zeros((r, per_head), x_bf.dtype)
    rows = []
    for h in range(GLA_HEADS):
        rows.append(jnp.concatenate(
            [x_bf[:, h * per_head:(h + 1) * per_head] if hh == h else zero for hh in range(GLA_HEADS)], axis=1))
    return jnp.concatenate(rows, axis=0)


def _gla_post(o, og, w):
    parts = []
    for h in range(GLA_HEADS):
        oh = o[:, h * GLA_DV:(h + 1) * GLA_DV]
        parts.append(oh * lax.rsqrt(jnp.mean(oh * oh, axis=-1, keepdims=True) + EPS) * w.gnorm[...])
    on = jnp.concatenate(parts, axis=1)
    on = on * _silu(og)
    return _dot(on.astype(BF16), w.wgp[...])


def _pool_post(pooled, w):
    pb = pooled.astype(BF16)
    mixed = jnp.concatenate(
        [_dot(pb[:, g * POOL_GROUP_DIM:(g + 1) * POOL_GROUP_DIM], w.wpg[0, g].astype(BF16))
         for g in range(len(POOL_WINDOWS))], axis=1)
    return _dot((mixed * w.pscale[...]).astype(BF16), w.wpp[...])


def _merge(x, y_a, y_b, sa, sb, w):
    merged = sa * y_a + sb * y_b
    return x + _dot(merged.astype(BF16), w.wout[...])


def _decay_columns(decay_row):
    return jnp.transpose(jnp.broadcast_to(decay_row, (LANES, decay_row.shape[1])))


def _pair_block_diag(a, b):
    zero = jnp.zeros(a.shape, a.dtype)
    return jnp.concatenate([jnp.concatenate([a, zero], axis=1), jnp.concatenate([zero, b], axis=1)], axis=0)


def _state_update(s_heads, kd_bf, v_bf, decay_row):
    dcol = _decay_columns(decay_row)
    out = []
    for p in range(GLA_HEADS // 2):
        upd = _dot_tn(kd_bf[:, 2 * p * GLA_DK:(2 * p + 2) * GLA_DK], v_bf[:, 2 * p * GLA_DV:(2 * p + 2) * GLA_DV])
        for j in range(2):
            h = 2 * p + j
            rows = slice(h * GLA_DK, (h + 1) * GLA_DK)
            out.append(dcol[rows] * s_heads[h] + upd[j * GLA_DK:(j + 1) * GLA_DK, j * GLA_DV:(j + 1) * GLA_DV])
    return out


def _stage_weights(wint_hbm, wpp_hbm, wgp_hbm, wout_hbm, wgk_ref, w, stage, zr_stage, sem, zr_sem):
    plan = []
    for r in range(0, MAIN_W, STAGE_ROWS):
        if r < POOL_WIDTH:
            col = U_COL + r
        elif r < POOL_WIDTH + 2 * GLA_KW:
            col = QK_COL + r - POOL_WIDTH
        else:
            col = VOG_COL + r - (POOL_WIDTH + 2 * GLA_KW)
        plan.append((wint_hbm, r, w.wmain, col, True))
    for r in range(0, 2 * D_MODEL, STAGE_ROWS):
        plan.append((wint_hbm, GAB_LO + r, w.wgab, r, True))
    for src, dst, n_rows in ((wpp_hbm, w.wpp, POOL_WIDTH), (wgp_hbm, w.wgp, GLA_VW), (wout_hbm, w.wout, D_MODEL)):
        for r in range(0, n_rows, STAGE_ROWS):
            plan.append((src, r, dst, r, False))

    def copy(j):
        src, r0 = plan[j][0], plan[j][1]
        return pltpu.make_async_copy(src.at[0, pl.ds(r0, STAGE_ROWS), :], stage.at[j % 2], sem.at[j % 2])

    def zr_copy():
        return pltpu.make_async_copy(
            wint_hbm.at[0, pl.ds(MAIN_W, GLA_GATE_RANK), :], zr_stage.at[pl.ds(0, GLA_GATE_RANK), :], zr_sem.at[0])

    copy(0).start()
    copy(1).start()
    zr_copy().start()

    w.wgk[...] = jnp.zeros(w.wgk.shape, BF16)
    w.wgk[0:GLA_GATE_RANK, :] = wgk_ref[0].astype(BF16)
    zr_stage[GLA_GATE_RANK:, :] = jnp.zeros((LANES - GLA_GATE_RANK, D_MODEL), F32)

    for j in range(len(plan)):
        copy(j).wait()
        _, _, dst, d0, transposed = plan[j]
        slab = stage[j % 2]
        if transposed:
            dst[:, d0:d0 + STAGE_ROWS] = jnp.transpose(slab).astype(BF16)
        else:
            dst[d0:d0 + STAGE_ROWS, :] = slab.astype(BF16)
        if j + 2 < len(plan):
            copy(j + 2).start()
    zr_copy().wait()
    w.wmain[:, ZR_COL:MAIN_COLS] = jnp.transpose(zr_stage[...]).astype(BF16)


def _prompt_tile(b_first, b_idx, t_idx, n_t, x_ref, meta_ref, w, x2_ref, pbuf_ref, sout_ref,
                 ext_ref, lvl_ref, s_ref, meta_tail_ref, meta_s_ref):
    tile = x_ref.shape[1]
    n_chunks = tile // GLA_CHUNK

    @pl.when(b_first)
    def _():
        _, u, _, k, v, _, z = _in_proj(meta_ref[...], w)
        meta_tail_ref[...] = u
        g = _log_sigmoid(z) * (1.0 / GLA_TAU)
        b = _chunk_cumsum(g, N_META)
        b_last = b[N_META - 1:N_META, :]
        kd = k * jnp.exp(b_last - b)
        zero_s = [jnp.zeros((GLA_DK, GLA_DV), F32)] * GLA_HEADS
        s_new = _state_update(zero_s, kd.astype(BF16), v.astype(BF16), jnp.exp(b_last))
        for hd in range(GLA_HEADS):
            meta_s_ref[hd] = s_new[hd]

    @pl.when(t_idx == 0)
    def _():
        ext_ref[0:POOL_PAD, :] = jnp.zeros((POOL_PAD, POOL_WIDTH), F32)
        lvl_ref[:, 0:POOL_PAD, :] = jnp.zeros((lvl_ref.shape[0], POOL_PAD, POOL_WIDTH), F32)
        ext_ref[POOL_PAD:POOL_PAD + TAIL_ROWS, :] = meta_tail_ref[...]
        s_ref[...] = meta_s_ref[...]

    x = x_ref[0]
    xn, u, q, k, v, og, z = _in_proj(x, w)

    base = POOL_PAD + TAIL_ROWS
    span = TAIL_ROWS + tile
    ext_ref[base:base + tile, :] = u
    cur = ext_ref[POOL_PAD:POOL_PAD + span, :]
    pooled = []
    for gi, win in enumerate(POOL_WINDOWS):
        shift = win // 2
        lo = gi * POOL_GROUP_DIM
        prev_ref = ext_ref if gi == 0 else lvl_ref.at[gi - 1]
        cur = cur[:, (POOL_GROUP_DIM if gi else 0):] + prev_ref[POOL_PAD - shift:POOL_PAD - shift + span, lo:]
        pooled.append(cur[TAIL_ROWS:, 0:POOL_GROUP_DIM] * (1.0 / win) - u[:, lo:lo + POOL_GROUP_DIM])
        if gi + 1 < len(POOL_WINDOWS):
            lvl_ref[gi, POOL_PAD:POOL_PAD + span, lo:] = cur
    y_a = _pool_post(jnp.concatenate(pooled, axis=1), w)
    ext_ref[POOL_PAD:base, :] = ext_ref[POOL_PAD + tile:base + tile, :]

    g = _log_sigmoid(z) * (1.0 / GLA_TAU)
    b, b_last_rows = _chunk_cumsum_wide(g, GLA_CHUNK)
    b_last = jnp.concatenate([jnp.broadcast_to(r, (GLA_CHUNK, GLA_KW)) for r in b_last_rows], axis=0)
    qe = (q * jnp.exp(b)).astype(BF16)
    ke = k * jnp.exp(-b)
    kd = (k * jnp.exp(b_last - b)).astype(BF16)
    v_bf = v.astype(BF16)

    k_masks = _head_lane_mask(GLA_KW, GLA_DK)
    row_i = lax.broadcasted_iota(jnp.int32, (GLA_CHUNK, GLA_KW), 0)
    col_j = lax.broadcasted_iota(jnp.int32, (GLA_CHUNK, GLA_KW), 1) % GLA_CHUNK
    causal = col_j <= row_i

    gate_cols = 2 * D_MODEL // n_chunks
    gate_parts = []
    s_heads = [s_ref[hd] for hd in range(GLA_HEADS)]
    o_chunks = []
    for c in range(n_chunks):
        gate_parts.append(_sigmoid(_gate_proj(xn, w, c * gate_cols, (c + 1) * gate_cols)))
        rows = slice(c * GLA_CHUNK, (c + 1) * GLA_CHUNK)
        ke_c = ke[rows]
        ke_bd = jnp.concatenate([jnp.where(k_masks[hd], ke_c, 0.0) for hd in range(GLA_HEADS)], axis=0).astype(BF16)
        att = jnp.where(causal, _dot_nt(qe[rows], ke_bd), 0.0).astype(BF16)
        o_pairs = []
        for p in range(GLA_HEADS // 2):
            h0, h1 = 2 * p, 2 * p + 1
            lanes_k = slice(h0 * GLA_DK, (h1 + 1) * GLA_DK)
            lanes_j = slice(h0 * GLA_CHUNK, (h1 + 1) * GLA_CHUNK)
            v0 = v_bf[rows, h0 * GLA_DV:(h0 + 1) * GLA_DV]
            v1 = v_bf[rows, h1 * GLA_DV:(h1 + 1) * GLA_DV]
            rhs = jnp.concatenate([_pair_block_diag(s_heads[h0].astype(BF16), s_heads[h1].astype(BF16)),
                                   _pair_block_diag(v0, v1)], axis=0)
            o_pairs.append(_dot(jnp.concatenate([qe[rows, lanes_k], att[:, lanes_j]], axis=1), rhs))
        o_chunks.append(jnp.concatenate(o_pairs, axis=1))
        s_heads = _state_update(s_heads, kd[rows], v_bf[rows], jnp.exp(b_last_rows[c]))
    for hd in range(GLA_HEADS):
        s_ref[hd] = s_heads[hd]

    y_b = _gla_post(jnp.concatenate(o_chunks, axis=0), og, w)
    sg = jnp.concatenate(gate_parts, axis=1)
    x2_ref[0] = _merge(x, y_a, y_b, sg[:, :D_MODEL], sg[:, D_MODEL:], w)

    @pl.when(t_idx == n_t - 1)
    def _():
        sout_ref[0, 0] = s_ref[...]

    for bb in range(pbuf_ref.shape[2]):
        @pl.when((t_idx == n_t - 1) & (b_idx == bb))
        def _():
            for r in range(POOL_BUF):
                row = base - POOL_BUF + r
                pbuf_ref[0, r, bb:bb + 1, :] = ext_ref[row:row + 1, :]


def _sample_block(x_ref, pool_ref, sin_ref, w, x2_ref, pbuf_ref, sout_ref,
                  xs_ref, us_ref, pooled_ref, qm_ref, kdx_ref, rhs_ref, oi_ref):
    nb, seq, _ = x_ref.shape
    rows_pb = SAMPLE_ROWS
    m = nb * rows_pb

    xs_ref[...] = jnp.zeros(xs_ref.shape, F32)
    xs_ref[:, 0:seq, :] = x_ref[...]
    x = xs_ref[...].reshape(m, D_MODEL)
    xn, u, q, k, v, og, z = _in_proj(x, w)

    pooled_ref[...] = jnp.zeros(pooled_ref.shape, F32)
    for gi, win in enumerate(POOL_WINDOWS):
        cols = slice(gi * POOL_GROUP_DIM, (gi + 1) * POOL_GROUP_DIM)
        us_ref[gi] = u[:, cols]
        tok = [us_ref[gi, pl.ds(t, nb, stride=rows_pb), :] for t in range(seq)]
        hist = [pool_ref[0, r, :, cols] for r in range(POOL_BUF)] + tok
        for t in range(seq):
            acc = tok[t]
            for n in range(1, win):
                acc = acc + hist[POOL_BUF + t - n]
            pooled_ref[gi, pl.ds(t, nb, stride=rows_pb), :] = acc * (1.0 / win) - tok[t]
        for r in range(POOL_BUF):
            pbuf_ref[0, r, :, cols] = hist[seq + r]
    y_a = _pool_post(jnp.concatenate([pooled_ref[gi] for gi in range(len(POOL_WINDOWS))], axis=1), w)

    r8 = lax.broadcasted_iota(jnp.int32, (m, 1), 0) % rows_pb
    g = jnp.where(r8 < seq, _log_sigmoid(z) * (1.0 / GLA_TAU), 0.0)
    b = _chunk_cumsum(g, rows_pb)
    b3 = b.reshape(nb, rows_pb, GLA_KW)
    b_last = jnp.broadcast_to(b3[:, seq - 1:seq, :], b3.shape).reshape(m, GLA_KW)
    qe = q * jnp.exp(b)
    ke = k * jnp.exp(-b)
    kd = k * jnp.exp(b_last - b)
    decay = jnp.exp(b_last)
    v_bf = v.astype(BF16)

    k_masks = _head_lane_mask(GLA_KW, GLA_DK)
    ke_bd = jnp.concatenate([jnp.where(k_masks[hd], ke, 0.0) for hd in range(GLA_HEADS)], axis=0).astype(BF16)
    row_i = lax.broadcasted_iota(jnp.int32, (m, GLA_HEADS * m), 0)
    col_j = lax.broadcasted_iota(jnp.int32, (m, GLA_HEADS * m), 1) % m
    keep = (row_i // rows_pb == col_j // rows_pb) & (col_j <= row_i)
    att = jnp.where(keep, _dot_nt(qe.astype(BF16), ke_bd), 0.0).astype(BF16)
    o_intra = _dot(att, _block_diag_rows(v_bf, GLA_DV))

    qe3 = qe.reshape(nb, rows_pb, GLA_KW)
    qm_ref[...] = jnp.concatenate([jnp.where(k_masks[hd], qe3, 0.0) for hd in range(GLA_HEADS)], axis=1).astype(BF16)
    d_hi = decay.astype(BF16).astype(F32)
    d_lo = decay - d_hi
    kdx = jnp.where(r8 == seq, d_hi, jnp.where(r8 == seq + 1, d_lo, kd))
    kdx_ref[...] = kdx.reshape(nb, rows_pb, GLA_KW)
    ones_rows = jnp.where((r8 == seq) | (r8 == seq + 1), 1.0, 0.0) + jnp.zeros((m, GLA_DV), F32)
    rhs = jnp.concatenate(
        [piece for hd in range(GLA_HEADS) for piece in (v[:, hd * GLA_DV:(hd + 1) * GLA_DV], ones_rows)], axis=1)
    rhs_ref[...] = rhs.reshape(nb, rows_pb, 2 * GLA_VW)

    def per_batch(i, carry):
        s_all = sin_ref[0, i]
        s_flat = s_all.reshape(GLA_KW, GLA_DV).astype(BF16)
        oi_ref[i] = _dot(qm_ref[i], s_flat)
        kdt = jnp.transpose(kdx_ref[i]).astype(BF16)
        rhs_i = rhs_ref[i].astype(BF16)
        for hd in range(GLA_HEADS):
            r = _dot(kdt[hd * GLA_DK:(hd + 1) * GLA_DK], rhs_i[:, hd * 2 * GLA_DV:(hd + 1) * 2 * GLA_DV])
            sout_ref[0, i, hd] = r[:, GLA_DV:] * s_all[hd] + r[:, :GLA_DV]
        return carry

    lax.fori_loop(0, nb, per_batch, 0, unroll=4)

    oi = oi_ref[...]
    o_inter = jnp.concatenate([oi[:, hd * rows_pb:(hd + 1) * rows_pb, :] for hd in range(GLA_HEADS)], axis=2)
    o = o_intra + o_inter.reshape(m, GLA_VW)
    y_b = _gla_post(o, og, w)
    sg = _sigmoid(_gate_proj(xn, w, 0, 2 * D_MODEL))
    x2 = _merge(x, y_a, y_b, sg[:, :D_MODEL], sg[:, D_MODEL:], w).reshape(nb, rows_pb, D_MODEL)
    for bi in range(nb):
        x2_ref[bi * seq:(bi + 1) * seq, :] = x2[bi, 0:seq, :]


def _mixer_kernel(n_s, n_t,
                  xs_in, pool_in, s_in, xp_in, meta_ref, gmix_ref, bgk_ref, pscale_ref, gnorm_ref, wgk_ref, wpg_ref,
                  wint_hbm, wpp_hbm, wgp_hbm, wout_hbm, wfi_in, wfo_in,
                  x2s_out, pools_out, ss_out, x2p_out, poolp_out, sp_out, wfi_out, wfo_out,
                  wmain_s, wgab_s, wgk_s, wpp_s, wgp_s, wout_s, stage, zr_stage, sem, zr_sem,
                  ext_ref, lvl_ref, s_ref, meta_tail_ref, meta_s_ref,
                  xs_ref, us_ref, pooled_ref, qm_ref, kdx_ref, rhs_ref, oi_ref):
    i = pl.program_id(0)
    w = _Weights(gmix_ref, bgk_ref, pscale_ref, gnorm_ref, wpg_ref,
                 wmain_s, wgab_s, wgk_s, wpp_s, wgp_s, wout_s)

    @pl.when(i == 0)
    def _():
        _stage_weights(wint_hbm, wpp_hbm, wgp_hbm, wout_hbm, wgk_ref, w, stage, zr_stage, sem, zr_sem)

    wfi_out[...] = wfi_in[0].astype(BF16)
    wfo_out[...] = wfo_in[0].astype(BF16)

    @pl.when(i < n_s)
    def _():
        _sample_block(xs_in, pool_in, s_in, w, x2s_out, pools_out, ss_out,
                      xs_ref, us_ref, pooled_ref, qm_ref, kdx_ref, rhs_ref, oi_ref)

    @pl.when(i >= n_s)
    def _():
        t_idx = (i - n_s) % n_t
        _prompt_tile(i == n_s, (i - n_s) // n_t, t_idx, n_t, xp_in, meta_ref, w, x2p_out, poolp_out, sp_out,
                     ext_ref, lvl_ref, s_ref, meta_tail_ref, meta_s_ref)


def _ffn_kernel(n_p, seq, xp_ref, xs_ref, gffn_ref, wi_ref, wo_ref, gfin_ref, yp_ref, ys_ref):
    i = pl.program_id(0)
    x = jnp.where(i < n_p, xp_ref[0], xs_ref[...])
    h = _rms(x, gffn_ref[...]).astype(BF16)
    acc = x
    for c in range(D_FF // FFN_CHUNK):
        lo = c * FFN_CHUNK
        gate = _dot(h, wi_ref[:, lo:lo + FFN_CHUNK])
        up = _dot(h, wi_ref[:, D_FF + lo:D_FF + lo + FFN_CHUNK])
        act = (_silu(gate) * up).astype(BF16)
        acc = acc + _dot(act, wo_ref[lo:lo + FFN_CHUNK, :])
    y = _rms(acc, gfin_ref[...])

    @pl.when(i < n_p)
    def _():
        yp_ref[0] = y

    @pl.when(i == n_p)
    def _():
        for bi in range(ys_ref.shape[0]):
            ys_ref[bi] = y[bi * seq:(bi + 1) * seq, :]


def _const_spec(shape):
    zeros = (0,) * len(shape)
    return pl.BlockSpec(shape, lambda *_: zeros, pipeline_mode=pl.Buffered(1))


def kernel(x_prompt, x_sample, state_pool, state_gla, meta_tokens, g_mix, w_in, w_gk_up, b_gk, w_pool_group,
           pool_scale, w_pool_proj, g_gla_norm, w_gla_proj, w_out, g_ffn, w_ffn_in, w_ffn_out, g_final):
    depth = w_in.shape[0]
    assert depth == 1, "single-layer trunk only"
    bp, tp, d = x_prompt.shape
    bs, ts, _ = x_sample.shape
    nbb = SAMPLE_BATCH_BLOCK
    assert d == D_MODEL and w_in.shape == (1, D_MODEL, IN_DIM) and meta_tokens.shape == (N_META, D_MODEL)
    assert tp % PROMPT_TILE == 0 and bs % nbb == 0 and ts + 2 <= SAMPLE_ROWS and bs * ts == FFN_TILE
    n_t = tp // PROMPT_TILE
    n_p = bp * n_t
    n_s = bs // nbb
    n_fi = D_MODEL // FFN_IN_CAST_ROWS
    n_fo = D_FF // FFN_OUT_CAST_ROWS
    assert n_fi <= n_p and n_fo <= n_p

    def s_idx(i):
        return jnp.minimum(i, n_s - 1)

    def p_idx(i):
        return jnp.maximum(i - n_s, 0)

    w_in_t = jnp.transpose(w_in, (0, 2, 1))
    pool_hist = jnp.transpose(state_pool, (0, 2, 1, 3))
    small = (meta_tokens, g_mix, b_gk, pool_scale, g_gla_norm, w_gk_up, w_pool_group)
    hbm = pl.BlockSpec(memory_space=pl.ANY)
    in_specs = (
        [pl.BlockSpec((nbb, ts, d), lambda i: (s_idx(i), 0, 0)),
         pl.BlockSpec((1, POOL_BUF, nbb, POOL_WIDTH), lambda i: (0, 0, s_idx(i), 0)),
         pl.BlockSpec((1, nbb, GLA_HEADS, GLA_DK, GLA_DV), lambda i: (0, s_idx(i), 0, 0, 0)),
         pl.BlockSpec((1, PROMPT_TILE, d), lambda i: (p_idx(i) // n_t, p_idx(i) % n_t, 0))]
        + [_const_spec(a.shape) for a in small]
        + [hbm, hbm, hbm, hbm,
           pl.BlockSpec((1, FFN_IN_CAST_ROWS, 2 * D_FF), lambda i: (0, jnp.minimum(p_idx(i), n_fi - 1), 0)),
           pl.BlockSpec((1, FFN_OUT_CAST_ROWS, d), lambda i: (0, jnp.minimum(p_idx(i), n_fo - 1), 0))])
    out_specs = [
        pl.BlockSpec((nbb * ts, d), lambda i: (s_idx(i), 0)),
        pl.BlockSpec((1, POOL_BUF, nbb, POOL_WIDTH), lambda i: (0, 0, s_idx(i), 0)),
        pl.BlockSpec((1, nbb, GLA_HEADS, GLA_DK, GLA_DV), lambda i: (0, s_idx(i), 0, 0, 0)),
        pl.BlockSpec((1, PROMPT_TILE, d), lambda i: (p_idx(i) // n_t, p_idx(i) % n_t, 0)),
        pl.BlockSpec((1, POOL_BUF, bp, POOL_WIDTH), lambda i: (0, 0, 0, 0)),
        pl.BlockSpec((1, 1, GLA_HEADS, GLA_DK, GLA_DV), lambda i: (0, p_idx(i) // n_t, 0, 0, 0)),
        pl.BlockSpec((FFN_IN_CAST_ROWS, 2 * D_FF), lambda i: (jnp.minimum(p_idx(i), n_fi - 1), 0)),
        pl.BlockSpec((FFN_OUT_CAST_ROWS, d), lambda i: (jnp.minimum(p_idx(i), n_fo - 1), 0))]
    out_shape = [
        jax.ShapeDtypeStruct((bs * ts, d), F32),
        jax.ShapeDtypeStruct((1, POOL_BUF, bs, POOL_WIDTH), F32),
        jax.ShapeDtypeStruct(state_gla.shape, F32),
        jax.ShapeDtypeStruct(x_prompt.shape, F32),
        jax.ShapeDtypeStruct((1, POOL_BUF, bp, POOL_WIDTH), F32),
        jax.ShapeDtypeStruct((1, bp, GLA_HEADS, GLA_DK, GLA_DV), F32),
        jax.ShapeDtypeStruct((D_MODEL, 2 * D_FF), BF16),
        jax.ShapeDtypeStruct((D_FF, D_MODEL), BF16)]
    scratch_shapes = [
        pltpu.VMEM((D_MODEL, MAIN_COLS), BF16),
        pltpu.VMEM((D_MODEL, 2 * D_MODEL), BF16), pltpu.VMEM((LANES, GLA_KW), BF16),
        pltpu.VMEM((POOL_WIDTH, D_MODEL), BF16), pltpu.VMEM((GLA_VW, D_MODEL), BF16),
        pltpu.VMEM((D_MODEL, D_MODEL), BF16),
        pltpu.VMEM((2, STAGE_ROWS, D_MODEL), F32), pltpu.VMEM((LANES, D_MODEL), F32),
        pltpu.SemaphoreType.DMA((2,)), pltpu.SemaphoreType.DMA((1,)),
        pltpu.VMEM((POOL_PAD + TAIL_ROWS + PROMPT_TILE, POOL_WIDTH), F32),
        pltpu.VMEM((len(POOL_WINDOWS) - 1, POOL_PAD + TAIL_ROWS + PROMPT_TILE, POOL_WIDTH), F32),
        pltpu.VMEM((GLA_HEADS, GLA_DK, GLA_DV), F32),
        pltpu.VMEM((N_META, POOL_WIDTH), F32),
        pltpu.VMEM((GLA_HEADS, GLA_DK, GLA_DV), F32),
        pltpu.VMEM((nbb, SAMPLE_ROWS, D_MODEL), F32),
        pltpu.VMEM((len(POOL_WINDOWS), nbb * SAMPLE_ROWS, POOL_GROUP_DIM), F32),
        pltpu.VMEM((len(POOL_WINDOWS), nbb * SAMPLE_ROWS, POOL_GROUP_DIM), F32),
        pltpu.VMEM((nbb, GLA_HEADS * SAMPLE_ROWS, GLA_KW), BF16),
        pltpu.VMEM((nbb, SAMPLE_ROWS, GLA_KW), F32),
        pltpu.VMEM((nbb, SAMPLE_ROWS, 2 * GLA_VW), F32),
        pltpu.VMEM((nbb, GLA_HEADS * SAMPLE_ROWS, GLA_DV), F32)]

    def mixer(*refs):
        _mixer_kernel(n_s, n_t, *refs)

    x2_s, pool_s, gla_s, x2_p, pool_p, gla_p, wfi_bf, wfo_bf = pl.pallas_call(
        mixer,
        grid=(n_s + n_p,),
        in_specs=in_specs,
        out_specs=out_specs,
        out_shape=out_shape,
        scratch_shapes=scratch_shapes,
        compiler_params=pltpu.CompilerParams(dimension_semantics=("arbitrary",), vmem_limit_bytes=VMEM_LIMIT),
        name="mixer",
    )(x_sample, pool_hist, state_gla, x_prompt, *small, w_in_t, w_pool_proj, w_gla_proj, w_out, w_ffn_in, w_ffn_out)

    def ffn(*refs):
        _ffn_kernel(n_p, ts, *refs)

    def fp_idx(i):
        return jnp.minimum(i, n_p - 1)

    ffn_small = (g_ffn, wfi_bf, wfo_bf, g_final.reshape(1, D_MODEL))
    y_prompt, y_sample = pl.pallas_call(
        ffn,
        grid=(n_p + 1,),
        in_specs=[pl.BlockSpec((1, FFN_TILE, d), lambda i: (fp_idx(i) // n_t, fp_idx(i) % n_t, 0)),
                  _const_spec(x2_s.shape)] + [_const_spec(a.shape) for a in ffn_small],
        out_specs=[pl.BlockSpec((1, FFN_TILE, d), lambda i: (fp_idx(i) // n_t, fp_idx(i) % n_t, 0)),
                   pl.BlockSpec(x_sample.shape, lambda i: (0, 0, 0))],
        out_shape=[jax.ShapeDtypeStruct(x_prompt.shape, F32), jax.ShapeDtypeStruct(x_sample.shape, F32)],
        compiler_params=pltpu.CompilerParams(dimension_semantics=("arbitrary",), vmem_limit_bytes=VMEM_LIMIT),
        name="ffn",
    )(x2_p, x2_s, *ffn_small)
    pool_p = jnp.transpose(pool_p, (0, 2, 1, 3))
    pool_s = jnp.transpose(pool_s, (0, 2, 1, 3))
    return y_prompt, y_sample, pool_p, gla_p, pool_s, gla_s
```

```python
import jax
import jax.numpy as jnp
from jax import lax
from jax.experimental import pallas as pl
from jax.experimental.pallas import tpu as pltpu

F32 = jnp.float32
BF16 = jnp.bfloat16

D_MODEL = 1024
N_META = 16
POOL_WIDTH = 512
POOL_WINDOWS = (2, 4, 8, 16)
POOL_GROUP_DIM = 128
POOL_BUF = 15
GLA_HEADS = 4
GLA_DV = 128
GLA_DK = 64
GLA_KW = GLA_HEADS * GLA_DK
GLA_VW = GLA_HEADS * GLA_DV
GLA_GATE_RANK = 16
GLA_TAU = 16.0
GLA_CHUNK = 64
D_FF = 2816
EPS = 1e-6

LANES = 128
SUBLANES = 8
MAIN_W = POOL_WIDTH + 2 * GLA_KW + 2 * GLA_VW
U_COL, VOG_COL, QK_COL, ZR_COL = 0, POOL_WIDTH, POOL_WIDTH + 2 * GLA_VW, MAIN_W
MAIN_COLS = MAIN_W + 128
GAB_LO = MAIN_W + GLA_GATE_RANK
IN_DIM = GAB_LO + 2 * D_MODEL
TAIL_ROWS = 16
POOL_PAD = 8

PROMPT_TILE = 512
FFN_TILE = 512
FFN_CHUNK = 256
SAMPLE_BATCH_BLOCK = 16
SAMPLE_ROWS = 8
STAGE_ROWS = 256
FFN_IN_CAST_ROWS = 32
FFN_OUT_CAST_ROWS = 128
VMEM_LIMIT = 60 * 1024 * 1024


def _dot(a, b):
    return jnp.dot(a, b, preferred_element_type=F32)


def _dot_nt(a, b):
    return lax.dot_general(a, b, (((1,), (1,)), ((), ())), preferred_element_type=F32)


def _dot_tn(a, b):
    return lax.dot_general(a, b, (((0,), (0,)), ((), ())), preferred_element_type=F32)


def _rms(x, g):
    return x * lax.rsqrt(jnp.mean(x * x, axis=-1, keepdims=True) + EPS) * g


def _rms_split(x, g):
    r = lax.rsqrt(jnp.mean(x * x, axis=-1, keepdims=True) + EPS)
    return (x * g).astype(BF16), r


def _sigmoid(x):
    return 0.5 * jnp.tanh(0.5 * x) + 0.5


def _silu(x):
    half = 0.5 * x
    return half * jnp.tanh(half) + half


def _log_sigmoid(x):
    return jnp.minimum(x, 0.0) - jnp.log(1.0 + jnp.exp(-jnp.abs(x)))


def _split_bf16(x):
    hi = x.astype(BF16)
    lo = (x - hi.astype(F32)).astype(BF16)
    return hi, lo


class _Weights:
    def __init__(self, gmix, bgk, pscale, gnorm, wpg, wmain, wgab, wgk, wpp, wgp, wout):
        self.gmix, self.bgk, self.pscale, self.gnorm, self.wpg = gmix, bgk, pscale, gnorm, wpg
        self.wmain, self.wgab, self.wgk = wmain, wgab, wgk
        self.wpp, self.wgp, self.wout = wpp, wgp, wout


def _in_proj(x, w):
    h, r = _rms_split(x, w.gmix[...])
    qkz = _dot(h, w.wmain[:, QK_COL:MAIN_COLS])
    q = qkz[:, :GLA_KW] * (r * (GLA_DK ** -0.5))
    k = qkz[:, GLA_KW:2 * GLA_KW] * r
    zr = qkz[:, 2 * GLA_KW:] * r
    z = _dot(zr.astype(BF16), w.wgk[...]) + w.bgk[...]
    u = _dot(h, w.wmain[:, U_COL:VOG_COL]) * r
    vog = _dot(h, w.wmain[:, VOG_COL:QK_COL]) * r
    v = vog[:, :GLA_VW]
    og = vog[:, GLA_VW:]
    return (h, r), u, q, k, v, og, z


def _gate_proj(xn, w, lo, hi):
    h, r = xn
    return _dot(h, w.wgab[:, lo:hi]) * r


def _chunk_cumsum_wide(g, chunk):
    n = g.shape[0] // chunk
    r = lax.broadcasted_iota(jnp.int32, (chunk, chunk), 0)
    c = lax.broadcasted_iota(jnp.int32, (chunk, chunk), 1)
    tri = jnp.where(c <= r, 1.0, 0.0).astype(BF16)
    hi, lo = _split_bf16(jnp.concatenate([g[j * chunk:(j + 1) * chunk] for j in range(n)], axis=1))
    wide = _dot(tri, hi) + _dot(tri, lo)
    width = g.shape[1]
    parts = [wide[:, j * width:(j + 1) * width] for j in range(n)]
    return jnp.concatenate(parts, axis=0), [p[chunk - 1:chunk, :] for p in parts]


def _chunk_cumsum(g, chunk):
    m = g.shape[0]
    r = lax.broadcasted_iota(jnp.int32, (m, m), 0)
    c = lax.broadcasted_iota(jnp.int32, (m, m), 1)
    tri = jnp.where((r // chunk == c // chunk) & (c <= r), 1.0, 0.0).astype(BF16)
    hi, lo = _split_bf16(g)
    return _dot(tri, hi) + _dot(tri, lo)


def _head_lane_mask(width, per_head):
    lane = lax.broadcasted_iota(jnp.int32, (1, width), 1)
    return [(lane // per_head) == h for h in range(GLA_HEADS)]


def _block_diag_rows(x_bf, per_head):
    r = x_bf.shape[0]
    zero = jnp.zeros((r, per_head), x_bf.dtype)
    rows = []
    for h in range(GLA_HEADS):
        rows.append(jnp.concatenate(
            [x_bf[:, h * per_head:(h + 1) * per_head] if hh == h else zero for hh in range(GLA_HEADS)], axis=1))
    return jnp.concatenate(rows, axis=0)


def _gla_post(o, og, w):
    parts = []
    for h in range(GLA_HEADS):
        oh = o[:, h * GLA_DV:(h + 1) * GLA_DV]
        parts.append(oh * lax.rsqrt(jnp.mean(oh * oh, axis=-1, keepdims=True) + EPS) * w.gnorm[...])
    on = jnp.concatenate(parts, axis=1)
    on = on * _silu(og)
    return _dot(on.astype(BF16), w.wgp[...])


def _pool_post(pooled, w):
    pb = pooled.astype(BF16)
    mixed = jnp.concatenate(
        [_dot(pb[:, g * POOL_GROUP_DIM:(g + 1) * POOL_GROUP_DIM], w.wpg[0, g].astype(BF16))
         for g in range(len(POOL_WINDOWS))], axis=1)
    return _dot((mixed * w.pscale[...]).astype(BF16), w.wpp[...])


def _merge(x, y_a, y_b, sa, sb, w):
    merged = sa * y_a + sb * y_b
    return x + _dot(merged.astype(BF16), w.wout[...])


def _decay_columns(decay_row):
    return jnp.transpose(jnp.broadcast_to(decay_row, (LANES, decay_row.shape[1])))


def _pair_block_diag(a, b):
    zero = jnp.zeros(a.shape, a.dtype)
    return jnp.concatenate([jnp.concatenate([a, zero], axis=1), jnp.concatenate([zero, b], axis=1)], axis=0)


def _state_update(s_heads, kd_bf, v_bf, decay_row):
    dcol = _decay_columns(decay_row)
    out = []
    for p in range(GLA_HEADS // 2):
        upd = _dot_tn(kd_bf[:, 2 * p * GLA_DK:(2 * p + 2) * GLA_DK], v_bf[:, 2 * p * GLA_DV:(2 * p + 2) * GLA_DV])
        for j in range(2):
            h = 2 * p + j
            rows = slice(h * GLA_DK, (h + 1) * GLA_DK)
            out.append(dcol[rows] * s_heads[h] + upd[j * GLA_DK:(j + 1) * GLA_DK, j * GLA_DV:(j + 1) * GLA_DV])
    return out


def _stage_weights(wint_hbm, wpp_hbm, wgp_hbm, wout_hbm, wgk_ref, w, stage, zr_stage, sem, zr_sem):
    plan = []
    for r in range(0, MAIN_W, STAGE_ROWS):
        if r < POOL_WIDTH:
            col = U_COL + r
        elif r < POOL_WIDTH + 2 * GLA_KW:
            col = QK_COL + r - POOL_WIDTH
        else:
            col = VOG_COL + r - (POOL_WIDTH + 2 * GLA_KW)
        plan.append((wint_hbm, r, w.wmain, col, True))
    for r in range(0, 2 * D_MODEL, STAGE_ROWS):
        plan.append((wint_hbm, GAB_LO + r, w.wgab, r, True))
    for src, dst, n_rows in ((wpp_hbm, w.wpp, POOL_WIDTH), (wgp_hbm, w.wgp, GLA_VW), (wout_hbm, w.wout, D_MODEL)):
        for r in range(0, n_rows, STAGE_ROWS):
            plan.append((src, r, dst, r, False))

    def copy(j):
        src, r0 = plan[j][0], plan[j][1]
        return pltpu.make_async_copy(src.at[0, pl.ds(r0, STAGE_ROWS), :], stage.at[j % 2], sem.at[j % 2])

    def zr_copy():
        return pltpu.make_async_copy(
            wint_hbm.at[0, pl.ds(MAIN_W, GLA_GATE_RANK), :], zr_stage.at[pl.ds(0, GLA_GATE_RANK), :], zr_sem.at[0])

    copy(0).start()
    copy(1).start()
    zr_copy().start()

    w.wgk[...] = jnp.zeros(w.wgk.shape, BF16)
    w.wgk[0:GLA_GATE_RANK, :] = wgk_ref[0].astype(BF16)
    zr_stage[GLA_GATE_RANK:, :] = jnp.zeros((LANES - GLA_GATE_RANK, D_MODEL), F32)

    for j in range(len(plan)):
        copy(j).wait()
        _, _, dst, d0, transposed = plan[j]
        slab = stage[j % 2]
        if transposed:
            dst[:, d0:d0 + STAGE_ROWS] = jnp.transpose(slab).astype(BF16)
        else:
            dst[d0:d0 + STAGE_ROWS, :] = slab.astype(BF16)
        if j + 2 < len(plan):
            copy(j + 2).start()
    zr_copy().wait()
    w.wmain[:, ZR_COL:MAIN_COLS] = jnp.transpose(zr_stage[...]).astype(BF16)


def _prompt_tile(b_first, b_idx, t_idx, n_t, x_ref, meta_ref, w, x2_ref, pbuf_ref, sout_ref,
                 ext_ref, lvl_ref, s_ref, meta_tail_ref, meta_s_ref):
    tile = x_ref.shape[1]
    n_chunks = tile // GLA_CHUNK

    @pl.when(b_first)
    def _():
        _, u, _, k, v, _, z = _in_proj(meta_ref[...], w)
        meta_tail_ref[...] = u
        g = _log_sigmoid(z) * (1.0 / GLA_TAU)
        b = _chunk_cumsum(g, N_META)
        b_last = b[N_META - 1:N_META, :]
        kd = k * jnp.exp(b_last - b)
        zero_s = [jnp.zeros((GLA_DK, GLA_DV), F32)] * GLA_HEADS
        s_new = _state_update(zero_s, kd.astype(BF16), v.astype(BF16), jnp.exp(b_last))
        for hd in range(GLA_HEADS):
            meta_s_ref[hd] = s_new[hd]

    @pl.when(t_idx == 0)
    def _():
        ext_ref[0:POOL_PAD, :] = jnp.zeros((POOL_PAD, POOL_WIDTH), F32)
        lvl_ref[:, 0:POOL_PAD, :] = jnp.zeros((lvl_ref.shape[0], POOL_PAD, POOL_WIDTH), F32)
        ext_ref[POOL_PAD:POOL_PAD + TAIL_ROWS, :] = meta_tail_ref[...]
        s_ref[...] = meta_s_ref[...]

    x = x_ref[0]
    xn, u, q, k, v, og, z = _in_proj(x, w)

    base = POOL_PAD + TAIL_ROWS
    span = TAIL_ROWS + tile
    ext_ref[base:base + tile, :] = u
    cur = ext_ref[POOL_PAD:POOL_PAD + span, :]
    pooled = []
    for gi, win in enumerate(POOL_WINDOWS):
        shift = win // 2
        lo = gi * POOL_GROUP_DIM
        prev_ref = ext_ref if gi == 0 else lvl_ref.at[gi - 1]
        cur = cur[:, (POOL_GROUP_DIM if gi else 0):] + prev_ref[POOL_PAD - shift:POOL_PAD - shift + span, lo:]
        pooled.append(cur[TAIL_ROWS:, 0:POOL_GROUP_DIM] * (1.0 / win) - u[:, lo:lo + POOL_GROUP_DIM])
        if gi + 1 < len(POOL_WINDOWS):
            lvl_ref[gi, POOL_PAD:POOL_PAD + span, lo:] = cur
    y_a = _pool_post(jnp.concatenate(pooled, axis=1), w)
    ext_ref[POOL_PAD:base, :] = ext_ref[POOL_PAD + tile:base + tile, :]

    g = _log_sigmoid(z) * (1.0 / GLA_TAU)
    b, b_last_rows = _chunk_cumsum_wide(g, GLA_CHUNK)
    b_last = jnp.concatenate([jnp.broadcast_to(r, (GLA_CHUNK, GLA_KW)) for r in b_last_rows], axis=0)
    qe = (q * jnp.exp(b)).astype(BF16)
    ke = k * jnp.exp(-b)
    kd = (k * jnp.exp(b_last - b)).astype(BF16)
    v_bf = v.astype(BF16)

    k_masks = _head_lane_mask(GLA_KW, GLA_DK)
    row_i = lax.broadcasted_iota(jnp.int32, (GLA_CHUNK, GLA_KW), 0)
    col_j = lax.broadcasted_iota(jnp.int32, (GLA_CHUNK, GLA_KW), 1) % GLA_CHUNK
    causal = col_j <= row_i

    gate_cols = 2 * D_MODEL // n_chunks
    gate_parts = []
    s_heads = [s_ref[hd] for hd in range(GLA_HEADS)]
    o_chunks = []
    for c in range(n_chunks):
        gate_parts.append(_sigmoid(_gate_proj(xn, w, c * gate_cols, (c + 1) * gate_cols)))
        rows = slice(c * GLA_CHUNK, (c + 1) * GLA_CHUNK)
        ke_c = ke[rows]
        ke_bd = jnp.concatenate([jnp.where(k_masks[hd], ke_c, 0.0) for hd in range(GLA_HEADS)], axis=0).astype(BF16)
        att = jnp.where(causal, _dot_nt(qe[rows], ke_bd), 0.0).astype(BF16)
        o_pairs = []
        for p in range(GLA_HEADS // 2):
            h0, h1 = 2 * p, 2 * p + 1
            lanes_k = slice(h0 * GLA_DK, (h1 + 1) * GLA_DK)
            lanes_j = slice(h0 * GLA_CHUNK, (h1 + 1) * GLA_CHUNK)
            v0 = v_bf[rows, h0 * GLA_DV:(h0 + 1) * GLA_DV]
            v1 = v_bf[rows, h1 * GLA_DV:(h1 + 1) * GLA_DV]
            rhs = jnp.concatenate([_pair_block_diag(s_heads[h0].astype(BF16), s_heads[h1].astype(BF16)),
                                   _pair_block_diag(v0, v1)], axis=0)
            o_pairs.append(_dot(jnp.concatenate([qe[rows, lanes_k], att[:, lanes_j]], axis=1), rhs))
        o_chunks.append(jnp.concatenate(o_pairs, axis=1))
        s_heads = _state_update(s_heads, kd[rows], v_bf[rows], jnp.exp(b_last_rows[c]))
    for hd in range(GLA_HEADS):
        s_ref[hd] = s_heads[hd]

    y_b = _gla_post(jnp.concatenate(o_chunks, axis=0), og, w)
    sg = jnp.concatenate(gate_parts, axis=1)
    x2_ref[0] = _merge(x, y_a, y_b, sg[:, :D_MODEL], sg[:, D_MODEL:], w)

    @pl.when(t_idx == n_t - 1)
    def _():
        sout_ref[0, 0] = s_ref[...]

    for bb in range(pbuf_ref.shape[2]):
        @pl.when((t_idx == n_t - 1) & (b_idx == bb))
        def _():
            for r in range(POOL_BUF):
                row = base - POOL_BUF + r
                pbuf_ref[0, r, bb:bb + 1, :] = ext_ref[row:row + 1, :]


def _sample_block(x_ref, pool_ref, sin_ref, w, x2_ref, pbuf_ref, sout_ref,
                  xs_ref, us_ref, pooled_ref, qm_ref, kdx_ref, rhs_ref, oi_ref):
    nb, seq, _ = x_ref.shape
    rows_pb = SAMPLE_ROWS
    m = nb * rows_pb

    xs_ref[...] = jnp.zeros(xs_ref.shape, F32)
    xs_ref[:, 0:seq, :] = x_ref[...]
    x = xs_ref[...].reshape(m, D_MODEL)
    xn, u, q, k, v, og, z = _in_proj(x, w)

    pooled_ref[...] = jnp.zeros(pooled_ref.shape, F32)
    for gi, win in enumerate(POOL_WINDOWS):
        cols = slice(gi * POOL_GROUP_DIM, (gi + 1) * POOL_GROUP_DIM)
        us_ref[gi] = u[:, cols]
        tok = [us_ref[gi, pl.ds(t, nb, stride=rows_pb), :] for t in range(seq)]
        hist = [pool_ref[0, r, :, cols] for r in range(POOL_BUF)] + tok
        for t in range(seq):
            acc = tok[t]
            for n in range(1, win):
                acc = acc + hist[POOL_BUF + t - n]
            pooled_ref[gi, pl.ds(t, nb, stride=rows_pb), :] = acc * (1.0 / win) - tok[t]
        for r in range(POOL_BUF):
            pbuf_ref[0, r, :, cols] = hist[seq + r]
    y_a = _pool_post(jnp.concatenate([pooled_ref[gi] for gi in range(len(POOL_WINDOWS))], axis=1), w)

    r8 = lax.broadcasted_iota(jnp.int32, (m, 1), 0) % rows_pb
    g = jnp.where(r8 < seq, _log_sigmoid(z) * (1.0 / GLA_TAU), 0.0)
    b = _chunk_cumsum(g, rows_pb)
    b3 = b.reshape(nb, rows_pb, GLA_KW)
    b_last = jnp.broadcast_to(b3[:, seq - 1:seq, :], b3.shape).reshape(m, GLA_KW)
    qe = q * jnp.exp(b)
    ke = k * jnp.exp(-b)
    kd = k * jnp.exp(b_last - b)
    decay = jnp.exp(b_last)
    v_bf = v.astype(BF16)

    k_masks = _head_lane_mask(GLA_KW, GLA_DK)
    ke_bd = jnp.concatenate([jnp.where(k_masks[hd], ke, 0.0) for hd in range(GLA_HEADS)], axis=0).astype(BF16)
    row_i = lax.broadcasted_iota(jnp.int32, (m, GLA_HEADS * m), 0)
    col_j = lax.broadcasted_iota(jnp.int32, (m, GLA_HEADS * m), 1) % m
    keep = (row_i // rows_pb == col_j // rows_pb) & (col_j <= row_i)
    att = jnp.where(keep, _dot_nt(qe.astype(BF16), ke_bd), 0.0).astype(BF16)
    o_intra = _dot(att, _block_diag_rows(v_bf, GLA_DV))

    qe3 = qe.reshape(nb, rows_pb, GLA_KW)
    qm_ref[...] = jnp.concatenate([jnp.where(k_masks[hd], qe3, 0.0) for hd in range(GLA_HEADS)], axis=1).astype(BF16)
    d_hi = decay.astype(BF16).astype(F32)
    d_lo = decay - d_hi
    kdx = jnp.where(r8 == seq, d_hi, jnp.where(r8 == seq + 1, d_lo, kd))
    kdx_ref[...] = kdx.reshape(nb, rows_pb, GLA_KW)
    ones_rows = jnp.where((r8 == seq) | (r8 == seq + 1), 1.0, 0.0) + jnp.zeros((m, GLA_DV), F32)
    rhs = jnp.concatenate(
        [piece for hd in range(GLA_HEADS) for piece in (v[:, hd * GLA_DV:(hd + 1) * GLA_DV], ones_rows)], axis=1)
    rhs_ref[...] = rhs.reshape(nb, rows_pb, 2 * GLA_VW)

    def per_batch(i, carry):
        s_all = sin_ref[0, i]
        s_flat = s_all.reshape(GLA_KW, GLA_DV).astype(BF16)
        oi_ref[i] = _dot(qm_ref[i], s_flat)
        kdt = jnp.transpose(kdx_ref[i]).astype(BF16)
        rhs_i = rhs_ref[i].astype(BF16)
        for hd in range(GLA_HEADS):
            r = _dot(kdt[hd * GLA_DK:(hd + 1) * GLA_DK], rhs_i[:, hd * 2 * GLA_DV:(hd + 1) * 2 * GLA_DV])
            sout_ref[0, i, hd] = r[:, GLA_DV:] * s_all[hd] + r[:, :GLA_DV]
        return carry

    lax.fori_loop(0, nb, per_batch, 0, unroll=4)

    oi = oi_ref[...]
    o_inter = jnp.concatenate([oi[:, hd * rows_pb:(hd + 1) * rows_pb, :] for hd in range(GLA_HEADS)], axis=2)
    o = o_intra + o_inter.reshape(m, GLA_VW)
    y_b = _gla_post(o, og, w)
    sg = _sigmoid(_gate_proj(xn, w, 0, 2 * D_MODEL))
    x2 = _merge(x, y_a, y_b, sg[:, :D_MODEL], sg[:, D_MODEL:], w).reshape(nb, rows_pb, D_MODEL)
    for bi in range(nb):
        x2_ref[bi * seq:(bi + 1) * seq, :] = x2[bi, 0:seq, :]


def _mixer_kernel(n_s, n_t,
                  xs_in, pool_in, s_in, xp_in, meta_ref, gmix_ref, bgk_ref, pscale_ref, gnorm_ref, wgk_ref, wpg_ref,
                  wint_hbm, wpp_hbm, wgp_hbm, wout_hbm, wfi_in, wfo_in,
                  x2s_out, pools_out, ss_out, x2p_out, poolp_out, sp_out, wfi_out, wfo_out,
                  wmain_s, wgab_s, wgk_s, wpp_s, wgp_s, wout_s, stage, zr_stage, sem, zr_sem,
                  ext_ref, lvl_ref, s_ref, meta_tail_ref, meta_s_ref,
                  xs_ref, us_ref, pooled_ref, qm_ref, kdx_ref, rhs_ref, oi_ref):
    i = pl.program_id(0)
    w = _Weights(gmix_ref, bgk_ref, pscale_ref, gnorm_ref, wpg_ref,
                 wmain_s, wgab_s, wgk_s, wpp_s, wgp_s, wout_s)

    @pl.when(i == 0)
    def _():
        _stage_weights(wint_hbm, wpp_hbm, wgp_hbm, wout_hbm, wgk_ref, w, stage, zr_stage, sem, zr_sem)

    wfi_out[...] = wfi_in[0].astype(BF16)
    wfo_out[...] = wfo_in[0].astype(BF16)

    @pl.when(i < n_s)
    def _():
        _sample_block(xs_in, pool_in, s_in, w, x2s_out, pools_out, ss_out,
                      xs_ref, us_ref, pooled_ref, qm_ref, kdx_ref, rhs_ref, oi_ref)

    @pl.when(i >= n_s)
    def _():
        t_idx = (i - n_s) % n_t
        _prompt_tile(i == n_s, (i - n_s) // n_t, t_idx, n_t, xp_in, meta_ref, w, x2p_out, poolp_out, sp_out,
                     ext_ref, lvl_ref, s_ref, meta_tail_ref, meta_s_ref)


def _ffn_kernel(seq, xp_ref, xs_ref, gffn_ref, wi_ref, wo_ref, gfin_ref, yp_ref, ys_ref):
    i = pl.program_id(0)
    x = jnp.where(i == 0, xs_ref[...], xp_ref[0])
    h, r = _rms_split(x, gffn_ref[...])
    acc = x
    for c in range(D_FF // FFN_CHUNK):
        lo = c * FFN_CHUNK
        gate = _dot(h, wi_ref[:, lo:lo + FFN_CHUNK]) * r
        up = _dot(h, wi_ref[:, D_FF + lo:D_FF + lo + FFN_CHUNK]) * r
        act = (_silu(gate) * up).astype(BF16)
        acc = acc + _dot(act, wo_ref[lo:lo + FFN_CHUNK, :])
    yp_ref[0] = _rms(acc, gfin_ref[...])

    @pl.when(i == 0)
    def _():
        for bi in range(ys_ref.shape[0]):
            ys_ref[bi] = yp_ref[0, bi * seq:(bi + 1) * seq, :]


def _const_spec(shape):
    zeros = (0,) * len(shape)
    return pl.BlockSpec(shape, lambda *_: zeros, pipeline_mode=pl.Buffered(1))


def kernel(x_prompt, x_sample, state_pool, state_gla, meta_tokens, g_mix, w_in, w_gk_up, b_gk, w_pool_group,
           pool_scale, w_pool_proj, g_gla_norm, w_gla_proj, w_out, g_ffn, w_ffn_in, w_ffn_out, g_final):
    depth = w_in.shape[0]
    assert depth == 1, "single-layer trunk only"
    bp, tp, d = x_prompt.shape
    bs, ts, _ = x_sample.shape
    nbb = SAMPLE_BATCH_BLOCK
    assert d == D_MODEL and w_in.shape == (1, D_MODEL, IN_DIM) and meta_tokens.shape == (N_META, D_MODEL)
    assert tp % PROMPT_TILE == 0 and bs % nbb == 0 and ts + 2 <= SAMPLE_ROWS and bs * ts == FFN_TILE
    n_t = tp // PROMPT_TILE
    n_p = bp * n_t
    n_s = bs // nbb
    n_fi = D_MODEL // FFN_IN_CAST_ROWS
    n_fo = D_FF // FFN_OUT_CAST_ROWS
    assert n_fi <= n_p and n_fo <= n_p

    def s_idx(i):
        return jnp.minimum(i, n_s - 1)

    def p_idx(i):
        return jnp.maximum(i - n_s, 0)

    w_in_t = jnp.transpose(w_in, (0, 2, 1))
    pool_hist = jnp.transpose(state_pool, (0, 2, 1, 3))
    small = (meta_tokens, g_mix, b_gk, pool_scale, g_gla_norm, w_gk_up, w_pool_group)
    hbm = pl.BlockSpec(memory_space=pl.ANY)
    in_specs = (
        [pl.BlockSpec((nbb, ts, d), lambda i: (s_idx(i), 0, 0)),
         pl.BlockSpec((1, POOL_BUF, nbb, POOL_WIDTH), lambda i: (0, 0, s_idx(i), 0)),
         pl.BlockSpec((1, nbb, GLA_HEADS, GLA_DK, GLA_DV), lambda i: (0, s_idx(i), 0, 0, 0)),
         pl.BlockSpec((1, PROMPT_TILE, d), lambda i: (p_idx(i) // n_t, p_idx(i) % n_t, 0))]
        + [_const_spec(a.shape) for a in small]
        + [hbm, hbm, hbm, hbm,
           pl.BlockSpec((1, FFN_IN_CAST_ROWS, 2 * D_FF), lambda i: (0, jnp.minimum(p_idx(i), n_fi - 1), 0)),
           pl.BlockSpec((1, FFN_OUT_CAST_ROWS, d), lambda i: (0, jnp.minimum(p_idx(i), n_fo - 1), 0))])
    out_specs = [
        pl.BlockSpec((nbb * ts, d), lambda i: (s_idx(i), 0)),
        pl.BlockSpec((1, POOL_BUF, nbb, POOL_WIDTH), lambda i: (0, 0, s_idx(i), 0)),
        pl.BlockSpec((1, nbb, GLA_HEADS, GLA_DK, GLA_DV), lambda i: (0, s_idx(i), 0, 0, 0)),
        pl.BlockSpec((1, PROMPT_TILE, d), lambda i: (p_idx(i) // n_t, p_idx(i) % n_t, 0)),
        pl.BlockSpec((1, POOL_BUF, bp, POOL_WIDTH), lambda i: (0, 0, 0, 0)),
        pl.BlockSpec((1, 1, GLA_HEADS, GLA_DK, GLA_DV), lambda i: (0, p_idx(i) // n_t, 0, 0, 0)),
        pl.BlockSpec((FFN_IN_CAST_ROWS, 2 * D_FF), lambda i: (jnp.minimum(p_idx(i), n_fi - 1), 0)),
        pl.BlockSpec((FFN_OUT_CAST_ROWS, d), lambda i: (jnp.minimum(p_idx(i), n_fo - 1), 0))]
    out_shape = [
        jax.ShapeDtypeStruct((bs * ts, d), F32),
        jax.ShapeDtypeStruct((1, POOL_BUF, bs, POOL_WIDTH), F32),
        jax.ShapeDtypeStruct(state_gla.shape, F32),
        jax.ShapeDtypeStruct(x_prompt.shape, F32),
        jax.ShapeDtypeStruct((1, POOL_BUF, bp, POOL_WIDTH), F32),
        jax.ShapeDtypeStruct((1, bp, GLA_HEADS, GLA_DK, GLA_DV), F32),
        jax.ShapeDtypeStruct((D_MODEL, 2 * D_FF), BF16),
        jax.ShapeDtypeStruct((D_FF, D_MODEL), BF16)]
    scratch_shapes = [
        pltpu.VMEM((D_MODEL, MAIN_COLS), BF16),
        pltpu.VMEM((D_MODEL, 2 * D_MODEL), BF16), pltpu.VMEM((LANES, GLA_KW), BF16),
        pltpu.VMEM((POOL_WIDTH, D_MODEL), BF16), pltpu.VMEM((GLA_VW, D_MODEL), BF16),
        pltpu.VMEM((D_MODEL, D_MODEL), BF16),
        pltpu.VMEM((2, STAGE_ROWS, D_MODEL), F32), pltpu.VMEM((LANES, D_MODEL), F32),
        pltpu.SemaphoreType.DMA((2,)), pltpu.SemaphoreType.DMA((1,)),
        pltpu.VMEM((POOL_PAD + TAIL_ROWS + PROMPT_TILE, POOL_WIDTH), F32),
        pltpu.VMEM((len(POOL_WINDOWS) - 1, POOL_PAD + TAIL_ROWS + PROMPT_TILE, POOL_WIDTH), F32),
        pltpu.VMEM((GLA_HEADS, GLA_DK, GLA_DV), F32),
        pltpu.VMEM((N_META, POOL_WIDTH), F32),
        pltpu.VMEM((GLA_HEADS, GLA_DK, GLA_DV), F32),
        pltpu.VMEM((nbb, SAMPLE_ROWS, D_MODEL), F32),
        pltpu.VMEM((len(POOL_WINDOWS), nbb * SAMPLE_ROWS, POOL_GROUP_DIM), F32),
        pltpu.VMEM((len(POOL_WINDOWS), nbb * SAMPLE_ROWS, POOL_GROUP_DIM), F32),
        pltpu.VMEM((nbb, GLA_HEADS * SAMPLE_ROWS, GLA_KW), BF16),
        pltpu.VMEM((nbb, SAMPLE_ROWS, GLA_KW), F32),
        pltpu.VMEM((nbb, SAMPLE_ROWS, 2 * GLA_VW), F32),
        pltpu.VMEM((nbb, GLA_HEADS * SAMPLE_ROWS, GLA_DV), F32)]

    def mixer(*refs):
        _mixer_kernel(n_s, n_t, *refs)

    x2_s, pool_s, gla_s, x2_p, pool_p, gla_p, wfi_bf, wfo_bf = pl.pallas_call(
        mixer,
        grid=(n_s + n_p,),
        in_specs=in_specs,
        out_specs=out_specs,
        out_shape=out_shape,
        scratch_shapes=scratch_shapes,
        compiler_params=pltpu.CompilerParams(dimension_semantics=("arbitrary",), vmem_limit_bytes=VMEM_LIMIT),
        name="mixer",
    )(x_sample, pool_hist, state_gla, x_prompt, *small, w_in_t, w_pool_proj, w_gla_proj, w_out, w_ffn_in, w_ffn_out)

    def ffn(*refs):
        _ffn_kernel(ts, *refs)

    def fp_idx(i):
        return jnp.maximum(i - 1, 0)

    ffn_small = (g_ffn, wfi_bf, wfo_bf, g_final.reshape(1, D_MODEL))
    y_prompt, y_sample = pl.pallas_call(
        ffn,
        grid=(n_p + 1,),
        in_specs=[pl.BlockSpec((1, FFN_TILE, d), lambda i: (fp_idx(i) // n_t, fp_idx(i) % n_t, 0)),
                  _const_spec(x2_s.shape)] + [_const_spec(a.shape) for a in ffn_small],
        out_specs=[pl.BlockSpec((1, FFN_TILE, d), lambda i: (fp_idx(i) // n_t, fp_idx(i) % n_t, 0)),
                   pl.BlockSpec(x_sample.shape, lambda i: (0, 0, 0))],
        out_shape=[jax.ShapeDtypeStruct(x_prompt.shape, F32), jax.ShapeDtypeStruct(x_sample.shape, F32)],
        compiler_params=pltpu.CompilerParams(dimension_semantics=("arbitrary",), vmem_limit_bytes=VMEM_LIMIT),
        name="ffn",
    )(x2_p, x2_s, *ffn_small)
    pool_p = jnp.transpose(pool_p, (0, 2, 1, 3))
    pool_s = jnp.transpose(pool_s, (0, 2, 1, 3))
    return y_prompt, y_sample, pool_p, gla_p, pool_s, gla_s
```

```python
import jax
import jax.numpy as jnp
from jax import lax
from jax.experimental import pallas as pl
from jax.experimental.pallas import tpu as pltpu

F32 = jnp.float32
BF16 = jnp.bfloat16

D_MODEL = 1024
N_META = 16
POOL_WIDTH = 512
POOL_WINDOWS = (2, 4, 8, 16)
POOL_GROUP_DIM = 128
POOL_BUF = 15
GLA_HEADS = 4
GLA_DV = 128
GLA_DK = 64
GLA_KW = GLA_HEADS * GLA_DK
GLA_VW = GLA_HEADS * GLA_DV
GLA_GATE_RANK = 16
GLA_TAU = 16.0
GLA_CHUNK = 64
D_FF = 2816
EPS = 1e-6

LANES = 128
SUBLANES = 8
MAIN_W = POOL_WIDTH + 2 * GLA_KW + 2 * GLA_VW
U_COL, VOG_COL, QK_COL, ZR_COL = 0, POOL_WIDTH, POOL_WIDTH + 2 * GLA_VW, MAIN_W
MAIN_COLS = MAIN_W + 128
GAB_LO = MAIN_W + GLA_GATE_RANK
IN_DIM = GAB_LO + 2 * D_MODEL
TAIL_ROWS = 16
POOL_PAD = 8

PROMPT_TILE = 512
FFN_TILE = 512
FFN_CHUNK = 256
SAMPLE_BATCH_BLOCK = 16
SAMPLE_ROWS = 8
STAGE_ROWS = 256
FFN_IN_CAST_ROWS = 32
FFN_OUT_CAST_ROWS = 128
VMEM_LIMIT = 60 * 1024 * 1024


def _dot(a, b):
    return jnp.dot(a, b, preferred_element_type=F32)


def _dot_nt(a, b):
    return lax.dot_general(a, b, (((1,), (1,)), ((), ())), preferred_element_type=F32)


def _dot_tn(a, b):
    return lax.dot_general(a, b, (((0,), (0,)), ((), ())), preferred_element_type=F32)


def _rms(x, g):
    return x * lax.rsqrt(jnp.mean(x * x, axis=-1, keepdims=True) + EPS) * g


def _rms_split(x, g):
    r = lax.rsqrt(jnp.mean(x * x, axis=-1, keepdims=True) + EPS)
    return (x * g).astype(BF16), r


def _sigmoid(x):
    return 0.5 * jnp.tanh(0.5 * x) + 0.5


def _silu(x):
    half = 0.5 * x
    return half * jnp.tanh(half) + half


def _log_sigmoid(x):
    return jnp.minimum(x, 0.0) - jnp.log(1.0 + jnp.exp(-jnp.abs(x)))


def _split_bf16(x):
    hi = x.astype(BF16)
    lo = (x - hi.astype(F32)).astype(BF16)
    return hi, lo


class _Weights:
    def __init__(self, gmix, bgk, pscale, gnorm, wpg, wmain, wgab, wgk, wpp, wgp, wout):
        self.gmix, self.bgk, self.pscale, self.gnorm, self.wpg = gmix, bgk, pscale, gnorm, wpg
        self.wmain, self.wgab, self.wgk = wmain, wgab, wgk
        self.wpp, self.wgp, self.wout = wpp, wgp, wout


def _in_proj(x, w):
    h, r = _rms_split(x, w.gmix[...])
    qkz = _dot(h, w.wmain[:, QK_COL:MAIN_COLS])
    q = qkz[:, :GLA_KW] * (r * (GLA_DK ** -0.5))
    k = qkz[:, GLA_KW:2 * GLA_KW] * r
    zr = qkz[:, 2 * GLA_KW:] * r
    z = _dot(zr.astype(BF16), w.wgk[...]) + w.bgk[...]
    u = _dot(h, w.wmain[:, U_COL:VOG_COL]) * r
    vog = _dot(h, w.wmain[:, VOG_COL:QK_COL]) * r
    v = vog[:, :GLA_VW]
    og = vog[:, GLA_VW:]
    return (h, r), u, q, k, v, og, z


def _gate_proj(xn, w, lo, hi):
    h, r = xn
    return _dot(h, w.wgab[:, lo:hi]) * r


def _chunk_cumsum_wide(g, chunk):
    n = g.shape[0] // chunk
    r = lax.broadcasted_iota(jnp.int32, (chunk, chunk), 0)
    c = lax.broadcasted_iota(jnp.int32, (chunk, chunk), 1)
    tri = jnp.where(c <= r, 1.0, 0.0).astype(BF16)
    hi, lo = _split_bf16(jnp.concatenate([g[j * chunk:(j + 1) * chunk] for j in range(n)], axis=1))
    wide = _dot(tri, hi) + _dot(tri, lo)
    width = g.shape[1]
    parts = [wide[:, j * width:(j + 1) * width] for j in range(n)]
    return jnp.concatenate(parts, axis=0), [p[chunk - 1:chunk, :] for p in parts]


def _chunk_cumsum(g, chunk):
    m = g.shape[0]
    r = lax.broadcasted_iota(jnp.int32, (m, m), 0)
    c = lax.broadcasted_iota(jnp.int32, (m, m), 1)
    tri = jnp.where((r // chunk == c // chunk) & (c <= r), 1.0, 0.0).astype(BF16)
    hi, lo = _split_bf16(g)
    return _dot(tri, hi) + _dot(tri, lo)


def _head_lane_mask(width, per_head):
    lane = lax.broadcasted_iota(jnp.int32, (1, width), 1)
    return [(lane // per_head) == h for h in range(GLA_HEADS)]


def _block_diag_rows(x_bf, per_head):
    r = x_bf.shape[0]
    zero = jnp.zeros((r, per_head), x_bf.dtype)
    rows = []
    for h in range(GLA_HEADS):
        rows.append(jnp.concatenate(
            [x_bf[:, h * per_head:(h + 1) * per_head] if hh == h else zero for hh in range(GLA_HEADS)], axis=1))
    return jnp.concatenate(rows, axis=0)


def _gla_post(o, og, w):
    parts = []
    for h in range(GLA_HEADS):
        oh = o[:, h * GLA_DV:(h + 1) * GLA_DV]
        parts.append(oh * lax.rsqrt(jnp.mean(oh * oh, axis=-1, keepdims=True) + EPS) * w.gnorm[...])
    on = jnp.concatenate(parts, axis=1)
    on = on * _silu(og)
    return _dot(on.astype(BF16), w.wgp[...])


def _pool_post(pooled, w):
    pb = pooled.astype(BF16)
    mixed = jnp.concatenate(
        [_dot(pb[:, g * POOL_GROUP_DIM:(g + 1) * POOL_GROUP_DIM], w.wpg[0, g].astype(BF16))
         for g in range(len(POOL_WINDOWS))], axis=1)
    return _dot((mixed * w.pscale[...]).astype(BF16), w.wpp[...])


def _merge(x, y_a, y_b, sa, sb, w):
    merged = sa * y_a + sb * y_b
    return x + _dot(merged.astype(BF16), w.wout[...])


def _decay_columns(decay_row):
    return jnp.transpose(jnp.broadcast_to(decay_row, (LANES, decay_row.shape[1])))


def _pair_block_diag(a, b):
    zero = jnp.zeros(a.shape, a.dtype)
    return jnp.concatenate([jnp.concatenate([a, zero], axis=1), jnp.concatenate([zero, b], axis=1)], axis=0)


def _state_update(s_heads, kd_bf, v_bf, decay_row):
    dcol = _decay_columns(decay_row)
    out = []
    for p in range(GLA_HEADS // 2):
        upd = _dot_tn(kd_bf[:, 2 * p * GLA_DK:(2 * p + 2) * GLA_DK], v_bf[:, 2 * p * GLA_DV:(2 * p + 2) * GLA_DV])
        for j in range(2):
            h = 2 * p + j
            rows = slice(h * GLA_DK, (h + 1) * GLA_DK)
            out.append(dcol[rows] * s_heads[h] + upd[j * GLA_DK:(j + 1) * GLA_DK, j * GLA_DV:(j + 1) * GLA_DV])
    return out


def _stage_weights(wint_hbm, wpp_hbm, wgp_hbm, wout_hbm, wgk_ref, w, stage, zr_stage, sem, zr_sem):
    plan = []
    for r in range(0, MAIN_W, STAGE_ROWS):
        if r < POOL_WIDTH:
            col = U_COL + r
        elif r < POOL_WIDTH + 2 * GLA_KW:
            col = QK_COL + r - POOL_WIDTH
        else:
            col = VOG_COL + r - (POOL_WIDTH + 2 * GLA_KW)
        plan.append((wint_hbm, r, w.wmain, col, True))
    for r in range(0, 2 * D_MODEL, STAGE_ROWS):
        plan.append((wint_hbm, GAB_LO + r, w.wgab, r, True))
    for src, dst, n_rows in ((wpp_hbm, w.wpp, POOL_WIDTH), (wgp_hbm, w.wgp, GLA_VW), (wout_hbm, w.wout, D_MODEL)):
        for r in range(0, n_rows, STAGE_ROWS):
            plan.append((src, r, dst, r, False))

    def copy(j):
        src, r0 = plan[j][0], plan[j][1]
        return pltpu.make_async_copy(src.at[0, pl.ds(r0, STAGE_ROWS), :], stage.at[j % 2], sem.at[j % 2])

    def zr_copy():
        return pltpu.make_async_copy(
            wint_hbm.at[0, pl.ds(MAIN_W, GLA_GATE_RANK), :], zr_stage.at[pl.ds(0, GLA_GATE_RANK), :], zr_sem.at[0])

    copy(0).start()
    copy(1).start()
    zr_copy().start()

    w.wgk[...] = jnp.zeros(w.wgk.shape, BF16)
    w.wgk[0:GLA_GATE_RANK, :] = wgk_ref[0].astype(BF16)
    zr_stage[GLA_GATE_RANK:, :] = jnp.zeros((LANES - GLA_GATE_RANK, D_MODEL), F32)

    for j in range(len(plan)):
        copy(j).wait()
        _, _, dst, d0, transposed = plan[j]
        slab = stage[j % 2]
        if transposed:
            dst[:, d0:d0 + STAGE_ROWS] = jnp.transpose(slab).astype(BF16)
        else:
            dst[d0:d0 + STAGE_ROWS, :] = slab.astype(BF16)
        if j + 2 < len(plan):
            copy(j + 2).start()
    zr_copy().wait()
    w.wmain[:, ZR_COL:MAIN_COLS] = jnp.transpose(zr_stage[...]).astype(BF16)


def _prompt_tile(b_first, b_idx, t_idx, n_t, x_ref, meta_ref, w, x2_ref, pbuf_ref, sout_ref,
                 ext_ref, lvl_ref, s_ref, meta_tail_ref, meta_s_ref):
    tile = x_ref.shape[1]
    n_chunks = tile // GLA_CHUNK

    @pl.when(b_first)
    def _():
        _, u, _, k, v, _, z = _in_proj(meta_ref[...], w)
        meta_tail_ref[...] = u
        g = _log_sigmoid(z) * (1.0 / GLA_TAU)
        b = _chunk_cumsum(g, N_META)
        b_last = b[N_META - 1:N_META, :]
        kd = k * jnp.exp(b_last - b)
        zero_s = [jnp.zeros((GLA_DK, GLA_DV), F32)] * GLA_HEADS
        s_new = _state_update(zero_s, kd.astype(BF16), v.astype(BF16), jnp.exp(b_last))
        for hd in range(GLA_HEADS):
            meta_s_ref[hd] = s_new[hd]

    @pl.when(t_idx == 0)
    def _():
        ext_ref[0:POOL_PAD, :] = jnp.zeros((POOL_PAD, POOL_WIDTH), F32)
        lvl_ref[:, 0:POOL_PAD, :] = jnp.zeros((lvl_ref.shape[0], POOL_PAD, POOL_WIDTH), F32)
        ext_ref[POOL_PAD:POOL_PAD + TAIL_ROWS, :] = meta_tail_ref[...]
        s_ref[...] = meta_s_ref[...]

    x = x_ref[0]
    xn = _rms_split(x, w.gmix[...])
    h, r = xn
    gate_cols = 2 * D_MODEL // n_chunks
    gate_parts = [None] * n_chunks

    def gate_slice(c):
        gate_parts[c] = _sigmoid(_gate_proj(xn, w, c * gate_cols, (c + 1) * gate_cols))

    qkz = _dot(h, w.wmain[:, QK_COL:MAIN_COLS])
    u = _dot(h, w.wmain[:, U_COL:VOG_COL]) * r
    v = _dot(h, w.wmain[:, VOG_COL:VOG_COL + GLA_VW]) * r
    q = qkz[:, :GLA_KW] * (r * (GLA_DK ** -0.5))
    k = qkz[:, GLA_KW:2 * GLA_KW] * r
    zr = qkz[:, 2 * GLA_KW:] * r
    z = _dot(zr.astype(BF16), w.wgk[...]) + w.bgk[...]
    gate_slice(0)
    g = _log_sigmoid(z) * (1.0 / GLA_TAU)
    b, b_last_rows = _chunk_cumsum_wide(g, GLA_CHUNK)
    gate_slice(1)
    og = _dot(h, w.wmain[:, VOG_COL + GLA_VW:QK_COL]) * r

    base = POOL_PAD + TAIL_ROWS
    span = TAIL_ROWS + tile
    ext_ref[base:base + tile, :] = u
    cur = ext_ref[POOL_PAD:POOL_PAD + span, :]
    pooled = []
    for gi, win in enumerate(POOL_WINDOWS):
        shift = win // 2
        lo = gi * POOL_GROUP_DIM
        prev_ref = ext_ref if gi == 0 else lvl_ref.at[gi - 1]
        cur = cur[:, (POOL_GROUP_DIM if gi else 0):] + prev_ref[POOL_PAD - shift:POOL_PAD - shift + span, lo:]
        pooled.append(cur[TAIL_ROWS:, 0:POOL_GROUP_DIM] * (1.0 / win) - u[:, lo:lo + POOL_GROUP_DIM])
        if gi + 1 < len(POOL_WINDOWS):
            lvl_ref[gi, POOL_PAD:POOL_PAD + span, lo:] = cur
    y_a = _pool_post(jnp.concatenate(pooled, axis=1), w)
    ext_ref[POOL_PAD:base, :] = ext_ref[POOL_PAD + tile:base + tile, :]

    b_last = jnp.concatenate([jnp.broadcast_to(r, (GLA_CHUNK, GLA_KW)) for r in b_last_rows], axis=0)
    qe = (q * jnp.exp(b)).astype(BF16)
    ke = k * jnp.exp(-b)
    kd = (k * jnp.exp(b_last - b)).astype(BF16)
    v_bf = v.astype(BF16)

    k_masks = _head_lane_mask(GLA_KW, GLA_DK)
    row_i = lax.broadcasted_iota(jnp.int32, (GLA_CHUNK, GLA_KW), 0)
    col_j = lax.broadcasted_iota(jnp.int32, (GLA_CHUNK, GLA_KW), 1) % GLA_CHUNK
    causal = col_j <= row_i

    s_heads = [s_ref[hd] for hd in range(GLA_HEADS)]
    o_chunks = []
    for c in range(n_chunks):
        if c + 2 < n_chunks:
            gate_slice(c + 2)
        rows = slice(c * GLA_CHUNK, (c + 1) * GLA_CHUNK)
        ke_c = ke[rows]
        ke_bd = jnp.concatenate([jnp.where(k_masks[hd], ke_c, 0.0) for hd in range(GLA_HEADS)], axis=0).astype(BF16)
        att = jnp.where(causal, _dot_nt(qe[rows], ke_bd), 0.0).astype(BF16)
        o_pairs = []
        for p in range(GLA_HEADS // 2):
            h0, h1 = 2 * p, 2 * p + 1
            lanes_k = slice(h0 * GLA_DK, (h1 + 1) * GLA_DK)
            lanes_j = slice(h0 * GLA_CHUNK, (h1 + 1) * GLA_CHUNK)
            v0 = v_bf[rows, h0 * GLA_DV:(h0 + 1) * GLA_DV]
            v1 = v_bf[rows, h1 * GLA_DV:(h1 + 1) * GLA_DV]
            rhs = jnp.concatenate([_pair_block_diag(s_heads[h0].astype(BF16), s_heads[h1].astype(BF16)),
                                   _pair_block_diag(v0, v1)], axis=0)
            o_pairs.append(_dot(jnp.concatenate([qe[rows, lanes_k], att[:, lanes_j]], axis=1), rhs))
        o_chunks.append(jnp.concatenate(o_pairs, axis=1))
        s_heads = _state_update(s_heads, kd[rows], v_bf[rows], jnp.exp(b_last_rows[c]))
    for hd in range(GLA_HEADS):
        s_ref[hd] = s_heads[hd]

    y_b = _gla_post(jnp.concatenate(o_chunks, axis=0), og, w)
    sg = jnp.concatenate(gate_parts, axis=1)
    x2_ref[0] = _merge(x, y_a, y_b, sg[:, :D_MODEL], sg[:, D_MODEL:], w)

    @pl.when(t_idx == n_t - 1)
    def _():
        sout_ref[0, 0] = s_ref[...]

    for bb in range(pbuf_ref.shape[2]):
        @pl.when((t_idx == n_t - 1) & (b_idx == bb))
        def _():
            for r in range(POOL_BUF):
                row = base - POOL_BUF + r
                pbuf_ref[0, r, bb:bb + 1, :] = ext_ref[row:row + 1, :]


def _sample_block(x_ref, pool_ref, sin_ref, w, x2_ref, pbuf_ref, sout_ref,
                  xs_ref, us_ref, pooled_ref, qm_ref, kdx_ref, rhs_ref, oi_ref):
    nb, seq, _ = x_ref.shape
    rows_pb = SAMPLE_ROWS
    m = nb * rows_pb

    xs_ref[...] = jnp.zeros(xs_ref.shape, F32)
    xs_ref[:, 0:seq, :] = x_ref[...]
    x = xs_ref[...].reshape(m, D_MODEL)
    xn, u, q, k, v, og, z = _in_proj(x, w)

    pooled_ref[...] = jnp.zeros(pooled_ref.shape, F32)
    for gi, win in enumerate(POOL_WINDOWS):
        cols = slice(gi * POOL_GROUP_DIM, (gi + 1) * POOL_GROUP_DIM)
        us_ref[gi] = u[:, cols]
        tok = [us_ref[gi, pl.ds(t, nb, stride=rows_pb), :] for t in range(seq)]
        hist = [pool_ref[0, r, :, cols] for r in range(POOL_BUF)] + tok
        for t in range(seq):
            acc = tok[t]
            for n in range(1, win):
                acc = acc + hist[POOL_BUF + t - n]
            pooled_ref[gi, pl.ds(t, nb, stride=rows_pb), :] = acc * (1.0 / win) - tok[t]
        for r in range(POOL_BUF):
            pbuf_ref[0, r, :, cols] = hist[seq + r]
    y_a = _pool_post(jnp.concatenate([pooled_ref[gi] for gi in range(len(POOL_WINDOWS))], axis=1), w)

    r8 = lax.broadcasted_iota(jnp.int32, (m, 1), 0) % rows_pb
    g = jnp.where(r8 < seq, _log_sigmoid(z) * (1.0 / GLA_TAU), 0.0)
    b = _chunk_cumsum(g, rows_pb)
    b3 = b.reshape(nb, rows_pb, GLA_KW)
    b_last = jnp.broadcast_to(b3[:, seq - 1:seq, :], b3.shape).reshape(m, GLA_KW)
    qe = q * jnp.exp(b)
    ke = k * jnp.exp(-b)
    kd = k * jnp.exp(b_last - b)
    decay = jnp.exp(b_last)
    v_bf = v.astype(BF16)

    k_masks = _head_lane_mask(GLA_KW, GLA_DK)
    ke_bd = jnp.concatenate([jnp.where(k_masks[hd], ke, 0.0) for hd in range(GLA_HEADS)], axis=0).astype(BF16)
    row_i = lax.broadcasted_iota(jnp.int32, (m, GLA_HEADS * m), 0)
    col_j = lax.broadcasted_iota(jnp.int32, (m, GLA_HEADS * m), 1) % m
    keep = (row_i // rows_pb == col_j // rows_pb) & (col_j <= row_i)
    att = jnp.where(keep, _dot_nt(qe.astype(BF16), ke_bd), 0.0).astype(BF16)
    o_intra = _dot(att, _block_diag_rows(v_bf, GLA_DV))

    qe3 = qe.reshape(nb, rows_pb, GLA_KW)
    qm_ref[...] = jnp.concatenate([jnp.where(k_masks[hd], qe3, 0.0) for hd in range(GLA_HEADS)], axis=1).astype(BF16)
    d_hi = decay.astype(BF16).astype(F32)
    d_lo = decay - d_hi
    kdx = jnp.where(r8 == seq, d_hi, jnp.where(r8 == seq + 1, d_lo, kd))
    kdx_ref[...] = kdx.reshape(nb, rows_pb, GLA_KW)
    ones_rows = jnp.where((r8 == seq) | (r8 == seq + 1), 1.0, 0.0) + jnp.zeros((m, GLA_DV), F32)
    rhs = jnp.concatenate(
        [piece for hd in range(GLA_HEADS) for piece in (v[:, hd * GLA_DV:(hd + 1) * GLA_DV], ones_rows)], axis=1)
    rhs_ref[...] = rhs.reshape(nb, rows_pb, 2 * GLA_VW)

    def per_batch(i, carry):
        s_all = sin_ref[0, i]
        s_flat = s_all.reshape(GLA_KW, GLA_DV).astype(BF16)
        oi_ref[i] = _dot(qm_ref[i], s_flat)
        kdt = jnp.transpose(kdx_ref[i]).astype(BF16)
        rhs_i = rhs_ref[i].astype(BF16)
        for hd in range(GLA_HEADS):
            r = _dot(kdt[hd * GLA_DK:(hd + 1) * GLA_DK], rhs_i[:, hd * 2 * GLA_DV:(hd + 1) * 2 * GLA_DV])
            sout_ref[0, i, hd] = r[:, GLA_DV:] * s_all[hd] + r[:, :GLA_DV]
        return carry

    lax.fori_loop(0, nb, per_batch, 0, unroll=4)

    oi = oi_ref[...]
    o_inter = jnp.concatenate([oi[:, hd * rows_pb:(hd + 1) * rows_pb, :] for hd in range(GLA_HEADS)], axis=2)
    o = o_intra + o_inter.reshape(m, GLA_VW)
    y_b = _gla_post(o, og, w)
    sg = _sigmoid(_gate_proj(xn, w, 0, 2 * D_MODEL))
    x2 = _merge(x, y_a, y_b, sg[:, :D_MODEL], sg[:, D_MODEL:], w).reshape(nb, rows_pb, D_MODEL)
    for bi in range(nb):
        x2_ref[bi * seq:(bi + 1) * seq, :] = x2[bi, 0:seq, :]


def _mixer_kernel(n_s, n_t,
                  xs_in, pool_in, s_in, xp_in, meta_ref, gmix_ref, bgk_ref, pscale_ref, gnorm_ref, wgk_ref, wpg_ref,
                  wint_hbm, wpp_hbm, wgp_hbm, wout_hbm, wfi_in, wfo_in,
                  x2s_out, pools_out, ss_out, x2p_out, poolp_out, sp_out, wfi_out, wfo_out,
                  wmain_s, wgab_s, wgk_s, wpp_s, wgp_s, wout_s, stage, zr_stage, sem, zr_sem,
                  ext_ref, lvl_ref, s_ref, meta_tail_ref, meta_s_ref,
                  xs_ref, us_ref, pooled_ref, qm_ref, kdx_ref, rhs_ref, oi_ref):
    i = pl.program_id(0)
    w = _Weights(gmix_ref, bgk_ref, pscale_ref, gnorm_ref, wpg_ref,
                 wmain_s, wgab_s, wgk_s, wpp_s, wgp_s, wout_s)

    @pl.when(i == 0)
    def _():
        _stage_weights(wint_hbm, wpp_hbm, wgp_hbm, wout_hbm, wgk_ref, w, stage, zr_stage, sem, zr_sem)

    wfi_out[...] = wfi_in[0].astype(BF16)
    wfo_out[...] = wfo_in[0].astype(BF16)

    @pl.when(i < n_s)
    def _():
        _sample_block(xs_in, pool_in, s_in, w, x2s_out, pools_out, ss_out,
                      xs_ref, us_ref, pooled_ref, qm_ref, kdx_ref, rhs_ref, oi_ref)

    @pl.when(i >= n_s)
    def _():
        t_idx = (i - n_s) % n_t
        _prompt_tile(i == n_s, (i - n_s) // n_t, t_idx, n_t, xp_in, meta_ref, w, x2p_out, poolp_out, sp_out,
                     ext_ref, lvl_ref, s_ref, meta_tail_ref, meta_s_ref)


def _ffn_kernel(seq, xp_ref, xs_ref, gffn_ref, wi_ref, wo_ref, gfin_ref, yp_ref, ys_ref):
    i = pl.program_id(0)
    x = jnp.where(i == 0, xs_ref[...], xp_ref[0])
    h, r = _rms_split(x, gffn_ref[...])
    acc = x
    for c in range(D_FF // FFN_CHUNK):
        lo = c * FFN_CHUNK
        gate = _dot(h, wi_ref[:, lo:lo + FFN_CHUNK]) * r
        up = _dot(h, wi_ref[:, D_FF + lo:D_FF + lo + FFN_CHUNK]) * r
        act = (_silu(gate) * up).astype(BF16)
        acc = acc + _dot(act, wo_ref[lo:lo + FFN_CHUNK, :])
    yp_ref[0] = _rms(acc, gfin_ref[...])

    @pl.when(i == 0)
    def _():
        for bi in range(ys_ref.shape[0]):
            ys_ref[bi] = yp_ref[0, bi * seq:(bi + 1) * seq, :]


def _const_spec(shape):
    zeros = (0,) * len(shape)
    return pl.BlockSpec(shape, lambda *_: zeros, pipeline_mode=pl.Buffered(1))


def kernel(x_prompt, x_sample, state_pool, state_gla, meta_tokens, g_mix, w_in, w_gk_up, b_gk, w_pool_group,
           pool_scale, w_pool_proj, g_gla_norm, w_gla_proj, w_out, g_ffn, w_ffn_in, w_ffn_out, g_final):
    depth = w_in.shape[0]
    assert depth == 1, "single-layer trunk only"
    bp, tp, d = x_prompt.shape
    bs, ts, _ = x_sample.shape
    nbb = SAMPLE_BATCH_BLOCK
    assert d == D_MODEL and w_in.shape == (1, D_MODEL, IN_DIM) and meta_tokens.shape == (N_META, D_MODEL)
    assert tp % PROMPT_TILE == 0 and bs % nbb == 0 and ts + 2 <= SAMPLE_ROWS and bs * ts == FFN_TILE
    n_t = tp // PROMPT_TILE
    n_p = bp * n_t
    n_s = bs // nbb
    n_fi = D_MODEL // FFN_IN_CAST_ROWS
    n_fo = D_FF // FFN_OUT_CAST_ROWS
    assert n_fi <= n_p and n_fo <= n_p

    def s_idx(i):
        return jnp.minimum(i, n_s - 1)

    def p_idx(i):
        return jnp.maximum(i - n_s, 0)

    w_in_t = jnp.transpose(w_in, (0, 2, 1))
    pool_hist = jnp.transpose(state_pool, (0, 2, 1, 3))
    small = (meta_tokens, g_mix, b_gk, pool_scale, g_gla_norm, w_gk_up, w_pool_group)
    hbm = pl.BlockSpec(memory_space=pl.ANY)
    in_specs = (
        [pl.BlockSpec((nbb, ts, d), lambda i: (s_idx(i), 0, 0)),
         pl.BlockSpec((1, POOL_BUF, nbb, POOL_WIDTH), lambda i: (0, 0, s_idx(i), 0)),
         pl.BlockSpec((1, nbb, GLA_HEADS, GLA_DK, GLA_DV), lambda i: (0, s_idx(i), 0, 0, 0)),
         pl.BlockSpec((1, PROMPT_TILE, d), lambda i: (p_idx(i) // n_t, p_idx(i) % n_t, 0))]
        + [_const_spec(a.shape) for a in small]
        + [hbm, hbm, hbm, hbm,
           pl.BlockSpec((1, FFN_IN_CAST_ROWS, 2 * D_FF), lambda i: (0, jnp.minimum(p_idx(i), n_fi - 1), 0)),
           pl.BlockSpec((1, FFN_OUT_CAST_ROWS, d), lambda i: (0, jnp.minimum(p_idx(i), n_fo - 1), 0))])
    out_specs = [
        pl.BlockSpec((nbb * ts, d), lambda i: (s_idx(i), 0)),
        pl.BlockSpec((1, POOL_BUF, nbb, POOL_WIDTH), lambda i: (0, 0, s_idx(i), 0)),
        pl.BlockSpec((1, nbb, GLA_HEADS, GLA_DK, GLA_DV), lambda i: (0, s_idx(i), 0, 0, 0)),
        pl.BlockSpec((1, PROMPT_TILE, d), lambda i: (p_idx(i) // n_t, p_idx(i) % n_t, 0)),
        pl.BlockSpec((1, POOL_BUF, bp, POOL_WIDTH), lambda i: (0, 0, 0, 0)),
        pl.BlockSpec((1, 1, GLA_HEADS, GLA_DK, GLA_DV), lambda i: (0, p_idx(i) // n_t, 0, 0, 0)),
        pl.BlockSpec((FFN_IN_CAST_ROWS, 2 * D_FF), lambda i: (jnp.minimum(p_idx(i), n_fi - 1), 0)),
        pl.BlockSpec((FFN_OUT_CAST_ROWS, d), lambda i: (jnp.minimum(p_idx(i), n_fo - 1), 0))]
    out_shape = [
        jax.ShapeDtypeStruct((bs * ts, d), F32),
        jax.ShapeDtypeStruct((1, POOL_BUF, bs, POOL_WIDTH), F32),
        jax.ShapeDtypeStruct(state_gla.shape, F32),
        jax.ShapeDtypeStruct(x_prompt.shape, F32),
        jax.ShapeDtypeStruct((1, POOL_BUF, bp, POOL_WIDTH), F32),
        jax.ShapeDtypeStruct((1, bp, GLA_HEADS, GLA_DK, GLA_DV), F32),
        jax.ShapeDtypeStruct((D_MODEL, 2 * D_FF), BF16),
        jax.ShapeDtypeStruct((D_FF, D_MODEL), BF16)]
    scratch_shapes = [
        pltpu.VMEM((D_MODEL, MAIN_COLS), BF16),
        pltpu.VMEM((D_MODEL, 2 * D_MODEL), BF16), pltpu.VMEM((LANES, GLA_KW), BF16),
        pltpu.VMEM((POOL_WIDTH, D_MODEL), BF16), pltpu.VMEM((GLA_VW, D_MODEL), BF16),
        pltpu.VMEM((D_MODEL, D_MODEL), BF16),
        pltpu.VMEM((2, STAGE_ROWS, D_MODEL), F32), pltpu.VMEM((LANES, D_MODEL), F32),
        pltpu.SemaphoreType.DMA((2,)), pltpu.SemaphoreType.DMA((1,)),
        pltpu.VMEM((POOL_PAD + TAIL_ROWS + PROMPT_TILE, POOL_WIDTH), F32),
        pltpu.VMEM((len(POOL_WINDOWS) - 1, POOL_PAD + TAIL_ROWS + PROMPT_TILE, POOL_WIDTH), F32),
        pltpu.VMEM((GLA_HEADS, GLA_DK, GLA_DV), F32),
        pltpu.VMEM((N_META, POOL_WIDTH), F32),
        pltpu.VMEM((GLA_HEADS, GLA_DK, GLA_DV), F32),
        pltpu.VMEM((nbb, SAMPLE_ROWS, D_MODEL), F32),
        pltpu.VMEM((len(POOL_WINDOWS), nbb * SAMPLE_ROWS, POOL_GROUP_DIM), F32),
        pltpu.VMEM((len(POOL_WINDOWS), nbb * SAMPLE_ROWS, POOL_GROUP_DIM), F32),
        pltpu.VMEM((nbb, GLA_HEADS * SAMPLE_ROWS, GLA_KW), BF16),
        pltpu.VMEM((nbb, SAMPLE_ROWS, GLA_KW), F32),
        pltpu.VMEM((nbb, SAMPLE_ROWS, 2 * GLA_VW), F32),
        pltpu.VMEM((nbb, GLA_HEADS * SAMPLE_ROWS, GLA_DV), F32)]

    def mixer(*refs):
        _mixer_kernel(n_s, n_t, *refs)

    x2_s, pool_s, gla_s, x2_p, pool_p, gla_p, wfi_bf, wfo_bf = pl.pallas_call(
        mixer,
        grid=(n_s + n_p,),
        in_specs=in_specs,
        out_specs=out_specs,
        out_shape=out_shape,
        scratch_shapes=scratch_shapes,
        compiler_params=pltpu.CompilerParams(dimension_semantics=("arbitrary",), vmem_limit_bytes=VMEM_LIMIT),
        name="mixer",
    )(x_sample, pool_hist, state_gla, x_prompt, *small, w_in_t, w_pool_proj, w_gla_proj, w_out, w_ffn_in, w_ffn_out)

    def ffn(*refs):
        _ffn_kernel(ts, *refs)

    def fp_idx(i):
        return jnp.maximum(i - 1, 0)

    ffn_small = (g_ffn, wfi_bf, wfo_bf, g_final.reshape(1, D_MODEL))
    y_prompt, y_sample = pl.pallas_call(
        ffn,
        grid=(n_p + 1,),
        in_specs=[pl.BlockSpec((1, FFN_TILE, d), lambda i: (fp_idx(i) // n_t, fp_idx(i) % n_t, 0)),
                  _const_spec(x2_s.shape)] + [_const_spec(a.shape) for a in ffn_small],
        out_specs=[pl.BlockSpec((1, FFN_TILE, d), lambda i: (fp_idx(i) // n_t, fp_idx(i) % n_t, 0)),
                   pl.BlockSpec(x_sample.shape, lambda i: (0, 0, 0))],
        out_shape=[jax.ShapeDtypeStruct(x_prompt.shape, F32), jax.ShapeDtypeStruct(x_sample.shape, F32)],
        compiler_params=pltpu.CompilerParams(dimension_semantics=("arbitrary",), vmem_limit_bytes=VMEM_LIMIT),
        name="ffn",
    )(x2_p, x2_s, *ffn_small)
    pool_p = jnp.transpose(pool_p, (0, 2, 1, 3))
    pool_s = jnp.transpose(pool_s, (0, 2, 1, 3))
    return y_prompt, y_sample, pool_p, gla_p, pool_s, gla_s
```

```python
import jax
import jax.numpy as jnp
from jax import lax
from jax.experimental import pallas as pl
from jax.experimental.pallas import tpu as pltpu

F32 = jnp.float32
BF16 = jnp.bfloat16

D_MODEL = 1024
N_META = 16
POOL_WIDTH = 512
POOL_WINDOWS = (2, 4, 8, 16)
POOL_GROUP_DIM = 128
POOL_BUF = 15
GLA_HEADS = 4
GLA_DV = 128
GLA_DK = 64
GLA_KW = GLA_HEADS * GLA_DK
GLA_VW = GLA_HEADS * GLA_DV
GLA_GATE_RANK = 16
GLA_TAU = 16.0
GLA_CHUNK = 64
D_FF = 2816
EPS = 1e-6

LANES = 128
SUBLANES = 8
MAIN_W = POOL_WIDTH + 2 * GLA_KW + 2 * GLA_VW
U_COL, VOG_COL, QK_COL, ZR_COL = 0, POOL_WIDTH, POOL_WIDTH + 2 * GLA_VW, MAIN_W
MAIN_COLS = MAIN_W + 128
GAB_LO = MAIN_W + GLA_GATE_RANK
IN_DIM = GAB_LO + 2 * D_MODEL
TAIL_ROWS = 16
POOL_PAD = 8

PROMPT_TILE = 512
FFN_TILE = 512
FFN_CHUNK = 256
SAMPLE_BATCH_BLOCK = 16
SAMPLE_ROWS = 8
SAMPLE_GATE_SLICES = 8
STAGE_ROWS = 256
FFN_IN_CAST_ROWS = 32
FFN_OUT_CAST_ROWS = 128
VMEM_LIMIT = 60 * 1024 * 1024


def _dot(a, b):
    return jnp.dot(a, b, preferred_element_type=F32)


def _dot_nt(a, b):
    return lax.dot_general(a, b, (((1,), (1,)), ((), ())), preferred_element_type=F32)


def _dot_tn(a, b):
    return lax.dot_general(a, b, (((0,), (0,)), ((), ())), preferred_element_type=F32)


def _rms(x, g):
    return x * lax.rsqrt(jnp.mean(x * x, axis=-1, keepdims=True) + EPS) * g


def _rms_split(x, g):
    r = lax.rsqrt(jnp.mean(x * x, axis=-1, keepdims=True) + EPS)
    return (x * g).astype(BF16), r


def _sigmoid(x):
    return 0.5 * jnp.tanh(0.5 * x) + 0.5


def _silu(x):
    half = 0.5 * x
    return half * jnp.tanh(half) + half


def _log_sigmoid(x):
    return jnp.minimum(x, 0.0) - jnp.log(1.0 + jnp.exp(-jnp.abs(x)))


def _split_bf16(x):
    hi = x.astype(BF16)
    lo = (x - hi.astype(F32)).astype(BF16)
    return hi, lo


class _Weights:
    def __init__(self, gmix, bgk, pscale, gnorm, wpg, wmain, wgab, wgk, wpp, wgp, wout):
        self.gmix, self.bgk, self.pscale, self.gnorm, self.wpg = gmix, bgk, pscale, gnorm, wpg
        self.wmain, self.wgab, self.wgk = wmain, wgab, wgk
        self.wpp, self.wgp, self.wout = wpp, wgp, wout


def _in_proj(x, w):
    h, r = _rms_split(x, w.gmix[...])
    qkz = _dot(h, w.wmain[:, QK_COL:MAIN_COLS])
    q = qkz[:, :GLA_KW] * (r * (GLA_DK ** -0.5))
    k = qkz[:, GLA_KW:2 * GLA_KW] * r
    zr = qkz[:, 2 * GLA_KW:] * r
    z = _dot(zr.astype(BF16), w.wgk[...]) + w.bgk[...]
    u = _dot(h, w.wmain[:, U_COL:VOG_COL]) * r
    vog = _dot(h, w.wmain[:, VOG_COL:QK_COL]) * r
    v = vog[:, :GLA_VW]
    og = vog[:, GLA_VW:]
    return (h, r), u, q, k, v, og, z


def _gate_proj(xn, w, lo, hi):
    h, r = xn
    return _dot(h, w.wgab[:, lo:hi]) * r


def _chunk_cumsum_wide(g, chunk):
    n = g.shape[0] // chunk
    r = lax.broadcasted_iota(jnp.int32, (chunk, chunk), 0)
    c = lax.broadcasted_iota(jnp.int32, (chunk, chunk), 1)
    tri = jnp.where(c <= r, 1.0, 0.0).astype(BF16)
    hi, lo = _split_bf16(jnp.concatenate([g[j * chunk:(j + 1) * chunk] for j in range(n)], axis=1))
    wide = _dot(tri, hi) + _dot(tri, lo)
    width = g.shape[1]
    parts = [wide[:, j * width:(j + 1) * width] for j in range(n)]
    return jnp.concatenate(parts, axis=0), [p[chunk - 1:chunk, :] for p in parts]


def _chunk_cumsum(g, chunk):
    m = g.shape[0]
    r = lax.broadcasted_iota(jnp.int32, (m, m), 0)
    c = lax.broadcasted_iota(jnp.int32, (m, m), 1)
    tri = jnp.where((r // chunk == c // chunk) & (c <= r), 1.0, 0.0).astype(BF16)
    hi, lo = _split_bf16(g)
    return _dot(tri, hi) + _dot(tri, lo)


def _head_lane_mask(width, per_head):
    lane = lax.broadcasted_iota(jnp.int32, (1, width), 1)
    return [(lane // per_head) == h for h in range(GLA_HEADS)]


def _block_diag_rows(x_bf, per_head):
    r = x_bf.shape[0]
    zero = jnp.zeros((r, per_head), x_bf.dtype)
    rows = []
    for h in range(GLA_HEADS):
        rows.append(jnp.concatenate(
            [x_bf[:, h * per_head:(h + 1) * per_head] if hh == h else zero for hh in range(GLA_HEADS)], axis=1))
    return jnp.concatenate(rows, axis=0)


def _gla_post(o, og, w):
    parts = []
    for h in range(GLA_HEADS):
        oh = o[:, h * GLA_DV:(h + 1) * GLA_DV]
        parts.append(oh * lax.rsqrt(jnp.mean(oh * oh, axis=-1, keepdims=True) + EPS) * w.gnorm[...])
    on = jnp.concatenate(parts, axis=1)
    on = on * _silu(og)
    return _dot(on.astype(BF16), w.wgp[...])


def _pool_post(pooled, w):
    pb = pooled.astype(BF16)
    mixed = jnp.concatenate(
        [_dot(pb[:, g * POOL_GROUP_DIM:(g + 1) * POOL_GROUP_DIM], w.wpg[0, g].astype(BF16))
         for g in range(len(POOL_WINDOWS))], axis=1)
    return _dot((mixed * w.pscale[...]).astype(BF16), w.wpp[...])


def _merge(x, y_a, y_b, sa, sb, w):
    merged = sa * y_a + sb * y_b
    return x + _dot(merged.astype(BF16), w.wout[...])


def _decay_columns(decay_row):
    return jnp.transpose(jnp.broadcast_to(decay_row, (LANES, decay_row.shape[1])))


def _pair_block_diag(a, b):
    zero = jnp.zeros(a.shape, a.dtype)
    return jnp.concatenate([jnp.concatenate([a, zero], axis=1), jnp.concatenate([zero, b], axis=1)], axis=0)


def _state_update(s_heads, kd_bf, v_bf, decay_row):
    dcol = _decay_columns(decay_row)
    out = []
    for p in range(GLA_HEADS // 2):
        upd = _dot_tn(kd_bf[:, 2 * p * GLA_DK:(2 * p + 2) * GLA_DK], v_bf[:, 2 * p * GLA_DV:(2 * p + 2) * GLA_DV])
        for j in range(2):
            h = 2 * p + j
            rows = slice(h * GLA_DK, (h + 1) * GLA_DK)
            out.append(dcol[rows] * s_heads[h] + upd[j * GLA_DK:(j + 1) * GLA_DK, j * GLA_DV:(j + 1) * GLA_DV])
    return out


def _stage_weights(wint_hbm, wpp_hbm, wgp_hbm, wout_hbm, wgk_ref, w, stage, zr_stage, sem, zr_sem):
    plan = []
    for r in range(0, MAIN_W, STAGE_ROWS):
        if r < POOL_WIDTH:
            col = U_COL + r
        elif r < POOL_WIDTH + 2 * GLA_KW:
            col = QK_COL + r - POOL_WIDTH
        else:
            col = VOG_COL + r - (POOL_WIDTH + 2 * GLA_KW)
        plan.append((wint_hbm, r, w.wmain, col, True))
    for r in range(0, 2 * D_MODEL, STAGE_ROWS):
        plan.append((wint_hbm, GAB_LO + r, w.wgab, r, True))
    for src, dst, n_rows in ((wpp_hbm, w.wpp, POOL_WIDTH), (wgp_hbm, w.wgp, GLA_VW), (wout_hbm, w.wout, D_MODEL)):
        for r in range(0, n_rows, STAGE_ROWS):
            plan.append((src, r, dst, r, False))

    def copy(j):
        src, r0 = plan[j][0], plan[j][1]
        return pltpu.make_async_copy(src.at[0, pl.ds(r0, STAGE_ROWS), :], stage.at[j % 2], sem.at[j % 2])

    def zr_copy():
        return pltpu.make_async_copy(
            wint_hbm.at[0, pl.ds(MAIN_W, GLA_GATE_RANK), :], zr_stage.at[pl.ds(0, GLA_GATE_RANK), :], zr_sem.at[0])

    copy(0).start()
    copy(1).start()
    zr_copy().start()

    w.wgk[...] = jnp.zeros(w.wgk.shape, BF16)
    w.wgk[0:GLA_GATE_RANK, :] = wgk_ref[0].astype(BF16)
    zr_stage[GLA_GATE_RANK:, :] = jnp.zeros((LANES - GLA_GATE_RANK, D_MODEL), F32)

    for j in range(len(plan)):
        copy(j).wait()
        _, _, dst, d0, transposed = plan[j]
        slab = stage[j % 2]
        if transposed:
            dst[:, d0:d0 + STAGE_ROWS] = jnp.transpose(slab).astype(BF16)
        else:
            dst[d0:d0 + STAGE_ROWS, :] = slab.astype(BF16)
        if j + 2 < len(plan):
            copy(j + 2).start()
    zr_copy().wait()
    w.wmain[:, ZR_COL:MAIN_COLS] = jnp.transpose(zr_stage[...]).astype(BF16)


def _prompt_tile(b_first, b_idx, t_idx, n_t, x_ref, meta_ref, w, x2_ref, pbuf_ref, sout_ref,
                 ext_ref, lvl_ref, s_ref, meta_tail_ref, meta_s_ref):
    tile = x_ref.shape[1]
    n_chunks = tile // GLA_CHUNK

    @pl.when(b_first)
    def _():
        _, u, _, k, v, _, z = _in_proj(meta_ref[...], w)
        meta_tail_ref[...] = u
        g = _log_sigmoid(z) * (1.0 / GLA_TAU)
        b = _chunk_cumsum(g, N_META)
        b_last = b[N_META - 1:N_META, :]
        kd = k * jnp.exp(b_last - b)
        zero_s = [jnp.zeros((GLA_DK, GLA_DV), F32)] * GLA_HEADS
        s_new = _state_update(zero_s, kd.astype(BF16), v.astype(BF16), jnp.exp(b_last))
        for hd in range(GLA_HEADS):
            meta_s_ref[hd] = s_new[hd]

    @pl.when(t_idx == 0)
    def _():
        ext_ref[0:POOL_PAD, :] = jnp.zeros((POOL_PAD, POOL_WIDTH), F32)
        lvl_ref[:, 0:POOL_PAD, :] = jnp.zeros((lvl_ref.shape[0], POOL_PAD, POOL_WIDTH), F32)
        ext_ref[POOL_PAD:POOL_PAD + TAIL_ROWS, :] = meta_tail_ref[...]
        s_ref[...] = meta_s_ref[...]

    x = x_ref[0]
    xn = _rms_split(x, w.gmix[...])
    h, r = xn
    gate_cols = 2 * D_MODEL // n_chunks
    gate_parts = [None] * n_chunks

    def gate_slice(c):
        gate_parts[c] = _sigmoid(_gate_proj(xn, w, c * gate_cols, (c + 1) * gate_cols))

    qkz = _dot(h, w.wmain[:, QK_COL:MAIN_COLS])
    u = _dot(h, w.wmain[:, U_COL:VOG_COL]) * r
    v = _dot(h, w.wmain[:, VOG_COL:VOG_COL + GLA_VW]) * r
    q = qkz[:, :GLA_KW] * (r * (GLA_DK ** -0.5))
    k = qkz[:, GLA_KW:2 * GLA_KW] * r
    zr = qkz[:, 2 * GLA_KW:] * r
    z = _dot(zr.astype(BF16), w.wgk[...]) + w.bgk[...]
    gate_slice(0)
    g = _log_sigmoid(z) * (1.0 / GLA_TAU)
    b, b_last_rows = _chunk_cumsum_wide(g, GLA_CHUNK)
    gate_slice(1)
    og = _dot(h, w.wmain[:, VOG_COL + GLA_VW:QK_COL]) * r

    base = POOL_PAD + TAIL_ROWS
    span = TAIL_ROWS + tile
    ext_ref[base:base + tile, :] = u
    cur = ext_ref[POOL_PAD:POOL_PAD + span, :]
    pooled = []
    for gi, win in enumerate(POOL_WINDOWS):
        shift = win // 2
        lo = gi * POOL_GROUP_DIM
        prev_ref = ext_ref if gi == 0 else lvl_ref.at[gi - 1]
        cur = cur[:, (POOL_GROUP_DIM if gi else 0):] + prev_ref[POOL_PAD - shift:POOL_PAD - shift + span, lo:]
        pooled.append(cur[TAIL_ROWS:, 0:POOL_GROUP_DIM] * (1.0 / win) - u[:, lo:lo + POOL_GROUP_DIM])
        if gi + 1 < len(POOL_WINDOWS):
            lvl_ref[gi, POOL_PAD:POOL_PAD + span, lo:] = cur
    y_a = _pool_post(jnp.concatenate(pooled, axis=1), w)
    ext_ref[POOL_PAD:base, :] = ext_ref[POOL_PAD + tile:base + tile, :]

    b_last = jnp.concatenate([jnp.broadcast_to(r, (GLA_CHUNK, GLA_KW)) for r in b_last_rows], axis=0)
    qe = (q * jnp.exp(b)).astype(BF16)
    ke = k * jnp.exp(-b)
    kd = (k * jnp.exp(b_last - b)).astype(BF16)
    v_bf = v.astype(BF16)

    k_masks = _head_lane_mask(GLA_KW, GLA_DK)
    row_i = lax.broadcasted_iota(jnp.int32, (GLA_CHUNK, GLA_KW), 0)
    col_j = lax.broadcasted_iota(jnp.int32, (GLA_CHUNK, GLA_KW), 1) % GLA_CHUNK
    causal = col_j <= row_i

    s_heads = [s_ref[hd] for hd in range(GLA_HEADS)]
    o_chunks = []
    for c in range(n_chunks):
        if c + 2 < n_chunks:
            gate_slice(c + 2)
        rows = slice(c * GLA_CHUNK, (c + 1) * GLA_CHUNK)
        ke_c = ke[rows]
        ke_bd = jnp.concatenate([jnp.where(k_masks[hd], ke_c, 0.0) for hd in range(GLA_HEADS)], axis=0).astype(BF16)
        att = jnp.where(causal, _dot_nt(qe[rows], ke_bd), 0.0).astype(BF16)
        o_pairs = []
        for p in range(GLA_HEADS // 2):
            h0, h1 = 2 * p, 2 * p + 1
            lanes_k = slice(h0 * GLA_DK, (h1 + 1) * GLA_DK)
            lanes_j = slice(h0 * GLA_CHUNK, (h1 + 1) * GLA_CHUNK)
            v0 = v_bf[rows, h0 * GLA_DV:(h0 + 1) * GLA_DV]
            v1 = v_bf[rows, h1 * GLA_DV:(h1 + 1) * GLA_DV]
            rhs = jnp.concatenate([_pair_block_diag(s_heads[h0].astype(BF16), s_heads[h1].astype(BF16)),
                                   _pair_block_diag(v0, v1)], axis=0)
            o_pairs.append(_dot(jnp.concatenate([qe[rows, lanes_k], att[:, lanes_j]], axis=1), rhs))
        o_chunks.append(jnp.concatenate(o_pairs, axis=1))
        s_heads = _state_update(s_heads, kd[rows], v_bf[rows], jnp.exp(b_last_rows[c]))
    for hd in range(GLA_HEADS):
        s_ref[hd] = s_heads[hd]

    y_b = _gla_post(jnp.concatenate(o_chunks, axis=0), og, w)
    sg = jnp.concatenate(gate_parts, axis=1)
    x2_ref[0] = _merge(x, y_a, y_b, sg[:, :D_MODEL], sg[:, D_MODEL:], w)

    @pl.when(t_idx == n_t - 1)
    def _():
        sout_ref[0, 0] = s_ref[...]

    for bb in range(pbuf_ref.shape[2]):
        @pl.when((t_idx == n_t - 1) & (b_idx == bb))
        def _():
            for r in range(POOL_BUF):
                row = base - POOL_BUF + r
                pbuf_ref[0, r, bb:bb + 1, :] = ext_ref[row:row + 1, :]


def _sample_block(x_ref, pool_ref, sin_ref, w, x2_ref, pbuf_ref, sout_ref,
                  xs_ref, us_ref, pooled_ref, qm_ref, kdx_ref, rhs_ref, oi_ref):
    nb, seq, _ = x_ref.shape
    rows_pb = SAMPLE_ROWS
    m = nb * rows_pb

    xs_ref[...] = jnp.zeros(xs_ref.shape, F32)
    xs_ref[:, 0:seq, :] = x_ref[...]
    x = xs_ref[...].reshape(m, D_MODEL)
    xn, u, q, k, v, og, z = _in_proj(x, w)

    pooled_ref[...] = jnp.zeros(pooled_ref.shape, F32)
    for gi, win in enumerate(POOL_WINDOWS):
        cols = slice(gi * POOL_GROUP_DIM, (gi + 1) * POOL_GROUP_DIM)
        us_ref[gi] = u[:, cols]
        tok = [us_ref[gi, pl.ds(t, nb, stride=rows_pb), :] for t in range(seq)]
        hist = [pool_ref[0, r, :, cols] for r in range(POOL_BUF)] + tok
        for t in range(seq):
            acc = tok[t]
            for n in range(1, win):
                acc = acc + hist[POOL_BUF + t - n]
            pooled_ref[gi, pl.ds(t, nb, stride=rows_pb), :] = acc * (1.0 / win) - tok[t]
        for r in range(POOL_BUF):
            pbuf_ref[0, r, :, cols] = hist[seq + r]
    y_a = _pool_post(jnp.concatenate([pooled_ref[gi] for gi in range(len(POOL_WINDOWS))], axis=1), w)

    r8 = lax.broadcasted_iota(jnp.int32, (m, 1), 0) % rows_pb
    g = jnp.where(r8 < seq, _log_sigmoid(z) * (1.0 / GLA_TAU), 0.0)
    b = _chunk_cumsum(g, rows_pb)
    b3 = b.reshape(nb, rows_pb, GLA_KW)
    b_last = jnp.broadcast_to(b3[:, seq - 1:seq, :], b3.shape).reshape(m, GLA_KW)
    qe = q * jnp.exp(b)
    ke = k * jnp.exp(-b)
    kd = k * jnp.exp(b_last - b)
    decay = jnp.exp(b_last)
    v_bf = v.astype(BF16)

    k_masks = _head_lane_mask(GLA_KW, GLA_DK)
    ke_bd = jnp.concatenate([jnp.where(k_masks[hd], ke, 0.0) for hd in range(GLA_HEADS)], axis=0).astype(BF16)
    row_i = lax.broadcasted_iota(jnp.int32, (m, GLA_HEADS * m), 0)
    col_j = lax.broadcasted_iota(jnp.int32, (m, GLA_HEADS * m), 1) % m
    keep = (row_i // rows_pb == col_j // rows_pb) & (col_j <= row_i)
    att = jnp.where(keep, _dot_nt(qe.astype(BF16), ke_bd), 0.0).astype(BF16)
    o_intra = _dot(att, _block_diag_rows(v_bf, GLA_DV))

    qe3 = qe.reshape(nb, rows_pb, GLA_KW)
    qm_ref[...] = jnp.concatenate([jnp.where(k_masks[hd], qe3, 0.0) for hd in range(GLA_HEADS)], axis=1).astype(BF16)
    d_hi = decay.astype(BF16).astype(F32)
    d_lo = decay - d_hi
    kdx = jnp.where(r8 == seq, d_hi, jnp.where(r8 == seq + 1, d_lo, kd))
    kdx_ref[...] = kdx.reshape(nb, rows_pb, GLA_KW)
    ones_rows = jnp.where((r8 == seq) | (r8 == seq + 1), 1.0, 0.0) + jnp.zeros((m, GLA_DV), F32)
    rhs = jnp.concatenate(
        [piece for hd in range(GLA_HEADS) for piece in (v[:, hd * GLA_DV:(hd + 1) * GLA_DV], ones_rows)], axis=1)
    rhs_ref[...] = rhs.reshape(nb, rows_pb, 2 * GLA_VW)

    def per_batch(i):
        s_all = sin_ref[0, i]
        s_flat = s_all.reshape(GLA_KW, GLA_DV).astype(BF16)
        oi_ref[i] = _dot(qm_ref[i], s_flat)
        kdt = jnp.transpose(kdx_ref[i]).astype(BF16)
        rhs_i = rhs_ref[i].astype(BF16)
        for hd in range(GLA_HEADS):
            r = _dot(kdt[hd * GLA_DK:(hd + 1) * GLA_DK], rhs_i[:, hd * 2 * GLA_DV:(hd + 1) * 2 * GLA_DV])
            sout_ref[0, i, hd] = r[:, GLA_DV:] * s_all[hd] + r[:, :GLA_DV]

    batches_per_slice = nb // SAMPLE_GATE_SLICES
    gate_cols = 2 * D_MODEL // SAMPLE_GATE_SLICES
    gate_parts = []
    for i in range(nb):
        if i % batches_per_slice == 0:
            c = i // batches_per_slice
            gate_parts.append(_sigmoid(_gate_proj(xn, w, c * gate_cols, (c + 1) * gate_cols)))
        per_batch(i)

    oi = oi_ref[...]
    o_inter = jnp.concatenate([oi[:, hd * rows_pb:(hd + 1) * rows_pb, :] for hd in range(GLA_HEADS)], axis=2)
    o = o_intra + o_inter.reshape(m, GLA_VW)
    y_b = _gla_post(o, og, w)
    sg = jnp.concatenate(gate_parts, axis=1)
    x2 = _merge(x, y_a, y_b, sg[:, :D_MODEL], sg[:, D_MODEL:], w).reshape(nb, rows_pb, D_MODEL)
    for bi in range(nb):
        x2_ref[bi * seq:(bi + 1) * seq, :] = x2[bi, 0:seq, :]


def _mixer_kernel(n_s, n_t,
                  xs_in, pool_in, s_in, xp_in, meta_ref, gmix_ref, bgk_ref, pscale_ref, gnorm_ref, wgk_ref, wpg_ref,
                  wint_hbm, wpp_hbm, wgp_hbm, wout_hbm, wfi_in, wfo_in,
                  x2s_out, pools_out, ss_out, x2p_out, poolp_out, sp_out, wfi_out, wfo_out,
                  wmain_s, wgab_s, wgk_s, wpp_s, wgp_s, wout_s, stage, zr_stage, sem, zr_sem,
                  ext_ref, lvl_ref, s_ref, meta_tail_ref, meta_s_ref,
                  xs_ref, us_ref, pooled_ref, qm_ref, kdx_ref, rhs_ref, oi_ref):
    i = pl.program_id(0)
    w = _Weights(gmix_ref, bgk_ref, pscale_ref, gnorm_ref, wpg_ref,
                 wmain_s, wgab_s, wgk_s, wpp_s, wgp_s, wout_s)

    @pl.when(i == 0)
    def _():
        _stage_weights(wint_hbm, wpp_hbm, wgp_hbm, wout_hbm, wgk_ref, w, stage, zr_stage, sem, zr_sem)

    wfi_out[...] = wfi_in[0].astype(BF16)
    wfo_out[...] = wfo_in[0].astype(BF16)

    @pl.when(i < n_s)
    def _():
        _sample_block(xs_in, pool_in, s_in, w, x2s_out, pools_out, ss_out,
                      xs_ref, us_ref, pooled_ref, qm_ref, kdx_ref, rhs_ref, oi_ref)

    @pl.when(i >= n_s)
    def _():
        t_idx = (i - n_s) % n_t
        _prompt_tile(i == n_s, (i - n_s) // n_t, t_idx, n_t, xp_in, meta_ref, w, x2p_out, poolp_out, sp_out,
                     ext_ref, lvl_ref, s_ref, meta_tail_ref, meta_s_ref)


def _ffn_kernel(seq, xp_ref, xs_ref, gffn_ref, wi_ref, wo_ref, gfin_ref, yp_ref, ys_ref):
    i = pl.program_id(0)
    x = jnp.where(i == 0, xs_ref[...], xp_ref[0])
    h, r = _rms_split(x, gffn_ref[...])
    acc = x
    for c in range(D_FF // FFN_CHUNK):
        lo = c * FFN_CHUNK
        gate = _dot(h, wi_ref[:, lo:lo + FFN_CHUNK]) * r
        up = _dot(h, wi_ref[:, D_FF + lo:D_FF + lo + FFN_CHUNK]) * r
        act = (_silu(gate) * up).astype(BF16)
        acc = acc + _dot(act, wo_ref[lo:lo + FFN_CHUNK, :])
    yp_ref[0] = _rms(acc, gfin_ref[...])

    @pl.when(i == 0)
    def _():
        for bi in range(ys_ref.shape[0]):
            ys_ref[bi] = yp_ref[0, bi * seq:(bi + 1) * seq, :]


def _const_spec(shape):
    zeros = (0,) * len(shape)
    return pl.BlockSpec(shape, lambda *_: zeros, pipeline_mode=pl.Buffered(1))


def kernel(x_prompt, x_sample, state_pool, state_gla, meta_tokens, g_mix, w_in, w_gk_up, b_gk, w_pool_group,
           pool_scale, w_pool_proj, g_gla_norm, w_gla_proj, w_out, g_ffn, w_ffn_in, w_ffn_out, g_final):
    depth = w_in.shape[0]
    assert depth == 1, "single-layer trunk only"
    bp, tp, d = x_prompt.shape
    bs, ts, _ = x_sample.shape
    nbb = SAMPLE_BATCH_BLOCK
    assert d == D_MODEL and w_in.shape == (1, D_MODEL, IN_DIM) and meta_tokens.shape == (N_META, D_MODEL)
    assert tp % PROMPT_TILE == 0 and bs % nbb == 0 and ts + 2 <= SAMPLE_ROWS and bs * ts == FFN_TILE
    n_t = tp // PROMPT_TILE
    n_p = bp * n_t
    n_s = bs // nbb
    n_fi = D_MODEL // FFN_IN_CAST_ROWS
    n_fo = D_FF // FFN_OUT_CAST_ROWS
    assert n_fi <= n_p and n_fo <= n_p

    def s_idx(i):
        return jnp.minimum(i, n_s - 1)

    def p_idx(i):
        return jnp.maximum(i - n_s, 0)

    w_in_t = jnp.transpose(w_in, (0, 2, 1))
    pool_hist = jnp.transpose(state_pool, (0, 2, 1, 3))
    small = (meta_tokens, g_mix, b_gk, pool_scale, g_gla_norm, w_gk_up, w_pool_group)
    hbm = pl.BlockSpec(memory_space=pl.ANY)
    in_specs = (
        [pl.BlockSpec((nbb, ts, d), lambda i: (s_idx(i), 0, 0)),
         pl.BlockSpec((1, POOL_BUF, nbb, POOL_WIDTH), lambda i: (0, 0, s_idx(i), 0)),
         pl.BlockSpec((1, nbb, GLA_HEADS, GLA_DK, GLA_DV), lambda i: (0, s_idx(i), 0, 0, 0)),
         pl.BlockSpec((1, PROMPT_TILE, d), lambda i: (p_idx(i) // n_t, p_idx(i) % n_t, 0))]
        + [_const_spec(a.shape) for a in small]
        + [hbm, hbm, hbm, hbm,
           pl.BlockSpec((1, FFN_IN_CAST_ROWS, 2 * D_FF), lambda i: (0, jnp.minimum(p_idx(i), n_fi - 1), 0)),
           pl.BlockSpec((1, FFN_OUT_CAST_ROWS, d), lambda i: (0, jnp.minimum(p_idx(i), n_fo - 1), 0))])
    out_specs = [
        pl.BlockSpec((nbb * ts, d), lambda i: (s_idx(i), 0)),
        pl.BlockSpec((1, POOL_BUF, nbb, POOL_WIDTH), lambda i: (0, 0, s_idx(i), 0)),
        pl.BlockSpec((1, nbb, GLA_HEADS, GLA_DK, GLA_DV), lambda i: (0, s_idx(i), 0, 0, 0)),
        pl.BlockSpec((1, PROMPT_TILE, d), lambda i: (p_idx(i) // n_t, p_idx(i) % n_t, 0)),
        pl.BlockSpec((1, POOL_BUF, bp, POOL_WIDTH), lambda i: (0, 0, 0, 0)),
        pl.BlockSpec((1, 1, GLA_HEADS, GLA_DK, GLA_DV), lambda i: (0, p_idx(i) // n_t, 0, 0, 0)),
        pl.BlockSpec((FFN_IN_CAST_ROWS, 2 * D_FF), lambda i: (jnp.minimum(p_idx(i), n_fi - 1), 0)),
        pl.BlockSpec((FFN_OUT_CAST_ROWS, d), lambda i: (jnp.minimum(p_idx(i), n_fo - 1), 0))]
    out_shape = [
        jax.ShapeDtypeStruct((bs * ts, d), F32),
        jax.ShapeDtypeStruct((1, POOL_BUF, bs, POOL_WIDTH), F32),
        jax.ShapeDtypeStruct(state_gla.shape, F32),
        jax.ShapeDtypeStruct(x_prompt.shape, F32),
        jax.ShapeDtypeStruct((1, POOL_BUF, bp, POOL_WIDTH), F32),
        jax.ShapeDtypeStruct((1, bp, GLA_HEADS, GLA_DK, GLA_DV), F32),
        jax.ShapeDtypeStruct((D_MODEL, 2 * D_FF), BF16),
        jax.ShapeDtypeStruct((D_FF, D_MODEL), BF16)]
    scratch_shapes = [
        pltpu.VMEM((D_MODEL, MAIN_COLS), BF16),
        pltpu.VMEM((D_MODEL, 2 * D_MODEL), BF16), pltpu.VMEM((LANES, GLA_KW), BF16),
        pltpu.VMEM((POOL_WIDTH, D_MODEL), BF16), pltpu.VMEM((GLA_VW, D_MODEL), BF16),
        pltpu.VMEM((D_MODEL, D_MODEL), BF16),
        pltpu.VMEM((2, STAGE_ROWS, D_MODEL), F32), pltpu.VMEM((LANES, D_MODEL), F32),
        pltpu.SemaphoreType.DMA((2,)), pltpu.SemaphoreType.DMA((1,)),
        pltpu.VMEM((POOL_PAD + TAIL_ROWS + PROMPT_TILE, POOL_WIDTH), F32),
        pltpu.VMEM((len(POOL_WINDOWS) - 1, POOL_PAD + TAIL_ROWS + PROMPT_TILE, POOL_WIDTH), F32),
        pltpu.VMEM((GLA_HEADS, GLA_DK, GLA_DV), F32),
        pltpu.VMEM((N_META, POOL_WIDTH), F32),
        pltpu.VMEM((GLA_HEADS, GLA_DK, GLA_DV), F32),
        pltpu.VMEM((nbb, SAMPLE_ROWS, D_MODEL), F32),
        pltpu.VMEM((len(POOL_WINDOWS), nbb * SAMPLE_ROWS, POOL_GROUP_DIM), F32),
        pltpu.VMEM((len(POOL_WINDOWS), nbb * SAMPLE_ROWS, POOL_GROUP_DIM), F32),
        pltpu.VMEM((nbb, GLA_HEADS * SAMPLE_ROWS, GLA_KW), BF16),
        pltpu.VMEM((nbb, SAMPLE_ROWS, GLA_KW), F32),
        pltpu.VMEM((nbb, SAMPLE_ROWS, 2 * GLA_VW), F32),
        pltpu.VMEM((nbb, GLA_HEADS * SAMPLE_ROWS, GLA_DV), F32)]

    def mixer(*refs):
        _mixer_kernel(n_s, n_t, *refs)

    x2_s, pool_s, gla_s, x2_p, pool_p, gla_p, wfi_bf, wfo_bf = pl.pallas_call(
        mixer,
        grid=(n_s + n_p,),
        in_specs=in_specs,
        out_specs=out_specs,
        out_shape=out_shape,
        scratch_shapes=scratch_shapes,
        compiler_params=pltpu.CompilerParams(dimension_semantics=("arbitrary",), vmem_limit_bytes=VMEM_LIMIT),
        name="mixer",
    )(x_sample, pool_hist, state_gla, x_prompt, *small, w_in_t, w_pool_proj, w_gla_proj, w_out, w_ffn_in, w_ffn_out)

    def ffn(*refs):
        _ffn_kernel(ts, *refs)

    def fp_idx(i):
        return jnp.maximum(i - 1, 0)

    ffn_small = (g_ffn, wfi_bf, wfo_bf, g_final.reshape(1, D_MODEL))
    y_prompt, y_sample = pl.pallas_call(
        ffn,
        grid=(n_p + 1,),
        in_specs=[pl.BlockSpec((1, FFN_TILE, d), lambda i: (fp_idx(i) // n_t, fp_idx(i) % n_t, 0)),
                  _const_spec(x2_s.shape)] + [_const_spec(a.shape) for a in ffn_small],
        out_specs=[pl.BlockSpec((1, FFN_TILE, d), lambda i: (fp_idx(i) // n_t, fp_idx(i) % n_t, 0)),
                   pl.BlockSpec(x_sample.shape, lambda i: (0, 0, 0))],
        out_shape=[jax.ShapeDtypeStruct(x_prompt.shape, F32), jax.ShapeDtypeStruct(x_sample.shape, F32)],
        compiler_params=pltpu.CompilerParams(dimension_semantics=("arbitrary",), vmem_limit_bytes=VMEM_LIMIT),
        name="ffn",
    )(x2_p, x2_s, *ffn_small)
    pool_p = jnp.transpose(pool_p, (0, 2, 1, 3))
    pool_s = jnp.transpose(pool_s, (0, 2, 1, 3))
    return y_prompt, y_sample, pool_p, gla_p, pool_s, gla_s
```

```python
import jax
import jax.numpy as jnp
from jax import lax
from jax.experimental import pallas as pl
from jax.experimental.pallas import tpu as pltpu

F32 = jnp.float32
BF16 = jnp.bfloat16

D_MODEL = 1024
N_META = 16
POOL_WIDTH = 512
POOL_WINDOWS = (2, 4, 8, 16)
POOL_GROUP_DIM = 128
POOL_BUF = 15
GLA_HEADS = 4
GLA_DV = 128
GLA_DK = 64
GLA_KW = GLA_HEADS * GLA_DK
GLA_VW = GLA_HEADS * GLA_DV
GLA_GATE_RANK = 16
GLA_TAU = 16.0
GLA_CHUNK = 64
D_FF = 2816
EPS = 1e-6

LANES = 128
SUBLANES = 8
MAIN_W = POOL_WIDTH + 2 * GLA_KW + 2 * GLA_VW
U_COL, VOG_COL, QK_COL, ZR_COL = 0, POOL_WIDTH, POOL_WIDTH + 2 * GLA_VW, MAIN_W
MAIN_COLS = MAIN_W + 128
GAB_LO = MAIN_W + GLA_GATE_RANK
IN_DIM = GAB_LO + 2 * D_MODEL
TAIL_ROWS = 16
POOL_PAD = 8

PROMPT_TILE = 512
FFN_TILE = 512
FFN_CHUNK = 256
SAMPLE_BATCH_BLOCK = 16
SAMPLE_ROWS = 8
SAMPLE_GATE_SLICES = 8
STAGE_ROWS = 256
FFN_IN_CAST_ROWS = 32
FFN_OUT_CAST_ROWS = 128
VMEM_LIMIT = 60 * 1024 * 1024


def _dot(a, b):
    return jnp.dot(a, b, preferred_element_type=F32)


def _dot_nt(a, b):
    return lax.dot_general(a, b, (((1,), (1,)), ((), ())), preferred_element_type=F32)


def _dot_tn(a, b):
    return lax.dot_general(a, b, (((0,), (0,)), ((), ())), preferred_element_type=F32)


def _rms(x, g):
    return x * lax.rsqrt(jnp.mean(x * x, axis=-1, keepdims=True) + EPS) * g


def _rms_split(x, g):
    r = lax.rsqrt(jnp.mean(x * x, axis=-1, keepdims=True) + EPS)
    return (x * g).astype(BF16), r


def _sigmoid(x):
    return 0.5 * jnp.tanh(0.5 * x) + 0.5


def _silu(x):
    half = 0.5 * x
    return half * jnp.tanh(half) + half


def _log_sigmoid(x):
    return jnp.minimum(x, 0.0) - jnp.log(1.0 + jnp.exp(-jnp.abs(x)))


def _split_bf16(x):
    hi = x.astype(BF16)
    lo = (x - hi.astype(F32)).astype(BF16)
    return hi, lo


class _Weights:
    def __init__(self, gmix, bgk, pscale, gnorm, wpg, wmain, wgab, wgk, wpp, wgp, wout):
        self.gmix, self.bgk, self.pscale, self.gnorm, self.wpg = gmix, bgk, pscale, gnorm, wpg
        self.wmain, self.wgab, self.wgk = wmain, wgab, wgk
        self.wpp, self.wgp, self.wout = wpp, wgp, wout


def _in_proj(x, w):
    h, r = _rms_split(x, w.gmix[...])
    qkz = _dot(h, w.wmain[:, QK_COL:MAIN_COLS])
    q = qkz[:, :GLA_KW] * (r * (GLA_DK ** -0.5))
    k = qkz[:, GLA_KW:2 * GLA_KW] * r
    zr = qkz[:, 2 * GLA_KW:] * r
    z = _dot(zr.astype(BF16), w.wgk[...]) + w.bgk[...]
    u = _dot(h, w.wmain[:, U_COL:VOG_COL]) * r
    vog = _dot(h, w.wmain[:, VOG_COL:QK_COL]) * r
    v = vog[:, :GLA_VW]
    og = vog[:, GLA_VW:]
    return (h, r), u, q, k, v, og, z


def _gate_proj(xn, w, lo, hi):
    h, r = xn
    return _dot(h, w.wgab[:, lo:hi]) * r


def _chunk_cumsum_wide(g, chunk):
    n = g.shape[0] // chunk
    r = lax.broadcasted_iota(jnp.int32, (chunk, chunk), 0)
    c = lax.broadcasted_iota(jnp.int32, (chunk, chunk), 1)
    tri = jnp.where(c <= r, 1.0, 0.0).astype(BF16)
    hi, lo = _split_bf16(jnp.concatenate([g[j * chunk:(j + 1) * chunk] for j in range(n)], axis=1))
    wide = _dot(tri, hi) + _dot(tri, lo)
    width = g.shape[1]
    parts = [wide[:, j * width:(j + 1) * width] for j in range(n)]
    return jnp.concatenate(parts, axis=0), [p[chunk - 1:chunk, :] for p in parts]


def _chunk_cumsum(g, chunk):
    m = g.shape[0]
    r = lax.broadcasted_iota(jnp.int32, (m, m), 0)
    c = lax.broadcasted_iota(jnp.int32, (m, m), 1)
    tri = jnp.where((r // chunk == c // chunk) & (c <= r), 1.0, 0.0).astype(BF16)
    hi, lo = _split_bf16(g)
    return _dot(tri, hi) + _dot(tri, lo)


def _head_lane_mask(width, per_head):
    lane = lax.broadcasted_iota(jnp.int32, (1, width), 1)
    return [(lane // per_head) == h for h in range(GLA_HEADS)]


def _block_diag_rows(x_bf, per_head):
    r = x_bf.shape[0]
    zero = jnp.zeros((r, per_head), x_bf.dtype)
    rows = []
    for h in range(GLA_HEADS):
        rows.append(jnp.concatenate(
            [x_bf[:, h * per_head:(h + 1) * per_head] if hh == h else zero for hh in range(GLA_HEADS)], axis=1))
    return jnp.concatenate(rows, axis=0)


def _gla_post(o, og, w):
    parts = []
    for h in range(GLA_HEADS):
        oh = o[:, h * GLA_DV:(h + 1) * GLA_DV]
        parts.append(oh * lax.rsqrt(jnp.mean(oh * oh, axis=-1, keepdims=True) + EPS) * w.gnorm[...])
    on = jnp.concatenate(parts, axis=1)
    on = on * _silu(og)
    return _dot(on.astype(BF16), w.wgp[...])


def _pool_post(pooled, w):
    pb = pooled.astype(BF16)
    mixed = []
    for p in range(len(POOL_WINDOWS) // 2):
        w_pair = _pair_block_diag(w.wpg[0, 2 * p].astype(BF16), w.wpg[0, 2 * p + 1].astype(BF16))
        mixed.append(_dot(pb[:, 2 * p * POOL_GROUP_DIM:(2 * p + 2) * POOL_GROUP_DIM], w_pair))
    mixed = jnp.concatenate(mixed, axis=1)
    return _dot((mixed * w.pscale[...]).astype(BF16), w.wpp[...])


def _merge(x, y_a, y_b, sa, sb, w):
    merged = sa * y_a + sb * y_b
    return x + _dot(merged.astype(BF16), w.wout[...])


def _decay_columns(decay_row):
    return jnp.transpose(jnp.broadcast_to(decay_row, (LANES, decay_row.shape[1])))


def _pair_block_diag(a, b):
    zero = jnp.zeros(a.shape, a.dtype)
    return jnp.concatenate([jnp.concatenate([a, zero], axis=1), jnp.concatenate([zero, b], axis=1)], axis=0)


def _state_update(s_heads, kd_bf, v_bf, decay_row):
    dcol = _decay_columns(decay_row)
    out = []
    for p in range(GLA_HEADS // 2):
        upd = _dot_tn(kd_bf[:, 2 * p * GLA_DK:(2 * p + 2) * GLA_DK], v_bf[:, 2 * p * GLA_DV:(2 * p + 2) * GLA_DV])
        for j in range(2):
            h = 2 * p + j
            rows = slice(h * GLA_DK, (h + 1) * GLA_DK)
            out.append(dcol[rows] * s_heads[h] + upd[j * GLA_DK:(j + 1) * GLA_DK, j * GLA_DV:(j + 1) * GLA_DV])
    return out


def _stage_weights(wint_hbm, wpp_hbm, wgp_hbm, wout_hbm, wgk_ref, w, stage, zr_stage, sem, zr_sem):
    plan = []
    for r in range(0, MAIN_W, STAGE_ROWS):
        if r < POOL_WIDTH:
            col = U_COL + r
        elif r < POOL_WIDTH + 2 * GLA_KW:
            col = QK_COL + r - POOL_WIDTH
        else:
            col = VOG_COL + r - (POOL_WIDTH + 2 * GLA_KW)
        plan.append((wint_hbm, r, w.wmain, col, True))
    for r in range(0, 2 * D_MODEL, STAGE_ROWS):
        plan.append((wint_hbm, GAB_LO + r, w.wgab, r, True))
    for src, dst, n_rows in ((wpp_hbm, w.wpp, POOL_WIDTH), (wgp_hbm, w.wgp, GLA_VW), (wout_hbm, w.wout, D_MODEL)):
        for r in range(0, n_rows, STAGE_ROWS):
            plan.append((src, r, dst, r, False))

    def copy(j):
        src, r0 = plan[j][0], plan[j][1]
        return pltpu.make_async_copy(src.at[0, pl.ds(r0, STAGE_ROWS), :], stage.at[j % 2], sem.at[j % 2])

    def zr_copy():
        return pltpu.make_async_copy(
            wint_hbm.at[0, pl.ds(MAIN_W, GLA_GATE_RANK), :], zr_stage.at[pl.ds(0, GLA_GATE_RANK), :], zr_sem.at[0])

    copy(0).start()
    copy(1).start()
    zr_copy().start()

    w.wgk[...] = jnp.zeros(w.wgk.shape, BF16)
    w.wgk[0:GLA_GATE_RANK, :] = wgk_ref[0].astype(BF16)
    zr_stage[GLA_GATE_RANK:, :] = jnp.zeros((LANES - GLA_GATE_RANK, D_MODEL), F32)

    for j in range(len(plan)):
        copy(j).wait()
        _, _, dst, d0, transposed = plan[j]
        slab = stage[j % 2]
        if transposed:
            dst[:, d0:d0 + STAGE_ROWS] = jnp.transpose(slab).astype(BF16)
        else:
            dst[d0:d0 + STAGE_ROWS, :] = slab.astype(BF16)
        if j + 2 < len(plan):
            copy(j + 2).start()
    zr_copy().wait()
    w.wmain[:, ZR_COL:MAIN_COLS] = jnp.transpose(zr_stage[...]).astype(BF16)


def _prompt_tile(b_first, b_idx, t_idx, n_t, x_ref, meta_ref, w, x2_ref, pbuf_ref, sout_ref,
                 ext_ref, lvl_ref, s_ref, meta_tail_ref, meta_s_ref):
    tile = x_ref.shape[1]
    n_chunks = tile // GLA_CHUNK

    @pl.when(b_first)
    def _():
        _, u, _, k, v, _, z = _in_proj(meta_ref[...], w)
        meta_tail_ref[...] = u
        g = _log_sigmoid(z) * (1.0 / GLA_TAU)
        b = _chunk_cumsum(g, N_META)
        b_last = b[N_META - 1:N_META, :]
        kd = k * jnp.exp(b_last - b)
        zero_s = [jnp.zeros((GLA_DK, GLA_DV), F32)] * GLA_HEADS
        s_new = _state_update(zero_s, kd.astype(BF16), v.astype(BF16), jnp.exp(b_last))
        for hd in range(GLA_HEADS):
            meta_s_ref[hd] = s_new[hd]

    @pl.when(t_idx == 0)
    def _():
        ext_ref[0:POOL_PAD, :] = jnp.zeros((POOL_PAD, POOL_WIDTH), F32)
        lvl_ref[:, 0:POOL_PAD, :] = jnp.zeros((lvl_ref.shape[0], POOL_PAD, POOL_WIDTH), F32)
        ext_ref[POOL_PAD:POOL_PAD + TAIL_ROWS, :] = meta_tail_ref[...]
        s_ref[...] = meta_s_ref[...]

    x = x_ref[0]
    xn = _rms_split(x, w.gmix[...])
    h, r = xn
    gate_cols = 2 * D_MODEL // n_chunks
    gate_parts = [None] * n_chunks

    def gate_slice(c):
        gate_parts[c] = _sigmoid(_gate_proj(xn, w, c * gate_cols, (c + 1) * gate_cols))

    qkz = _dot(h, w.wmain[:, QK_COL:MAIN_COLS])
    u = _dot(h, w.wmain[:, U_COL:VOG_COL]) * r
    v = _dot(h, w.wmain[:, VOG_COL:VOG_COL + GLA_VW]) * r
    q = qkz[:, :GLA_KW] * (r * (GLA_DK ** -0.5))
    k = qkz[:, GLA_KW:2 * GLA_KW] * r
    zr = qkz[:, 2 * GLA_KW:] * r
    z = _dot(zr.astype(BF16), w.wgk[...]) + w.bgk[...]
    gate_slice(0)
    g = _log_sigmoid(z) * (1.0 / GLA_TAU)
    b, b_last_rows = _chunk_cumsum_wide(g, GLA_CHUNK)
    gate_slice(1)
    og_parts = []

    base = POOL_PAD + TAIL_ROWS
    span = TAIL_ROWS + tile
    ext_ref[base:base + tile, :] = u
    cur = ext_ref[POOL_PAD:POOL_PAD + span, :]
    pooled = []
    for gi, win in enumerate(POOL_WINDOWS):
        shift = win // 2
        lo = gi * POOL_GROUP_DIM
        prev_ref = ext_ref if gi == 0 else lvl_ref.at[gi - 1]
        cur = cur[:, (POOL_GROUP_DIM if gi else 0):] + prev_ref[POOL_PAD - shift:POOL_PAD - shift + span, lo:]
        pooled.append(cur[TAIL_ROWS:, 0:POOL_GROUP_DIM] * (1.0 / win) - u[:, lo:lo + POOL_GROUP_DIM])
        if gi + 1 < len(POOL_WINDOWS):
            lvl_ref[gi, POOL_PAD:POOL_PAD + span, lo:] = cur
    y_a = _pool_post(jnp.concatenate(pooled, axis=1), w)
    ext_ref[POOL_PAD:base, :] = ext_ref[POOL_PAD + tile:base + tile, :]

    b_last = jnp.concatenate([jnp.broadcast_to(r, (GLA_CHUNK, GLA_KW)) for r in b_last_rows], axis=0)
    qe = (q * jnp.exp(b)).astype(BF16)
    ke = k * jnp.exp(-b)
    kd = (k * jnp.exp(b_last - b)).astype(BF16)
    v_bf = v.astype(BF16)

    k_masks = _head_lane_mask(GLA_KW, GLA_DK)
    row_i = lax.broadcasted_iota(jnp.int32, (GLA_CHUNK, GLA_KW), 0)
    col_j = lax.broadcasted_iota(jnp.int32, (GLA_CHUNK, GLA_KW), 1) % GLA_CHUNK
    causal = col_j <= row_i

    s_heads = [s_ref[hd] for hd in range(GLA_HEADS)]
    o_chunks = []
    for c in range(n_chunks):
        if c + 2 < n_chunks:
            gate_slice(c + 2)
        else:
            og_lo = VOG_COL + GLA_VW + len(og_parts) * (GLA_VW // 2)
            og_parts.append(_dot(h, w.wmain[:, og_lo:og_lo + GLA_VW // 2]) * r)
        rows = slice(c * GLA_CHUNK, (c + 1) * GLA_CHUNK)
        ke_c = ke[rows]
        ke_bd = jnp.concatenate([jnp.where(k_masks[hd], ke_c, 0.0) for hd in range(GLA_HEADS)], axis=0).astype(BF16)
        att = jnp.where(causal, _dot_nt(qe[rows], ke_bd), 0.0).astype(BF16)
        o_pairs = []
        for p in range(GLA_HEADS // 2):
            h0, h1 = 2 * p, 2 * p + 1
            lanes_k = slice(h0 * GLA_DK, (h1 + 1) * GLA_DK)
            lanes_j = slice(h0 * GLA_CHUNK, (h1 + 1) * GLA_CHUNK)
            v0 = v_bf[rows, h0 * GLA_DV:(h0 + 1) * GLA_DV]
            v1 = v_bf[rows, h1 * GLA_DV:(h1 + 1) * GLA_DV]
            rhs = jnp.concatenate([_pair_block_diag(s_heads[h0].astype(BF16), s_heads[h1].astype(BF16)),
                                   _pair_block_diag(v0, v1)], axis=0)
            o_pairs.append(_dot(jnp.concatenate([qe[rows, lanes_k], att[:, lanes_j]], axis=1), rhs))
        o_chunks.append(jnp.concatenate(o_pairs, axis=1))
        s_heads = _state_update(s_heads, kd[rows], v_bf[rows], jnp.exp(b_last_rows[c]))
    for hd in range(GLA_HEADS):
        s_ref[hd] = s_heads[hd]

    y_b = _gla_post(jnp.concatenate(o_chunks, axis=0), jnp.concatenate(og_parts, axis=1), w)
    sg = jnp.concatenate(gate_parts, axis=1)
    x2_ref[0] = _merge(x, y_a, y_b, sg[:, :D_MODEL], sg[:, D_MODEL:], w)

    @pl.when(t_idx == n_t - 1)
    def _():
        sout_ref[0, 0] = s_ref[...]

    for bb in range(pbuf_ref.shape[2]):
        @pl.when((t_idx == n_t - 1) & (b_idx == bb))
        def _():
            for r in range(POOL_BUF):
                row = base - POOL_BUF + r
                pbuf_ref[0, r, bb:bb + 1, :] = ext_ref[row:row + 1, :]


def _sample_block(x_ref, pool_ref, sin_ref, w, x2_ref, pbuf_ref, sout_ref,
                  xs_ref, us_ref, pooled_ref, qm_ref, kdx_ref, rhs_ref, oi_ref):
    nb, seq, _ = x_ref.shape
    rows_pb = SAMPLE_ROWS
    m = nb * rows_pb

    xs_ref[...] = jnp.zeros(xs_ref.shape, F32)
    xs_ref[:, 0:seq, :] = x_ref[...]
    x = xs_ref[...].reshape(m, D_MODEL)
    xn, u, q, k, v, og, z = _in_proj(x, w)

    pooled_ref[...] = jnp.zeros(pooled_ref.shape, F32)
    for gi, win in enumerate(POOL_WINDOWS):
        cols = slice(gi * POOL_GROUP_DIM, (gi + 1) * POOL_GROUP_DIM)
        us_ref[gi] = u[:, cols]
        tok = [us_ref[gi, pl.ds(t, nb, stride=rows_pb), :] for t in range(seq)]
        hist = [pool_ref[0, r, :, cols] for r in range(POOL_BUF)] + tok
        for t in range(seq):
            acc = tok[t]
            for n in range(1, win):
                acc = acc + hist[POOL_BUF + t - n]
            pooled_ref[gi, pl.ds(t, nb, stride=rows_pb), :] = acc * (1.0 / win) - tok[t]
        for r in range(POOL_BUF):
            pbuf_ref[0, r, :, cols] = hist[seq + r]
    y_a = _pool_post(jnp.concatenate([pooled_ref[gi] for gi in range(len(POOL_WINDOWS))], axis=1), w)

    r8 = lax.broadcasted_iota(jnp.int32, (m, 1), 0) % rows_pb
    g = jnp.where(r8 < seq, _log_sigmoid(z) * (1.0 / GLA_TAU), 0.0)
    b = _chunk_cumsum(g, rows_pb)
    b3 = b.reshape(nb, rows_pb, GLA_KW)
    b_last = jnp.broadcast_to(b3[:, seq - 1:seq, :], b3.shape).reshape(m, GLA_KW)
    qe = q * jnp.exp(b)
    ke = k * jnp.exp(-b)
    kd = k * jnp.exp(b_last - b)
    decay = jnp.exp(b_last)
    v_bf = v.astype(BF16)

    k_masks = _head_lane_mask(GLA_KW, GLA_DK)
    ke_bd = jnp.concatenate([jnp.where(k_masks[hd], ke, 0.0) for hd in range(GLA_HEADS)], axis=0).astype(BF16)
    row_i = lax.broadcasted_iota(jnp.int32, (m, GLA_HEADS * m), 0)
    col_j = lax.broadcasted_iota(jnp.int32, (m, GLA_HEADS * m), 1) % m
    keep = (row_i // rows_pb == col_j // rows_pb) & (col_j <= row_i)
    att = jnp.where(keep, _dot_nt(qe.astype(BF16), ke_bd), 0.0).astype(BF16)
    o_intra = _dot(att, _block_diag_rows(v_bf, GLA_DV))

    qe3 = qe.reshape(nb, rows_pb, GLA_KW)
    qm_ref[...] = jnp.concatenate([jnp.where(k_masks[hd], qe3, 0.0) for hd in range(GLA_HEADS)], axis=1).astype(BF16)
    d_hi = decay.astype(BF16).astype(F32)
    d_lo = decay - d_hi
    kdx = jnp.where(r8 == seq, d_hi, jnp.where(r8 == seq + 1, d_lo, kd))
    kdx_ref[...] = kdx.reshape(nb, rows_pb, GLA_KW)
    ones_rows = jnp.where((r8 == seq) | (r8 == seq + 1), 1.0, 0.0) + jnp.zeros((m, GLA_DV), F32)
    rhs = jnp.concatenate(
        [piece for hd in range(GLA_HEADS) for piece in (v[:, hd * GLA_DV:(hd + 1) * GLA_DV], ones_rows)], axis=1)
    rhs_ref[...] = rhs.reshape(nb, rows_pb, 2 * GLA_VW)

    def per_batch(i):
        s_all = sin_ref[0, i]
        s_flat = s_all.reshape(GLA_KW, GLA_DV).astype(BF16)
        oi_ref[i] = _dot(qm_ref[i], s_flat)
        kdt = jnp.transpose(kdx_ref[i]).astype(BF16)
        rhs_i = rhs_ref[i].astype(BF16)
        for hd in range(GLA_HEADS):
            r = _dot(kdt[hd * GLA_DK:(hd + 1) * GLA_DK], rhs_i[:, hd * 2 * GLA_DV:(hd + 1) * 2 * GLA_DV])
            sout_ref[0, i, hd] = r[:, GLA_DV:] * s_all[hd] + r[:, :GLA_DV]

    batches_per_slice = nb // SAMPLE_GATE_SLICES
    gate_cols = 2 * D_MODEL // SAMPLE_GATE_SLICES
    gate_parts = []
    for i in range(nb):
        if i % batches_per_slice == 0:
            c = i // batches_per_slice
            gate_parts.append(_sigmoid(_gate_proj(xn, w, c * gate_cols, (c + 1) * gate_cols)))
        per_batch(i)

    oi = oi_ref[...]
    o_inter = jnp.concatenate([oi[:, hd * rows_pb:(hd + 1) * rows_pb, :] for hd in range(GLA_HEADS)], axis=2)
    o = o_intra + o_inter.reshape(m, GLA_VW)
    y_b = _gla_post(o, og, w)
    sg = jnp.concatenate(gate_parts, axis=1)
    x2 = _merge(x, y_a, y_b, sg[:, :D_MODEL], sg[:, D_MODEL:], w).reshape(nb, rows_pb, D_MODEL)
    for bi in range(nb):
        x2_ref[bi * seq:(bi + 1) * seq, :] = x2[bi, 0:seq, :]


def _mixer_kernel(n_s, n_t,
                  xs_in, pool_in, s_in, xp_in, meta_ref, gmix_ref, bgk_ref, pscale_ref, gnorm_ref, wgk_ref, wpg_ref,
                  wint_hbm, wpp_hbm, wgp_hbm, wout_hbm, wfi_in, wfo_in,
                  x2s_out, pools_out, ss_out, x2p_out, poolp_out, sp_out, wfi_out, wfo_out,
                  wmain_s, wgab_s, wgk_s, wpp_s, wgp_s, wout_s, stage, zr_stage, sem, zr_sem,
                  ext_ref, lvl_ref, s_ref, meta_tail_ref, meta_s_ref,
                  xs_ref, us_ref, pooled_ref, qm_ref, kdx_ref, rhs_ref, oi_ref):
    i = pl.program_id(0)
    w = _Weights(gmix_ref, bgk_ref, pscale_ref, gnorm_ref, wpg_ref,
                 wmain_s, wgab_s, wgk_s, wpp_s, wgp_s, wout_s)

    @pl.when(i == 0)
    def _():
        _stage_weights(wint_hbm, wpp_hbm, wgp_hbm, wout_hbm, wgk_ref, w, stage, zr_stage, sem, zr_sem)

    wfi_out[...] = wfi_in[0].astype(BF16)
    wfo_out[...] = wfo_in[0].astype(BF16)

    @pl.when(i < n_s)
    def _():
        _sample_block(xs_in, pool_in, s_in, w, x2s_out, pools_out, ss_out,
                      xs_ref, us_ref, pooled_ref, qm_ref, kdx_ref, rhs_ref, oi_ref)

    @pl.when(i >= n_s)
    def _():
        t_idx = (i - n_s) % n_t
        _prompt_tile(i == n_s, (i - n_s) // n_t, t_idx, n_t, xp_in, meta_ref, w, x2p_out, poolp_out, sp_out,
                     ext_ref, lvl_ref, s_ref, meta_tail_ref, meta_s_ref)


def _ffn_kernel(seq, xp_ref, xs_ref, gffn_ref, wi_ref, wo_ref, gfin_ref, yp_ref, ys_ref):
    i = pl.program_id(0)
    x = jnp.where(i == 0, xs_ref[...], xp_ref[0])
    h, r = _rms_split(x, gffn_ref[...])
    acc = x
    for c in range(D_FF // FFN_CHUNK):
        lo = c * FFN_CHUNK
        gate = _dot(h, wi_ref[:, lo:lo + FFN_CHUNK]) * r
        up = _dot(h, wi_ref[:, D_FF + lo:D_FF + lo + FFN_CHUNK]) * r
        act = (_silu(gate) * up).astype(BF16)
        acc = acc + _dot(act, wo_ref[lo:lo + FFN_CHUNK, :])
    yp_ref[0] = _rms(acc, gfin_ref[...])

    @pl.when(i == 0)
    def _():
        for bi in range(ys_ref.shape[0]):
            ys_ref[bi] = yp_ref[0, bi * seq:(bi + 1) * seq, :]


def _const_spec(shape):
    zeros = (0,) * len(shape)
    return pl.BlockSpec(shape, lambda *_: zeros, pipeline_mode=pl.Buffered(1))


def kernel(x_prompt, x_sample, state_pool, state_gla, meta_tokens, g_mix, w_in, w_gk_up, b_gk, w_pool_group,
           pool_scale, w_pool_proj, g_gla_norm, w_gla_proj, w_out, g_ffn, w_ffn_in, w_ffn_out, g_final):
    depth = w_in.shape[0]
    assert depth == 1, "single-layer trunk only"
    bp, tp, d = x_prompt.shape
    bs, ts, _ = x_sample.shape
    nbb = SAMPLE_BATCH_BLOCK
    assert d == D_MODEL and w_in.shape == (1, D_MODEL, IN_DIM) and meta_tokens.shape == (N_META, D_MODEL)
    assert tp % PROMPT_TILE == 0 and bs % nbb == 0 and ts + 2 <= SAMPLE_ROWS and bs * ts == FFN_TILE
    n_t = tp // PROMPT_TILE
    n_p = bp * n_t
    n_s = bs // nbb
    n_fi = D_MODEL // FFN_IN_CAST_ROWS
    n_fo = D_FF // FFN_OUT_CAST_ROWS
    assert n_fi <= n_p and n_fo <= n_p

    def s_idx(i):
        return jnp.minimum(i, n_s - 1)

    def p_idx(i):
        return jnp.maximum(i - n_s, 0)

    w_in_t = jnp.transpose(w_in, (0, 2, 1))
    pool_hist = jnp.transpose(state_pool, (0, 2, 1, 3))
    small = (meta_tokens, g_mix, b_gk, pool_scale, g_gla_norm, w_gk_up, w_pool_group)
    hbm = pl.BlockSpec(memory_space=pl.ANY)
    in_specs = (
        [pl.BlockSpec((nbb, ts, d), lambda i: (s_idx(i), 0, 0)),
         pl.BlockSpec((1, POOL_BUF, nbb, POOL_WIDTH), lambda i: (0, 0, s_idx(i), 0)),
         pl.BlockSpec((1, nbb, GLA_HEADS, GLA_DK, GLA_DV), lambda i: (0, s_idx(i), 0, 0, 0)),
         pl.BlockSpec((1, PROMPT_TILE, d), lambda i: (p_idx(i) // n_t, p_idx(i) % n_t, 0))]
        + [_const_spec(a.shape) for a in small]
        + [hbm, hbm, hbm, hbm,
           pl.BlockSpec((1, FFN_IN_CAST_ROWS, 2 * D_FF), lambda i: (0, jnp.minimum(p_idx(i), n_fi - 1), 0)),
           pl.BlockSpec((1, FFN_OUT_CAST_ROWS, d), lambda i: (0, jnp.minimum(p_idx(i), n_fo - 1), 0))])
    out_specs = [
        pl.BlockSpec((nbb * ts, d), lambda i: (s_idx(i), 0)),
        pl.BlockSpec((1, POOL_BUF, nbb, POOL_WIDTH), lambda i: (0, 0, s_idx(i), 0)),
        pl.BlockSpec((1, nbb, GLA_HEADS, GLA_DK, GLA_DV), lambda i: (0, s_idx(i), 0, 0, 0)),
        pl.BlockSpec((1, PROMPT_TILE, d), lambda i: (p_idx(i) // n_t, p_idx(i) % n_t, 0)),
        pl.BlockSpec((1, POOL_BUF, bp, POOL_WIDTH), lambda i: (0, 0, 0, 0)),
        pl.BlockSpec((1, 1, GLA_HEADS, GLA_DK, GLA_DV), lambda i: (0, p_idx(i) // n_t, 0, 0, 0)),
        pl.BlockSpec((FFN_IN_CAST_ROWS, 2 * D_FF), lambda i: (jnp.minimum(p_idx(i), n_fi - 1), 0)),
        pl.BlockSpec((FFN_OUT_CAST_ROWS, d), lambda i: (jnp.minimum(p_idx(i), n_fo - 1), 0))]
    out_shape = [
        jax.ShapeDtypeStruct((bs * ts, d), F32),
        jax.ShapeDtypeStruct((1, POOL_BUF, bs, POOL_WIDTH), F32),
        jax.ShapeDtypeStruct(state_gla.shape, F32),
        jax.ShapeDtypeStruct(x_prompt.shape, F32),
        jax.ShapeDtypeStruct((1, POOL_BUF, bp, POOL_WIDTH), F32),
        jax.ShapeDtypeStruct((1, bp, GLA_HEADS, GLA_DK, GLA_DV), F32),
        jax.ShapeDtypeStruct((D_MODEL, 2 * D_FF), BF16),
        jax.ShapeDtypeStruct((D_FF, D_MODEL), BF16)]
    scratch_shapes = [
        pltpu.VMEM((D_MODEL, MAIN_COLS), BF16),
        pltpu.VMEM((D_MODEL, 2 * D_MODEL), BF16), pltpu.VMEM((LANES, GLA_KW), BF16),
        pltpu.VMEM((POOL_WIDTH, D_MODEL), BF16), pltpu.VMEM((GLA_VW, D_MODEL), BF16),
        pltpu.VMEM((D_MODEL, D_MODEL), BF16),
        pltpu.VMEM((2, STAGE_ROWS, D_MODEL), F32), pltpu.VMEM((LANES, D_MODEL), F32),
        pltpu.SemaphoreType.DMA((2,)), pltpu.SemaphoreType.DMA((1,)),
        pltpu.VMEM((POOL_PAD + TAIL_ROWS + PROMPT_TILE, POOL_WIDTH), F32),
        pltpu.VMEM((len(POOL_WINDOWS) - 1, POOL_PAD + TAIL_ROWS + PROMPT_TILE, POOL_WIDTH), F32),
        pltpu.VMEM((GLA_HEADS, GLA_DK, GLA_DV), F32),
        pltpu.VMEM((N_META, POOL_WIDTH), F32),
        pltpu.VMEM((GLA_HEADS, GLA_DK, GLA_DV), F32),
        pltpu.VMEM((nbb, SAMPLE_ROWS, D_MODEL), F32),
        pltpu.VMEM((len(POOL_WINDOWS), nbb * SAMPLE_ROWS, POOL_GROUP_DIM), F32),
        pltpu.VMEM((len(POOL_WINDOWS), nbb * SAMPLE_ROWS, POOL_GROUP_DIM), F32),
        pltpu.VMEM((nbb, GLA_HEADS * SAMPLE_ROWS, GLA_KW), BF16),
        pltpu.VMEM((nbb, SAMPLE_ROWS, GLA_KW), F32),
        pltpu.VMEM((nbb, SAMPLE_ROWS, 2 * GLA_VW), F32),
        pltpu.VMEM((nbb, GLA_HEADS * SAMPLE_ROWS, GLA_DV), F32)]

    def mixer(*refs):
        _mixer_kernel(n_s, n_t, *refs)

    x2_s, pool_s, gla_s, x2_p, pool_p, gla_p, wfi_bf, wfo_bf = pl.pallas_call(
        mixer,
        grid=(n_s + n_p,),
        in_specs=in_specs,
        out_specs=out_specs,
        out_shape=out_shape,
        scratch_shapes=scratch_shapes,
        compiler_params=pltpu.CompilerParams(dimension_semantics=("arbitrary",), vmem_limit_bytes=VMEM_LIMIT),
        name="mixer",
    )(x_sample, pool_hist, state_gla, x_prompt, *small, w_in_t, w_pool_proj, w_gla_proj, w_out, w_ffn_in, w_ffn_out)

    def ffn(*refs):
        _ffn_kernel(ts, *refs)

    def fp_idx(i):
        return jnp.maximum(i - 1, 0)

    ffn_small = (g_ffn, wfi_bf, wfo_bf, g_final.reshape(1, D_MODEL))
    y_prompt, y_sample = pl.pallas_call(
        ffn,
        grid=(n_p + 1,),
        in_specs=[pl.BlockSpec((1, FFN_TILE, d), lambda i: (fp_idx(i) // n_t, fp_idx(i) % n_t, 0)),
                  _const_spec(x2_s.shape)] + [_const_spec(a.shape) for a in ffn_small],
        out_specs=[pl.BlockSpec((1, FFN_TILE, d), lambda i: (fp_idx(i) // n_t, fp_idx(i) % n_t, 0)),
                   pl.BlockSpec(x_sample.shape, lambda i: (0, 0, 0))],
        out_shape=[jax.ShapeDtypeStruct(x_prompt.shape, F32), jax.ShapeDtypeStruct(x_sample.shape, F32)],
        compiler_params=pltpu.CompilerParams(dimension_semantics=("arbitrary",), vmem_limit_bytes=VMEM_LIMIT),
        name="ffn",
    )(x2_p, x2_s, *ffn_small)
    pool_p = jnp.transpose(pool_p, (0, 2, 1, 3))
    pool_s = jnp.transpose(pool_s, (0, 2, 1, 3))
    return y_prompt, y_sample, pool_p, gla_p, pool_s, gla_s
```

```python
import jax
import jax.numpy as jnp
from jax import lax
from jax.experimental import pallas as pl
from jax.experimental.pallas import tpu as pltpu

F32 = jnp.float32
BF16 = jnp.bfloat16

D_MODEL = 1024
N_META = 16
POOL_WIDTH = 512
POOL_WINDOWS = (2, 4, 8, 16)
POOL_GROUP_DIM = 128
POOL_BUF = 15
GLA_HEADS = 4
GLA_DV = 128
GLA_DK = 64
GLA_KW = GLA_HEADS * GLA_DK
GLA_VW = GLA_HEADS * GLA_DV
GLA_GATE_RANK = 16
GLA_TAU = 16.0
GLA_CHUNK = 64
D_FF = 2816
EPS = 1e-6

LANES = 128
SUBLANES = 8
MAIN_W = POOL_WIDTH + 2 * GLA_KW + 2 * GLA_VW
U_COL, VOG_COL, QK_COL, ZR_COL = 0, POOL_WIDTH, POOL_WIDTH + 2 * GLA_VW, MAIN_W
MAIN_COLS = MAIN_W + 128
GAB_LO = MAIN_W + GLA_GATE_RANK
IN_DIM = GAB_LO + 2 * D_MODEL
TAIL_ROWS = 16
POOL_PAD = 8

PROMPT_TILE = 512
FFN_TILE = 512
FFN_CHUNK = 256
FFN_OUT_GROUP = 4
SAMPLE_BATCH_BLOCK = 16
SAMPLE_ROWS = 8
SAMPLE_GATE_SLICES = 8
STAGE_ROWS = 256
FFN_IN_CAST_ROWS = 32
FFN_OUT_CAST_ROWS = 128
VMEM_LIMIT = 60 * 1024 * 1024


def _dot(a, b):
    return jnp.dot(a, b, preferred_element_type=F32)


def _dot_nt(a, b):
    return lax.dot_general(a, b, (((1,), (1,)), ((), ())), preferred_element_type=F32)


def _dot_tn(a, b):
    return lax.dot_general(a, b, (((0,), (0,)), ((), ())), preferred_element_type=F32)


def _rms(x, g):
    return x * lax.rsqrt(jnp.mean(x * x, axis=-1, keepdims=True) + EPS) * g


def _rms_split(x, g):
    r = lax.rsqrt(jnp.mean(x * x, axis=-1, keepdims=True) + EPS)
    return (x * g).astype(BF16), r


def _sigmoid(x):
    return 0.5 * jnp.tanh(0.5 * x) + 0.5


def _silu(x):
    half = 0.5 * x
    return half * jnp.tanh(half) + half


def _log_sigmoid(x):
    return jnp.minimum(x, 0.0) - jnp.log(1.0 + jnp.exp(-jnp.abs(x)))


def _split_bf16(x):
    hi = x.astype(BF16)
    lo = (x - hi.astype(F32)).astype(BF16)
    return hi, lo


class _Weights:
    def __init__(self, gmix, bgk, pscale, gnorm, wpg, wmain, wgab, wgk, wpp, wgp, wout):
        self.gmix, self.bgk, self.pscale, self.gnorm, self.wpg = gmix, bgk, pscale, gnorm, wpg
        self.wmain, self.wgab, self.wgk = wmain, wgab, wgk
        self.wpp, self.wgp, self.wout = wpp, wgp, wout


def _in_proj(x, w):
    h, r = _rms_split(x, w.gmix[...])
    qkz = _dot(h, w.wmain[:, QK_COL:MAIN_COLS])
    q = qkz[:, :GLA_KW] * (r * (GLA_DK ** -0.5))
    k = qkz[:, GLA_KW:2 * GLA_KW] * r
    zr = qkz[:, 2 * GLA_KW:] * r
    z = _dot(zr.astype(BF16), w.wgk[...]) + w.bgk[...]
    u = _dot(h, w.wmain[:, U_COL:VOG_COL]) * r
    vog = _dot(h, w.wmain[:, VOG_COL:QK_COL]) * r
    v = vog[:, :GLA_VW]
    og = vog[:, GLA_VW:]
    return (h, r), u, q, k, v, og, z


def _gate_proj(xn, w, lo, hi):
    h, r = xn
    return _dot(h, w.wgab[:, lo:hi]) * r


def _chunk_cumsum_wide(g, chunk):
    n = g.shape[0] // chunk
    r = lax.broadcasted_iota(jnp.int32, (chunk, chunk), 0)
    c = lax.broadcasted_iota(jnp.int32, (chunk, chunk), 1)
    tri = jnp.where(c <= r, 1.0, 0.0).astype(BF16)
    hi, lo = _split_bf16(jnp.concatenate([g[j * chunk:(j + 1) * chunk] for j in range(n)], axis=1))
    wide = _dot(tri, hi) + _dot(tri, lo)
    width = g.shape[1]
    parts = [wide[:, j * width:(j + 1) * width] for j in range(n)]
    return jnp.concatenate(parts, axis=0), [p[chunk - 1:chunk, :] for p in parts]


def _chunk_cumsum(g, chunk):
    m = g.shape[0]
    r = lax.broadcasted_iota(jnp.int32, (m, m), 0)
    c = lax.broadcasted_iota(jnp.int32, (m, m), 1)
    tri = jnp.where((r // chunk == c // chunk) & (c <= r), 1.0, 0.0).astype(BF16)
    hi, lo = _split_bf16(g)
    return _dot(tri, hi) + _dot(tri, lo)


def _head_lane_mask(width, per_head):
    lane = lax.broadcasted_iota(jnp.int32, (1, width), 1)
    return [(lane // per_head) == h for h in range(GLA_HEADS)]


def _block_diag_rows(x_bf, per_head):
    r = x_bf.shape[0]
    zero = jnp.zeros((r, per_head), x_bf.dtype)
    rows = []
    for h in range(GLA_HEADS):
        rows.append(jnp.concatenate(
            [x_bf[:, h * per_head:(h + 1) * per_head] if hh == h else zero for hh in range(GLA_HEADS)], axis=1))
    return jnp.concatenate(rows, axis=0)


def _gla_post(o, og, w):
    parts = []
    for h in range(GLA_HEADS):
        oh = o[:, h * GLA_DV:(h + 1) * GLA_DV]
        parts.append(oh * lax.rsqrt(jnp.mean(oh * oh, axis=-1, keepdims=True) + EPS) * w.gnorm[...])
    on = jnp.concatenate(parts, axis=1)
    on = on * _silu(og)
    return _dot(on.astype(BF16), w.wgp[...])


def _pool_post(pooled, w):
    pb = pooled.astype(BF16)
    mixed = []
    for p in range(len(POOL_WINDOWS) // 2):
        w_pair = _pair_block_diag(w.wpg[0, 2 * p].astype(BF16), w.wpg[0, 2 * p + 1].astype(BF16))
        mixed.append(_dot(pb[:, 2 * p * POOL_GROUP_DIM:(2 * p + 2) * POOL_GROUP_DIM], w_pair))
    mixed = jnp.concatenate(mixed, axis=1)
    return _dot((mixed * w.pscale[...]).astype(BF16), w.wpp[...])


def _merge(x, y_a, y_b, sa, sb, w):
    merged = sa * y_a + sb * y_b
    return x + _dot(merged.astype(BF16), w.wout[...])


def _decay_columns(decay_row):
    return jnp.transpose(jnp.broadcast_to(decay_row, (LANES, decay_row.shape[1])))


def _pair_block_diag(a, b):
    zero = jnp.zeros(a.shape, a.dtype)
    return jnp.concatenate([jnp.concatenate([a, zero], axis=1), jnp.concatenate([zero, b], axis=1)], axis=0)


def _state_update(s_heads, kd_bf, v_bf, decay_row):
    dcol = _decay_columns(decay_row)
    out = []
    for p in range(GLA_HEADS // 2):
        upd = _dot_tn(kd_bf[:, 2 * p * GLA_DK:(2 * p + 2) * GLA_DK], v_bf[:, 2 * p * GLA_DV:(2 * p + 2) * GLA_DV])
        for j in range(2):
            h = 2 * p + j
            rows = slice(h * GLA_DK, (h + 1) * GLA_DK)
            out.append(dcol[rows] * s_heads[h] + upd[j * GLA_DK:(j + 1) * GLA_DK, j * GLA_DV:(j + 1) * GLA_DV])
    return out


def _stage_weights(wint_hbm, wpp_hbm, wgp_hbm, wout_hbm, wgk_ref, w, stage, zr_stage, sem, zr_sem):
    plan = []
    for r in range(0, MAIN_W, STAGE_ROWS):
        if r < POOL_WIDTH:
            col = U_COL + r
        elif r < POOL_WIDTH + 2 * GLA_KW:
            col = QK_COL + r - POOL_WIDTH
        else:
            col = VOG_COL + r - (POOL_WIDTH + 2 * GLA_KW)
        plan.append((wint_hbm, r, w.wmain, col, True))
    for r in range(0, 2 * D_MODEL, STAGE_ROWS):
        plan.append((wint_hbm, GAB_LO + r, w.wgab, r, True))
    for src, dst, n_rows in ((wpp_hbm, w.wpp, POOL_WIDTH), (wgp_hbm, w.wgp, GLA_VW), (wout_hbm, w.wout, D_MODEL)):
        for r in range(0, n_rows, STAGE_ROWS):
            plan.append((src, r, dst, r, False))

    def copy(j):
        src, r0 = plan[j][0], plan[j][1]
        return pltpu.make_async_copy(src.at[0, pl.ds(r0, STAGE_ROWS), :], stage.at[j % 2], sem.at[j % 2])

    def zr_copy():
        return pltpu.make_async_copy(
            wint_hbm.at[0, pl.ds(MAIN_W, GLA_GATE_RANK), :], zr_stage.at[pl.ds(0, GLA_GATE_RANK), :], zr_sem.at[0])

    copy(0).start()
    copy(1).start()
    zr_copy().start()

    w.wgk[...] = jnp.zeros(w.wgk.shape, BF16)
    w.wgk[0:GLA_GATE_RANK, :] = wgk_ref[0].astype(BF16)
    zr_stage[GLA_GATE_RANK:, :] = jnp.zeros((LANES - GLA_GATE_RANK, D_MODEL), F32)

    for j in range(len(plan)):
        copy(j).wait()
        _, _, dst, d0, transposed = plan[j]
        slab = stage[j % 2]
        if transposed:
            dst[:, d0:d0 + STAGE_ROWS] = jnp.transpose(slab).astype(BF16)
        else:
            dst[d0:d0 + STAGE_ROWS, :] = slab.astype(BF16)
        if j + 2 < len(plan):
            copy(j + 2).start()
    zr_copy().wait()
    w.wmain[:, ZR_COL:MAIN_COLS] = jnp.transpose(zr_stage[...]).astype(BF16)


def _prompt_tile(b_first, b_idx, t_idx, n_t, x_ref, meta_ref, w, x2_ref, pbuf_ref, sout_ref,
                 ext_ref, lvl_ref, s_ref, meta_tail_ref, meta_s_ref):
    tile = x_ref.shape[1]
    n_chunks = tile // GLA_CHUNK

    @pl.when(b_first)
    def _():
        _, u, _, k, v, _, z = _in_proj(meta_ref[...], w)
        meta_tail_ref[...] = u
        g = _log_sigmoid(z) * (1.0 / GLA_TAU)
        b = _chunk_cumsum(g, N_META)
        b_last = b[N_META - 1:N_META, :]
        kd = k * jnp.exp(b_last - b)
        zero_s = [jnp.zeros((GLA_DK, GLA_DV), F32)] * GLA_HEADS
        s_new = _state_update(zero_s, kd.astype(BF16), v.astype(BF16), jnp.exp(b_last))
        for hd in range(GLA_HEADS):
            meta_s_ref[hd] = s_new[hd]

    @pl.when(t_idx == 0)
    def _():
        ext_ref[0:POOL_PAD, :] = jnp.zeros((POOL_PAD, POOL_WIDTH), F32)
        lvl_ref[:, 0:POOL_PAD, :] = jnp.zeros((lvl_ref.shape[0], POOL_PAD, POOL_WIDTH), F32)
        ext_ref[POOL_PAD:POOL_PAD + TAIL_ROWS, :] = meta_tail_ref[...]
        s_ref[...] = meta_s_ref[...]

    x = x_ref[0]
    xn = _rms_split(x, w.gmix[...])
    h, r = xn
    gate_cols = 2 * D_MODEL // n_chunks
    gate_parts = [None] * n_chunks

    def gate_slice(c):
        gate_parts[c] = _sigmoid(_gate_proj(xn, w, c * gate_cols, (c + 1) * gate_cols))

    qkz = _dot(h, w.wmain[:, QK_COL:MAIN_COLS])
    u = _dot(h, w.wmain[:, U_COL:VOG_COL]) * r
    v = _dot(h, w.wmain[:, VOG_COL:VOG_COL + GLA_VW]) * r
    q = qkz[:, :GLA_KW] * (r * (GLA_DK ** -0.5))
    k = qkz[:, GLA_KW:2 * GLA_KW] * r
    zr = qkz[:, 2 * GLA_KW:] * r
    z = _dot(zr.astype(BF16), w.wgk[...]) + w.bgk[...]
    gate_slice(0)
    g = _log_sigmoid(z) * (1.0 / GLA_TAU)
    b, b_last_rows = _chunk_cumsum_wide(g, GLA_CHUNK)
    gate_slice(1)
    og_parts = []

    base = POOL_PAD + TAIL_ROWS
    span = TAIL_ROWS + tile
    ext_ref[base:base + tile, :] = u
    cur = ext_ref[POOL_PAD:POOL_PAD + span, :]
    pooled = []
    for gi, win in enumerate(POOL_WINDOWS):
        shift = win // 2
        lo = gi * POOL_GROUP_DIM
        prev_ref = ext_ref if gi == 0 else lvl_ref.at[gi - 1]
        cur = cur[:, (POOL_GROUP_DIM if gi else 0):] + prev_ref[POOL_PAD - shift:POOL_PAD - shift + span, lo:]
        pooled.append(cur[TAIL_ROWS:, 0:POOL_GROUP_DIM] * (1.0 / win) - u[:, lo:lo + POOL_GROUP_DIM])
        if gi + 1 < len(POOL_WINDOWS):
            lvl_ref[gi, POOL_PAD:POOL_PAD + span, lo:] = cur
    y_a = _pool_post(jnp.concatenate(pooled, axis=1), w)
    ext_ref[POOL_PAD:base, :] = ext_ref[POOL_PAD + tile:base + tile, :]

    b_last = jnp.concatenate([jnp.broadcast_to(r, (GLA_CHUNK, GLA_KW)) for r in b_last_rows], axis=0)
    qe = (q * jnp.exp(b)).astype(BF16)
    ke = k * jnp.exp(-b)
    kd = (k * jnp.exp(b_last - b)).astype(BF16)
    v_bf = v.astype(BF16)

    k_masks = _head_lane_mask(GLA_KW, GLA_DK)
    row_i = lax.broadcasted_iota(jnp.int32, (GLA_CHUNK, GLA_KW), 0)
    col_j = lax.broadcasted_iota(jnp.int32, (GLA_CHUNK, GLA_KW), 1) % GLA_CHUNK
    causal = col_j <= row_i

    s_heads = [s_ref[hd] for hd in range(GLA_HEADS)]
    o_chunks = []
    for c in range(n_chunks):
        if c + 2 < n_chunks:
            gate_slice(c + 2)
        else:
            og_lo = VOG_COL + GLA_VW + len(og_parts) * (GLA_VW // 2)
            og_parts.append(_dot(h, w.wmain[:, og_lo:og_lo + GLA_VW // 2]) * r)
        rows = slice(c * GLA_CHUNK, (c + 1) * GLA_CHUNK)
        ke_c = ke[rows]
        ke_bd = jnp.concatenate([jnp.where(k_masks[hd], ke_c, 0.0) for hd in range(GLA_HEADS)], axis=0).astype(BF16)
        att = jnp.where(causal, _dot_nt(qe[rows], ke_bd), 0.0).astype(BF16)
        o_pairs = []
        for p in range(GLA_HEADS // 2):
            h0, h1 = 2 * p, 2 * p + 1
            lanes_k = slice(h0 * GLA_DK, (h1 + 1) * GLA_DK)
            lanes_j = slice(h0 * GLA_CHUNK, (h1 + 1) * GLA_CHUNK)
            v0 = v_bf[rows, h0 * GLA_DV:(h0 + 1) * GLA_DV]
            v1 = v_bf[rows, h1 * GLA_DV:(h1 + 1) * GLA_DV]
            rhs = jnp.concatenate([_pair_block_diag(s_heads[h0].astype(BF16), s_heads[h1].astype(BF16)),
                                   _pair_block_diag(v0, v1)], axis=0)
            o_pairs.append(_dot(jnp.concatenate([qe[rows, lanes_k], att[:, lanes_j]], axis=1), rhs))
        o_chunks.append(jnp.concatenate(o_pairs, axis=1))
        s_heads = _state_update(s_heads, kd[rows], v_bf[rows], jnp.exp(b_last_rows[c]))
    for hd in range(GLA_HEADS):
        s_ref[hd] = s_heads[hd]

    y_b = _gla_post(jnp.concatenate(o_chunks, axis=0), jnp.concatenate(og_parts, axis=1), w)
    sg = jnp.concatenate(gate_parts, axis=1)
    x2_ref[0] = _merge(x, y_a, y_b, sg[:, :D_MODEL], sg[:, D_MODEL:], w)

    @pl.when(t_idx == n_t - 1)
    def _():
        sout_ref[0, 0] = s_ref[...]

    for bb in range(pbuf_ref.shape[2]):
        @pl.when((t_idx == n_t - 1) & (b_idx == bb))
        def _():
            for r in range(POOL_BUF):
                row = base - POOL_BUF + r
                pbuf_ref[0, r, bb:bb + 1, :] = ext_ref[row:row + 1, :]


def _sample_block(x_ref, pool_ref, sin_ref, w, x2_ref, pbuf_ref, sout_ref,
                  xs_ref, us_ref, pooled_ref, qm_ref, kdx_ref, rhs_ref, oi_ref):
    nb, seq, _ = x_ref.shape
    rows_pb = SAMPLE_ROWS
    m = nb * rows_pb

    xs_ref[...] = jnp.zeros(xs_ref.shape, F32)
    xs_ref[:, 0:seq, :] = x_ref[...]
    x = xs_ref[...].reshape(m, D_MODEL)
    xn, u, q, k, v, og, z = _in_proj(x, w)

    pooled_ref[...] = jnp.zeros(pooled_ref.shape, F32)
    for gi, win in enumerate(POOL_WINDOWS):
        cols = slice(gi * POOL_GROUP_DIM, (gi + 1) * POOL_GROUP_DIM)
        us_ref[gi] = u[:, cols]
        tok = [us_ref[gi, pl.ds(t, nb, stride=rows_pb), :] for t in range(seq)]
        hist = [pool_ref[0, r, :, cols] for r in range(POOL_BUF)] + tok
        for t in range(seq):
            acc = tok[t]
            for n in range(1, win):
                acc = acc + hist[POOL_BUF + t - n]
            pooled_ref[gi, pl.ds(t, nb, stride=rows_pb), :] = acc * (1.0 / win) - tok[t]
        for r in range(POOL_BUF):
            pbuf_ref[0, r, :, cols] = hist[seq + r]
    y_a = _pool_post(jnp.concatenate([pooled_ref[gi] for gi in range(len(POOL_WINDOWS))], axis=1), w)

    r8 = lax.broadcasted_iota(jnp.int32, (m, 1), 0) % rows_pb
    g = jnp.where(r8 < seq, _log_sigmoid(z) * (1.0 / GLA_TAU), 0.0)
    b = _chunk_cumsum(g, rows_pb)
    b3 = b.reshape(nb, rows_pb, GLA_KW)
    b_last = jnp.broadcast_to(b3[:, seq - 1:seq, :], b3.shape).reshape(m, GLA_KW)
    qe = q * jnp.exp(b)
    ke = k * jnp.exp(-b)
    kd = k * jnp.exp(b_last - b)
    decay = jnp.exp(b_last)
    v_bf = v.astype(BF16)

    k_masks = _head_lane_mask(GLA_KW, GLA_DK)
    ke_bd = jnp.concatenate([jnp.where(k_masks[hd], ke, 0.0) for hd in range(GLA_HEADS)], axis=0).astype(BF16)
    row_i = lax.broadcasted_iota(jnp.int32, (m, GLA_HEADS * m), 0)
    col_j = lax.broadcasted_iota(jnp.int32, (m, GLA_HEADS * m), 1) % m
    keep = (row_i // rows_pb == col_j // rows_pb) & (col_j <= row_i)
    att = jnp.where(keep, _dot_nt(qe.astype(BF16), ke_bd), 0.0).astype(BF16)
    o_intra = _dot(att, _block_diag_rows(v_bf, GLA_DV))

    qe3 = qe.reshape(nb, rows_pb, GLA_KW)
    qm_ref[...] = jnp.concatenate([jnp.where(k_masks[hd], qe3, 0.0) for hd in range(GLA_HEADS)], axis=1).astype(BF16)
    d_hi = decay.astype(BF16).astype(F32)
    d_lo = decay - d_hi
    kdx = jnp.where(r8 == seq, d_hi, jnp.where(r8 == seq + 1, d_lo, kd))
    kdx_ref[...] = kdx.reshape(nb, rows_pb, GLA_KW)
    ones_rows = jnp.where((r8 == seq) | (r8 == seq + 1), 1.0, 0.0) + jnp.zeros((m, GLA_DV), F32)
    rhs = jnp.concatenate(
        [piece for hd in range(GLA_HEADS) for piece in (v[:, hd * GLA_DV:(hd + 1) * GLA_DV], ones_rows)], axis=1)
    rhs_ref[...] = rhs.reshape(nb, rows_pb, 2 * GLA_VW)

    def per_batch(i):
        s_all = sin_ref[0, i]
        s_flat = s_all.reshape(GLA_KW, GLA_DV).astype(BF16)
        oi_ref[i] = _dot(qm_ref[i], s_flat)
        kdt = jnp.transpose(kdx_ref[i]).astype(BF16)
        rhs_i = rhs_ref[i].astype(BF16)
        for hd in range(GLA_HEADS):
            r = _dot(kdt[hd * GLA_DK:(hd + 1) * GLA_DK], rhs_i[:, hd * 2 * GLA_DV:(hd + 1) * 2 * GLA_DV])
            sout_ref[0, i, hd] = r[:, GLA_DV:] * s_all[hd] + r[:, :GLA_DV]

    batches_per_slice = nb // SAMPLE_GATE_SLICES
    gate_cols = 2 * D_MODEL // SAMPLE_GATE_SLICES
    gate_parts = []
    for i in range(nb):
        if i % batches_per_slice == 0:
            c = i // batches_per_slice
            gate_parts.append(_sigmoid(_gate_proj(xn, w, c * gate_cols, (c + 1) * gate_cols)))
        per_batch(i)

    oi = oi_ref[...]
    o_inter = jnp.concatenate([oi[:, hd * rows_pb:(hd + 1) * rows_pb, :] for hd in range(GLA_HEADS)], axis=2)
    o = o_intra + o_inter.reshape(m, GLA_VW)
    y_b = _gla_post(o, og, w)
    sg = jnp.concatenate(gate_parts, axis=1)
    x2 = _merge(x, y_a, y_b, sg[:, :D_MODEL], sg[:, D_MODEL:], w).reshape(nb, rows_pb, D_MODEL)
    for bi in range(nb):
        x2_ref[bi * seq:(bi + 1) * seq, :] = x2[bi, 0:seq, :]


def _mixer_kernel(n_s, n_t,
                  xs_in, pool_in, s_in, xp_in, meta_ref, gmix_ref, bgk_ref, pscale_ref, gnorm_ref, wgk_ref, wpg_ref,
                  wint_hbm, wpp_hbm, wgp_hbm, wout_hbm, wfi_in, wfo_in,
                  x2s_out, pools_out, ss_out, x2p_out, poolp_out, sp_out, wfi_out, wfo_out,
                  wmain_s, wgab_s, wgk_s, wpp_s, wgp_s, wout_s, stage, zr_stage, sem, zr_sem,
                  ext_ref, lvl_ref, s_ref, meta_tail_ref, meta_s_ref,
                  xs_ref, us_ref, pooled_ref, qm_ref, kdx_ref, rhs_ref, oi_ref):
    i = pl.program_id(0)
    w = _Weights(gmix_ref, bgk_ref, pscale_ref, gnorm_ref, wpg_ref,
                 wmain_s, wgab_s, wgk_s, wpp_s, wgp_s, wout_s)

    @pl.when(i == 0)
    def _():
        _stage_weights(wint_hbm, wpp_hbm, wgp_hbm, wout_hbm, wgk_ref, w, stage, zr_stage, sem, zr_sem)

    wfi_out[...] = wfi_in[0].astype(BF16)
    wfo_out[...] = wfo_in[0].astype(BF16)

    @pl.when(i < n_s)
    def _():
        _sample_block(xs_in, pool_in, s_in, w, x2s_out, pools_out, ss_out,
                      xs_ref, us_ref, pooled_ref, qm_ref, kdx_ref, rhs_ref, oi_ref)

    @pl.when(i >= n_s)
    def _():
        t_idx = (i - n_s) % n_t
        _prompt_tile(i == n_s, (i - n_s) // n_t, t_idx, n_t, xp_in, meta_ref, w, x2p_out, poolp_out, sp_out,
                     ext_ref, lvl_ref, s_ref, meta_tail_ref, meta_s_ref)


def _ffn_kernel(seq, xp_ref, xs_ref, gffn_ref, wi_ref, wo_ref, gfin_ref, yp_ref, ys_ref):
    i = pl.program_id(0)
    x = jnp.where(i == 0, xs_ref[...], xp_ref[0])
    h, r = _rms_split(x, gffn_ref[...])
    acc = x
    group = []
    for lo in range(0, D_FF, FFN_CHUNK):
        gate = _dot(h, wi_ref[:, lo:lo + FFN_CHUNK]) * r
        up = _dot(h, wi_ref[:, D_FF + lo:D_FF + lo + FFN_CHUNK]) * r
        group.append((_silu(gate) * up).astype(BF16))
        if len(group) == FFN_OUT_GROUP or lo + FFN_CHUNK == D_FF:
            first = lo + FFN_CHUNK - len(group) * FFN_CHUNK
            acc = acc + _dot(jnp.concatenate(group, axis=1), wo_ref[first:lo + FFN_CHUNK, :])
            group = []
    yp_ref[0] = _rms(acc, gfin_ref[...])

    @pl.when(i == 0)
    def _():
        for bi in range(ys_ref.shape[0]):
            ys_ref[bi] = yp_ref[0, bi * seq:(bi + 1) * seq, :]


def _const_spec(shape):
    zeros = (0,) * len(shape)
    return pl.BlockSpec(shape, lambda *_: zeros, pipeline_mode=pl.Buffered(1))


def kernel(x_prompt, x_sample, state_pool, state_gla, meta_tokens, g_mix, w_in, w_gk_up, b_gk, w_pool_group,
           pool_scale, w_pool_proj, g_gla_norm, w_gla_proj, w_out, g_ffn, w_ffn_in, w_ffn_out, g_final):
    depth = w_in.shape[0]
    assert depth == 1, "single-layer trunk only"
    bp, tp, d = x_prompt.shape
    bs, ts, _ = x_sample.shape
    nbb = SAMPLE_BATCH_BLOCK
    assert d == D_MODEL and w_in.shape == (1, D_MODEL, IN_DIM) and meta_tokens.shape == (N_META, D_MODEL)
    assert tp % PROMPT_TILE == 0 and bs % nbb == 0 and ts + 2 <= SAMPLE_ROWS and bs * ts == FFN_TILE
    n_t = tp // PROMPT_TILE
    n_p = bp * n_t
    n_s = bs // nbb
    n_fi = D_MODEL // FFN_IN_CAST_ROWS
    n_fo = D_FF // FFN_OUT_CAST_ROWS
    assert n_fi <= n_p and n_fo <= n_p

    def s_idx(i):
        return jnp.minimum(i, n_s - 1)

    def p_idx(i):
        return jnp.maximum(i - n_s, 0)

    w_in_t = jnp.transpose(w_in, (0, 2, 1))
    pool_hist = jnp.transpose(state_pool, (0, 2, 1, 3))
    small = (meta_tokens, g_mix, b_gk, pool_scale, g_gla_norm, w_gk_up, w_pool_group)
    hbm = pl.BlockSpec(memory_space=pl.ANY)
    in_specs = (
        [pl.BlockSpec((nbb, ts, d), lambda i: (s_idx(i), 0, 0)),
         pl.BlockSpec((1, POOL_BUF, nbb, POOL_WIDTH), lambda i: (0, 0, s_idx(i), 0)),
         pl.BlockSpec((1, nbb, GLA_HEADS, GLA_DK, GLA_DV), lambda i: (0, s_idx(i), 0, 0, 0)),
         pl.BlockSpec((1, PROMPT_TILE, d), lambda i: (p_idx(i) // n_t, p_idx(i) % n_t, 0))]
        + [_const_spec(a.shape) for a in small]
        + [hbm, hbm, hbm, hbm,
           pl.BlockSpec((1, FFN_IN_CAST_ROWS, 2 * D_FF), lambda i: (0, jnp.minimum(p_idx(i), n_fi - 1), 0)),
           pl.BlockSpec((1, FFN_OUT_CAST_ROWS, d), lambda i: (0, jnp.minimum(p_idx(i), n_fo - 1), 0))])
    out_specs = [
        pl.BlockSpec((nbb * ts, d), lambda i: (s_idx(i), 0)),
        pl.BlockSpec((1, POOL_BUF, nbb, POOL_WIDTH), lambda i: (0, 0, s_idx(i), 0)),
        pl.BlockSpec((1, nbb, GLA_HEADS, GLA_DK, GLA_DV), lambda i: (0, s_idx(i), 0, 0, 0)),
        pl.BlockSpec((1, PROMPT_TILE, d), lambda i: (p_idx(i) // n_t, p_idx(i) % n_t, 0)),
        pl.BlockSpec((1, POOL_BUF, bp, POOL_WIDTH), lambda i: (0, 0, 0, 0)),
        pl.BlockSpec((1, 1, GLA_HEADS, GLA_DK, GLA_DV), lambda i: (0, p_idx(i) // n_t, 0, 0, 0)),
        pl.BlockSpec((FFN_IN_CAST_ROWS, 2 * D_FF), lambda i: (jnp.minimum(p_idx(i), n_fi - 1), 0)),
        pl.BlockSpec((FFN_OUT_CAST_ROWS, d), lambda i: (jnp.minimum(p_idx(i), n_fo - 1), 0))]
    out_shape = [
        jax.ShapeDtypeStruct((bs * ts, d), F32),
        jax.ShapeDtypeStruct((1, POOL_BUF, bs, POOL_WIDTH), F32),
        jax.ShapeDtypeStruct(state_gla.shape, F32),
        jax.ShapeDtypeStruct(x_prompt.shape, F32),
        jax.ShapeDtypeStruct((1, POOL_BUF, bp, POOL_WIDTH), F32),
        jax.ShapeDtypeStruct((1, bp, GLA_HEADS, GLA_DK, GLA_DV), F32),
        jax.ShapeDtypeStruct((D_MODEL, 2 * D_FF), BF16),
        jax.ShapeDtypeStruct((D_FF, D_MODEL), BF16)]
    scratch_shapes = [
        pltpu.VMEM((D_MODEL, MAIN_COLS), BF16),
        pltpu.VMEM((D_MODEL, 2 * D_MODEL), BF16), pltpu.VMEM((LANES, GLA_KW), BF16),
        pltpu.VMEM((POOL_WIDTH, D_MODEL), BF16), pltpu.VMEM((GLA_VW, D_MODEL), BF16),
        pltpu.VMEM((D_MODEL, D_MODEL), BF16),
        pltpu.VMEM((2, STAGE_ROWS, D_MODEL), F32), pltpu.VMEM((LANES, D_MODEL), F32),
        pltpu.SemaphoreType.DMA((2,)), pltpu.SemaphoreType.DMA((1,)),
        pltpu.VMEM((POOL_PAD + TAIL_ROWS + PROMPT_TILE, POOL_WIDTH), F32),
        pltpu.VMEM((len(POOL_WINDOWS) - 1, POOL_PAD + TAIL_ROWS + PROMPT_TILE, POOL_WIDTH), F32),
        pltpu.VMEM((GLA_HEADS, GLA_DK, GLA_DV), F32),
        pltpu.VMEM((N_META, POOL_WIDTH), F32),
        pltpu.VMEM((GLA_HEADS, GLA_DK, GLA_DV), F32),
        pltpu.VMEM((nbb, SAMPLE_ROWS, D_MODEL), F32),
        pltpu.VMEM((len(POOL_WINDOWS), nbb * SAMPLE_ROWS, POOL_GROUP_DIM), F32),
        pltpu.VMEM((len(POOL_WINDOWS), nbb * SAMPLE_ROWS, POOL_GROUP_DIM), F32),
        pltpu.VMEM((nbb, GLA_HEADS * SAMPLE_ROWS, GLA_KW), BF16),
        pltpu.VMEM((nbb, SAMPLE_ROWS, GLA_KW), F32),
        pltpu.VMEM((nbb, SAMPLE_ROWS, 2 * GLA_VW), F32),
        pltpu.VMEM((nbb, GLA_HEADS * SAMPLE_ROWS, GLA_DV), F32)]

    def mixer(*refs):
        _mixer_kernel(n_s, n_t, *refs)

    x2_s, pool_s, gla_s, x2_p, pool_p, gla_p, wfi_bf, wfo_bf = pl.pallas_call(
        mixer,
        grid=(n_s + n_p,),
        in_specs=in_specs,
        out_specs=out_specs,
        out_shape=out_shape,
        scratch_shapes=scratch_shapes,
        compiler_params=pltpu.CompilerParams(dimension_semantics=("arbitrary",), vmem_limit_bytes=VMEM_LIMIT),
        name="mixer",
    )(x_sample, pool_hist, state_gla, x_prompt, *small, w_in_t, w_pool_proj, w_gla_proj, w_out, w_ffn_in, w_ffn_out)

    def ffn(*refs):
        _ffn_kernel(ts, *refs)

    def fp_idx(i):
        return jnp.maximum(i - 1, 0)

    ffn_small = (g_ffn, wfi_bf, wfo_bf, g_final.reshape(1, D_MODEL))
    y_prompt, y_sample = pl.pallas_call(
        ffn,
        grid=(n_p + 1,),
        in_specs=[pl.BlockSpec((1, FFN_TILE, d), lambda i: (fp_idx(i) // n_t, fp_idx(i) % n_t, 0)),
                  _const_spec(x2_s.shape)] + [_const_spec(a.shape) for a in ffn_small],
        out_specs=[pl.BlockSpec((1, FFN_TILE, d), lambda i: (fp_idx(i) // n_t, fp_idx(i) % n_t, 0)),
                   pl.BlockSpec(x_sample.shape, lambda i: (0, 0, 0))],
        out_shape=[jax.ShapeDtypeStruct(x_prompt.shape, F32), jax.ShapeDtypeStruct(x_sample.shape, F32)],
        compiler_params=pltpu.CompilerParams(dimension_semantics=("arbitrary",), vmem_limit_bytes=VMEM_LIMIT),
        name="ffn",
    )(x2_p, x2_s, *ffn_small)
    pool_p = jnp.transpose(pool_p, (0, 2, 1, 3))
    pool_s = jnp.transpose(pool_s, (0, 2, 1, 3))
    return y_prompt, y_sample, pool_p, gla_p, pool_s, gla_s
```

```python
import jax
import jax.numpy as jnp
from jax import lax
from jax.experimental import pallas as pl
from jax.experimental.pallas import tpu as pltpu

F32 = jnp.float32
BF16 = jnp.bfloat16

D_MODEL = 1024
N_META = 16
POOL_WIDTH = 512
POOL_WINDOWS = (2, 4, 8, 16)
POOL_GROUP_DIM = 128
POOL_BUF = 15
GLA_HEADS = 4
GLA_DV = 128
GLA_DK = 64
GLA_KW = GLA_HEADS * GLA_DK
GLA_VW = GLA_HEADS * GLA_DV
GLA_GATE_RANK = 16
GLA_TAU = 16.0
GLA_CHUNK = 64
D_FF = 2816
EPS = 1e-6

LANES = 128
SUBLANES = 8
MAIN_W = POOL_WIDTH + 2 * GLA_KW + 2 * GLA_VW
U_COL, VOG_COL, QK_COL, ZR_COL = 0, POOL_WIDTH, POOL_WIDTH + 2 * GLA_VW, MAIN_W
MAIN_COLS = MAIN_W + LANES
GAB_LO = MAIN_W + GLA_GATE_RANK
IN_DIM = GAB_LO + 2 * D_MODEL
TAIL_ROWS = 16
POOL_PAD = SUBLANES

PROMPT_TILE = 512
FFN_TILE = 512
FFN_CHUNK = 256
FFN_OUT_GROUP = 4
SAMPLE_BATCH_BLOCK = 16
SAMPLE_ROWS = SUBLANES
SAMPLE_GATE_SLICES = 8
STAGE_ROWS = 256
FFN_IN_CAST_ROWS = 32
FFN_OUT_CAST_ROWS = 128
V7X_VMEM_BYTES = 64 * 1024 * 1024
VMEM_RESERVE = 4 * 1024 * 1024
VMEM_LIMIT = V7X_VMEM_BYTES - VMEM_RESERVE


def _dot(a, b):
    return jnp.dot(a, b, preferred_element_type=F32)


def _dot_nt(a, b):
    return lax.dot_general(a, b, (((1,), (1,)), ((), ())), preferred_element_type=F32)


def _dot_tn(a, b):
    return lax.dot_general(a, b, (((0,), (0,)), ((), ())), preferred_element_type=F32)


def _rms(x, g):
    return x * lax.rsqrt(jnp.mean(x * x, axis=-1, keepdims=True) + EPS) * g


def _rms_split(x, g):
    r = lax.rsqrt(jnp.mean(x * x, axis=-1, keepdims=True) + EPS)
    return (x * g).astype(BF16), r


def _sigmoid(x):
    return 0.5 * jnp.tanh(0.5 * x) + 0.5


def _silu(x):
    half = 0.5 * x
    return half * jnp.tanh(half) + half


def _log_sigmoid(x):
    return jnp.minimum(x, 0.0) - jnp.log(1.0 + jnp.exp(-jnp.abs(x)))


def _split_bf16(x):
    hi = x.astype(BF16)
    lo = (x - hi.astype(F32)).astype(BF16)
    return hi, lo


class _Weights:
    def __init__(self, gmix, bgk, pscale, gnorm, wpg, wmain, wgab, wgk, wpp, wgp, wout):
        self.gmix, self.bgk, self.pscale, self.gnorm, self.wpg = gmix, bgk, pscale, gnorm, wpg
        self.wmain, self.wgab, self.wgk = wmain, wgab, wgk
        self.wpp, self.wgp, self.wout = wpp, wgp, wout


def _in_proj(x, w):
    h, r = _rms_split(x, w.gmix[...])
    qkz = _dot(h, w.wmain[:, QK_COL:MAIN_COLS])
    q = qkz[:, :GLA_KW] * (r * (GLA_DK ** -0.5))
    k = qkz[:, GLA_KW:2 * GLA_KW] * r
    zr = qkz[:, 2 * GLA_KW:] * r
    z = _dot(zr.astype(BF16), w.wgk[...]) + w.bgk[...]
    u = _dot(h, w.wmain[:, U_COL:VOG_COL]) * r
    vog = _dot(h, w.wmain[:, VOG_COL:QK_COL]) * r
    v = vog[:, :GLA_VW]
    og = vog[:, GLA_VW:]
    return (h, r), u, q, k, v, og, z


def _gate_proj(xn, w, lo, hi):
    h, r = xn
    return _dot(h, w.wgab[:, lo:hi]) * r


def _chunk_cumsum_wide(g, chunk):
    n = g.shape[0] // chunk
    r = lax.broadcasted_iota(jnp.int32, (chunk, chunk), 0)
    c = lax.broadcasted_iota(jnp.int32, (chunk, chunk), 1)
    tri = jnp.where(c <= r, 1.0, 0.0).astype(BF16)
    hi, lo = _split_bf16(jnp.concatenate([g[j * chunk:(j + 1) * chunk] for j in range(n)], axis=1))
    wide = _dot(tri, hi) + _dot(tri, lo)
    width = g.shape[1]
    parts = [wide[:, j * width:(j + 1) * width] for j in range(n)]
    return jnp.concatenate(parts, axis=0), [p[chunk - 1:chunk, :] for p in parts]


def _chunk_cumsum(g, chunk):
    m = g.shape[0]
    r = lax.broadcasted_iota(jnp.int32, (m, m), 0)
    c = lax.broadcasted_iota(jnp.int32, (m, m), 1)
    tri = jnp.where((r // chunk == c // chunk) & (c <= r), 1.0, 0.0).astype(BF16)
    hi, lo = _split_bf16(g)
    return _dot(tri, hi) + _dot(tri, lo)


def _head_lane_mask(width, per_head):
    lane = lax.broadcasted_iota(jnp.int32, (1, width), 1)
    return [(lane // per_head) == h for h in range(GLA_HEADS)]


def _block_diag_rows(x_bf, per_head):
    r = x_bf.shape[0]
    zero = jnp.zeros((r, per_head), x_bf.dtype)
    rows = []
    for h in range(GLA_HEADS):
        rows.append(jnp.concatenate(
            [x_bf[:, h * per_head:(h + 1) * per_head] if hh == h else zero for hh in range(GLA_HEADS)], axis=1))
    return jnp.concatenate(rows, axis=0)


def _gla_post(o, og, w):
    parts = []
    for h in range(GLA_HEADS):
        oh = o[:, h * GLA_DV:(h + 1) * GLA_DV]
        parts.append(oh * lax.rsqrt(jnp.mean(oh * oh, axis=-1, keepdims=True) + EPS) * w.gnorm[...])
    on = jnp.concatenate(parts, axis=1)
    on = on * _silu(og)
    return _dot(on.astype(BF16), w.wgp[...])


def _pool_post(pooled, w):
    pb = pooled.astype(BF16)
    mixed = []
    for p in range(len(POOL_WINDOWS) // 2):
        w_pair = _pair_block_diag(w.wpg[0, 2 * p].astype(BF16), w.wpg[0, 2 * p + 1].astype(BF16))
        mixed.append(_dot(pb[:, 2 * p * POOL_GROUP_DIM:(2 * p + 2) * POOL_GROUP_DIM], w_pair))
    mixed = jnp.concatenate(mixed, axis=1)
    return _dot((mixed * w.pscale[...]).astype(BF16), w.wpp[...])


def _merge(x, y_a, y_b, sa, sb, w):
    merged = sa * y_a + sb * y_b
    return x + _dot(merged.astype(BF16), w.wout[...])


def _decay_columns(decay_row):
    return jnp.transpose(jnp.broadcast_to(decay_row, (LANES, decay_row.shape[1])))


def _pair_block_diag(a, b):
    zero = jnp.zeros(a.shape, a.dtype)
    return jnp.concatenate([jnp.concatenate([a, zero], axis=1), jnp.concatenate([zero, b], axis=1)], axis=0)


def _state_update(s_heads, kd_bf, v_bf, decay_row):
    dcol = _decay_columns(decay_row)
    out = []
    for p in range(GLA_HEADS // 2):
        upd = _dot_tn(kd_bf[:, 2 * p * GLA_DK:(2 * p + 2) * GLA_DK], v_bf[:, 2 * p * GLA_DV:(2 * p + 2) * GLA_DV])
        for j in range(2):
            h = 2 * p + j
            rows = slice(h * GLA_DK, (h + 1) * GLA_DK)
            out.append(dcol[rows] * s_heads[h] + upd[j * GLA_DK:(j + 1) * GLA_DK, j * GLA_DV:(j + 1) * GLA_DV])
    return out


def _stage_weights(wint_hbm, wpp_hbm, wgp_hbm, wout_hbm, wgk_ref, w, stage, zr_stage, sem, zr_sem):
    plan = []
    for r in range(0, MAIN_W, STAGE_ROWS):
        if r < POOL_WIDTH:
            col = U_COL + r
        elif r < POOL_WIDTH + 2 * GLA_KW:
            col = QK_COL + r - POOL_WIDTH
        else:
            col = VOG_COL + r - (POOL_WIDTH + 2 * GLA_KW)
        plan.append((wint_hbm, r, w.wmain, col, True))
    for r in range(0, 2 * D_MODEL, STAGE_ROWS):
        plan.append((wint_hbm, GAB_LO + r, w.wgab, r, True))
    for src, dst, n_rows in ((wpp_hbm, w.wpp, POOL_WIDTH), (wgp_hbm, w.wgp, GLA_VW), (wout_hbm, w.wout, D_MODEL)):
        for r in range(0, n_rows, STAGE_ROWS):
            plan.append((src, r, dst, r, False))

    def copy(j):
        src, r0 = plan[j][0], plan[j][1]
        return pltpu.make_async_copy(src.at[0, pl.ds(r0, STAGE_ROWS), :], stage.at[j % 2], sem.at[j % 2])

    def zr_copy():
        return pltpu.make_async_copy(
            wint_hbm.at[0, pl.ds(MAIN_W, GLA_GATE_RANK), :], zr_stage.at[pl.ds(0, GLA_GATE_RANK), :], zr_sem.at[0])

    copy(0).start()
    copy(1).start()
    zr_copy().start()

    w.wgk[...] = jnp.zeros(w.wgk.shape, BF16)
    w.wgk[0:GLA_GATE_RANK, :] = wgk_ref[0].astype(BF16)
    zr_stage[GLA_GATE_RANK:, :] = jnp.zeros((LANES - GLA_GATE_RANK, D_MODEL), F32)

    for j in range(len(plan)):
        copy(j).wait()
        _, _, dst, d0, transposed = plan[j]
        slab = stage[j % 2]
        if transposed:
            dst[:, d0:d0 + STAGE_ROWS] = jnp.transpose(slab).astype(BF16)
        else:
            dst[d0:d0 + STAGE_ROWS, :] = slab.astype(BF16)
        if j + 2 < len(plan):
            copy(j + 2).start()
    zr_copy().wait()
    w.wmain[:, ZR_COL:MAIN_COLS] = jnp.transpose(zr_stage[...]).astype(BF16)


def _prompt_tile(b_first, b_idx, t_idx, n_t, x_ref, meta_ref, w, x2_ref, pbuf_ref, sout_ref,
                 ext_ref, lvl_ref, s_ref, meta_tail_ref, meta_s_ref):
    tile = x_ref.shape[1]
    n_chunks = tile // GLA_CHUNK

    @pl.when(b_first)
    def _():
        _, u, _, k, v, _, z = _in_proj(meta_ref[...], w)
        meta_tail_ref[...] = u
        g = _log_sigmoid(z) * (1.0 / GLA_TAU)
        b = _chunk_cumsum(g, N_META)
        b_last = b[N_META - 1:N_META, :]
        kd = k * jnp.exp(b_last - b)
        zero_s = [jnp.zeros((GLA_DK, GLA_DV), F32)] * GLA_HEADS
        s_new = _state_update(zero_s, kd.astype(BF16), v.astype(BF16), jnp.exp(b_last))
        for hd in range(GLA_HEADS):
            meta_s_ref[hd] = s_new[hd]

    @pl.when(t_idx == 0)
    def _():
        ext_ref[0:POOL_PAD, :] = jnp.zeros((POOL_PAD, POOL_WIDTH), F32)
        lvl_ref[:, 0:POOL_PAD, :] = jnp.zeros((lvl_ref.shape[0], POOL_PAD, POOL_WIDTH), F32)
        ext_ref[POOL_PAD:POOL_PAD + TAIL_ROWS, :] = meta_tail_ref[...]
        s_ref[...] = meta_s_ref[...]

    x = x_ref[0]
    xn = _rms_split(x, w.gmix[...])
    h, r = xn
    gate_cols = 2 * D_MODEL // n_chunks
    gate_parts = [None] * n_chunks

    def gate_slice(c):
        gate_parts[c] = _sigmoid(_gate_proj(xn, w, c * gate_cols, (c + 1) * gate_cols))

    qkz = _dot(h, w.wmain[:, QK_COL:MAIN_COLS])
    u = _dot(h, w.wmain[:, U_COL:VOG_COL]) * r
    v = _dot(h, w.wmain[:, VOG_COL:VOG_COL + GLA_VW]) * r
    q = qkz[:, :GLA_KW] * (r * (GLA_DK ** -0.5))
    k = qkz[:, GLA_KW:2 * GLA_KW] * r
    zr = qkz[:, 2 * GLA_KW:] * r
    z = _dot(zr.astype(BF16), w.wgk[...]) + w.bgk[...]
    gate_slice(0)
    g = _log_sigmoid(z) * (1.0 / GLA_TAU)
    b, b_last_rows = _chunk_cumsum_wide(g, GLA_CHUNK)
    gate_slice(1)
    og_parts = []

    base = POOL_PAD + TAIL_ROWS
    span = TAIL_ROWS + tile
    ext_ref[base:base + tile, :] = u
    cur = ext_ref[POOL_PAD:POOL_PAD + span, :]
    pooled = []
    for gi, win in enumerate(POOL_WINDOWS):
        shift = win // 2
        lo = gi * POOL_GROUP_DIM
        prev_ref = ext_ref if gi == 0 else lvl_ref.at[gi - 1]
        cur = cur[:, (POOL_GROUP_DIM if gi else 0):] + prev_ref[POOL_PAD - shift:POOL_PAD - shift + span, lo:]
        pooled.append(cur[TAIL_ROWS:, 0:POOL_GROUP_DIM] * (1.0 / win) - u[:, lo:lo + POOL_GROUP_DIM])
        if gi + 1 < len(POOL_WINDOWS):
            lvl_ref[gi, POOL_PAD:POOL_PAD + span, lo:] = cur
    y_a = _pool_post(jnp.concatenate(pooled, axis=1), w)
    ext_ref[POOL_PAD:base, :] = ext_ref[POOL_PAD + tile:base + tile, :]

    b_last = jnp.concatenate([jnp.broadcast_to(r, (GLA_CHUNK, GLA_KW)) for r in b_last_rows], axis=0)
    qe = (q * jnp.exp(b)).astype(BF16)
    ke = k * jnp.exp(-b)
    kd = (k * jnp.exp(b_last - b)).astype(BF16)
    v_bf = v.astype(BF16)

    k_masks = _head_lane_mask(GLA_KW, GLA_DK)
    row_i = lax.broadcasted_iota(jnp.int32, (GLA_CHUNK, GLA_KW), 0)
    col_j = lax.broadcasted_iota(jnp.int32, (GLA_CHUNK, GLA_KW), 1) % GLA_CHUNK
    causal = col_j <= row_i

    s_heads = [s_ref[hd] for hd in range(GLA_HEADS)]
    o_chunks = []
    for c in range(n_chunks):
        if c + 2 < n_chunks:
            gate_slice(c + 2)
        else:
            og_lo = VOG_COL + GLA_VW + len(og_parts) * (GLA_VW // 2)
            og_parts.append(_dot(h, w.wmain[:, og_lo:og_lo + GLA_VW // 2]) * r)
        rows = slice(c * GLA_CHUNK, (c + 1) * GLA_CHUNK)
        ke_c = ke[rows]
        ke_bd = jnp.concatenate([jnp.where(k_masks[hd], ke_c, 0.0) for hd in range(GLA_HEADS)], axis=0).astype(BF16)
        att = jnp.where(causal, _dot_nt(qe[rows], ke_bd), 0.0).astype(BF16)
        o_pairs = []
        for p in range(GLA_HEADS // 2):
            h0, h1 = 2 * p, 2 * p + 1
            lanes_k = slice(h0 * GLA_DK, (h1 + 1) * GLA_DK)
            lanes_j = slice(h0 * GLA_CHUNK, (h1 + 1) * GLA_CHUNK)
            v0 = v_bf[rows, h0 * GLA_DV:(h0 + 1) * GLA_DV]
            v1 = v_bf[rows, h1 * GLA_DV:(h1 + 1) * GLA_DV]
            rhs = jnp.concatenate([_pair_block_diag(s_heads[h0].astype(BF16), s_heads[h1].astype(BF16)),
                                   _pair_block_diag(v0, v1)], axis=0)
            o_pairs.append(_dot(jnp.concatenate([qe[rows, lanes_k], att[:, lanes_j]], axis=1), rhs))
        o_chunks.append(jnp.concatenate(o_pairs, axis=1))
        s_heads = _state_update(s_heads, kd[rows], v_bf[rows], jnp.exp(b_last_rows[c]))
    for hd in range(GLA_HEADS):
        s_ref[hd] = s_heads[hd]

    y_b = _gla_post(jnp.concatenate(o_chunks, axis=0), jnp.concatenate(og_parts, axis=1), w)
    sg = jnp.concatenate(gate_parts, axis=1)
    x2_ref[0] = _merge(x, y_a, y_b, sg[:, :D_MODEL], sg[:, D_MODEL:], w)

    @pl.when(t_idx == n_t - 1)
    def _():
        sout_ref[0, 0] = s_ref[...]

    for bb in range(pbuf_ref.shape[2]):
        @pl.when((t_idx == n_t - 1) & (b_idx == bb))
        def _():
            for r in range(POOL_BUF):
                row = base - POOL_BUF + r
                pbuf_ref[0, r, bb:bb + 1, :] = ext_ref[row:row + 1, :]


def _sample_block(x_ref, pool_ref, sin_ref, w, x2_ref, pbuf_ref, sout_ref,
                  xs_ref, us_ref, pooled_ref, qm_ref, kdx_ref, rhs_ref, oi_ref):
    nb, seq, _ = x_ref.shape
    rows_pb = SAMPLE_ROWS
    m = nb * rows_pb

    xs_ref[...] = jnp.zeros(xs_ref.shape, F32)
    xs_ref[:, 0:seq, :] = x_ref[...]
    x = xs_ref[...].reshape(m, D_MODEL)
    xn, u, q, k, v, og, z = _in_proj(x, w)

    pooled_ref[...] = jnp.zeros(pooled_ref.shape, F32)
    for gi, win in enumerate(POOL_WINDOWS):
        cols = slice(gi * POOL_GROUP_DIM, (gi + 1) * POOL_GROUP_DIM)
        us_ref[gi] = u[:, cols]
        tok = [us_ref[gi, pl.ds(t, nb, stride=rows_pb), :] for t in range(seq)]
        hist = [pool_ref[0, r, :, cols] for r in range(POOL_BUF)] + tok
        for t in range(seq):
            acc = tok[t]
            for n in range(1, win):
                acc = acc + hist[POOL_BUF + t - n]
            pooled_ref[gi, pl.ds(t, nb, stride=rows_pb), :] = acc * (1.0 / win) - tok[t]
        for r in range(POOL_BUF):
            pbuf_ref[0, r, :, cols] = hist[seq + r]
    y_a = _pool_post(jnp.concatenate([pooled_ref[gi] for gi in range(len(POOL_WINDOWS))], axis=1), w)

    r8 = lax.broadcasted_iota(jnp.int32, (m, 1), 0) % rows_pb
    g = jnp.where(r8 < seq, _log_sigmoid(z) * (1.0 / GLA_TAU), 0.0)
    b = _chunk_cumsum(g, rows_pb)
    b3 = b.reshape(nb, rows_pb, GLA_KW)
    b_last = jnp.broadcast_to(b3[:, seq - 1:seq, :], b3.shape).reshape(m, GLA_KW)
    qe = q * jnp.exp(b)
    ke = k * jnp.exp(-b)
    kd = k * jnp.exp(b_last - b)
    decay = jnp.exp(b_last)
    v_bf = v.astype(BF16)

    k_masks = _head_lane_mask(GLA_KW, GLA_DK)
    ke_bd = jnp.concatenate([jnp.where(k_masks[hd], ke, 0.0) for hd in range(GLA_HEADS)], axis=0).astype(BF16)
    row_i = lax.broadcasted_iota(jnp.int32, (m, GLA_HEADS * m), 0)
    col_j = lax.broadcasted_iota(jnp.int32, (m, GLA_HEADS * m), 1) % m
    keep = (row_i // rows_pb == col_j // rows_pb) & (col_j <= row_i)
    att = jnp.where(keep, _dot_nt(qe.astype(BF16), ke_bd), 0.0).astype(BF16)
    o_intra = _dot(att, _block_diag_rows(v_bf, GLA_DV))

    qe3 = qe.reshape(nb, rows_pb, GLA_KW)
    qm_ref[...] = jnp.concatenate([jnp.where(k_masks[hd], qe3, 0.0) for hd in range(GLA_HEADS)], axis=1).astype(BF16)
    d_hi = decay.astype(BF16).astype(F32)
    d_lo = decay - d_hi
    kdx = jnp.where(r8 == seq, d_hi, jnp.where(r8 == seq + 1, d_lo, kd))
    kdx_ref[...] = kdx.reshape(nb, rows_pb, GLA_KW)
    ones_rows = jnp.where((r8 == seq) | (r8 == seq + 1), 1.0, 0.0) + jnp.zeros((m, GLA_DV), F32)
    rhs = jnp.concatenate(
        [piece for hd in range(GLA_HEADS) for piece in (v[:, hd * GLA_DV:(hd + 1) * GLA_DV], ones_rows)], axis=1)
    rhs_ref[...] = rhs.reshape(nb, rows_pb, 2 * GLA_VW)

    def per_batch(i):
        s_all = sin_ref[0, i]
        s_flat = s_all.reshape(GLA_KW, GLA_DV).astype(BF16)
        oi_ref[i] = _dot(qm_ref[i], s_flat)
        kdt = jnp.transpose(kdx_ref[i]).astype(BF16)
        rhs_i = rhs_ref[i].astype(BF16)
        for hd in range(GLA_HEADS):
            r = _dot(kdt[hd * GLA_DK:(hd + 1) * GLA_DK], rhs_i[:, hd * 2 * GLA_DV:(hd + 1) * 2 * GLA_DV])
            sout_ref[0, i, hd] = r[:, GLA_DV:] * s_all[hd] + r[:, :GLA_DV]

    batches_per_slice = nb // SAMPLE_GATE_SLICES
    gate_cols = 2 * D_MODEL // SAMPLE_GATE_SLICES
    gate_parts = []
    for i in range(nb):
        if i % batches_per_slice == 0:
            c = i // batches_per_slice
            gate_parts.append(_sigmoid(_gate_proj(xn, w, c * gate_cols, (c + 1) * gate_cols)))
        per_batch(i)

    oi = oi_ref[...]
    o_inter = jnp.concatenate([oi[:, hd * rows_pb:(hd + 1) * rows_pb, :] for hd in range(GLA_HEADS)], axis=2)
    o = o_intra + o_inter.reshape(m, GLA_VW)
    y_b = _gla_post(o, og, w)
    sg = jnp.concatenate(gate_parts, axis=1)
    x2 = _merge(x, y_a, y_b, sg[:, :D_MODEL], sg[:, D_MODEL:], w).reshape(nb, rows_pb, D_MODEL)
    for bi in range(nb):
        x2_ref[bi * seq:(bi + 1) * seq, :] = x2[bi, 0:seq, :]


def _mixer_kernel(n_s, n_t,
                  xs_in, pool_in, s_in, xp_in, meta_ref, gmix_ref, bgk_ref, pscale_ref, gnorm_ref, wgk_ref, wpg_ref,
                  wint_hbm, wpp_hbm, wgp_hbm, wout_hbm, wfi_in, wfo_in,
                  x2s_out, pools_out, ss_out, x2p_out, poolp_out, sp_out, wfi_out, wfo_out,
                  wmain_s, wgab_s, wgk_s, wpp_s, wgp_s, wout_s, stage, zr_stage, sem, zr_sem,
                  ext_ref, lvl_ref, s_ref, meta_tail_ref, meta_s_ref,
                  xs_ref, us_ref, pooled_ref, qm_ref, kdx_ref, rhs_ref, oi_ref):
    i = pl.program_id(0)
    w = _Weights(gmix_ref, bgk_ref, pscale_ref, gnorm_ref, wpg_ref,
                 wmain_s, wgab_s, wgk_s, wpp_s, wgp_s, wout_s)

    @pl.when(i == 0)
    def _():
        _stage_weights(wint_hbm, wpp_hbm, wgp_hbm, wout_hbm, wgk_ref, w, stage, zr_stage, sem, zr_sem)

    wfi_out[...] = wfi_in[0].astype(BF16)
    wfo_out[...] = wfo_in[0].astype(BF16)

    @pl.when(i < n_s)
    def _():
        _sample_block(xs_in, pool_in, s_in, w, x2s_out, pools_out, ss_out,
                      xs_ref, us_ref, pooled_ref, qm_ref, kdx_ref, rhs_ref, oi_ref)

    @pl.when(i >= n_s)
    def _():
        t_idx = (i - n_s) % n_t
        _prompt_tile(i == n_s, (i - n_s) // n_t, t_idx, n_t, xp_in, meta_ref, w, x2p_out, poolp_out, sp_out,
                     ext_ref, lvl_ref, s_ref, meta_tail_ref, meta_s_ref)


def _ffn_kernel(seq, xp_ref, xs_ref, gffn_ref, wi_ref, wo_ref, gfin_ref, yp_ref, ys_ref):
    i = pl.program_id(0)
    x = jnp.where(i == 0, xs_ref[...], xp_ref[0])
    h, r = _rms_split(x, gffn_ref[...])
    acc = x
    group = []
    for lo in range(0, D_FF, FFN_CHUNK):
        gate = _dot(h, wi_ref[:, lo:lo + FFN_CHUNK]) * r
        up = _dot(h, wi_ref[:, D_FF + lo:D_FF + lo + FFN_CHUNK]) * r
        group.append((_silu(gate) * up).astype(BF16))
        if len(group) == FFN_OUT_GROUP or lo + FFN_CHUNK == D_FF:
            first = lo + FFN_CHUNK - len(group) * FFN_CHUNK
            acc = acc + _dot(jnp.concatenate(group, axis=1), wo_ref[first:lo + FFN_CHUNK, :])
            group = []
    yp_ref[0] = _rms(acc, gfin_ref[...])

    @pl.when(i == 0)
    def _():
        for bi in range(ys_ref.shape[0]):
            ys_ref[bi] = yp_ref[0, bi * seq:(bi + 1) * seq, :]


def _const_spec(shape):
    zeros = (0,) * len(shape)
    return pl.BlockSpec(shape, lambda *_: zeros, pipeline_mode=pl.Buffered(1))


def kernel(x_prompt, x_sample, state_pool, state_gla, meta_tokens, g_mix, w_in, w_gk_up, b_gk, w_pool_group,
           pool_scale, w_pool_proj, g_gla_norm, w_gla_proj, w_out, g_ffn, w_ffn_in, w_ffn_out, g_final):
    depth = w_in.shape[0]
    assert depth == 1, "single-layer trunk only"
    bp, tp, d = x_prompt.shape
    bs, ts, _ = x_sample.shape
    nbb = SAMPLE_BATCH_BLOCK
    assert d == D_MODEL and w_in.shape == (1, D_MODEL, IN_DIM) and meta_tokens.shape == (N_META, D_MODEL)
    assert tp % PROMPT_TILE == 0 and tp % FFN_TILE == 0 and PROMPT_TILE % GLA_CHUNK == 0
    assert bs % nbb == 0 and ts + 2 <= SAMPLE_ROWS and bs * ts == FFN_TILE
    n_t = tp // PROMPT_TILE
    n_p = bp * n_t
    n_s = bs // nbb
    n_fi = D_MODEL // FFN_IN_CAST_ROWS
    n_fo = D_FF // FFN_OUT_CAST_ROWS
    assert n_fi <= n_p and n_fo <= n_p

    def s_idx(i):
        return jnp.minimum(i, n_s - 1)

    def p_idx(i):
        return jnp.maximum(i - n_s, 0)

    w_in_t = jnp.transpose(w_in, (0, 2, 1))
    pool_hist = jnp.transpose(state_pool, (0, 2, 1, 3))
    small = (meta_tokens, g_mix, b_gk, pool_scale, g_gla_norm, w_gk_up, w_pool_group)
    hbm = pl.BlockSpec(memory_space=pl.ANY)
    in_specs = (
        [pl.BlockSpec((nbb, ts, d), lambda i: (s_idx(i), 0, 0)),
         pl.BlockSpec((1, POOL_BUF, nbb, POOL_WIDTH), lambda i: (0, 0, s_idx(i), 0)),
         pl.BlockSpec((1, nbb, GLA_HEADS, GLA_DK, GLA_DV), lambda i: (0, s_idx(i), 0, 0, 0)),
         pl.BlockSpec((1, PROMPT_TILE, d), lambda i: (p_idx(i) // n_t, p_idx(i) % n_t, 0))]
        + [_const_spec(a.shape) for a in small]
        + [hbm, hbm, hbm, hbm,
           pl.BlockSpec((1, FFN_IN_CAST_ROWS, 2 * D_FF), lambda i: (0, jnp.minimum(p_idx(i), n_fi - 1), 0)),
           pl.BlockSpec((1, FFN_OUT_CAST_ROWS, d), lambda i: (0, jnp.minimum(p_idx(i), n_fo - 1), 0))])
    out_specs = [
        pl.BlockSpec((nbb * ts, d), lambda i: (s_idx(i), 0)),
        pl.BlockSpec((1, POOL_BUF, nbb, POOL_WIDTH), lambda i: (0, 0, s_idx(i), 0)),
        pl.BlockSpec((1, nbb, GLA_HEADS, GLA_DK, GLA_DV), lambda i: (0, s_idx(i), 0, 0, 0)),
        pl.BlockSpec((1, PROMPT_TILE, d), lambda i: (p_idx(i) // n_t, p_idx(i) % n_t, 0)),
        pl.BlockSpec((1, POOL_BUF, bp, POOL_WIDTH), lambda i: (0, 0, 0, 0)),
        pl.BlockSpec((1, 1, GLA_HEADS, GLA_DK, GLA_DV), lambda i: (0, p_idx(i) // n_t, 0, 0, 0)),
        pl.BlockSpec((FFN_IN_CAST_ROWS, 2 * D_FF), lambda i: (jnp.minimum(p_idx(i), n_fi - 1), 0)),
        pl.BlockSpec((FFN_OUT_CAST_ROWS, d), lambda i: (jnp.minimum(p_idx(i), n_fo - 1), 0))]
    out_shape = [
        jax.ShapeDtypeStruct((bs * ts, d), F32),
        jax.ShapeDtypeStruct((1, POOL_BUF, bs, POOL_WIDTH), F32),
        jax.ShapeDtypeStruct(state_gla.shape, F32),
        jax.ShapeDtypeStruct(x_prompt.shape, F32),
        jax.ShapeDtypeStruct((1, POOL_BUF, bp, POOL_WIDTH), F32),
        jax.ShapeDtypeStruct((1, bp, GLA_HEADS, GLA_DK, GLA_DV), F32),
        jax.ShapeDtypeStruct((D_MODEL, 2 * D_FF), BF16),
        jax.ShapeDtypeStruct((D_FF, D_MODEL), BF16)]
    scratch_shapes = [
        pltpu.VMEM((D_MODEL, MAIN_COLS), BF16),
        pltpu.VMEM((D_MODEL, 2 * D_MODEL), BF16), pltpu.VMEM((LANES, GLA_KW), BF16),
        pltpu.VMEM((POOL_WIDTH, D_MODEL), BF16), pltpu.VMEM((GLA_VW, D_MODEL), BF16),
        pltpu.VMEM((D_MODEL, D_MODEL), BF16),
        pltpu.VMEM((2, STAGE_ROWS, D_MODEL), F32), pltpu.VMEM((LANES, D_MODEL), F32),
        pltpu.SemaphoreType.DMA((2,)), pltpu.SemaphoreType.DMA((1,)),
        pltpu.VMEM((POOL_PAD + TAIL_ROWS + PROMPT_TILE, POOL_WIDTH), F32),
        pltpu.VMEM((len(POOL_WINDOWS) - 1, POOL_PAD + TAIL_ROWS + PROMPT_TILE, POOL_WIDTH), F32),
        pltpu.VMEM((GLA_HEADS, GLA_DK, GLA_DV), F32),
        pltpu.VMEM((N_META, POOL_WIDTH), F32),
        pltpu.VMEM((GLA_HEADS, GLA_DK, GLA_DV), F32),
        pltpu.VMEM((nbb, SAMPLE_ROWS, D_MODEL), F32),
        pltpu.VMEM((len(POOL_WINDOWS), nbb * SAMPLE_ROWS, POOL_GROUP_DIM), F32),
        pltpu.VMEM((len(POOL_WINDOWS), nbb * SAMPLE_ROWS, POOL_GROUP_DIM), F32),
        pltpu.VMEM((nbb, GLA_HEADS * SAMPLE_ROWS, GLA_KW), BF16),
        pltpu.VMEM((nbb, SAMPLE_ROWS, GLA_KW), F32),
        pltpu.VMEM((nbb, SAMPLE_ROWS, 2 * GLA_VW), F32),
        pltpu.VMEM((nbb, GLA_HEADS * SAMPLE_ROWS, GLA_DV), F32)]

    def mixer(*refs):
        _mixer_kernel(n_s, n_t, *refs)

    x2_s, pool_s, gla_s, x2_p, pool_p, gla_p, wfi_bf, wfo_bf = pl.pallas_call(
        mixer,
        grid=(n_s + n_p,),
        in_specs=in_specs,
        out_specs=out_specs,
        out_shape=out_shape,
        scratch_shapes=scratch_shapes,
        compiler_params=pltpu.CompilerParams(dimension_semantics=("arbitrary",), vmem_limit_bytes=VMEM_LIMIT),
        name="mixer",
    )(x_sample, pool_hist, state_gla, x_prompt, *small, w_in_t, w_pool_proj, w_gla_proj, w_out, w_ffn_in, w_ffn_out)

    def ffn(*refs):
        _ffn_kernel(ts, *refs)

    n_tf = tp // FFN_TILE

    def prompt_tile_index(i):
        tile_id = jnp.maximum(i - 1, 0)
        return (tile_id // n_tf, tile_id % n_tf, 0)

    ffn_small = (g_ffn, wfi_bf, wfo_bf, g_final.reshape(1, D_MODEL))
    y_prompt, y_sample = pl.pallas_call(
        ffn,
        grid=(bp * n_tf + 1,),
        in_specs=[pl.BlockSpec((1, FFN_TILE, d), prompt_tile_index),
                  _const_spec(x2_s.shape)] + [_const_spec(a.shape) for a in ffn_small],
        out_specs=[pl.BlockSpec((1, FFN_TILE, d), prompt_tile_index),
                   pl.BlockSpec(x_sample.shape, lambda i: (0, 0, 0))],
        out_shape=[jax.ShapeDtypeStruct(x_prompt.shape, F32), jax.ShapeDtypeStruct(x_sample.shape, F32)],
        compiler_params=pltpu.CompilerParams(dimension_semantics=("arbitrary",), vmem_limit_bytes=VMEM_LIMIT),
        name="ffn",
    )(x2_p, x2_s, *ffn_small)
    pool_p = jnp.transpose(pool_p, (0, 2, 1, 3))
    pool_s = jnp.transpose(pool_s, (0, 2, 1, 3))
    return y_prompt, y_sample, pool_p, gla_p, pool_s, gla_s
```

```python
import jax
import jax.numpy as jnp
from jax import lax
from jax.experimental import pallas as pl
from jax.experimental.pallas import tpu as pltpu

F32 = jnp.float32
BF16 = jnp.bfloat16

D_MODEL = 1024
N_META = 16
POOL_WIDTH = 512
POOL_WINDOWS = (2, 4, 8, 16)
POOL_GROUP_DIM = 128
POOL_BUF = 15
GLA_HEADS = 4
GLA_DV = 128
GLA_DK = 64
GLA_KW = GLA_HEADS * GLA_DK
GLA_VW = GLA_HEADS * GLA_DV
GLA_GATE_RANK = 16
GLA_TAU = 16.0
GLA_CHUNK = 64
D_FF = 2816
EPS = 1e-6

LANES = 128
SUBLANES = 8
MAIN_W = POOL_WIDTH + 2 * GLA_KW + 2 * GLA_VW
U_COL, VOG_COL, QK_COL, ZR_COL = 0, POOL_WIDTH, POOL_WIDTH + 2 * GLA_VW, MAIN_W
MAIN_COLS = MAIN_W + LANES
GAB_LO = MAIN_W + GLA_GATE_RANK
IN_DIM = GAB_LO + 2 * D_MODEL
TAIL_ROWS = 16
POOL_PAD = SUBLANES

PROMPT_TILE = 512
FFN_TILE = 512
FFN_CHUNK = 256
FFN_OUT_GROUP = 4
SAMPLE_BATCH_BLOCK = 16
SAMPLE_ROWS = SUBLANES
SAMPLE_GATE_SLICES = 8
STAGE_ROWS = 256
FFN_IN_CAST_ROWS = 32
FFN_OUT_CAST_ROWS = 128
V7X_VMEM_BYTES = 64 * 1024 * 1024
VMEM_RESERVE = 4 * 1024 * 1024
VMEM_LIMIT = V7X_VMEM_BYTES - VMEM_RESERVE


def _dot(a, b):
    return jnp.dot(a, b, preferred_element_type=F32)


def _dot_nt(a, b):
    return lax.dot_general(a, b, (((1,), (1,)), ((), ())), preferred_element_type=F32)


def _dot_tn(a, b):
    return lax.dot_general(a, b, (((0,), (0,)), ((), ())), preferred_element_type=F32)


def _rms(x, g):
    return x * lax.rsqrt(jnp.mean(x * x, axis=-1, keepdims=True) + EPS) * g


def _rms_split(x, g):
    r = lax.rsqrt(jnp.mean(x * x, axis=-1, keepdims=True) + EPS)
    return (x * g).astype(BF16), r


def _sigmoid(x):
    return 0.5 * jnp.tanh(0.5 * x) + 0.5


def _silu(x):
    half = 0.5 * x
    return half * jnp.tanh(half) + half


def _log_sigmoid(x):
    return jnp.minimum(x, 0.0) - jnp.log(1.0 + jnp.exp(-jnp.abs(x)))


def _split_bf16(x):
    hi = x.astype(BF16)
    lo = (x - hi.astype(F32)).astype(BF16)
    return hi, lo


class _Weights:
    def __init__(self, gmix, bgk, pscale, gnorm, wpg, wmain, wgab, wgk, wpp, wgp, wout):
        self.gmix, self.bgk, self.pscale, self.gnorm, self.wpg = gmix, bgk, pscale, gnorm, wpg
        self.wmain, self.wgab, self.wgk = wmain, wgab, wgk
        self.wpp, self.wgp, self.wout = wpp, wgp, wout


def _in_proj(x, w):
    h, r = _rms_split(x, w.gmix[...])
    qkz = _dot(h, w.wmain[:, QK_COL:MAIN_COLS])
    q = qkz[:, :GLA_KW] * (r * (GLA_DK ** -0.5))
    k = qkz[:, GLA_KW:2 * GLA_KW] * r
    zr = qkz[:, 2 * GLA_KW:] * r
    z = _dot(zr.astype(BF16), w.wgk[...]) + w.bgk[...]
    u = _dot(h, w.wmain[:, U_COL:VOG_COL]) * r
    vog = _dot(h, w.wmain[:, VOG_COL:QK_COL]) * r
    v = vog[:, :GLA_VW]
    og = vog[:, GLA_VW:]
    return (h, r), u, q, k, v, og, z


def _gate_proj(xn, w, lo, hi):
    h, r = xn
    return _dot(h, w.wgab[:, lo:hi]) * r


def _chunk_cumsum_wide(g, chunk):
    n = g.shape[0] // chunk
    r = lax.broadcasted_iota(jnp.int32, (chunk, chunk), 0)
    c = lax.broadcasted_iota(jnp.int32, (chunk, chunk), 1)
    tri = jnp.where(c <= r, 1.0, 0.0).astype(BF16)
    hi, lo = _split_bf16(jnp.concatenate([g[j * chunk:(j + 1) * chunk] for j in range(n)], axis=1))
    wide = _dot(tri, hi) + _dot(tri, lo)
    width = g.shape[1]
    parts = [wide[:, j * width:(j + 1) * width] for j in range(n)]
    return jnp.concatenate(parts, axis=0), [p[chunk - 1:chunk, :] for p in parts]


def _chunk_cumsum(g, chunk):
    m = g.shape[0]
    r = lax.broadcasted_iota(jnp.int32, (m, m), 0)
    c = lax.broadcasted_iota(jnp.int32, (m, m), 1)
    tri = jnp.where((r // chunk == c // chunk) & (c <= r), 1.0, 0.0).astype(BF16)
    hi, lo = _split_bf16(g)
    return _dot(tri, hi) + _dot(tri, lo)


def _head_lane_mask(width, per_head):
    lane = lax.broadcasted_iota(jnp.int32, (1, width), 1)
    return [(lane // per_head) == h for h in range(GLA_HEADS)]


def _block_diag_rows(x_bf, per_head):
    r = x_bf.shape[0]
    zero = jnp.zeros((r, per_head), x_bf.dtype)
    rows = []
    for h in range(GLA_HEADS):
        rows.append(jnp.concatenate(
            [x_bf[:, h * per_head:(h + 1) * per_head] if hh == h else zero for hh in range(GLA_HEADS)], axis=1))
    return jnp.concatenate(rows, axis=0)


def _gla_post(o, og, w):
    parts = []
    for h in range(GLA_HEADS):
        oh = o[:, h * GLA_DV:(h + 1) * GLA_DV]
        parts.append(oh * lax.rsqrt(jnp.mean(oh * oh, axis=-1, keepdims=True) + EPS) * w.gnorm[...])
    on = jnp.concatenate(parts, axis=1)
    on = on * _silu(og)
    return _dot(on.astype(BF16), w.wgp[...])


def _pool_post(pooled, w):
    pb = pooled.astype(BF16)
    mixed = []
    for p in range(len(POOL_WINDOWS) // 2):
        w_pair = _pair_block_diag(w.wpg[0, 2 * p].astype(BF16), w.wpg[0, 2 * p + 1].astype(BF16))
        mixed.append(_dot(pb[:, 2 * p * POOL_GROUP_DIM:(2 * p + 2) * POOL_GROUP_DIM], w_pair))
    mixed = jnp.concatenate(mixed, axis=1)
    return _dot((mixed * w.pscale[...]).astype(BF16), w.wpp[...])


def _merge(x, y_a, y_b, sa, sb, w):
    merged = sa * y_a + sb * y_b
    return x + _dot(merged.astype(BF16), w.wout[...])


def _decay_columns(decay_row):
    return jnp.transpose(jnp.broadcast_to(decay_row, (LANES, decay_row.shape[1])))


def _pair_block_diag(a, b):
    zero = jnp.zeros(a.shape, a.dtype)
    return jnp.concatenate([jnp.concatenate([a, zero], axis=1), jnp.concatenate([zero, b], axis=1)], axis=0)


def _state_update(s_heads, kd_bf, v_bf, decay_row):
    dcol = _decay_columns(decay_row)
    out = []
    for p in range(GLA_HEADS // 2):
        upd = _dot_tn(kd_bf[:, 2 * p * GLA_DK:(2 * p + 2) * GLA_DK], v_bf[:, 2 * p * GLA_DV:(2 * p + 2) * GLA_DV])
        for j in range(2):
            h = 2 * p + j
            rows = slice(h * GLA_DK, (h + 1) * GLA_DK)
            out.append(dcol[rows] * s_heads[h] + upd[j * GLA_DK:(j + 1) * GLA_DK, j * GLA_DV:(j + 1) * GLA_DV])
    return out


def _stage_weights(wint_hbm, wpp_hbm, wgp_hbm, wout_hbm, wgk_ref, w, stage, zr_stage, sem, zr_sem):
    plan = []
    for r in range(0, MAIN_W, STAGE_ROWS):
        if r < POOL_WIDTH:
            col = U_COL + r
        elif r < POOL_WIDTH + 2 * GLA_KW:
            col = QK_COL + r - POOL_WIDTH
        else:
            col = VOG_COL + r - (POOL_WIDTH + 2 * GLA_KW)
        plan.append((wint_hbm, r, w.wmain, col, True))
    for r in range(0, 2 * D_MODEL, STAGE_ROWS):
        plan.append((wint_hbm, GAB_LO + r, w.wgab, r, True))
    for src, dst, n_rows in ((wpp_hbm, w.wpp, POOL_WIDTH), (wgp_hbm, w.wgp, GLA_VW), (wout_hbm, w.wout, D_MODEL)):
        for r in range(0, n_rows, STAGE_ROWS):
            plan.append((src, r, dst, r, False))

    def copy(j):
        src, r0 = plan[j][0], plan[j][1]
        return pltpu.make_async_copy(src.at[0, pl.ds(r0, STAGE_ROWS), :], stage.at[j % 2], sem.at[j % 2])

    def zr_copy():
        return pltpu.make_async_copy(
            wint_hbm.at[0, pl.ds(MAIN_W, GLA_GATE_RANK), :], zr_stage.at[pl.ds(0, GLA_GATE_RANK), :], zr_sem.at[0])

    copy(0).start()
    copy(1).start()
    zr_copy().start()

    w.wgk[...] = jnp.zeros(w.wgk.shape, BF16)
    w.wgk[0:GLA_GATE_RANK, :] = wgk_ref[0].astype(BF16)
    zr_stage[GLA_GATE_RANK:, :] = jnp.zeros((LANES - GLA_GATE_RANK, D_MODEL), F32)

    for j in range(len(plan)):
        copy(j).wait()
        _, _, dst, d0, transposed = plan[j]
        slab = stage[j % 2]
        if transposed:
            dst[:, d0:d0 + STAGE_ROWS] = jnp.transpose(slab).astype(BF16)
        else:
            dst[d0:d0 + STAGE_ROWS, :] = slab.astype(BF16)
        if j + 2 < len(plan):
            copy(j + 2).start()
    zr_copy().wait()
    w.wmain[:, ZR_COL:MAIN_COLS] = jnp.transpose(zr_stage[...]).astype(BF16)


def _prompt_tile(b_first, b_idx, t_idx, n_t, x_ref, meta_ref, w, x2_ref, pbuf_ref, sout_ref,
                 ext_ref, lvl_ref, s_ref, meta_tail_ref, meta_s_ref):
    tile = x_ref.shape[1]
    n_chunks = tile // GLA_CHUNK

    @pl.when(b_first)
    def _():
        _, u, _, k, v, _, z = _in_proj(meta_ref[...], w)
        meta_tail_ref[...] = u
        g = _log_sigmoid(z) * (1.0 / GLA_TAU)
        b = _chunk_cumsum(g, N_META)
        b_last = b[N_META - 1:N_META, :]
        kd = k * jnp.exp(b_last - b)
        zero_s = [jnp.zeros((GLA_DK, GLA_DV), F32)] * GLA_HEADS
        s_new = _state_update(zero_s, kd.astype(BF16), v.astype(BF16), jnp.exp(b_last))
        for hd in range(GLA_HEADS):
            meta_s_ref[hd] = s_new[hd]

    @pl.when(t_idx == 0)
    def _():
        ext_ref[0:POOL_PAD, :] = jnp.zeros((POOL_PAD, POOL_WIDTH), F32)
        lvl_ref[:, 0:POOL_PAD, :] = jnp.zeros((lvl_ref.shape[0], POOL_PAD, POOL_WIDTH), F32)
        ext_ref[POOL_PAD:POOL_PAD + TAIL_ROWS, :] = meta_tail_ref[...]
        s_ref[...] = meta_s_ref[...]

    x = x_ref[0]
    xn = _rms_split(x, w.gmix[...])
    h, r = xn
    gate_cols = 2 * D_MODEL // n_chunks
    gate_parts = [None] * n_chunks

    def gate_slice(c):
        gate_parts[c] = _sigmoid(_gate_proj(xn, w, c * gate_cols, (c + 1) * gate_cols))

    qkz = _dot(h, w.wmain[:, QK_COL:MAIN_COLS])
    u = _dot(h, w.wmain[:, U_COL:VOG_COL]) * r
    v = _dot(h, w.wmain[:, VOG_COL:VOG_COL + GLA_VW]) * r
    q = qkz[:, :GLA_KW] * (r * (GLA_DK ** -0.5))
    k = qkz[:, GLA_KW:2 * GLA_KW] * r
    zr = qkz[:, 2 * GLA_KW:] * r
    z = _dot(zr.astype(BF16), w.wgk[...]) + w.bgk[...]
    gate_slice(0)
    g = _log_sigmoid(z) * (1.0 / GLA_TAU)
    b, b_last_rows = _chunk_cumsum_wide(g, GLA_CHUNK)
    gate_slice(1)
    og_parts = []

    base = POOL_PAD + TAIL_ROWS
    span = TAIL_ROWS + tile
    ext_ref[base:base + tile, :] = u
    cur = ext_ref[POOL_PAD:POOL_PAD + span, :]
    pooled = []
    for gi, win in enumerate(POOL_WINDOWS):
        shift = win // 2
        lo = gi * POOL_GROUP_DIM
        prev_ref = ext_ref if gi == 0 else lvl_ref.at[gi - 1]
        cur = cur[:, (POOL_GROUP_DIM if gi else 0):] + prev_ref[POOL_PAD - shift:POOL_PAD - shift + span, lo:]
        pooled.append(cur[TAIL_ROWS:, 0:POOL_GROUP_DIM] * (1.0 / win) - u[:, lo:lo + POOL_GROUP_DIM])
        if gi + 1 < len(POOL_WINDOWS):
            lvl_ref[gi, POOL_PAD:POOL_PAD + span, lo:] = cur
    y_a = _pool_post(jnp.concatenate(pooled, axis=1), w)
    ext_ref[POOL_PAD:base, :] = ext_ref[POOL_PAD + tile:base + tile, :]

    b_last = jnp.concatenate([jnp.broadcast_to(r, (GLA_CHUNK, GLA_KW)) for r in b_last_rows], axis=0)
    qe = (q * jnp.exp(b)).astype(BF16)
    ke = k * jnp.exp(-b)
    kd = (k * jnp.exp(b_last - b)).astype(BF16)
    v_bf = v.astype(BF16)

    k_masks = _head_lane_mask(GLA_KW, GLA_DK)
    row_i = lax.broadcasted_iota(jnp.int32, (GLA_CHUNK, GLA_KW), 0)
    col_j = lax.broadcasted_iota(jnp.int32, (GLA_CHUNK, GLA_KW), 1) % GLA_CHUNK
    causal = col_j <= row_i

    s_heads = [s_ref[hd] for hd in range(GLA_HEADS)]
    o_chunks = []
    def chunk_scores(c):
        rows = slice(c * GLA_CHUNK, (c + 1) * GLA_CHUNK)
        ke_c = ke[rows]
        ke_bd = jnp.concatenate([jnp.where(k_masks[hd], ke_c, 0.0) for hd in range(GLA_HEADS)], axis=0).astype(BF16)
        return jnp.where(causal, _dot_nt(qe[rows], ke_bd), 0.0).astype(BF16)

    att_next = chunk_scores(0)
    for c in range(n_chunks):
        att = att_next
        if c + 1 < n_chunks:
            att_next = chunk_scores(c + 1)
        if c + 2 < n_chunks:
            gate_slice(c + 2)
        else:
            og_lo = VOG_COL + GLA_VW + len(og_parts) * (GLA_VW // 2)
            og_parts.append(_dot(h, w.wmain[:, og_lo:og_lo + GLA_VW // 2]) * r)
        rows = slice(c * GLA_CHUNK, (c + 1) * GLA_CHUNK)
        o_pairs = []
        for p in range(GLA_HEADS // 2):
            h0, h1 = 2 * p, 2 * p + 1
            lanes_k = slice(h0 * GLA_DK, (h1 + 1) * GLA_DK)
            lanes_j = slice(h0 * GLA_CHUNK, (h1 + 1) * GLA_CHUNK)
            v0 = v_bf[rows, h0 * GLA_DV:(h0 + 1) * GLA_DV]
            v1 = v_bf[rows, h1 * GLA_DV:(h1 + 1) * GLA_DV]
            rhs = jnp.concatenate([_pair_block_diag(s_heads[h0].astype(BF16), s_heads[h1].astype(BF16)),
                                   _pair_block_diag(v0, v1)], axis=0)
            o_pairs.append(_dot(jnp.concatenate([qe[rows, lanes_k], att[:, lanes_j]], axis=1), rhs))
        o_chunks.append(jnp.concatenate(o_pairs, axis=1))
        s_heads = _state_update(s_heads, kd[rows], v_bf[rows], jnp.exp(b_last_rows[c]))
    for hd in range(GLA_HEADS):
        s_ref[hd] = s_heads[hd]

    y_b = _gla_post(jnp.concatenate(o_chunks, axis=0), jnp.concatenate(og_parts, axis=1), w)
    sg = jnp.concatenate(gate_parts, axis=1)
    x2_ref[0] = _merge(x, y_a, y_b, sg[:, :D_MODEL], sg[:, D_MODEL:], w)

    @pl.when(t_idx == n_t - 1)
    def _():
        sout_ref[0, 0] = s_ref[...]

    for bb in range(pbuf_ref.shape[2]):
        @pl.when((t_idx == n_t - 1) & (b_idx == bb))
        def _():
            for r in range(POOL_BUF):
                row = base - POOL_BUF + r
                pbuf_ref[0, r, bb:bb + 1, :] = ext_ref[row:row + 1, :]


def _sample_block(x_ref, pool_ref, sin_ref, w, x2_ref, pbuf_ref, sout_ref,
                  xs_ref, us_ref, pooled_ref, qm_ref, kdx_ref, rhs_ref, oi_ref):
    nb, seq, _ = x_ref.shape
    rows_pb = SAMPLE_ROWS
    m = nb * rows_pb

    xs_ref[...] = jnp.zeros(xs_ref.shape, F32)
    xs_ref[:, 0:seq, :] = x_ref[...]
    x = xs_ref[...].reshape(m, D_MODEL)
    xn, u, q, k, v, og, z = _in_proj(x, w)

    pooled_ref[...] = jnp.zeros(pooled_ref.shape, F32)
    for gi, win in enumerate(POOL_WINDOWS):
        cols = slice(gi * POOL_GROUP_DIM, (gi + 1) * POOL_GROUP_DIM)
        us_ref[gi] = u[:, cols]
        tok = [us_ref[gi, pl.ds(t, nb, stride=rows_pb), :] for t in range(seq)]
        hist = [pool_ref[0, r, :, cols] for r in range(POOL_BUF)] + tok
        for t in range(seq):
            acc = tok[t]
            for n in range(1, win):
                acc = acc + hist[POOL_BUF + t - n]
            pooled_ref[gi, pl.ds(t, nb, stride=rows_pb), :] = acc * (1.0 / win) - tok[t]
        for r in range(POOL_BUF):
            pbuf_ref[0, r, :, cols] = hist[seq + r]
    y_a = _pool_post(jnp.concatenate([pooled_ref[gi] for gi in range(len(POOL_WINDOWS))], axis=1), w)

    r8 = lax.broadcasted_iota(jnp.int32, (m, 1), 0) % rows_pb
    g = jnp.where(r8 < seq, _log_sigmoid(z) * (1.0 / GLA_TAU), 0.0)
    b = _chunk_cumsum(g, rows_pb)
    b3 = b.reshape(nb, rows_pb, GLA_KW)
    b_last = jnp.broadcast_to(b3[:, seq - 1:seq, :], b3.shape).reshape(m, GLA_KW)
    qe = q * jnp.exp(b)
    ke = k * jnp.exp(-b)
    kd = k * jnp.exp(b_last - b)
    decay = jnp.exp(b_last)
    v_bf = v.astype(BF16)

    k_masks = _head_lane_mask(GLA_KW, GLA_DK)
    ke_bd = jnp.concatenate([jnp.where(k_masks[hd], ke, 0.0) for hd in range(GLA_HEADS)], axis=0).astype(BF16)
    row_i = lax.broadcasted_iota(jnp.int32, (m, GLA_HEADS * m), 0)
    col_j = lax.broadcasted_iota(jnp.int32, (m, GLA_HEADS * m), 1) % m
    keep = (row_i // rows_pb == col_j // rows_pb) & (col_j <= row_i)
    att = jnp.where(keep, _dot_nt(qe.astype(BF16), ke_bd), 0.0).astype(BF16)
    o_intra = _dot(att, _block_diag_rows(v_bf, GLA_DV))

    qe3 = qe.reshape(nb, rows_pb, GLA_KW)
    qm_ref[...] = jnp.concatenate([jnp.where(k_masks[hd], qe3, 0.0) for hd in range(GLA_HEADS)], axis=1).astype(BF16)
    d_hi = decay.astype(BF16).astype(F32)
    d_lo = decay - d_hi
    kdx = jnp.where(r8 == seq, d_hi, jnp.where(r8 == seq + 1, d_lo, kd))
    kdx_ref[...] = kdx.reshape(nb, rows_pb, GLA_KW)
    ones_rows = jnp.where((r8 == seq) | (r8 == seq + 1), 1.0, 0.0) + jnp.zeros((m, GLA_DV), F32)
    rhs = jnp.concatenate(
        [piece for hd in range(GLA_HEADS) for piece in (v[:, hd * GLA_DV:(hd + 1) * GLA_DV], ones_rows)], axis=1)
    rhs_ref[...] = rhs.reshape(nb, rows_pb, 2 * GLA_VW)

    def per_batch(i):
        s_all = sin_ref[0, i]
        s_flat = s_all.reshape(GLA_KW, GLA_DV).astype(BF16)
        oi_ref[i] = _dot(qm_ref[i], s_flat)
        kdt = jnp.transpose(kdx_ref[i]).astype(BF16)
        rhs_i = rhs_ref[i].astype(BF16)
        for hd in range(GLA_HEADS):
            r = _dot(kdt[hd * GLA_DK:(hd + 1) * GLA_DK], rhs_i[:, hd * 2 * GLA_DV:(hd + 1) * 2 * GLA_DV])
            sout_ref[0, i, hd] = r[:, GLA_DV:] * s_all[hd] + r[:, :GLA_DV]

    batches_per_slice = nb // SAMPLE_GATE_SLICES
    gate_cols = 2 * D_MODEL // SAMPLE_GATE_SLICES
    gate_parts = []
    for i in range(nb):
        if i % batches_per_slice == 0:
            c = i // batches_per_slice
            gate_parts.append(_sigmoid(_gate_proj(xn, w, c * gate_cols, (c + 1) * gate_cols)))
        per_batch(i)

    oi = oi_ref[...]
    o_inter = jnp.concatenate([oi[:, hd * rows_pb:(hd + 1) * rows_pb, :] for hd in range(GLA_HEADS)], axis=2)
    o = o_intra + o_inter.reshape(m, GLA_VW)
    y_b = _gla_post(o, og, w)
    sg = jnp.concatenate(gate_parts, axis=1)
    x2 = _merge(x, y_a, y_b, sg[:, :D_MODEL], sg[:, D_MODEL:], w).reshape(nb, rows_pb, D_MODEL)
    for bi in range(nb):
        x2_ref[bi * seq:(bi + 1) * seq, :] = x2[bi, 0:seq, :]


def _mixer_kernel(n_s, n_t,
                  xs_in, pool_in, s_in, xp_in, meta_ref, gmix_ref, bgk_ref, pscale_ref, gnorm_ref, wgk_ref, wpg_ref,
                  wint_hbm, wpp_hbm, wgp_hbm, wout_hbm, wfi_in, wfo_in,
                  x2s_out, pools_out, ss_out, x2p_out, poolp_out, sp_out, wfi_out, wfo_out,
                  wmain_s, wgab_s, wgk_s, wpp_s, wgp_s, wout_s, stage, zr_stage, sem, zr_sem,
                  ext_ref, lvl_ref, s_ref, meta_tail_ref, meta_s_ref,
                  xs_ref, us_ref, pooled_ref, qm_ref, kdx_ref, rhs_ref, oi_ref):
    i = pl.program_id(0)
    w = _Weights(gmix_ref, bgk_ref, pscale_ref, gnorm_ref, wpg_ref,
                 wmain_s, wgab_s, wgk_s, wpp_s, wgp_s, wout_s)

    @pl.when(i == 0)
    def _():
        _stage_weights(wint_hbm, wpp_hbm, wgp_hbm, wout_hbm, wgk_ref, w, stage, zr_stage, sem, zr_sem)

    wfi_out[...] = wfi_in[0].astype(BF16)
    wfo_out[...] = wfo_in[0].astype(BF16)

    @pl.when(i < n_s)
    def _():
        _sample_block(xs_in, pool_in, s_in, w, x2s_out, pools_out, ss_out,
                      xs_ref, us_ref, pooled_ref, qm_ref, kdx_ref, rhs_ref, oi_ref)

    @pl.when(i >= n_s)
    def _():
        t_idx = (i - n_s) % n_t
        _prompt_tile(i == n_s, (i - n_s) // n_t, t_idx, n_t, xp_in, meta_ref, w, x2p_out, poolp_out, sp_out,
                     ext_ref, lvl_ref, s_ref, meta_tail_ref, meta_s_ref)


def _ffn_kernel(seq, xp_ref, xs_ref, gffn_ref, wi_ref, wo_ref, gfin_ref, yp_ref, ys_ref):
    i = pl.program_id(0)
    x = jnp.where(i == 0, xs_ref[...], xp_ref[0])
    h, r = _rms_split(x, gffn_ref[...])
    acc = x
    group = []
    for lo in range(0, D_FF, FFN_CHUNK):
        gate = _dot(h, wi_ref[:, lo:lo + FFN_CHUNK]) * r
        up = _dot(h, wi_ref[:, D_FF + lo:D_FF + lo + FFN_CHUNK]) * r
        group.append((_silu(gate) * up).astype(BF16))
        if len(group) == FFN_OUT_GROUP or lo + FFN_CHUNK == D_FF:
            first = lo + FFN_CHUNK - len(group) * FFN_CHUNK
            acc = acc + _dot(jnp.concatenate(group, axis=1), wo_ref[first:lo + FFN_CHUNK, :])
            group = []
    yp_ref[0] = _rms(acc, gfin_ref[...])

    @pl.when(i == 0)
    def _():
        for bi in range(ys_ref.shape[0]):
            ys_ref[bi] = yp_ref[0, bi * seq:(bi + 1) * seq, :]


def _const_spec(shape):
    zeros = (0,) * len(shape)
    return pl.BlockSpec(shape, lambda *_: zeros, pipeline_mode=pl.Buffered(1))


def kernel(x_prompt, x_sample, state_pool, state_gla, meta_tokens, g_mix, w_in, w_gk_up, b_gk, w_pool_group,
           pool_scale, w_pool_proj, g_gla_norm, w_gla_proj, w_out, g_ffn, w_ffn_in, w_ffn_out, g_final):
    depth = w_in.shape[0]
    assert depth == 1, "single-layer trunk only"
    bp, tp, d = x_prompt.shape
    bs, ts, _ = x_sample.shape
    nbb = SAMPLE_BATCH_BLOCK
    assert d == D_MODEL and w_in.shape == (1, D_MODEL, IN_DIM) and meta_tokens.shape == (N_META, D_MODEL)
    assert tp % PROMPT_TILE == 0 and tp % FFN_TILE == 0 and PROMPT_TILE % GLA_CHUNK == 0
    assert bs % nbb == 0 and ts + 2 <= SAMPLE_ROWS and bs * ts == FFN_TILE
    n_t = tp // PROMPT_TILE
    n_p = bp * n_t
    n_s = bs // nbb
    n_fi = D_MODEL // FFN_IN_CAST_ROWS
    n_fo = D_FF // FFN_OUT_CAST_ROWS
    assert n_fi <= n_p and n_fo <= n_p

    def s_idx(i):
        return jnp.minimum(i, n_s - 1)

    def p_idx(i):
        return jnp.maximum(i - n_s, 0)

    w_in_t = jnp.transpose(w_in, (0, 2, 1))
    pool_hist = jnp.transpose(state_pool, (0, 2, 1, 3))
    small = (meta_tokens, g_mix, b_gk, pool_scale, g_gla_norm, w_gk_up, w_pool_group)
    hbm = pl.BlockSpec(memory_space=pl.ANY)
    in_specs = (
        [pl.BlockSpec((nbb, ts, d), lambda i: (s_idx(i), 0, 0)),
         pl.BlockSpec((1, POOL_BUF, nbb, POOL_WIDTH), lambda i: (0, 0, s_idx(i), 0)),
         pl.BlockSpec((1, nbb, GLA_HEADS, GLA_DK, GLA_DV), lambda i: (0, s_idx(i), 0, 0, 0)),
         pl.BlockSpec((1, PROMPT_TILE, d), lambda i: (p_idx(i) // n_t, p_idx(i) % n_t, 0))]
        + [_const_spec(a.shape) for a in small]
        + [hbm, hbm, hbm, hbm,
           pl.BlockSpec((1, FFN_IN_CAST_ROWS, 2 * D_FF), lambda i: (0, jnp.minimum(p_idx(i), n_fi - 1), 0)),
           pl.BlockSpec((1, FFN_OUT_CAST_ROWS, d), lambda i: (0, jnp.minimum(p_idx(i), n_fo - 1), 0))])
    out_specs = [
        pl.BlockSpec((nbb * ts, d), lambda i: (s_idx(i), 0)),
        pl.BlockSpec((1, POOL_BUF, nbb, POOL_WIDTH), lambda i: (0, 0, s_idx(i), 0)),
        pl.BlockSpec((1, nbb, GLA_HEADS, GLA_DK, GLA_DV), lambda i: (0, s_idx(i), 0, 0, 0)),
        pl.BlockSpec((1, PROMPT_TILE, d), lambda i: (p_idx(i) // n_t, p_idx(i) % n_t, 0)),
        pl.BlockSpec((1, POOL_BUF, bp, POOL_WIDTH), lambda i: (0, 0, 0, 0)),
        pl.BlockSpec((1, 1, GLA_HEADS, GLA_DK, GLA_DV), lambda i: (0, p_idx(i) // n_t, 0, 0, 0)),
        pl.BlockSpec((FFN_IN_CAST_ROWS, 2 * D_FF), lambda i: (jnp.minimum(p_idx(i), n_fi - 1), 0)),
        pl.BlockSpec((FFN_OUT_CAST_ROWS, d), lambda i: (jnp.minimum(p_idx(i), n_fo - 1), 0))]
    out_shape = [
        jax.ShapeDtypeStruct((bs * ts, d), F32),
        jax.ShapeDtypeStruct((1, POOL_BUF, bs, POOL_WIDTH), F32),
        jax.ShapeDtypeStruct(state_gla.shape, F32),
        jax.ShapeDtypeStruct(x_prompt.shape, F32),
        jax.ShapeDtypeStruct((1, POOL_BUF, bp, POOL_WIDTH), F32),
        jax.ShapeDtypeStruct((1, bp, GLA_HEADS, GLA_DK, GLA_DV), F32),
        jax.ShapeDtypeStruct((D_MODEL, 2 * D_FF), BF16),
        jax.ShapeDtypeStruct((D_FF, D_MODEL), BF16)]
    scratch_shapes = [
        pltpu.VMEM((D_MODEL, MAIN_COLS), BF16),
        pltpu.VMEM((D_MODEL, 2 * D_MODEL), BF16), pltpu.VMEM((LANES, GLA_KW), BF16),
        pltpu.VMEM((POOL_WIDTH, D_MODEL), BF16), pltpu.VMEM((GLA_VW, D_MODEL), BF16),
        pltpu.VMEM((D_MODEL, D_MODEL), BF16),
        pltpu.VMEM((2, STAGE_ROWS, D_MODEL), F32), pltpu.VMEM((LANES, D_MODEL), F32),
        pltpu.SemaphoreType.DMA((2,)), pltpu.SemaphoreType.DMA((1,)),
        pltpu.VMEM((POOL_PAD + TAIL_ROWS + PROMPT_TILE, POOL_WIDTH), F32),
        pltpu.VMEM((len(POOL_WINDOWS) - 1, POOL_PAD + TAIL_ROWS + PROMPT_TILE, POOL_WIDTH), F32),
        pltpu.VMEM((GLA_HEADS, GLA_DK, GLA_DV), F32),
        pltpu.VMEM((N_META, POOL_WIDTH), F32),
        pltpu.VMEM((GLA_HEADS, GLA_DK, GLA_DV), F32),
        pltpu.VMEM((nbb, SAMPLE_ROWS, D_MODEL), F32),
        pltpu.VMEM((len(POOL_WINDOWS), nbb * SAMPLE_ROWS, POOL_GROUP_DIM), F32),
        pltpu.VMEM((len(POOL_WINDOWS), nbb * SAMPLE_ROWS, POOL_GROUP_DIM), F32),
        pltpu.VMEM((nbb, GLA_HEADS * SAMPLE_ROWS, GLA_KW), BF16),
        pltpu.VMEM((nbb, SAMPLE_ROWS, GLA_KW), F32),
        pltpu.VMEM((nbb, SAMPLE_ROWS, 2 * GLA_VW), F32),
        pltpu.VMEM((nbb, GLA_HEADS * SAMPLE_ROWS, GLA_DV), F32)]

    def mixer(*refs):
        _mixer_kernel(n_s, n_t, *refs)

    x2_s, pool_s, gla_s, x2_p, pool_p, gla_p, wfi_bf, wfo_bf = pl.pallas_call(
        mixer,
        grid=(n_s + n_p,),
        in_specs=in_specs,
        out_specs=out_specs,
        out_shape=out_shape,
        scratch_shapes=scratch_shapes,
        compiler_params=pltpu.CompilerParams(dimension_semantics=("arbitrary",), vmem_limit_bytes=VMEM_LIMIT),
        name="mixer",
    )(x_sample, pool_hist, state_gla, x_prompt, *small, w_in_t, w_pool_proj, w_gla_proj, w_out, w_ffn_in, w_ffn_out)

    def ffn(*refs):
        _ffn_kernel(ts, *refs)

    n_tf = tp // FFN_TILE

    def prompt_tile_index(i):
        tile_id = jnp.maximum(i - 1, 0)
        return (tile_id // n_tf, tile_id % n_tf, 0)

    ffn_small = (g_ffn, wfi_bf, wfo_bf, g_final.reshape(1, D_MODEL))
    y_prompt, y_sample = pl.pallas_call(
        ffn,
        grid=(bp * n_tf + 1,),
        in_specs=[pl.BlockSpec((1, FFN_TILE, d), prompt_tile_index),
                  _const_spec(x2_s.shape)] + [_const_spec(a.shape) for a in ffn_small],
        out_specs=[pl.BlockSpec((1, FFN_TILE, d), prompt_tile_index),
                   pl.BlockSpec(x_sample.shape, lambda i: (0, 0, 0))],
        out_shape=[jax.ShapeDtypeStruct(x_prompt.shape, F32), jax.ShapeDtypeStruct(x_sample.shape, F32)],
        compiler_params=pltpu.CompilerParams(dimension_semantics=("arbitrary",), vmem_limit_bytes=VMEM_LIMIT),
        name="ffn",
    )(x2_p, x2_s, *ffn_small)
    pool_p = jnp.transpose(pool_p, (0, 2, 1, 3))
    pool_s = jnp.transpose(pool_s, (0, 2, 1, 3))
    return y_prompt, y_sample, pool_p, gla_p, pool_s, gla_s
```

```python
import jax
import jax.numpy as jnp
from jax import lax
from jax.experimental import pallas as pl
from jax.experimental.pallas import tpu as pltpu

F32 = jnp.float32
BF16 = jnp.bfloat16

D_MODEL = 1024
N_META = 16
POOL_WIDTH = 512
POOL_WINDOWS = (2, 4, 8, 16)
POOL_GROUP_DIM = 128
POOL_BUF = 15
GLA_HEADS = 4
GLA_DV = 128
GLA_DK = 64
GLA_KW = GLA_HEADS * GLA_DK
GLA_VW = GLA_HEADS * GLA_DV
GLA_GATE_RANK = 16
GLA_TAU = 16.0
GLA_CHUNK = 64
D_FF = 2816
EPS = 1e-6

LANES = 128
SUBLANES = 8
MAIN_W = POOL_WIDTH + 2 * GLA_KW + 2 * GLA_VW
U_COL, VOG_COL, QK_COL, ZR_COL = 0, POOL_WIDTH, POOL_WIDTH + 2 * GLA_VW, MAIN_W
MAIN_COLS = MAIN_W + LANES
GAB_LO = MAIN_W + GLA_GATE_RANK
IN_DIM = GAB_LO + 2 * D_MODEL
TAIL_ROWS = 16
POOL_PAD = SUBLANES

PROMPT_TILE = 512
FFN_TILE = 512
FFN_CHUNK = 256
FFN_OUT_GROUP = 4
SAMPLE_BATCH_BLOCK = 16
SAMPLE_ROWS = SUBLANES
SAMPLE_GATE_SLICES = 8
STAGE_ROWS = 256
FFN_IN_CAST_ROWS = 32
FFN_OUT_CAST_ROWS = 128
V7X_VMEM_BYTES = 64 * 1024 * 1024
VMEM_RESERVE = 4 * 1024 * 1024
VMEM_LIMIT = V7X_VMEM_BYTES - VMEM_RESERVE


def _dot(a, b):
    return jnp.dot(a, b, preferred_element_type=F32)


def _dot_nt(a, b):
    return lax.dot_general(a, b, (((1,), (1,)), ((), ())), preferred_element_type=F32)


def _dot_tn(a, b):
    return lax.dot_general(a, b, (((0,), (0,)), ((), ())), preferred_element_type=F32)


def _rms(x, g):
    return x * lax.rsqrt(jnp.mean(x * x, axis=-1, keepdims=True) + EPS) * g


def _rms_split(x, g):
    r = lax.rsqrt(jnp.mean(x * x, axis=-1, keepdims=True) + EPS)
    return (x * g).astype(BF16), r


def _sigmoid(x):
    return 0.5 * jnp.tanh(0.5 * x) + 0.5


def _silu(x):
    half = 0.5 * x
    return half * jnp.tanh(half) + half


def _log_sigmoid(x):
    return jnp.minimum(x, 0.0) - jnp.log(1.0 + jnp.exp(-jnp.abs(x)))


def _split_bf16(x):
    hi = x.astype(BF16)
    lo = (x - hi.astype(F32)).astype(BF16)
    return hi, lo


class _Weights:
    def __init__(self, gmix, bgk, pscale, gnorm, wpg, wmain, wgab, wgk, wpp, wgp, wout):
        self.gmix, self.bgk, self.pscale, self.gnorm, self.wpg = gmix, bgk, pscale, gnorm, wpg
        self.wmain, self.wgab, self.wgk = wmain, wgab, wgk
        self.wpp, self.wgp, self.wout = wpp, wgp, wout


def _in_proj(x, w):
    h, r = _rms_split(x, w.gmix[...])
    qkz = _dot(h, w.wmain[:, QK_COL:MAIN_COLS])
    q = qkz[:, :GLA_KW] * (r * (GLA_DK ** -0.5))
    k = qkz[:, GLA_KW:2 * GLA_KW] * r
    zr = qkz[:, 2 * GLA_KW:] * r
    z = _dot(zr.astype(BF16), w.wgk[...]) + w.bgk[...]
    u = _dot(h, w.wmain[:, U_COL:VOG_COL]) * r
    vog = _dot(h, w.wmain[:, VOG_COL:QK_COL]) * r
    v = vog[:, :GLA_VW]
    og = vog[:, GLA_VW:]
    return (h, r), u, q, k, v, og, z


def _gate_proj(xn, w, lo, hi):
    h, r = xn
    return _dot(h, w.wgab[:, lo:hi]) * r


def _chunk_cumsum_wide(g, chunk):
    n = g.shape[0] // chunk
    r = lax.broadcasted_iota(jnp.int32, (chunk, chunk), 0)
    c = lax.broadcasted_iota(jnp.int32, (chunk, chunk), 1)
    tri = jnp.where(c <= r, 1.0, 0.0).astype(BF16)
    hi, lo = _split_bf16(jnp.concatenate([g[j * chunk:(j + 1) * chunk] for j in range(n)], axis=1))
    wide = _dot(tri, hi) + _dot(tri, lo)
    width = g.shape[1]
    parts = [wide[:, j * width:(j + 1) * width] for j in range(n)]
    return jnp.concatenate(parts, axis=0), [p[chunk - 1:chunk, :] for p in parts]


def _chunk_cumsum(g, chunk):
    m = g.shape[0]
    r = lax.broadcasted_iota(jnp.int32, (m, m), 0)
    c = lax.broadcasted_iota(jnp.int32, (m, m), 1)
    tri = jnp.where((r // chunk == c // chunk) & (c <= r), 1.0, 0.0).astype(BF16)
    hi, lo = _split_bf16(g)
    return _dot(tri, hi) + _dot(tri, lo)


def _head_lane_mask(width, per_head):
    lane = lax.broadcasted_iota(jnp.int32, (1, width), 1)
    return [(lane // per_head) == h for h in range(GLA_HEADS)]


def _block_diag_rows(x_bf, per_head):
    r = x_bf.shape[0]
    zero = jnp.zeros((r, per_head), x_bf.dtype)
    rows = []
    for h in range(GLA_HEADS):
        rows.append(jnp.concatenate(
            [x_bf[:, h * per_head:(h + 1) * per_head] if hh == h else zero for hh in range(GLA_HEADS)], axis=1))
    return jnp.concatenate(rows, axis=0)


def _gla_post(o, og, w):
    parts = []
    for h in range(GLA_HEADS):
        oh = o[:, h * GLA_DV:(h + 1) * GLA_DV]
        parts.append(oh * lax.rsqrt(jnp.mean(oh * oh, axis=-1, keepdims=True) + EPS) * w.gnorm[...])
    on = jnp.concatenate(parts, axis=1)
    on = on * _silu(og)
    return _dot(on.astype(BF16), w.wgp[...])


def _pool_post(pooled, w):
    pb = pooled.astype(BF16)
    mixed = []
    for p in range(len(POOL_WINDOWS) // 2):
        w_pair = _pair_block_diag(w.wpg[0, 2 * p].astype(BF16), w.wpg[0, 2 * p + 1].astype(BF16))
        mixed.append(_dot(pb[:, 2 * p * POOL_GROUP_DIM:(2 * p + 2) * POOL_GROUP_DIM], w_pair))
    mixed = jnp.concatenate(mixed, axis=1)
    return _dot((mixed * w.pscale[...]).astype(BF16), w.wpp[...])


def _merge(x, y_a, y_b, sa, sb, w):
    merged = sa * y_a + sb * y_b
    return x + _dot(merged.astype(BF16), w.wout[...])


def _decay_columns(decay_row):
    return jnp.transpose(jnp.broadcast_to(decay_row, (LANES, decay_row.shape[1])))


def _pair_block_diag(a, b):
    zero = jnp.zeros(a.shape, a.dtype)
    return jnp.concatenate([jnp.concatenate([a, zero], axis=1), jnp.concatenate([zero, b], axis=1)], axis=0)


def _state_update(s_heads, kd_bf, v_bf, decay_row):
    dcol = _decay_columns(decay_row)
    out = []
    for p in range(GLA_HEADS // 2):
        upd = _dot_tn(kd_bf[:, 2 * p * GLA_DK:(2 * p + 2) * GLA_DK], v_bf[:, 2 * p * GLA_DV:(2 * p + 2) * GLA_DV])
        for j in range(2):
            h = 2 * p + j
            rows = slice(h * GLA_DK, (h + 1) * GLA_DK)
            out.append(dcol[rows] * s_heads[h] + upd[j * GLA_DK:(j + 1) * GLA_DK, j * GLA_DV:(j + 1) * GLA_DV])
    return out


def _stage_weights(wint_hbm, wpp_hbm, wgp_hbm, wout_hbm, wgk_ref, w, stage, zr_stage, sem, zr_sem):
    plan = []
    for r in range(0, MAIN_W, STAGE_ROWS):
        if r < POOL_WIDTH:
            col = U_COL + r
        elif r < POOL_WIDTH + 2 * GLA_KW:
            col = QK_COL + r - POOL_WIDTH
        else:
            col = VOG_COL + r - (POOL_WIDTH + 2 * GLA_KW)
        plan.append((wint_hbm, r, w.wmain, col, True))
    for r in range(0, 2 * D_MODEL, STAGE_ROWS):
        plan.append((wint_hbm, GAB_LO + r, w.wgab, r, True))
    for src, dst, n_rows in ((wpp_hbm, w.wpp, POOL_WIDTH), (wgp_hbm, w.wgp, GLA_VW), (wout_hbm, w.wout, D_MODEL)):
        for r in range(0, n_rows, STAGE_ROWS):
            plan.append((src, r, dst, r, False))

    def copy(j):
        src, r0 = plan[j][0], plan[j][1]
        return pltpu.make_async_copy(src.at[0, pl.ds(r0, STAGE_ROWS), :], stage.at[j % 2], sem.at[j % 2])

    def zr_copy():
        return pltpu.make_async_copy(
            wint_hbm.at[0, pl.ds(MAIN_W, GLA_GATE_RANK), :], zr_stage.at[pl.ds(0, GLA_GATE_RANK), :], zr_sem.at[0])

    copy(0).start()
    copy(1).start()
    zr_copy().start()

    w.wgk[...] = jnp.zeros(w.wgk.shape, BF16)
    w.wgk[0:GLA_GATE_RANK, :] = wgk_ref[0].astype(BF16)
    zr_stage[GLA_GATE_RANK:, :] = jnp.zeros((LANES - GLA_GATE_RANK, D_MODEL), F32)

    for j in range(len(plan)):
        copy(j).wait()
        _, _, dst, d0, transposed = plan[j]
        slab = stage[j % 2]
        if transposed:
            dst[:, d0:d0 + STAGE_ROWS] = jnp.transpose(slab).astype(BF16)
        else:
            dst[d0:d0 + STAGE_ROWS, :] = slab.astype(BF16)
        if j + 2 < len(plan):
            copy(j + 2).start()
    zr_copy().wait()
    w.wmain[:, ZR_COL:MAIN_COLS] = jnp.transpose(zr_stage[...]).astype(BF16)


def _prompt_tile(b_first, b_idx, t_idx, n_t, x_ref, meta_ref, w, x2_ref, pbuf_ref, sout_ref,
                 ext_ref, lvl_ref, s_ref, meta_tail_ref, meta_s_ref):
    tile = x_ref.shape[1]
    n_chunks = tile // GLA_CHUNK

    @pl.when(b_first)
    def _():
        _, u, _, k, v, _, z = _in_proj(meta_ref[...], w)
        meta_tail_ref[...] = u
        g = _log_sigmoid(z) * (1.0 / GLA_TAU)
        b = _chunk_cumsum(g, N_META)
        b_last = b[N_META - 1:N_META, :]
        kd = k * jnp.exp(b_last - b)
        zero_s = [jnp.zeros((GLA_DK, GLA_DV), F32)] * GLA_HEADS
        s_new = _state_update(zero_s, kd.astype(BF16), v.astype(BF16), jnp.exp(b_last))
        for hd in range(GLA_HEADS):
            meta_s_ref[hd] = s_new[hd]

    @pl.when(t_idx == 0)
    def _():
        ext_ref[0:POOL_PAD, :] = jnp.zeros((POOL_PAD, POOL_WIDTH), F32)
        lvl_ref[:, 0:POOL_PAD, :] = jnp.zeros((lvl_ref.shape[0], POOL_PAD, POOL_WIDTH), F32)
        ext_ref[POOL_PAD:POOL_PAD + TAIL_ROWS, :] = meta_tail_ref[...]
        s_ref[...] = meta_s_ref[...]

    x = x_ref[0]
    xn = _rms_split(x, w.gmix[...])
    h, r = xn
    gate_cols = 2 * D_MODEL // n_chunks
    gate_parts = [None] * n_chunks

    def gate_slice(c):
        gate_parts[c] = _sigmoid(_gate_proj(xn, w, c * gate_cols, (c + 1) * gate_cols))

    qkz = _dot(h, w.wmain[:, QK_COL:MAIN_COLS])
    u = _dot(h, w.wmain[:, U_COL:VOG_COL]) * r
    v = _dot(h, w.wmain[:, VOG_COL:VOG_COL + GLA_VW]) * r
    q = qkz[:, :GLA_KW] * (r * (GLA_DK ** -0.5))
    k = qkz[:, GLA_KW:2 * GLA_KW] * r
    zr = qkz[:, 2 * GLA_KW:] * r
    z = _dot(zr.astype(BF16), w.wgk[...]) + w.bgk[...]
    gate_slice(0)
    g = _log_sigmoid(z) * (1.0 / GLA_TAU)
    b, b_last_rows = _chunk_cumsum_wide(g, GLA_CHUNK)
    gate_slice(1)
    og_parts = []

    base = POOL_PAD + TAIL_ROWS
    span = TAIL_ROWS + tile
    ext_ref[base:base + tile, :] = u
    cur = ext_ref[POOL_PAD:POOL_PAD + span, :]
    pooled = []
    for gi, win in enumerate(POOL_WINDOWS):
        shift = win // 2
        lo = gi * POOL_GROUP_DIM
        prev_ref = ext_ref if gi == 0 else lvl_ref.at[gi - 1]
        cur = cur[:, (POOL_GROUP_DIM if gi else 0):] + prev_ref[POOL_PAD - shift:POOL_PAD - shift + span, lo:]
        pooled.append(cur[TAIL_ROWS:, 0:POOL_GROUP_DIM] * (1.0 / win) - u[:, lo:lo + POOL_GROUP_DIM])
        if gi + 1 < len(POOL_WINDOWS):
            lvl_ref[gi, POOL_PAD:POOL_PAD + span, lo:] = cur
    y_a = _pool_post(jnp.concatenate(pooled, axis=1), w)
    ext_ref[POOL_PAD:base, :] = ext_ref[POOL_PAD + tile:base + tile, :]

    b_last = jnp.concatenate([jnp.broadcast_to(r, (GLA_CHUNK, GLA_KW)) for r in b_last_rows], axis=0)
    qe = (q * jnp.exp(b)).astype(BF16)
    ke = k * jnp.exp(-b)
    kd = (k * jnp.exp(b_last - b)).astype(BF16)
    v_bf = v.astype(BF16)

    k_masks = _head_lane_mask(GLA_KW, GLA_DK)
    row_i = lax.broadcasted_iota(jnp.int32, (GLA_CHUNK, GLA_KW), 0)
    col_j = lax.broadcasted_iota(jnp.int32, (GLA_CHUNK, GLA_KW), 1) % GLA_CHUNK
    causal = col_j <= row_i

    s_heads = [s_ref[hd] for hd in range(GLA_HEADS)]
    o_chunks = []
    for c in range(n_chunks):
        if c + 2 < n_chunks:
            gate_slice(c + 2)
        else:
            og_lo = VOG_COL + GLA_VW + len(og_parts) * (GLA_VW // 2)
            og_parts.append(_dot(h, w.wmain[:, og_lo:og_lo + GLA_VW // 2]) * r)
        rows = slice(c * GLA_CHUNK, (c + 1) * GLA_CHUNK)
        ke_c = ke[rows]
        ke_bd = jnp.concatenate([jnp.where(k_masks[hd], ke_c, 0.0) for hd in range(GLA_HEADS)], axis=0).astype(BF16)
        att = jnp.where(causal, _dot_nt(qe[rows], ke_bd), 0.0).astype(BF16)
        o_pairs = []
        for p in range(GLA_HEADS // 2):
            h0, h1 = 2 * p, 2 * p + 1
            lanes_k = slice(h0 * GLA_DK, (h1 + 1) * GLA_DK)
            lanes_j = slice(h0 * GLA_CHUNK, (h1 + 1) * GLA_CHUNK)
            v0 = v_bf[rows, h0 * GLA_DV:(h0 + 1) * GLA_DV]
            v1 = v_bf[rows, h1 * GLA_DV:(h1 + 1) * GLA_DV]
            rhs = jnp.concatenate([_pair_block_diag(s_heads[h0].astype(BF16), s_heads[h1].astype(BF16)),
                                   _pair_block_diag(v0, v1)], axis=0)
            o_pairs.append(_dot(jnp.concatenate([qe[rows, lanes_k], att[:, lanes_j]], axis=1), rhs))
        o_chunks.append(jnp.concatenate(o_pairs, axis=1))
        s_heads = _state_update(s_heads, kd[rows], v_bf[rows], jnp.exp(b_last_rows[c]))
    for hd in range(GLA_HEADS):
        s_ref[hd] = s_heads[hd]

    y_b = _gla_post(jnp.concatenate(o_chunks, axis=0), jnp.concatenate(og_parts, axis=1), w)
    sg = jnp.concatenate(gate_parts, axis=1)
    x2_ref[0] = _merge(x, y_a, y_b, sg[:, :D_MODEL], sg[:, D_MODEL:], w)

    @pl.when(t_idx == n_t - 1)
    def _():
        sout_ref[0, 0] = s_ref[...]

    for bb in range(pbuf_ref.shape[2]):
        @pl.when((t_idx == n_t - 1) & (b_idx == bb))
        def _():
            for r in range(POOL_BUF):
                row = base - POOL_BUF + r
                pbuf_ref[0, r, bb:bb + 1, :] = ext_ref[row:row + 1, :]


def _sample_block(x_ref, pool_ref, sin_ref, w, x2_ref, pbuf_ref, sout_ref,
                  xs_ref, us_ref, pooled_ref, qm_ref, kdx_ref, rhs_ref, oi_ref):
    nb, seq, _ = x_ref.shape
    rows_pb = SAMPLE_ROWS
    m = nb * rows_pb

    xs_ref[...] = jnp.zeros(xs_ref.shape, F32)
    xs_ref[:, 0:seq, :] = x_ref[...]
    x = xs_ref[...].reshape(m, D_MODEL)
    xn, u, q, k, v, og, z = _in_proj(x, w)

    pooled_ref[...] = jnp.zeros(pooled_ref.shape, F32)
    for gi, win in enumerate(POOL_WINDOWS):
        cols = slice(gi * POOL_GROUP_DIM, (gi + 1) * POOL_GROUP_DIM)
        us_ref[gi] = u[:, cols]
        tok = [us_ref[gi, pl.ds(t, nb, stride=rows_pb), :] for t in range(seq)]
        hist = [pool_ref[0, r, :, cols] for r in range(POOL_BUF)] + tok
        for t in range(seq):
            acc = tok[t]
            for n in range(1, win):
                acc = acc + hist[POOL_BUF + t - n]
            pooled_ref[gi, pl.ds(t, nb, stride=rows_pb), :] = acc * (1.0 / win) - tok[t]
        for r in range(POOL_BUF):
            pbuf_ref[0, r, :, cols] = hist[seq + r]
    y_a = _pool_post(jnp.concatenate([pooled_ref[gi] for gi in range(len(POOL_WINDOWS))], axis=1), w)

    r8 = lax.broadcasted_iota(jnp.int32, (m, 1), 0) % rows_pb
    g = jnp.where(r8 < seq, _log_sigmoid(z) * (1.0 / GLA_TAU), 0.0)
    b = _chunk_cumsum(g, rows_pb)
    b3 = b.reshape(nb, rows_pb, GLA_KW)
    b_last = jnp.broadcast_to(b3[:, seq - 1:seq, :], b3.shape).reshape(m, GLA_KW)
    qe = q * jnp.exp(b)
    ke = k * jnp.exp(-b)
    kd = k * jnp.exp(b_last - b)
    decay = jnp.exp(b_last)
    v_bf = v.astype(BF16)

    k_masks = _head_lane_mask(GLA_KW, GLA_DK)
    ke_bd = jnp.concatenate([jnp.where(k_masks[hd], ke, 0.0) for hd in range(GLA_HEADS)], axis=0).astype(BF16)
    row_i = lax.broadcasted_iota(jnp.int32, (m, GLA_HEADS * m), 0)
    col_j = lax.broadcasted_iota(jnp.int32, (m, GLA_HEADS * m), 1) % m
    keep = (row_i // rows_pb == col_j // rows_pb) & (col_j <= row_i)
    att = jnp.where(keep, _dot_nt(qe.astype(BF16), ke_bd), 0.0).astype(BF16)
    o_intra = _dot(att, _block_diag_rows(v_bf, GLA_DV))

    qe3 = qe.reshape(nb, rows_pb, GLA_KW)
    qm_ref[...] = jnp.concatenate([jnp.where(k_masks[hd], qe3, 0.0) for hd in range(GLA_HEADS)], axis=1).astype(BF16)
    d_hi = decay.astype(BF16).astype(F32)
    d_lo = decay - d_hi
    kdx = jnp.where(r8 == seq, d_hi, jnp.where(r8 == seq + 1, d_lo, kd))
    kdx_ref[...] = kdx.reshape(nb, rows_pb, GLA_KW)
    ones_rows = jnp.where((r8 == seq) | (r8 == seq + 1), 1.0, 0.0) + jnp.zeros((m, GLA_DV), F32)
    rhs = jnp.concatenate(
        [piece for hd in range(GLA_HEADS) for piece in (v[:, hd * GLA_DV:(hd + 1) * GLA_DV], ones_rows)], axis=1)
    rhs_ref[...] = rhs.reshape(nb, rows_pb, 2 * GLA_VW)

    def per_batch(i):
        s_all = sin_ref[0, i]
        s_flat = s_all.reshape(GLA_KW, GLA_DV).astype(BF16)
        oi_ref[i] = _dot(qm_ref[i], s_flat)
        kdt = jnp.transpose(kdx_ref[i]).astype(BF16)
        rhs_i = rhs_ref[i].astype(BF16)
        for hd in range(GLA_HEADS):
            r = _dot(kdt[hd * GLA_DK:(hd + 1) * GLA_DK], rhs_i[:, hd * 2 * GLA_DV:(hd + 1) * 2 * GLA_DV])
            sout_ref[0, i, hd] = r[:, GLA_DV:] * s_all[hd] + r[:, :GLA_DV]

    batches_per_slice = nb // SAMPLE_GATE_SLICES
    gate_cols = 2 * D_MODEL // SAMPLE_GATE_SLICES
    gate_parts = []
    for i in range(nb):
        if i % batches_per_slice == 0:
            c = i // batches_per_slice
            gate_parts.append(_sigmoid(_gate_proj(xn, w, c * gate_cols, (c + 1) * gate_cols)))
        per_batch(i)

    oi = oi_ref[...]
    o_inter = jnp.concatenate([oi[:, hd * rows_pb:(hd + 1) * rows_pb, :] for hd in range(GLA_HEADS)], axis=2)
    o = o_intra + o_inter.reshape(m, GLA_VW)
    y_b = _gla_post(o, og, w)
    sg = jnp.concatenate(gate_parts, axis=1)
    x2 = _merge(x, y_a, y_b, sg[:, :D_MODEL], sg[:, D_MODEL:], w).reshape(nb, rows_pb, D_MODEL)
    for bi in range(nb):
        x2_ref[bi * seq:(bi + 1) * seq, :] = x2[bi, 0:seq, :]


def _mixer_kernel(n_s, n_t,
                  xs_in, pool_in, s_in, xp_in, meta_ref, gmix_ref, bgk_ref, pscale_ref, gnorm_ref, wgk_ref, wpg_ref,
                  wint_hbm, wpp_hbm, wgp_hbm, wout_hbm, wfi_in, wfo_in,
                  x2s_out, pools_out, ss_out, x2p_out, poolp_out, sp_out, wfi_out, wfo_out,
                  wmain_s, wgab_s, wgk_s, wpp_s, wgp_s, wout_s, stage, zr_stage, sem, zr_sem,
                  ext_ref, lvl_ref, s_ref, meta_tail_ref, meta_s_ref,
                  xs_ref, us_ref, pooled_ref, qm_ref, kdx_ref, rhs_ref, oi_ref):
    i = pl.program_id(0)
    w = _Weights(gmix_ref, bgk_ref, pscale_ref, gnorm_ref, wpg_ref,
                 wmain_s, wgab_s, wgk_s, wpp_s, wgp_s, wout_s)

    @pl.when(i == 0)
    def _():
        _stage_weights(wint_hbm, wpp_hbm, wgp_hbm, wout_hbm, wgk_ref, w, stage, zr_stage, sem, zr_sem)

    wfi_bf = wfi_in[0].astype(BF16)
    for kc in range(wfi_out.shape[0]):
        wfi_out[kc] = wfi_bf[:, kc * FFN_CHUNK:(kc + 1) * FFN_CHUNK]
    wfo_out[...] = wfo_in[0].astype(BF16)

    @pl.when(i < n_s)
    def _():
        _sample_block(xs_in, pool_in, s_in, w, x2s_out, pools_out, ss_out,
                      xs_ref, us_ref, pooled_ref, qm_ref, kdx_ref, rhs_ref, oi_ref)

    @pl.when(i >= n_s)
    def _():
        t_idx = (i - n_s) % n_t
        _prompt_tile(i == n_s, (i - n_s) // n_t, t_idx, n_t, xp_in, meta_ref, w, x2p_out, poolp_out, sp_out,
                     ext_ref, lvl_ref, s_ref, meta_tail_ref, meta_s_ref)


def _ffn_weight_copies(wi_hbm, wo_hbm, wi_s, wo_s, sem):
    n_chunks = D_FF // FFN_CHUNK
    copies = []
    for c in range(n_chunks):
        rows = pl.ds(c * FFN_CHUNK, FFN_CHUNK)
        copies.append((
            pltpu.make_async_copy(wi_hbm.at[c], wi_s.at[c], sem.at[3 * c]),
            pltpu.make_async_copy(wi_hbm.at[n_chunks + c], wi_s.at[n_chunks + c], sem.at[3 * c + 1]),
            pltpu.make_async_copy(wo_hbm.at[rows, :], wo_s.at[rows, :], sem.at[3 * c + 2])))
    return copies


def _ffn_tile(x, gffn_ref, wi_s, wo_s, gfin_ref, before_chunk=None):
    n_chunks = D_FF // FFN_CHUNK
    h, r = _rms_split(x, gffn_ref[...])
    acc = x
    group = []
    for c in range(n_chunks):
        if before_chunk is not None:
            before_chunk(c)
        gate = _dot(h, wi_s[c]) * r
        up = _dot(h, wi_s[n_chunks + c]) * r
        group.append((_silu(gate) * up).astype(BF16))
        if len(group) == FFN_OUT_GROUP or c + 1 == n_chunks:
            first = (c + 1 - len(group)) * FFN_CHUNK
            acc = acc + _dot(jnp.concatenate(group, axis=1), wo_s[first:(c + 1) * FFN_CHUNK, :])
            group = []
    return _rms(acc, gfin_ref[...])


def _ffn_kernel(seq, xp_ref, xs_ref, gffn_ref, wi_hbm, wo_hbm, gfin_ref, yp_ref, ys_ref, wi_s, wo_s, sem):
    i = pl.program_id(0)

    @pl.when(i == 0)
    def _():
        copies = _ffn_weight_copies(wi_hbm, wo_hbm, wi_s, wo_s, sem)
        for chunk_copies in copies:
            for cp in chunk_copies:
                cp.start()

        def wait_chunk(c):
            for cp in copies[c]:
                cp.wait()

        y = _ffn_tile(xs_ref[...], gffn_ref, wi_s, wo_s, gfin_ref, before_chunk=wait_chunk)
        for bi in range(ys_ref.shape[0]):
            ys_ref[bi] = y[bi * seq:(bi + 1) * seq, :]

    @pl.when(i > 0)
    def _():
        yp_ref[0] = _ffn_tile(xp_ref[0], gffn_ref, wi_s, wo_s, gfin_ref)


def _const_spec(shape):
    zeros = (0,) * len(shape)
    return pl.BlockSpec(shape, lambda *_: zeros, pipeline_mode=pl.Buffered(1))


def kernel(x_prompt, x_sample, state_pool, state_gla, meta_tokens, g_mix, w_in, w_gk_up, b_gk, w_pool_group,
           pool_scale, w_pool_proj, g_gla_norm, w_gla_proj, w_out, g_ffn, w_ffn_in, w_ffn_out, g_final):
    depth = w_in.shape[0]
    assert depth == 1, "single-layer trunk only"
    bp, tp, d = x_prompt.shape
    bs, ts, _ = x_sample.shape
    nbb = SAMPLE_BATCH_BLOCK
    assert d == D_MODEL and w_in.shape == (1, D_MODEL, IN_DIM) and meta_tokens.shape == (N_META, D_MODEL)
    assert tp % PROMPT_TILE == 0 and tp % FFN_TILE == 0 and PROMPT_TILE % GLA_CHUNK == 0
    assert bs % nbb == 0 and ts + 2 <= SAMPLE_ROWS and bs * ts == FFN_TILE
    n_t = tp // PROMPT_TILE
    n_p = bp * n_t
    n_s = bs // nbb
    n_fi = D_MODEL // FFN_IN_CAST_ROWS
    n_fo = D_FF // FFN_OUT_CAST_ROWS
    assert n_fi <= n_p and n_fo <= n_p

    def s_idx(i):
        return jnp.minimum(i, n_s - 1)

    def p_idx(i):
        return jnp.maximum(i - n_s, 0)

    w_in_t = jnp.transpose(w_in, (0, 2, 1))
    pool_hist = jnp.transpose(state_pool, (0, 2, 1, 3))
    small = (meta_tokens, g_mix, b_gk, pool_scale, g_gla_norm, w_gk_up, w_pool_group)
    hbm = pl.BlockSpec(memory_space=pl.ANY)
    in_specs = (
        [pl.BlockSpec((nbb, ts, d), lambda i: (s_idx(i), 0, 0)),
         pl.BlockSpec((1, POOL_BUF, nbb, POOL_WIDTH), lambda i: (0, 0, s_idx(i), 0)),
         pl.BlockSpec((1, nbb, GLA_HEADS, GLA_DK, GLA_DV), lambda i: (0, s_idx(i), 0, 0, 0)),
         pl.BlockSpec((1, PROMPT_TILE, d), lambda i: (p_idx(i) // n_t, p_idx(i) % n_t, 0))]
        + [_const_spec(a.shape) for a in small]
        + [hbm, hbm, hbm, hbm,
           pl.BlockSpec((1, FFN_IN_CAST_ROWS, 2 * D_FF), lambda i: (0, jnp.minimum(p_idx(i), n_fi - 1), 0)),
           pl.BlockSpec((1, FFN_OUT_CAST_ROWS, d), lambda i: (0, jnp.minimum(p_idx(i), n_fo - 1), 0))])
    out_specs = [
        pl.BlockSpec((nbb * ts, d), lambda i: (s_idx(i), 0)),
        pl.BlockSpec((1, POOL_BUF, nbb, POOL_WIDTH), lambda i: (0, 0, s_idx(i), 0)),
        pl.BlockSpec((1, nbb, GLA_HEADS, GLA_DK, GLA_DV), lambda i: (0, s_idx(i), 0, 0, 0)),
        pl.BlockSpec((1, PROMPT_TILE, d), lambda i: (p_idx(i) // n_t, p_idx(i) % n_t, 0)),
        pl.BlockSpec((1, POOL_BUF, bp, POOL_WIDTH), lambda i: (0, 0, 0, 0)),
        pl.BlockSpec((1, 1, GLA_HEADS, GLA_DK, GLA_DV), lambda i: (0, p_idx(i) // n_t, 0, 0, 0)),
        pl.BlockSpec((2 * D_FF // FFN_CHUNK, FFN_IN_CAST_ROWS, FFN_CHUNK),
                     lambda i: (0, jnp.minimum(p_idx(i), n_fi - 1), 0)),
        pl.BlockSpec((FFN_OUT_CAST_ROWS, d), lambda i: (jnp.minimum(p_idx(i), n_fo - 1), 0))]
    out_shape = [
        jax.ShapeDtypeStruct((bs * ts, d), F32),
        jax.ShapeDtypeStruct((1, POOL_BUF, bs, POOL_WIDTH), F32),
        jax.ShapeDtypeStruct(state_gla.shape, F32),
        jax.ShapeDtypeStruct(x_prompt.shape, F32),
        jax.ShapeDtypeStruct((1, POOL_BUF, bp, POOL_WIDTH), F32),
        jax.ShapeDtypeStruct((1, bp, GLA_HEADS, GLA_DK, GLA_DV), F32),
        jax.ShapeDtypeStruct((2 * D_FF // FFN_CHUNK, D_MODEL, FFN_CHUNK), BF16),
        jax.ShapeDtypeStruct((D_FF, D_MODEL), BF16)]
    scratch_shapes = [
        pltpu.VMEM((D_MODEL, MAIN_COLS), BF16),
        pltpu.VMEM((D_MODEL, 2 * D_MODEL), BF16), pltpu.VMEM((LANES, GLA_KW), BF16),
        pltpu.VMEM((POOL_WIDTH, D_MODEL), BF16), pltpu.VMEM((GLA_VW, D_MODEL), BF16),
        pltpu.VMEM((D_MODEL, D_MODEL), BF16),
        pltpu.VMEM((2, STAGE_ROWS, D_MODEL), F32), pltpu.VMEM((LANES, D_MODEL), F32),
        pltpu.SemaphoreType.DMA((2,)), pltpu.SemaphoreType.DMA((1,)),
        pltpu.VMEM((POOL_PAD + TAIL_ROWS + PROMPT_TILE, POOL_WIDTH), F32),
        pltpu.VMEM((len(POOL_WINDOWS) - 1, POOL_PAD + TAIL_ROWS + PROMPT_TILE, POOL_WIDTH), F32),
        pltpu.VMEM((GLA_HEADS, GLA_DK, GLA_DV), F32),
        pltpu.VMEM((N_META, POOL_WIDTH), F32),
        pltpu.VMEM((GLA_HEADS, GLA_DK, GLA_DV), F32),
        pltpu.VMEM((nbb, SAMPLE_ROWS, D_MODEL), F32),
        pltpu.VMEM((len(POOL_WINDOWS), nbb * SAMPLE_ROWS, POOL_GROUP_DIM), F32),
        pltpu.VMEM((len(POOL_WINDOWS), nbb * SAMPLE_ROWS, POOL_GROUP_DIM), F32),
        pltpu.VMEM((nbb, GLA_HEADS * SAMPLE_ROWS, GLA_KW), BF16),
        pltpu.VMEM((nbb, SAMPLE_ROWS, GLA_KW), F32),
        pltpu.VMEM((nbb, SAMPLE_ROWS, 2 * GLA_VW), F32),
        pltpu.VMEM((nbb, GLA_HEADS * SAMPLE_ROWS, GLA_DV), F32)]

    def mixer(*refs):
        _mixer_kernel(n_s, n_t, *refs)

    x2_s, pool_s, gla_s, x2_p, pool_p, gla_p, wfi_bf, wfo_bf = pl.pallas_call(
        mixer,
        grid=(n_s + n_p,),
        in_specs=in_specs,
        out_specs=out_specs,
        out_shape=out_shape,
        scratch_shapes=scratch_shapes,
        compiler_params=pltpu.CompilerParams(dimension_semantics=("arbitrary",), vmem_limit_bytes=VMEM_LIMIT),
        name="mixer",
    )(x_sample, pool_hist, state_gla, x_prompt, *small, w_in_t, w_pool_proj, w_gla_proj, w_out, w_ffn_in, w_ffn_out)

    def ffn(*refs):
        _ffn_kernel(ts, *refs)

    n_tf = tp // FFN_TILE

    def prompt_tile_index(i):
        tile_id = jnp.maximum(i - 1, 0)
        return (tile_id // n_tf, tile_id % n_tf, 0)

    ffn_args = (g_ffn, wfi_bf, wfo_bf, g_final.reshape(1, D_MODEL))
    y_prompt, y_sample = pl.pallas_call(
        ffn,
        grid=(bp * n_tf + 1,),
        in_specs=[pl.BlockSpec((1, FFN_TILE, d), prompt_tile_index),
                  _const_spec(x2_s.shape), _const_spec(g_ffn.shape), hbm, hbm, _const_spec((1, D_MODEL))],
        out_specs=[pl.BlockSpec((1, FFN_TILE, d), prompt_tile_index),
                   pl.BlockSpec(x_sample.shape, lambda i: (0, 0, 0))],
        out_shape=[jax.ShapeDtypeStruct(x_prompt.shape, F32), jax.ShapeDtypeStruct(x_sample.shape, F32)],
        scratch_shapes=[pltpu.VMEM(wfi_bf.shape, BF16), pltpu.VMEM(wfo_bf.shape, BF16),
                        pltpu.SemaphoreType.DMA((3 * (D_FF // FFN_CHUNK),))],
        compiler_params=pltpu.CompilerParams(dimension_semantics=("arbitrary",), vmem_limit_bytes=VMEM_LIMIT),
        name="ffn",
    )(x2_p, x2_s, *ffn_args)
    pool_p = jnp.transpose(pool_p, (0, 2, 1, 3))
    pool_s = jnp.transpose(pool_s, (0, 2, 1, 3))
    return y_prompt, y_sample, pool_p, gla_p, pool_s, gla_s
```

```python
import jax
import jax.numpy as jnp
from jax import lax
from jax.experimental import pallas as pl
from jax.experimental.pallas import tpu as pltpu

F32 = jnp.float32
BF16 = jnp.bfloat16

D_MODEL = 1024
N_META = 16
POOL_WIDTH = 512
POOL_WINDOWS = (2, 4, 8, 16)
POOL_GROUP_DIM = 128
POOL_BUF = 15
GLA_HEADS = 4
GLA_DV = 128
GLA_DK = 64
GLA_KW = GLA_HEADS * GLA_DK
GLA_VW = GLA_HEADS * GLA_DV
GLA_GATE_RANK = 16
GLA_TAU = 16.0
GLA_CHUNK = 64
D_FF = 2816
EPS = 1e-6

LANES = 128
SUBLANES = 8
MAIN_W = POOL_WIDTH + 2 * GLA_KW + 2 * GLA_VW
U_COL, VOG_COL, QK_COL, ZR_COL = 0, POOL_WIDTH, POOL_WIDTH + 2 * GLA_VW, MAIN_W
MAIN_COLS = MAIN_W + LANES
GAB_LO = MAIN_W + GLA_GATE_RANK
IN_DIM = GAB_LO + 2 * D_MODEL
TAIL_ROWS = 16
POOL_PAD = SUBLANES

PROMPT_TILE = 512
FFN_TILE = 512
FFN_CHUNK = 256
FFN_OUT_GROUP = 4
SAMPLE_BATCH_BLOCK = 16
SAMPLE_ROWS = SUBLANES
SAMPLE_GATE_SLICES = 8
STAGE_ROWS = 256
STAGE_SLOTS = 8
FFN_IN_CAST_ROWS = 32
FFN_OUT_CAST_ROWS = 128
V7X_VMEM_BYTES = 64 * 1024 * 1024
VMEM_RESERVE = 4 * 1024 * 1024
VMEM_LIMIT = V7X_VMEM_BYTES - VMEM_RESERVE


def _dot(a, b):
    return jnp.dot(a, b, preferred_element_type=F32)


def _dot_nt(a, b):
    return lax.dot_general(a, b, (((1,), (1,)), ((), ())), preferred_element_type=F32)


def _dot_tn(a, b):
    return lax.dot_general(a, b, (((0,), (0,)), ((), ())), preferred_element_type=F32)


def _rms(x, g):
    return x * lax.rsqrt(jnp.mean(x * x, axis=-1, keepdims=True) + EPS) * g


def _rms_split(x, g):
    r = lax.rsqrt(jnp.mean(x * x, axis=-1, keepdims=True) + EPS)
    return (x * g).astype(BF16), r


def _sigmoid(x):
    return 0.5 * jnp.tanh(0.5 * x) + 0.5


def _silu(x):
    half = 0.5 * x
    return half * jnp.tanh(half) + half


def _log_sigmoid(x):
    return jnp.minimum(x, 0.0) - jnp.log(1.0 + jnp.exp(-jnp.abs(x)))


def _split_bf16(x):
    hi = x.astype(BF16)
    lo = (x - hi.astype(F32)).astype(BF16)
    return hi, lo


class _Weights:
    def __init__(self, gmix, bgk, pscale, gnorm, wpg, wmain, wgab, wgk, wpp, wgp, wout):
        self.gmix, self.bgk, self.pscale, self.gnorm, self.wpg = gmix, bgk, pscale, gnorm, wpg
        self.wmain, self.wgab, self.wgk = wmain, wgab, wgk
        self.wpp, self.wgp, self.wout = wpp, wgp, wout


def _in_proj(x, w):
    h, r = _rms_split(x, w.gmix[...])
    qkz = _dot(h, w.wmain[:, QK_COL:MAIN_COLS])
    q = qkz[:, :GLA_KW] * (r * (GLA_DK ** -0.5))
    k = qkz[:, GLA_KW:2 * GLA_KW] * r
    zr = qkz[:, 2 * GLA_KW:] * r
    z = _dot(zr.astype(BF16), w.wgk[...]) + w.bgk[...]
    u = _dot(h, w.wmain[:, U_COL:VOG_COL]) * r
    vog = _dot(h, w.wmain[:, VOG_COL:QK_COL]) * r
    v = vog[:, :GLA_VW]
    og = vog[:, GLA_VW:]
    return (h, r), u, q, k, v, og, z


def _gate_proj(xn, w, lo, hi):
    h, r = xn
    return _dot(h, w.wgab[:, lo:hi]) * r


def _chunk_cumsum_wide(g, chunk):
    n = g.shape[0] // chunk
    r = lax.broadcasted_iota(jnp.int32, (chunk, chunk), 0)
    c = lax.broadcasted_iota(jnp.int32, (chunk, chunk), 1)
    tri = jnp.where(c <= r, 1.0, 0.0).astype(BF16)
    hi, lo = _split_bf16(jnp.concatenate([g[j * chunk:(j + 1) * chunk] for j in range(n)], axis=1))
    wide = _dot(tri, hi) + _dot(tri, lo)
    width = g.shape[1]
    parts = [wide[:, j * width:(j + 1) * width] for j in range(n)]
    return jnp.concatenate(parts, axis=0), [p[chunk - 1:chunk, :] for p in parts]


def _chunk_cumsum(g, chunk):
    m = g.shape[0]
    r = lax.broadcasted_iota(jnp.int32, (m, m), 0)
    c = lax.broadcasted_iota(jnp.int32, (m, m), 1)
    tri = jnp.where((r // chunk == c // chunk) & (c <= r), 1.0, 0.0).astype(BF16)
    hi, lo = _split_bf16(g)
    return _dot(tri, hi) + _dot(tri, lo)


def _head_lane_mask(width, per_head):
    lane = lax.broadcasted_iota(jnp.int32, (1, width), 1)
    return [(lane // per_head) == h for h in range(GLA_HEADS)]


def _block_diag_rows(x_bf, per_head):
    r = x_bf.shape[0]
    zero = jnp.zeros((r, per_head), x_bf.dtype)
    rows = []
    for h in range(GLA_HEADS):
        rows.append(jnp.concatenate(
            [x_bf[:, h * per_head:(h + 1) * per_head] if hh == h else zero for hh in range(GLA_HEADS)], axis=1))
    return jnp.concatenate(rows, axis=0)


def _gla_post(o, og, w):
    parts = []
    for h in range(GLA_HEADS):
        oh = o[:, h * GLA_DV:(h + 1) * GLA_DV]
        parts.append(oh * lax.rsqrt(jnp.mean(oh * oh, axis=-1, keepdims=True) + EPS) * w.gnorm[...])
    on = jnp.concatenate(parts, axis=1)
    on = on * _silu(og)
    return _dot(on.astype(BF16), w.wgp[...])


def _pool_post(pooled, w):
    pb = pooled.astype(BF16)
    mixed = []
    for p in range(len(POOL_WINDOWS) // 2):
        w_pair = _pair_block_diag(w.wpg[0, 2 * p].astype(BF16), w.wpg[0, 2 * p + 1].astype(BF16))
        mixed.append(_dot(pb[:, 2 * p * POOL_GROUP_DIM:(2 * p + 2) * POOL_GROUP_DIM], w_pair))
    mixed = jnp.concatenate(mixed, axis=1)
    return _dot((mixed * w.pscale[...]).astype(BF16), w.wpp[...])


def _merge(x, y_a, y_b, sa, sb, w):
    merged = sa * y_a + sb * y_b
    return x + _dot(merged.astype(BF16), w.wout[...])


def _decay_columns(decay_row):
    return jnp.transpose(jnp.broadcast_to(decay_row, (LANES, decay_row.shape[1])))


def _pair_block_diag(a, b):
    zero = jnp.zeros(a.shape, a.dtype)
    return jnp.concatenate([jnp.concatenate([a, zero], axis=1), jnp.concatenate([zero, b], axis=1)], axis=0)


def _state_update(s_heads, kd_bf, v_bf, decay_row):
    dcol = _decay_columns(decay_row)
    out = []
    for p in range(GLA_HEADS // 2):
        upd = _dot_tn(kd_bf[:, 2 * p * GLA_DK:(2 * p + 2) * GLA_DK], v_bf[:, 2 * p * GLA_DV:(2 * p + 2) * GLA_DV])
        for j in range(2):
            h = 2 * p + j
            rows = slice(h * GLA_DK, (h + 1) * GLA_DK)
            out.append(dcol[rows] * s_heads[h] + upd[j * GLA_DK:(j + 1) * GLA_DK, j * GLA_DV:(j + 1) * GLA_DV])
    return out


def _stage_weights(wint_hbm, wpp_hbm, wgp_hbm, wout_hbm, wgk_ref, w, stage, zr_stage, sem, zr_sem):
    plan = []
    for r in range(0, MAIN_W, STAGE_ROWS):
        if r < POOL_WIDTH:
            col = U_COL + r
        elif r < POOL_WIDTH + 2 * GLA_KW:
            col = QK_COL + r - POOL_WIDTH
        else:
            col = VOG_COL + r - (POOL_WIDTH + 2 * GLA_KW)
        plan.append((wint_hbm, r, w.wmain, col, True))
    for r in range(0, 2 * D_MODEL, STAGE_ROWS):
        plan.append((wint_hbm, GAB_LO + r, w.wgab, r, True))
    for src, dst, n_rows in ((wpp_hbm, w.wpp, POOL_WIDTH), (wgp_hbm, w.wgp, GLA_VW), (wout_hbm, w.wout, D_MODEL)):
        for r in range(0, n_rows, STAGE_ROWS):
            plan.append((src, r, dst, r, False))

    def copy(j):
        src, r0 = plan[j][0], plan[j][1]
        return pltpu.make_async_copy(src.at[0, pl.ds(r0, STAGE_ROWS), :], stage.at[j % STAGE_SLOTS], sem.at[j % STAGE_SLOTS])

    def zr_copy():
        return pltpu.make_async_copy(
            wint_hbm.at[0, pl.ds(MAIN_W, GLA_GATE_RANK), :], zr_stage.at[pl.ds(0, GLA_GATE_RANK), :], zr_sem.at[0])

    for j in range(STAGE_SLOTS):
        copy(j).start()
    zr_copy().start()

    w.wgk[...] = jnp.zeros(w.wgk.shape, BF16)
    w.wgk[0:GLA_GATE_RANK, :] = wgk_ref[0].astype(BF16)
    zr_stage[GLA_GATE_RANK:, :] = jnp.zeros((LANES - GLA_GATE_RANK, D_MODEL), F32)

    for j in range(len(plan)):
        copy(j).wait()
        _, _, dst, d0, transposed = plan[j]
        slab = stage[j % STAGE_SLOTS]
        if transposed:
            dst[:, d0:d0 + STAGE_ROWS] = jnp.transpose(slab).astype(BF16)
        else:
            dst[d0:d0 + STAGE_ROWS, :] = slab.astype(BF16)
        if j + STAGE_SLOTS < len(plan):
            copy(j + STAGE_SLOTS).start()
    zr_copy().wait()
    w.wmain[:, ZR_COL:MAIN_COLS] = jnp.transpose(zr_stage[...]).astype(BF16)


def _prompt_tile(b_first, b_idx, t_idx, n_t, x_ref, meta_ref, w, x2_ref, pbuf_ref, sout_ref,
                 ext_ref, lvl_ref, s_ref, meta_tail_ref, meta_s_ref):
    tile = x_ref.shape[1]
    n_chunks = tile // GLA_CHUNK

    @pl.when(b_first)
    def _():
        _, u, _, k, v, _, z = _in_proj(meta_ref[...], w)
        meta_tail_ref[...] = u
        g = _log_sigmoid(z) * (1.0 / GLA_TAU)
        b = _chunk_cumsum(g, N_META)
        b_last = b[N_META - 1:N_META, :]
        kd = k * jnp.exp(b_last - b)
        zero_s = [jnp.zeros((GLA_DK, GLA_DV), F32)] * GLA_HEADS
        s_new = _state_update(zero_s, kd.astype(BF16), v.astype(BF16), jnp.exp(b_last))
        for hd in range(GLA_HEADS):
            meta_s_ref[hd] = s_new[hd]

    @pl.when(t_idx == 0)
    def _():
        ext_ref[0:POOL_PAD, :] = jnp.zeros((POOL_PAD, POOL_WIDTH), F32)
        lvl_ref[:, 0:POOL_PAD, :] = jnp.zeros((lvl_ref.shape[0], POOL_PAD, POOL_WIDTH), F32)
        ext_ref[POOL_PAD:POOL_PAD + TAIL_ROWS, :] = meta_tail_ref[...]
        s_ref[...] = meta_s_ref[...]

    x = x_ref[0]
    xn = _rms_split(x, w.gmix[...])
    h, r = xn
    gate_cols = 2 * D_MODEL // n_chunks
    gate_parts = [None] * n_chunks

    def gate_slice(c):
        gate_parts[c] = _sigmoid(_gate_proj(xn, w, c * gate_cols, (c + 1) * gate_cols))

    qkz = _dot(h, w.wmain[:, QK_COL:MAIN_COLS])
    u = _dot(h, w.wmain[:, U_COL:VOG_COL]) * r
    v = _dot(h, w.wmain[:, VOG_COL:VOG_COL + GLA_VW]) * r
    q = qkz[:, :GLA_KW] * (r * (GLA_DK ** -0.5))
    k = qkz[:, GLA_KW:2 * GLA_KW] * r
    zr = qkz[:, 2 * GLA_KW:] * r
    z = _dot(zr.astype(BF16), w.wgk[...]) + w.bgk[...]
    gate_slice(0)
    g = _log_sigmoid(z) * (1.0 / GLA_TAU)
    b, b_last_rows = _chunk_cumsum_wide(g, GLA_CHUNK)
    gate_slice(1)
    og_parts = []

    base = POOL_PAD + TAIL_ROWS
    span = TAIL_ROWS + tile
    ext_ref[base:base + tile, :] = u
    cur = ext_ref[POOL_PAD:POOL_PAD + span, :]
    pooled = []
    for gi, win in enumerate(POOL_WINDOWS):
        shift = win // 2
        lo = gi * POOL_GROUP_DIM
        prev_ref = ext_ref if gi == 0 else lvl_ref.at[gi - 1]
        cur = cur[:, (POOL_GROUP_DIM if gi else 0):] + prev_ref[POOL_PAD - shift:POOL_PAD - shift + span, lo:]
        pooled.append(cur[TAIL_ROWS:, 0:POOL_GROUP_DIM] * (1.0 / win) - u[:, lo:lo + POOL_GROUP_DIM])
        if gi + 1 < len(POOL_WINDOWS):
            lvl_ref[gi, POOL_PAD:POOL_PAD + span, lo:] = cur
    y_a = _pool_post(jnp.concatenate(pooled, axis=1), w)
    ext_ref[POOL_PAD:base, :] = ext_ref[POOL_PAD + tile:base + tile, :]

    b_last = jnp.concatenate([jnp.broadcast_to(r, (GLA_CHUNK, GLA_KW)) for r in b_last_rows], axis=0)
    qe = (q * jnp.exp(b)).astype(BF16)
    ke = k * jnp.exp(-b)
    kd = (k * jnp.exp(b_last - b)).astype(BF16)
    v_bf = v.astype(BF16)

    k_masks = _head_lane_mask(GLA_KW, GLA_DK)
    row_i = lax.broadcasted_iota(jnp.int32, (GLA_CHUNK, GLA_KW), 0)
    col_j = lax.broadcasted_iota(jnp.int32, (GLA_CHUNK, GLA_KW), 1) % GLA_CHUNK
    causal = col_j <= row_i

    s_heads = [s_ref[hd] for hd in range(GLA_HEADS)]
    o_chunks = []
    for c in range(n_chunks):
        if c + 2 < n_chunks:
            gate_slice(c + 2)
        else:
            og_lo = VOG_COL + GLA_VW + len(og_parts) * (GLA_VW // 2)
            og_parts.append(_dot(h, w.wmain[:, og_lo:og_lo + GLA_VW // 2]) * r)
        rows = slice(c * GLA_CHUNK, (c + 1) * GLA_CHUNK)
        ke_c = ke[rows]
        ke_bd = jnp.concatenate([jnp.where(k_masks[hd], ke_c, 0.0) for hd in range(GLA_HEADS)], axis=0).astype(BF16)
        att = jnp.where(causal, _dot_nt(qe[rows], ke_bd), 0.0).astype(BF16)
        o_pairs = []
        for p in range(GLA_HEADS // 2):
            h0, h1 = 2 * p, 2 * p + 1
            lanes_k = slice(h0 * GLA_DK, (h1 + 1) * GLA_DK)
            lanes_j = slice(h0 * GLA_CHUNK, (h1 + 1) * GLA_CHUNK)
            v0 = v_bf[rows, h0 * GLA_DV:(h0 + 1) * GLA_DV]
            v1 = v_bf[rows, h1 * GLA_DV:(h1 + 1) * GLA_DV]
            rhs = jnp.concatenate([_pair_block_diag(s_heads[h0].astype(BF16), s_heads[h1].astype(BF16)),
                                   _pair_block_diag(v0, v1)], axis=0)
            o_pairs.append(_dot(jnp.concatenate([qe[rows, lanes_k], att[:, lanes_j]], axis=1), rhs))
        o_chunks.append(jnp.concatenate(o_pairs, axis=1))
        s_heads = _state_update(s_heads, kd[rows], v_bf[rows], jnp.exp(b_last_rows[c]))
    for hd in range(GLA_HEADS):
        s_ref[hd] = s_heads[hd]

    y_b = _gla_post(jnp.concatenate(o_chunks, axis=0), jnp.concatenate(og_parts, axis=1), w)
    sg = jnp.concatenate(gate_parts, axis=1)
    x2_ref[0] = _merge(x, y_a, y_b, sg[:, :D_MODEL], sg[:, D_MODEL:], w)

    @pl.when(t_idx == n_t - 1)
    def _():
        sout_ref[0, 0] = s_ref[...]

    for bb in range(pbuf_ref.shape[2]):
        @pl.when((t_idx == n_t - 1) & (b_idx == bb))
        def _():
            for r in range(POOL_BUF):
                row = base - POOL_BUF + r
                pbuf_ref[0, r, bb:bb + 1, :] = ext_ref[row:row + 1, :]


def _sample_block(x_ref, pool_ref, sin_ref, w, x2_ref, pbuf_ref, sout_ref,
                  xs_ref, us_ref, pooled_ref, qm_ref, kdx_ref, rhs_ref, oi_ref):
    nb, seq, _ = x_ref.shape
    rows_pb = SAMPLE_ROWS
    m = nb * rows_pb

    xs_ref[...] = jnp.zeros(xs_ref.shape, F32)
    xs_ref[:, 0:seq, :] = x_ref[...]
    x = xs_ref[...].reshape(m, D_MODEL)
    xn, u, q, k, v, og, z = _in_proj(x, w)

    pooled_ref[...] = jnp.zeros(pooled_ref.shape, F32)
    for gi, win in enumerate(POOL_WINDOWS):
        cols = slice(gi * POOL_GROUP_DIM, (gi + 1) * POOL_GROUP_DIM)
        us_ref[gi] = u[:, cols]
        tok = [us_ref[gi, pl.ds(t, nb, stride=rows_pb), :] for t in range(seq)]
        hist = [pool_ref[0, r, :, cols] for r in range(POOL_BUF)] + tok
        for t in range(seq):
            acc = tok[t]
            for n in range(1, win):
                acc = acc + hist[POOL_BUF + t - n]
            pooled_ref[gi, pl.ds(t, nb, stride=rows_pb), :] = acc * (1.0 / win) - tok[t]
        for r in range(POOL_BUF):
            pbuf_ref[0, r, :, cols] = hist[seq + r]
    y_a = _pool_post(jnp.concatenate([pooled_ref[gi] for gi in range(len(POOL_WINDOWS))], axis=1), w)

    r8 = lax.broadcasted_iota(jnp.int32, (m, 1), 0) % rows_pb
    g = jnp.where(r8 < seq, _log_sigmoid(z) * (1.0 / GLA_TAU), 0.0)
    b = _chunk_cumsum(g, rows_pb)
    b3 = b.reshape(nb, rows_pb, GLA_KW)
    b_last = jnp.broadcast_to(b3[:, seq - 1:seq, :], b3.shape).reshape(m, GLA_KW)
    qe = q * jnp.exp(b)
    ke = k * jnp.exp(-b)
    kd = k * jnp.exp(b_last - b)
    decay = jnp.exp(b_last)
    v_bf = v.astype(BF16)

    k_masks = _head_lane_mask(GLA_KW, GLA_DK)
    ke_bd = jnp.concatenate([jnp.where(k_masks[hd], ke, 0.0) for hd in range(GLA_HEADS)], axis=0).astype(BF16)
    row_i = lax.broadcasted_iota(jnp.int32, (m, GLA_HEADS * m), 0)
    col_j = lax.broadcasted_iota(jnp.int32, (m, GLA_HEADS * m), 1) % m
    keep = (row_i // rows_pb == col_j // rows_pb) & (col_j <= row_i)
    att = jnp.where(keep, _dot_nt(qe.astype(BF16), ke_bd), 0.0).astype(BF16)
    o_intra = _dot(att, _block_diag_rows(v_bf, GLA_DV))

    qe3 = qe.reshape(nb, rows_pb, GLA_KW)
    qm_ref[...] = jnp.concatenate([jnp.where(k_masks[hd], qe3, 0.0) for hd in range(GLA_HEADS)], axis=1).astype(BF16)
    d_hi = decay.astype(BF16).astype(F32)
    d_lo = decay - d_hi
    kdx = jnp.where(r8 == seq, d_hi, jnp.where(r8 == seq + 1, d_lo, kd))
    kdx_ref[...] = kdx.reshape(nb, rows_pb, GLA_KW)
    ones_rows = jnp.where((r8 == seq) | (r8 == seq + 1), 1.0, 0.0) + jnp.zeros((m, GLA_DV), F32)
    rhs = jnp.concatenate(
        [piece for hd in range(GLA_HEADS) for piece in (v[:, hd * GLA_DV:(hd + 1) * GLA_DV], ones_rows)], axis=1)
    rhs_ref[...] = rhs.reshape(nb, rows_pb, 2 * GLA_VW)

    def per_batch(i):
        s_all = sin_ref[0, i]
        s_flat = s_all.reshape(GLA_KW, GLA_DV).astype(BF16)
        oi_ref[i] = _dot(qm_ref[i], s_flat)
        kdt = jnp.transpose(kdx_ref[i]).astype(BF16)
        rhs_i = rhs_ref[i].astype(BF16)
        for hd in range(GLA_HEADS):
            r = _dot(kdt[hd * GLA_DK:(hd + 1) * GLA_DK], rhs_i[:, hd * 2 * GLA_DV:(hd + 1) * 2 * GLA_DV])
            sout_ref[0, i, hd] = r[:, GLA_DV:] * s_all[hd] + r[:, :GLA_DV]

    batches_per_slice = nb // SAMPLE_GATE_SLICES
    gate_cols = 2 * D_MODEL // SAMPLE_GATE_SLICES
    gate_parts = []
    for i in range(nb):
        if i % batches_per_slice == 0:
            c = i // batches_per_slice
            gate_parts.append(_sigmoid(_gate_proj(xn, w, c * gate_cols, (c + 1) * gate_cols)))
        per_batch(i)

    oi = oi_ref[...]
    o_inter = jnp.concatenate([oi[:, hd * rows_pb:(hd + 1) * rows_pb, :] for hd in range(GLA_HEADS)], axis=2)
    o = o_intra + o_inter.reshape(m, GLA_VW)
    y_b = _gla_post(o, og, w)
    sg = jnp.concatenate(gate_parts, axis=1)
    x2 = _merge(x, y_a, y_b, sg[:, :D_MODEL], sg[:, D_MODEL:], w).reshape(nb, rows_pb, D_MODEL)
    for bi in range(nb):
        x2_ref[bi * seq:(bi + 1) * seq, :] = x2[bi, 0:seq, :]


def _mixer_kernel(n_s, n_t,
                  xs_in, pool_in, s_in, xp_in, meta_ref, gmix_ref, bgk_ref, pscale_ref, gnorm_ref, wgk_ref, wpg_ref,
                  wint_hbm, wpp_hbm, wgp_hbm, wout_hbm, wfi_in, wfo_in,
                  x2s_out, pools_out, ss_out, x2p_out, poolp_out, sp_out, wfi_out, wfo_out,
                  wmain_s, wgab_s, wgk_s, wpp_s, wgp_s, wout_s, stage, zr_stage, sem, zr_sem,
                  ext_ref, lvl_ref, s_ref, meta_tail_ref, meta_s_ref,
                  xs_ref, us_ref, pooled_ref, qm_ref, kdx_ref, rhs_ref, oi_ref):
    i = pl.program_id(0)
    w = _Weights(gmix_ref, bgk_ref, pscale_ref, gnorm_ref, wpg_ref,
                 wmain_s, wgab_s, wgk_s, wpp_s, wgp_s, wout_s)

    @pl.when(i == 0)
    def _():
        _stage_weights(wint_hbm, wpp_hbm, wgp_hbm, wout_hbm, wgk_ref, w, stage, zr_stage, sem, zr_sem)

    wfi_bf = wfi_in[0].astype(BF16)
    for kc in range(wfi_out.shape[0]):
        wfi_out[kc] = wfi_bf[:, kc * FFN_CHUNK:(kc + 1) * FFN_CHUNK]
    wfo_out[...] = wfo_in[0].astype(BF16)

    @pl.when(i < n_s)
    def _():
        _sample_block(xs_in, pool_in, s_in, w, x2s_out, pools_out, ss_out,
                      xs_ref, us_ref, pooled_ref, qm_ref, kdx_ref, rhs_ref, oi_ref)

    @pl.when(i >= n_s)
    def _():
        t_idx = (i - n_s) % n_t
        _prompt_tile(i == n_s, (i - n_s) // n_t, t_idx, n_t, xp_in, meta_ref, w, x2p_out, poolp_out, sp_out,
                     ext_ref, lvl_ref, s_ref, meta_tail_ref, meta_s_ref)


def _ffn_weight_copies(wi_hbm, wo_hbm, wi_s, wo_s, sem):
    n_chunks = D_FF // FFN_CHUNK
    copies = []
    for c in range(n_chunks):
        rows = pl.ds(c * FFN_CHUNK, FFN_CHUNK)
        copies.append((
            pltpu.make_async_copy(wi_hbm.at[c], wi_s.at[c], sem.at[3 * c]),
            pltpu.make_async_copy(wi_hbm.at[n_chunks + c], wi_s.at[n_chunks + c], sem.at[3 * c + 1]),
            pltpu.make_async_copy(wo_hbm.at[rows, :], wo_s.at[rows, :], sem.at[3 * c + 2])))
    return copies


def _ffn_tile(x, gffn_ref, wi_s, wo_s, gfin_ref, before_chunk=None):
    n_chunks = D_FF // FFN_CHUNK
    h, r = _rms_split(x, gffn_ref[...])
    acc = x
    group = []
    for c in range(n_chunks):
        if before_chunk is not None:
            before_chunk(c)
        gate = _dot(h, wi_s[c]) * r
        up = _dot(h, wi_s[n_chunks + c]) * r
        group.append((_silu(gate) * up).astype(BF16))
        if len(group) == FFN_OUT_GROUP or c + 1 == n_chunks:
            first = (c + 1 - len(group)) * FFN_CHUNK
            acc = acc + _dot(jnp.concatenate(group, axis=1), wo_s[first:(c + 1) * FFN_CHUNK, :])
            group = []
    return _rms(acc, gfin_ref[...])


def _ffn_kernel(seq, xp_ref, xs_ref, gffn_ref, wi_hbm, wo_hbm, gfin_ref, yp_ref, ys_ref, wi_s, wo_s, sem):
    i = pl.program_id(0)

    @pl.when(i == 0)
    def _():
        copies = _ffn_weight_copies(wi_hbm, wo_hbm, wi_s, wo_s, sem)
        for chunk_copies in copies:
            for cp in chunk_copies:
                cp.start()

        def wait_chunk(c):
            for cp in copies[c]:
                cp.wait()

        y = _ffn_tile(xs_ref[...], gffn_ref, wi_s, wo_s, gfin_ref, before_chunk=wait_chunk)
        for bi in range(ys_ref.shape[0]):
            ys_ref[bi] = y[bi * seq:(bi + 1) * seq, :]

    @pl.when(i > 0)
    def _():
        yp_ref[0] = _ffn_tile(xp_ref[0], gffn_ref, wi_s, wo_s, gfin_ref)


def _const_spec(shape):
    zeros = (0,) * len(shape)
    return pl.BlockSpec(shape, lambda *_: zeros, pipeline_mode=pl.Buffered(1))


def kernel(x_prompt, x_sample, state_pool, state_gla, meta_tokens, g_mix, w_in, w_gk_up, b_gk, w_pool_group,
           pool_scale, w_pool_proj, g_gla_norm, w_gla_proj, w_out, g_ffn, w_ffn_in, w_ffn_out, g_final):
    depth = w_in.shape[0]
    assert depth == 1, "single-layer trunk only"
    bp, tp, d = x_prompt.shape
    bs, ts, _ = x_sample.shape
    nbb = SAMPLE_BATCH_BLOCK
    assert d == D_MODEL and w_in.shape == (1, D_MODEL, IN_DIM) and meta_tokens.shape == (N_META, D_MODEL)
    assert tp % PROMPT_TILE == 0 and tp % FFN_TILE == 0 and PROMPT_TILE % GLA_CHUNK == 0
    assert bs % nbb == 0 and ts + 2 <= SAMPLE_ROWS and bs * ts == FFN_TILE
    n_t = tp // PROMPT_TILE
    n_p = bp * n_t
    n_s = bs // nbb
    n_fi = D_MODEL // FFN_IN_CAST_ROWS
    n_fo = D_FF // FFN_OUT_CAST_ROWS
    assert n_fi <= n_p and n_fo <= n_p

    def s_idx(i):
        return jnp.minimum(i, n_s - 1)

    def p_idx(i):
        return jnp.maximum(i - n_s, 0)

    w_in_t = jnp.transpose(w_in, (0, 2, 1))
    pool_hist = jnp.transpose(state_pool, (0, 2, 1, 3))
    small = (meta_tokens, g_mix, b_gk, pool_scale, g_gla_norm, w_gk_up, w_pool_group)
    hbm = pl.BlockSpec(memory_space=pl.ANY)
    in_specs = (
        [pl.BlockSpec((nbb, ts, d), lambda i: (s_idx(i), 0, 0)),
         pl.BlockSpec((1, POOL_BUF, nbb, POOL_WIDTH), lambda i: (0, 0, s_idx(i), 0)),
         pl.BlockSpec((1, nbb, GLA_HEADS, GLA_DK, GLA_DV), lambda i: (0, s_idx(i), 0, 0, 0)),
         pl.BlockSpec((1, PROMPT_TILE, d), lambda i: (p_idx(i) // n_t, p_idx(i) % n_t, 0))]
        + [_const_spec(a.shape) for a in small]
        + [hbm, hbm, hbm, hbm,
           pl.BlockSpec((1, FFN_IN_CAST_ROWS, 2 * D_FF), lambda i: (0, jnp.minimum(p_idx(i), n_fi - 1), 0)),
           pl.BlockSpec((1, FFN_OUT_CAST_ROWS, d), lambda i: (0, jnp.minimum(p_idx(i), n_fo - 1), 0))])
    out_specs = [
        pl.BlockSpec((nbb * ts, d), lambda i: (s_idx(i), 0)),
        pl.BlockSpec((1, POOL_BUF, nbb, POOL_WIDTH), lambda i: (0, 0, s_idx(i), 0)),
        pl.BlockSpec((1, nbb, GLA_HEADS, GLA_DK, GLA_DV), lambda i: (0, s_idx(i), 0, 0, 0)),
        pl.BlockSpec((1, PROMPT_TILE, d), lambda i: (p_idx(i) // n_t, p_idx(i) % n_t, 0)),
        pl.BlockSpec((1, POOL_BUF, bp, POOL_WIDTH), lambda i: (0, 0, 0, 0)),
        pl.BlockSpec((1, 1, GLA_HEADS, GLA_DK, GLA_DV), lambda i: (0, p_idx(i) // n_t, 0, 0, 0)),
        pl.BlockSpec((2 * D_FF // FFN_CHUNK, FFN_IN_CAST_ROWS, FFN_CHUNK),
                     lambda i: (0, jnp.minimum(p_idx(i), n_fi - 1), 0)),
        pl.BlockSpec((FFN_OUT_CAST_ROWS, d), lambda i: (jnp.minimum(p_idx(i), n_fo - 1), 0))]
    out_shape = [
        jax.ShapeDtypeStruct((bs * ts, d), F32),
        jax.ShapeDtypeStruct((1, POOL_BUF, bs, POOL_WIDTH), F32),
        jax.ShapeDtypeStruct(state_gla.shape, F32),
        jax.ShapeDtypeStruct(x_prompt.shape, F32),
        jax.ShapeDtypeStruct((1, POOL_BUF, bp, POOL_WIDTH), F32),
        jax.ShapeDtypeStruct((1, bp, GLA_HEADS, GLA_DK, GLA_DV), F32),
        jax.ShapeDtypeStruct((2 * D_FF // FFN_CHUNK, D_MODEL, FFN_CHUNK), BF16),
        jax.ShapeDtypeStruct((D_FF, D_MODEL), BF16)]
    scratch_shapes = [
        pltpu.VMEM((D_MODEL, MAIN_COLS), BF16),
        pltpu.VMEM((D_MODEL, 2 * D_MODEL), BF16), pltpu.VMEM((LANES, GLA_KW), BF16),
        pltpu.VMEM((POOL_WIDTH, D_MODEL), BF16), pltpu.VMEM((GLA_VW, D_MODEL), BF16),
        pltpu.VMEM((D_MODEL, D_MODEL), BF16),
        pltpu.VMEM((STAGE_SLOTS, STAGE_ROWS, D_MODEL), F32), pltpu.VMEM((LANES, D_MODEL), F32),
        pltpu.SemaphoreType.DMA((STAGE_SLOTS,)), pltpu.SemaphoreType.DMA((1,)),
        pltpu.VMEM((POOL_PAD + TAIL_ROWS + PROMPT_TILE, POOL_WIDTH), F32),
        pltpu.VMEM((len(POOL_WINDOWS) - 1, POOL_PAD + TAIL_ROWS + PROMPT_TILE, POOL_WIDTH), F32),
        pltpu.VMEM((GLA_HEADS, GLA_DK, GLA_DV), F32),
        pltpu.VMEM((N_META, POOL_WIDTH), F32),
        pltpu.VMEM((GLA_HEADS, GLA_DK, GLA_DV), F32),
        pltpu.VMEM((nbb, SAMPLE_ROWS, D_MODEL), F32),
        pltpu.VMEM((len(POOL_WINDOWS), nbb * SAMPLE_ROWS, POOL_GROUP_DIM), F32),
        pltpu.VMEM((len(POOL_WINDOWS), nbb * SAMPLE_ROWS, POOL_GROUP_DIM), F32),
        pltpu.VMEM((nbb, GLA_HEADS * SAMPLE_ROWS, GLA_KW), BF16),
        pltpu.VMEM((nbb, SAMPLE_ROWS, GLA_KW), F32),
        pltpu.VMEM((nbb, SAMPLE_ROWS, 2 * GLA_VW), F32),
        pltpu.VMEM((nbb, GLA_HEADS * SAMPLE_ROWS, GLA_DV), F32)]

    def mixer(*refs):
        _mixer_kernel(n_s, n_t, *refs)

    x2_s, pool_s, gla_s, x2_p, pool_p, gla_p, wfi_bf, wfo_bf = pl.pallas_call(
        mixer,
        grid=(n_s + n_p,),
        in_specs=in_specs,
        out_specs=out_specs,
        out_shape=out_shape,
        scratch_shapes=scratch_shapes,
        compiler_params=pltpu.CompilerParams(dimension_semantics=("arbitrary",), vmem_limit_bytes=VMEM_LIMIT),
        name="mixer",
    )(x_sample, pool_hist, state_gla, x_prompt, *small, w_in_t, w_pool_proj, w_gla_proj, w_out, w_ffn_in, w_ffn_out)

    def ffn(*refs):
        _ffn_kernel(ts, *refs)

    n_tf = tp // FFN_TILE

    def prompt_tile_index(i):
        tile_id = jnp.maximum(i - 1, 0)
        return (tile_id // n_tf, tile_id % n_tf, 0)

    ffn_args = (g_ffn, wfi_bf, wfo_bf, g_final.reshape(1, D_MODEL))
    y_prompt, y_sample = pl.pallas_call(
        ffn,
        grid=(bp * n_tf + 1,),
        in_specs=[pl.BlockSpec((1, FFN_TILE, d), prompt_tile_index),
                  _const_spec(x2_s.shape), _const_spec(g_ffn.shape), hbm, hbm, _const_spec((1, D_MODEL))],
        out_specs=[pl.BlockSpec((1, FFN_TILE, d), prompt_tile_index),
                   pl.BlockSpec(x_sample.shape, lambda i: (0, 0, 0))],
        out_shape=[jax.ShapeDtypeStruct(x_prompt.shape, F32), jax.ShapeDtypeStruct(x_sample.shape, F32)],
        scratch_shapes=[pltpu.VMEM(wfi_bf.shape, BF16), pltpu.VMEM(wfo_bf.shape, BF16),
                        pltpu.SemaphoreType.DMA((3 * (D_FF // FFN_CHUNK),))],
        compiler_params=pltpu.CompilerParams(dimension_semantics=("arbitrary",), vmem_limit_bytes=VMEM_LIMIT),
        name="ffn",
    )(x2_p, x2_s, *ffn_args)
    pool_p = jnp.transpose(pool_p, (0, 2, 1, 3))
    pool_s = jnp.transpose(pool_s, (0, 2, 1, 3))
    return y_prompt, y_sample, pool_p, gla_p, pool_s, gla_s
```

```python
import jax
import jax.numpy as jnp
from jax import lax
from jax.experimental import pallas as pl
from jax.experimental.pallas import tpu as pltpu

F32 = jnp.float32
BF16 = jnp.bfloat16

D_MODEL = 1024
N_META = 16
POOL_WIDTH = 512
POOL_WINDOWS = (2, 4, 8, 16)
POOL_GROUP_DIM = 128
POOL_BUF = 15
GLA_HEADS = 4
GLA_DV = 128
GLA_DK = 64
GLA_KW = GLA_HEADS * GLA_DK
GLA_VW = GLA_HEADS * GLA_DV
GLA_GATE_RANK = 16
GLA_TAU = 16.0
GLA_CHUNK = 64
D_FF = 2816
EPS = 1e-6

LANES = 128
SUBLANES = 8
MAIN_W = POOL_WIDTH + 2 * GLA_KW + 2 * GLA_VW
U_COL, VOG_COL, QK_COL, ZR_COL = 0, POOL_WIDTH, POOL_WIDTH + 2 * GLA_VW, MAIN_W
MAIN_COLS = MAIN_W + LANES
GAB_LO = MAIN_W + GLA_GATE_RANK
IN_DIM = GAB_LO + 2 * D_MODEL
TAIL_ROWS = 16
POOL_PAD = SUBLANES

PROMPT_TILE = 512
FFN_TILE = 512
FFN_CHUNK = 256
FFN_OUT_GROUP = 4
SAMPLE_BATCH_BLOCK = 16
SAMPLE_ROWS = SUBLANES
SAMPLE_GATE_SLICES = 8
STAGE_ROWS = 256
STAGE_SLOTS = 8
FFN_IN_CAST_ROWS = 32
FFN_OUT_CAST_ROWS = 128
V7X_VMEM_BYTES = 64 * 1024 * 1024
VMEM_RESERVE = 4 * 1024 * 1024
VMEM_LIMIT = V7X_VMEM_BYTES - VMEM_RESERVE


def _dot(a, b):
    return jnp.dot(a, b, preferred_element_type=F32)


def _dot_nt(a, b):
    return lax.dot_general(a, b, (((1,), (1,)), ((), ())), preferred_element_type=F32)


def _dot_tn(a, b):
    return lax.dot_general(a, b, (((0,), (0,)), ((), ())), preferred_element_type=F32)


def _rms(x, g):
    return x * lax.rsqrt(jnp.mean(x * x, axis=-1, keepdims=True) + EPS) * g


def _rms_split(x, g):
    r = lax.rsqrt(jnp.mean(x * x, axis=-1, keepdims=True) + EPS)
    return (x * g).astype(BF16), r


def _sigmoid(x):
    return 0.5 * jnp.tanh(0.5 * x) + 0.5


def _silu(x):
    half = 0.5 * x
    return half * jnp.tanh(half) + half


def _log_sigmoid(x):
    return jnp.minimum(x, 0.0) - jnp.log(1.0 + jnp.exp(-jnp.abs(x)))


def _split_bf16(x):
    hi = x.astype(BF16)
    lo = (x - hi.astype(F32)).astype(BF16)
    return hi, lo


class _Weights:
    def __init__(self, gmix, bgk, pscale, gnorm, wpg, wmain, wgab, wgk, wpp, wgp, wout):
        self.gmix, self.bgk, self.pscale, self.gnorm, self.wpg = gmix, bgk, pscale, gnorm, wpg
        self.wmain, self.wgab, self.wgk = wmain, wgab, wgk
        self.wpp, self.wgp, self.wout = wpp, wgp, wout


def _in_proj(x, w):
    h, r = _rms_split(x, w.gmix[...])
    qkz = _dot(h, w.wmain[:, QK_COL:MAIN_COLS])
    q = qkz[:, :GLA_KW] * (r * (GLA_DK ** -0.5))
    k = qkz[:, GLA_KW:2 * GLA_KW] * r
    zr = qkz[:, 2 * GLA_KW:] * r
    z = _dot(zr.astype(BF16), w.wgk[...]) + w.bgk[...]
    u = _dot(h, w.wmain[:, U_COL:VOG_COL]) * r
    vog = _dot(h, w.wmain[:, VOG_COL:QK_COL]) * r
    v = vog[:, :GLA_VW]
    og = vog[:, GLA_VW:]
    return (h, r), u, q, k, v, og, z


def _gate_proj(xn, w, lo, hi):
    h, r = xn
    return _dot(h, w.wgab[:, lo:hi]) * r


def _chunk_cumsum_wide(g, chunk):
    n = g.shape[0] // chunk
    r = lax.broadcasted_iota(jnp.int32, (chunk, chunk), 0)
    c = lax.broadcasted_iota(jnp.int32, (chunk, chunk), 1)
    tri = jnp.where(c <= r, 1.0, 0.0).astype(BF16)
    hi, lo = _split_bf16(jnp.concatenate([g[j * chunk:(j + 1) * chunk] for j in range(n)], axis=1))
    wide = _dot(tri, hi) + _dot(tri, lo)
    width = g.shape[1]
    parts = [wide[:, j * width:(j + 1) * width] for j in range(n)]
    return jnp.concatenate(parts, axis=0), [p[chunk - 1:chunk, :] for p in parts]


def _chunk_cumsum(g, chunk):
    m = g.shape[0]
    r = lax.broadcasted_iota(jnp.int32, (m, m), 0)
    c = lax.broadcasted_iota(jnp.int32, (m, m), 1)
    tri = jnp.where((r // chunk == c // chunk) & (c <= r), 1.0, 0.0).astype(BF16)
    hi, lo = _split_bf16(g)
    return _dot(tri, hi) + _dot(tri, lo)


def _head_lane_mask(width, per_head):
    lane = lax.broadcasted_iota(jnp.int32, (1, width), 1)
    return [(lane // per_head) == h for h in range(GLA_HEADS)]


def _block_diag_rows(x_bf, per_head):
    r = x_bf.shape[0]
    zero = jnp.zeros((r, per_head), x_bf.dtype)
    rows = []
    for h in range(GLA_HEADS):
        rows.append(jnp.concatenate(
            [x_bf[:, h * per_head:(h + 1) * per_head] if hh == h else zero for hh in range(GLA_HEADS)], axis=1))
    return jnp.concatenate(rows, axis=0)


def _gla_post(o, og, w):
    parts = []
    for h in range(GLA_HEADS):
        oh = o[:, h * GLA_DV:(h + 1) * GLA_DV]
        parts.append(oh * lax.rsqrt(jnp.mean(oh * oh, axis=-1, keepdims=True) + EPS) * w.gnorm[...])
    on = jnp.concatenate(parts, axis=1)
    on = on * _silu(og)
    return _dot(on.astype(BF16), w.wgp[...])


def _pool_post(pooled, w):
    pb = pooled.astype(BF16)
    mixed = []
    for p in range(len(POOL_WINDOWS) // 2):
        w_pair = _pair_block_diag(w.wpg[0, 2 * p].astype(BF16), w.wpg[0, 2 * p + 1].astype(BF16))
        mixed.append(_dot(pb[:, 2 * p * POOL_GROUP_DIM:(2 * p + 2) * POOL_GROUP_DIM], w_pair))
    mixed = jnp.concatenate(mixed, axis=1)
    return _dot((mixed * w.pscale[...]).astype(BF16), w.wpp[...])


def _merge(x, y_a, y_b, sa, sb, w):
    merged = sa * y_a + sb * y_b
    return x + _dot(merged.astype(BF16), w.wout[...])


def _decay_columns(decay_row):
    return jnp.transpose(jnp.broadcast_to(decay_row, (LANES, decay_row.shape[1])))


def _pair_block_diag(a, b):
    zero = jnp.zeros(a.shape, a.dtype)
    return jnp.concatenate([jnp.concatenate([a, zero], axis=1), jnp.concatenate([zero, b], axis=1)], axis=0)


def _state_update(s_heads, kd_bf, v_bf, decay_row):
    dcol = _decay_columns(decay_row)
    out = []
    for p in range(GLA_HEADS // 2):
        upd = _dot_tn(kd_bf[:, 2 * p * GLA_DK:(2 * p + 2) * GLA_DK], v_bf[:, 2 * p * GLA_DV:(2 * p + 2) * GLA_DV])
        for j in range(2):
            h = 2 * p + j
            rows = slice(h * GLA_DK, (h + 1) * GLA_DK)
            out.append(dcol[rows] * s_heads[h] + upd[j * GLA_DK:(j + 1) * GLA_DK, j * GLA_DV:(j + 1) * GLA_DV])
    return out


def _stage_weights(wint_hbm, wpp_hbm, wgp_hbm, wout_hbm, wgk_ref, w, stage, zr_stage, sem, zr_sem):
    plan = []
    for r in range(0, MAIN_W, STAGE_ROWS):
        if r < POOL_WIDTH:
            col = U_COL + r
        elif r < POOL_WIDTH + 2 * GLA_KW:
            col = QK_COL + r - POOL_WIDTH
        else:
            col = VOG_COL + r - (POOL_WIDTH + 2 * GLA_KW)
        plan.append((wint_hbm, r, w.wmain, col, True))
    for r in range(0, 2 * D_MODEL, STAGE_ROWS):
        plan.append((wint_hbm, GAB_LO + r, w.wgab, r, True))
    for src, dst, n_rows in ((wpp_hbm, w.wpp, POOL_WIDTH), (wgp_hbm, w.wgp, GLA_VW), (wout_hbm, w.wout, D_MODEL)):
        for r in range(0, n_rows, STAGE_ROWS):
            plan.append((src, r, dst, r, False))

    def copy(j):
        src, r0 = plan[j][0], plan[j][1]
        return pltpu.make_async_copy(src.at[0, pl.ds(r0, STAGE_ROWS), :], stage.at[j % STAGE_SLOTS], sem.at[j % STAGE_SLOTS])

    def zr_copy():
        return pltpu.make_async_copy(
            wint_hbm.at[0, pl.ds(MAIN_W, GLA_GATE_RANK), :], zr_stage.at[pl.ds(0, GLA_GATE_RANK), :], zr_sem.at[0])

    for j in range(STAGE_SLOTS):
        copy(j).start()
    zr_copy().start()

    w.wgk[...] = jnp.zeros(w.wgk.shape, BF16)
    w.wgk[0:GLA_GATE_RANK, :] = wgk_ref[0].astype(BF16)
    zr_stage[GLA_GATE_RANK:, :] = jnp.zeros((LANES - GLA_GATE_RANK, D_MODEL), F32)

    for j in range(len(plan)):
        copy(j).wait()
        _, _, dst, d0, transposed = plan[j]
        slab = stage[j % STAGE_SLOTS]
        if transposed:
            dst[:, d0:d0 + STAGE_ROWS] = jnp.transpose(slab.astype(BF16))
        else:
            dst[d0:d0 + STAGE_ROWS, :] = slab.astype(BF16)
        if j + STAGE_SLOTS < len(plan):
            copy(j + STAGE_SLOTS).start()
    zr_copy().wait()
    w.wmain[:, ZR_COL:MAIN_COLS] = jnp.transpose(zr_stage[...]).astype(BF16)


def _prompt_tile(b_first, b_idx, t_idx, n_t, x_ref, meta_ref, w, x2_ref, pbuf_ref, sout_ref,
                 ext_ref, lvl_ref, s_ref, meta_tail_ref, meta_s_ref):
    tile = x_ref.shape[1]
    n_chunks = tile // GLA_CHUNK

    @pl.when(b_first)
    def _():
        _, u, _, k, v, _, z = _in_proj(meta_ref[...], w)
        meta_tail_ref[...] = u
        g = _log_sigmoid(z) * (1.0 / GLA_TAU)
        b = _chunk_cumsum(g, N_META)
        b_last = b[N_META - 1:N_META, :]
        kd = k * jnp.exp(b_last - b)
        zero_s = [jnp.zeros((GLA_DK, GLA_DV), F32)] * GLA_HEADS
        s_new = _state_update(zero_s, kd.astype(BF16), v.astype(BF16), jnp.exp(b_last))
        for hd in range(GLA_HEADS):
            meta_s_ref[hd] = s_new[hd]

    @pl.when(t_idx == 0)
    def _():
        ext_ref[0:POOL_PAD, :] = jnp.zeros((POOL_PAD, POOL_WIDTH), F32)
        lvl_ref[:, 0:POOL_PAD, :] = jnp.zeros((lvl_ref.shape[0], POOL_PAD, POOL_WIDTH), F32)
        ext_ref[POOL_PAD:POOL_PAD + TAIL_ROWS, :] = meta_tail_ref[...]
        s_ref[...] = meta_s_ref[...]

    x = x_ref[0]
    xn = _rms_split(x, w.gmix[...])
    h, r = xn
    gate_cols = 2 * D_MODEL // n_chunks
    gate_parts = [None] * n_chunks

    def gate_slice(c):
        gate_parts[c] = _sigmoid(_gate_proj(xn, w, c * gate_cols, (c + 1) * gate_cols))

    qkz = _dot(h, w.wmain[:, QK_COL:MAIN_COLS])
    u = _dot(h, w.wmain[:, U_COL:VOG_COL]) * r
    v = _dot(h, w.wmain[:, VOG_COL:VOG_COL + GLA_VW]) * r
    q = qkz[:, :GLA_KW] * (r * (GLA_DK ** -0.5))
    k = qkz[:, GLA_KW:2 * GLA_KW] * r
    zr = qkz[:, 2 * GLA_KW:] * r
    z = _dot(zr.astype(BF16), w.wgk[...]) + w.bgk[...]
    gate_slice(0)
    g = _log_sigmoid(z) * (1.0 / GLA_TAU)
    b, b_last_rows = _chunk_cumsum_wide(g, GLA_CHUNK)
    gate_slice(1)
    og_parts = []

    base = POOL_PAD + TAIL_ROWS
    span = TAIL_ROWS + tile
    ext_ref[base:base + tile, :] = u
    cur = ext_ref[POOL_PAD:POOL_PAD + span, :]
    pooled = []
    for gi, win in enumerate(POOL_WINDOWS):
        shift = win // 2
        lo = gi * POOL_GROUP_DIM
        prev_ref = ext_ref if gi == 0 else lvl_ref.at[gi - 1]
        cur = cur[:, (POOL_GROUP_DIM if gi else 0):] + prev_ref[POOL_PAD - shift:POOL_PAD - shift + span, lo:]
        pooled.append(cur[TAIL_ROWS:, 0:POOL_GROUP_DIM] * (1.0 / win) - u[:, lo:lo + POOL_GROUP_DIM])
        if gi + 1 < len(POOL_WINDOWS):
            lvl_ref[gi, POOL_PAD:POOL_PAD + span, lo:] = cur
    y_a = _pool_post(jnp.concatenate(pooled, axis=1), w)
    ext_ref[POOL_PAD:base, :] = ext_ref[POOL_PAD + tile:base + tile, :]

    b_last = jnp.concatenate([jnp.broadcast_to(r, (GLA_CHUNK, GLA_KW)) for r in b_last_rows], axis=0)
    qe = (q * jnp.exp(b)).astype(BF16)
    ke = k * jnp.exp(-b)
    kd = (k * jnp.exp(b_last - b)).astype(BF16)
    v_bf = v.astype(BF16)

    k_masks = _head_lane_mask(GLA_KW, GLA_DK)
    row_i = lax.broadcasted_iota(jnp.int32, (GLA_CHUNK, GLA_KW), 0)
    col_j = lax.broadcasted_iota(jnp.int32, (GLA_CHUNK, GLA_KW), 1) % GLA_CHUNK
    causal = col_j <= row_i

    s_heads = [s_ref[hd] for hd in range(GLA_HEADS)]
    o_chunks = []
    for c in range(n_chunks):
        if c + 2 < n_chunks:
            gate_slice(c + 2)
        else:
            og_lo = VOG_COL + GLA_VW + len(og_parts) * (GLA_VW // 2)
            og_parts.append(_dot(h, w.wmain[:, og_lo:og_lo + GLA_VW // 2]) * r)
        rows = slice(c * GLA_CHUNK, (c + 1) * GLA_CHUNK)
        ke_c = ke[rows]
        ke_bd = jnp.concatenate([jnp.where(k_masks[hd], ke_c, 0.0) for hd in range(GLA_HEADS)], axis=0).astype(BF16)
        att = jnp.where(causal, _dot_nt(qe[rows], ke_bd), 0.0).astype(BF16)
        o_pairs = []
        for p in range(GLA_HEADS // 2):
            h0, h1 = 2 * p, 2 * p + 1
            lanes_k = slice(h0 * GLA_DK, (h1 + 1) * GLA_DK)
            lanes_j = slice(h0 * GLA_CHUNK, (h1 + 1) * GLA_CHUNK)
            v0 = v_bf[rows, h0 * GLA_DV:(h0 + 1) * GLA_DV]
            v1 = v_bf[rows, h1 * GLA_DV:(h1 + 1) * GLA_DV]
            rhs = jnp.concatenate([_pair_block_diag(s_heads[h0].astype(BF16), s_heads[h1].astype(BF16)),
                                   _pair_block_diag(v0, v1)], axis=0)
            o_pairs.append(_dot(jnp.concatenate([qe[rows, lanes_k], att[:, lanes_j]], axis=1), rhs))
        o_chunks.append(jnp.concatenate(o_pairs, axis=1))
        s_heads = _state_update(s_heads, kd[rows], v_bf[rows], jnp.exp(b_last_rows[c]))
    for hd in range(GLA_HEADS):
        s_ref[hd] = s_heads[hd]

    y_b = _gla_post(jnp.concatenate(o_chunks, axis=0), jnp.concatenate(og_parts, axis=1), w)
    sg = jnp.concatenate(gate_parts, axis=1)
    x2_ref[0] = _merge(x, y_a, y_b, sg[:, :D_MODEL], sg[:, D_MODEL:], w)

    @pl.when(t_idx == n_t - 1)
    def _():
        sout_ref[0, 0] = s_ref[...]

    for bb in range(pbuf_ref.shape[2]):
        @pl.when((t_idx == n_t - 1) & (b_idx == bb))
        def _():
            for r in range(POOL_BUF):
                row = base - POOL_BUF + r
                pbuf_ref[0, r, bb:bb + 1, :] = ext_ref[row:row + 1, :]


def _sample_block(x_ref, pool_ref, sin_ref, w, x2_ref, pbuf_ref, sout_ref,
                  xs_ref, us_ref, pooled_ref, qm_ref, kdx_ref, rhs_ref, oi_ref):
    nb, seq, _ = x_ref.shape
    rows_pb = SAMPLE_ROWS
    m = nb * rows_pb

    xs_ref[...] = jnp.zeros(xs_ref.shape, F32)
    xs_ref[:, 0:seq, :] = x_ref[...]
    x = xs_ref[...].reshape(m, D_MODEL)
    xn, u, q, k, v, og, z = _in_proj(x, w)

    pooled_ref[...] = jnp.zeros(pooled_ref.shape, F32)
    for gi, win in enumerate(POOL_WINDOWS):
        cols = slice(gi * POOL_GROUP_DIM, (gi + 1) * POOL_GROUP_DIM)
        us_ref[gi] = u[:, cols]
        tok = [us_ref[gi, pl.ds(t, nb, stride=rows_pb), :] for t in range(seq)]
        hist = [pool_ref[0, r, :, cols] for r in range(POOL_BUF)] + tok
        for t in range(seq):
            acc = tok[t]
            for n in range(1, win):
                acc = acc + hist[POOL_BUF + t - n]
            pooled_ref[gi, pl.ds(t, nb, stride=rows_pb), :] = acc * (1.0 / win) - tok[t]
        for r in range(POOL_BUF):
            pbuf_ref[0, r, :, cols] = hist[seq + r]
    y_a = _pool_post(jnp.concatenate([pooled_ref[gi] for gi in range(len(POOL_WINDOWS))], axis=1), w)

    r8 = lax.broadcasted_iota(jnp.int32, (m, 1), 0) % rows_pb
    g = jnp.where(r8 < seq, _log_sigmoid(z) * (1.0 / GLA_TAU), 0.0)
    b = _chunk_cumsum(g, rows_pb)
    b3 = b.reshape(nb, rows_pb, GLA_KW)
    b_last = jnp.broadcast_to(b3[:, seq - 1:seq, :], b3.shape).reshape(m, GLA_KW)
    qe = q * jnp.exp(b)
    ke = k * jnp.exp(-b)
    kd = k * jnp.exp(b_last - b)
    decay = jnp.exp(b_last)
    v_bf = v.astype(BF16)

    k_masks = _head_lane_mask(GLA_KW, GLA_DK)
    ke_bd = jnp.concatenate([jnp.where(k_masks[hd], ke, 0.0) for hd in range(GLA_HEADS)], axis=0).astype(BF16)
    row_i = lax.broadcasted_iota(jnp.int32, (m, GLA_HEADS * m), 0)
    col_j = lax.broadcasted_iota(jnp.int32, (m, GLA_HEADS * m), 1) % m
    keep = (row_i // rows_pb == col_j // rows_pb) & (col_j <= row_i)
    att = jnp.where(keep, _dot_nt(qe.astype(BF16), ke_bd), 0.0).astype(BF16)
    o_intra = _dot(att, _block_diag_rows(v_bf, GLA_DV))

    qe3 = qe.reshape(nb, rows_pb, GLA_KW)
    qm_ref[...] = jnp.concatenate([jnp.where(k_masks[hd], qe3, 0.0) for hd in range(GLA_HEADS)], axis=1).astype(BF16)
    d_hi = decay.astype(BF16).astype(F32)
    d_lo = decay - d_hi
    kdx = jnp.where(r8 == seq, d_hi, jnp.where(r8 == seq + 1, d_lo, kd))
    kdx_ref[...] = kdx.reshape(nb, rows_pb, GLA_KW)
    ones_rows = jnp.where((r8 == seq) | (r8 == seq + 1), 1.0, 0.0) + jnp.zeros((m, GLA_DV), F32)
    rhs = jnp.concatenate(
        [piece for hd in range(GLA_HEADS) for piece in (v[:, hd * GLA_DV:(hd + 1) * GLA_DV], ones_rows)], axis=1)
    rhs_ref[...] = rhs.reshape(nb, rows_pb, 2 * GLA_VW)

    def per_batch(i):
        s_all = sin_ref[0, i]
        s_flat = s_all.reshape(GLA_KW, GLA_DV).astype(BF16)
        oi_ref[i] = _dot(qm_ref[i], s_flat)
        kdt = jnp.transpose(kdx_ref[i]).astype(BF16)
        rhs_i = rhs_ref[i].astype(BF16)
        for hd in range(GLA_HEADS):
            r = _dot(kdt[hd * GLA_DK:(hd + 1) * GLA_DK], rhs_i[:, hd * 2 * GLA_DV:(hd + 1) * 2 * GLA_DV])
            sout_ref[0, i, hd] = r[:, GLA_DV:] * s_all[hd] + r[:, :GLA_DV]

    batches_per_slice = nb // SAMPLE_GATE_SLICES
    gate_cols = 2 * D_MODEL // SAMPLE_GATE_SLICES
    gate_parts = []
    for i in range(nb):
        if i % batches_per_slice == 0:
            c = i // batches_per_slice
            gate_parts.append(_sigmoid(_gate_proj(xn, w, c * gate_cols, (c + 1) * gate_cols)))
        per_batch(i)

    oi = oi_ref[...]
    o_inter = jnp.concatenate([oi[:, hd * rows_pb:(hd + 1) * rows_pb, :] for hd in range(GLA_HEADS)], axis=2)
    o = o_intra + o_inter.reshape(m, GLA_VW)
    y_b = _gla_post(o, og, w)
    sg = jnp.concatenate(gate_parts, axis=1)
    x2 = _merge(x, y_a, y_b, sg[:, :D_MODEL], sg[:, D_MODEL:], w).reshape(nb, rows_pb, D_MODEL)
    for bi in range(nb):
        x2_ref[bi * seq:(bi + 1) * seq, :] = x2[bi, 0:seq, :]


def _mixer_kernel(n_s, n_t,
                  xs_in, pool_in, s_in, xp_in, meta_ref, gmix_ref, bgk_ref, pscale_ref, gnorm_ref, wgk_ref, wpg_ref,
                  wint_hbm, wpp_hbm, wgp_hbm, wout_hbm, wfi_in, wfo_in,
                  x2s_out, pools_out, ss_out, x2p_out, poolp_out, sp_out, wfi_out, wfo_out,
                  wmain_s, wgab_s, wgk_s, wpp_s, wgp_s, wout_s, stage, zr_stage, sem, zr_sem,
                  ext_ref, lvl_ref, s_ref, meta_tail_ref, meta_s_ref,
                  xs_ref, us_ref, pooled_ref, qm_ref, kdx_ref, rhs_ref, oi_ref):
    i = pl.program_id(0)
    w = _Weights(gmix_ref, bgk_ref, pscale_ref, gnorm_ref, wpg_ref,
                 wmain_s, wgab_s, wgk_s, wpp_s, wgp_s, wout_s)

    @pl.when(i == 0)
    def _():
        _stage_weights(wint_hbm, wpp_hbm, wgp_hbm, wout_hbm, wgk_ref, w, stage, zr_stage, sem, zr_sem)

    wfi_bf = wfi_in[0].astype(BF16)
    for kc in range(wfi_out.shape[0]):
        cols = slice(kc * FFN_CHUNK, (kc + 1) * FFN_CHUNK)
        wfi_out[kc] = jnp.concatenate([wfi_bf[:, cols], wfi_bf[:, D_FF + kc * FFN_CHUNK:D_FF + (kc + 1) * FFN_CHUNK]],
                                      axis=1)
    wfo_out[...] = wfo_in[0].astype(BF16)

    @pl.when(i < n_s)
    def _():
        _sample_block(xs_in, pool_in, s_in, w, x2s_out, pools_out, ss_out,
                      xs_ref, us_ref, pooled_ref, qm_ref, kdx_ref, rhs_ref, oi_ref)

    @pl.when(i >= n_s)
    def _():
        t_idx = (i - n_s) % n_t
        _prompt_tile(i == n_s, (i - n_s) // n_t, t_idx, n_t, xp_in, meta_ref, w, x2p_out, poolp_out, sp_out,
                     ext_ref, lvl_ref, s_ref, meta_tail_ref, meta_s_ref)


def _ffn_weight_copies(wi_hbm, wo_hbm, wi_s, wo_s, sem):
    copies = []
    for c in range(D_FF // FFN_CHUNK):
        rows = pl.ds(c * FFN_CHUNK, FFN_CHUNK)
        copies.append((
            pltpu.make_async_copy(wi_hbm.at[c], wi_s.at[c], sem.at[2 * c]),
            pltpu.make_async_copy(wo_hbm.at[rows, :], wo_s.at[rows, :], sem.at[2 * c + 1])))
    return copies


def _ffn_tile(x, gffn_ref, wi_s, wo_s, gfin_ref, before_chunk=None):
    n_chunks = D_FF // FFN_CHUNK
    h, r = _rms_split(x, gffn_ref[...])
    acc = x
    group = []
    for c in range(n_chunks):
        if before_chunk is not None:
            before_chunk(c)
        gate_up = _dot(h, wi_s[c]) * r
        group.append((_silu(gate_up[:, :FFN_CHUNK]) * gate_up[:, FFN_CHUNK:]).astype(BF16))
        if len(group) == FFN_OUT_GROUP or c + 1 == n_chunks:
            first = (c + 1 - len(group)) * FFN_CHUNK
            acc = acc + _dot(jnp.concatenate(group, axis=1), wo_s[first:(c + 1) * FFN_CHUNK, :])
            group = []
    return _rms(acc, gfin_ref[...])


def _ffn_kernel(seq, xp_ref, xs_ref, gffn_ref, wi_hbm, wo_hbm, gfin_ref, yp_ref, ys_ref, wi_s, wo_s, sem):
    i = pl.program_id(0)

    @pl.when(i == 0)
    def _():
        copies = _ffn_weight_copies(wi_hbm, wo_hbm, wi_s, wo_s, sem)
        for chunk_copies in copies:
            for cp in chunk_copies:
                cp.start()

        def wait_chunk(c):
            for cp in copies[c]:
                cp.wait()

        y = _ffn_tile(xs_ref[...], gffn_ref, wi_s, wo_s, gfin_ref, before_chunk=wait_chunk)
        for bi in range(ys_ref.shape[0]):
            ys_ref[bi] = y[bi * seq:(bi + 1) * seq, :]

    @pl.when(i > 0)
    def _():
        yp_ref[0] = _ffn_tile(xp_ref[0], gffn_ref, wi_s, wo_s, gfin_ref)


def _const_spec(shape):
    zeros = (0,) * len(shape)
    return pl.BlockSpec(shape, lambda *_: zeros, pipeline_mode=pl.Buffered(1))


def kernel(x_prompt, x_sample, state_pool, state_gla, meta_tokens, g_mix, w_in, w_gk_up, b_gk, w_pool_group,
           pool_scale, w_pool_proj, g_gla_norm, w_gla_proj, w_out, g_ffn, w_ffn_in, w_ffn_out, g_final):
    depth = w_in.shape[0]
    assert depth == 1, "single-layer trunk only"
    bp, tp, d = x_prompt.shape
    bs, ts, _ = x_sample.shape
    nbb = SAMPLE_BATCH_BLOCK
    assert d == D_MODEL and w_in.shape == (1, D_MODEL, IN_DIM) and meta_tokens.shape == (N_META, D_MODEL)
    assert tp % PROMPT_TILE == 0 and tp % FFN_TILE == 0 and PROMPT_TILE % GLA_CHUNK == 0
    assert bs % nbb == 0 and ts + 2 <= SAMPLE_ROWS and bs * ts == FFN_TILE
    n_t = tp // PROMPT_TILE
    n_p = bp * n_t
    n_s = bs // nbb
    n_fi = D_MODEL // FFN_IN_CAST_ROWS
    n_fo = D_FF // FFN_OUT_CAST_ROWS
    assert n_fi <= n_p and n_fo <= n_p

    def s_idx(i):
        return jnp.minimum(i, n_s - 1)

    def p_idx(i):
        return jnp.maximum(i - n_s, 0)

    w_in_t = jnp.transpose(w_in, (0, 2, 1))
    pool_hist = jnp.transpose(state_pool, (0, 2, 1, 3))
    small = (meta_tokens, g_mix, b_gk, pool_scale, g_gla_norm, w_gk_up, w_pool_group)
    hbm = pl.BlockSpec(memory_space=pl.ANY)
    in_specs = (
        [pl.BlockSpec((nbb, ts, d), lambda i: (s_idx(i), 0, 0)),
         pl.BlockSpec((1, POOL_BUF, nbb, POOL_WIDTH), lambda i: (0, 0, s_idx(i), 0)),
         pl.BlockSpec((1, nbb, GLA_HEADS, GLA_DK, GLA_DV), lambda i: (0, s_idx(i), 0, 0, 0)),
         pl.BlockSpec((1, PROMPT_TILE, d), lambda i: (p_idx(i) // n_t, p_idx(i) % n_t, 0))]
        + [_const_spec(a.shape) for a in small]
        + [hbm, hbm, hbm, hbm,
           pl.BlockSpec((1, FFN_IN_CAST_ROWS, 2 * D_FF), lambda i: (0, jnp.minimum(p_idx(i), n_fi - 1), 0)),
           pl.BlockSpec((1, FFN_OUT_CAST_ROWS, d), lambda i: (0, jnp.minimum(p_idx(i), n_fo - 1), 0))])
    out_specs = [
        pl.BlockSpec((nbb * ts, d), lambda i: (s_idx(i), 0)),
        pl.BlockSpec((1, POOL_BUF, nbb, POOL_WIDTH), lambda i: (0, 0, s_idx(i), 0)),
        pl.BlockSpec((1, nbb, GLA_HEADS, GLA_DK, GLA_DV), lambda i: (0, s_idx(i), 0, 0, 0)),
        pl.BlockSpec((1, PROMPT_TILE, d), lambda i: (p_idx(i) // n_t, p_idx(i) % n_t, 0)),
        pl.BlockSpec((1, POOL_BUF, bp, POOL_WIDTH), lambda i: (0, 0, 0, 0)),
        pl.BlockSpec((1, 1, GLA_HEADS, GLA_DK, GLA_DV), lambda i: (0, p_idx(i) // n_t, 0, 0, 0)),
        pl.BlockSpec((D_FF // FFN_CHUNK, FFN_IN_CAST_ROWS, 2 * FFN_CHUNK),
                     lambda i: (0, jnp.minimum(p_idx(i), n_fi - 1), 0)),
        pl.BlockSpec((FFN_OUT_CAST_ROWS, d), lambda i: (jnp.minimum(p_idx(i), n_fo - 1), 0))]
    out_shape = [
        jax.ShapeDtypeStruct((bs * ts, d), F32),
        jax.ShapeDtypeStruct((1, POOL_BUF, bs, POOL_WIDTH), F32),
        jax.ShapeDtypeStruct(state_gla.shape, F32),
        jax.ShapeDtypeStruct(x_prompt.shape, F32),
        jax.ShapeDtypeStruct((1, POOL_BUF, bp, POOL_WIDTH), F32),
        jax.ShapeDtypeStruct((1, bp, GLA_HEADS, GLA_DK, GLA_DV), F32),
        jax.ShapeDtypeStruct((D_FF // FFN_CHUNK, D_MODEL, 2 * FFN_CHUNK), BF16),
        jax.ShapeDtypeStruct((D_FF, D_MODEL), BF16)]
    scratch_shapes = [
        pltpu.VMEM((D_MODEL, MAIN_COLS), BF16),
        pltpu.VMEM((D_MODEL, 2 * D_MODEL), BF16), pltpu.VMEM((LANES, GLA_KW), BF16),
        pltpu.VMEM((POOL_WIDTH, D_MODEL), BF16), pltpu.VMEM((GLA_VW, D_MODEL), BF16),
        pltpu.VMEM((D_MODEL, D_MODEL), BF16),
        pltpu.VMEM((STAGE_SLOTS, STAGE_ROWS, D_MODEL), F32), pltpu.VMEM((LANES, D_MODEL), F32),
        pltpu.SemaphoreType.DMA((STAGE_SLOTS,)), pltpu.SemaphoreType.DMA((1,)),
        pltpu.VMEM((POOL_PAD + TAIL_ROWS + PROMPT_TILE, POOL_WIDTH), F32),
        pltpu.VMEM((len(POOL_WINDOWS) - 1, POOL_PAD + TAIL_ROWS + PROMPT_TILE, POOL_WIDTH), F32),
        pltpu.VMEM((GLA_HEADS, GLA_DK, GLA_DV), F32),
        pltpu.VMEM((N_META, POOL_WIDTH), F32),
        pltpu.VMEM((GLA_HEADS, GLA_DK, GLA_DV), F32),
        pltpu.VMEM((nbb, SAMPLE_ROWS, D_MODEL), F32),
        pltpu.VMEM((len(POOL_WINDOWS), nbb * SAMPLE_ROWS, POOL_GROUP_DIM), F32),
        pltpu.VMEM((len(POOL_WINDOWS), nbb * SAMPLE_ROWS, POOL_GROUP_DIM), F32),
        pltpu.VMEM((nbb, GLA_HEADS * SAMPLE_ROWS, GLA_KW), BF16),
        pltpu.VMEM((nbb, SAMPLE_ROWS, GLA_KW), F32),
        pltpu.VMEM((nbb, SAMPLE_ROWS, 2 * GLA_VW), F32),
        pltpu.VMEM((nbb, GLA_HEADS * SAMPLE_ROWS, GLA_DV), F32)]

    def mixer(*refs):
        _mixer_kernel(n_s, n_t, *refs)

    x2_s, pool_s, gla_s, x2_p, pool_p, gla_p, wfi_bf, wfo_bf = pl.pallas_call(
        mixer,
        grid=(n_s + n_p,),
        in_specs=in_specs,
        out_specs=out_specs,
        out_shape=out_shape,
        scratch_shapes=scratch_shapes,
        compiler_params=pltpu.CompilerParams(dimension_semantics=("arbitrary",), vmem_limit_bytes=VMEM_LIMIT),
        name="mixer",
    )(x_sample, pool_hist, state_gla, x_prompt, *small, w_in_t, w_pool_proj, w_gla_proj, w_out, w_ffn_in, w_ffn_out)

    def ffn(*refs):
        _ffn_kernel(ts, *refs)

    n_tf = tp // FFN_TILE

    def prompt_tile_index(i):
        tile_id = jnp.maximum(i - 1, 0)
        return (tile_id // n_tf, tile_id % n_tf, 0)

    ffn_args = (g_ffn, wfi_bf, wfo_bf, g_final.reshape(1, D_MODEL))
    y_prompt, y_sample = pl.pallas_call(
        ffn,
        grid=(bp * n_tf + 1,),
        in_specs=[pl.BlockSpec((1, FFN_TILE, d), prompt_tile_index),
                  _const_spec(x2_s.shape), _const_spec(g_ffn.shape), hbm, hbm, _const_spec((1, D_MODEL))],
        out_specs=[pl.BlockSpec((1, FFN_TILE, d), prompt_tile_index),
                   pl.BlockSpec(x_sample.shape, lambda i: (0, 0, 0))],
        out_shape=[jax.ShapeDtypeStruct(x_prompt.shape, F32), jax.ShapeDtypeStruct(x_sample.shape, F32)],
        scratch_shapes=[pltpu.VMEM(wfi_bf.shape, BF16), pltpu.VMEM(wfo_bf.shape, BF16),
                        pltpu.SemaphoreType.DMA((2 * (D_FF // FFN_CHUNK),))],
        compiler_params=pltpu.CompilerParams(dimension_semantics=("arbitrary",), vmem_limit_bytes=VMEM_LIMIT),
        name="ffn",
    )(x2_p, x2_s, *ffn_args)
    pool_p = jnp.transpose(pool_p, (0, 2, 1, 3))
    pool_s = jnp.transpose(pool_s, (0, 2, 1, 3))
    return y_prompt, y_sample, pool_p, gla_p, pool_s, gla_s
```

```python
import jax
import jax.numpy as jnp
from jax import lax
from jax.experimental import pallas as pl
from jax.experimental.pallas import tpu as pltpu

F32 = jnp.float32
BF16 = jnp.bfloat16

D_MODEL = 1024
N_META = 16
POOL_WIDTH = 512
POOL_WINDOWS = (2, 4, 8, 16)
POOL_GROUP_DIM = 128
POOL_BUF = 15
GLA_HEADS = 4
GLA_DV = 128
GLA_DK = 64
GLA_KW = GLA_HEADS * GLA_DK
GLA_VW = GLA_HEADS * GLA_DV
GLA_GATE_RANK = 16
GLA_TAU = 16.0
GLA_CHUNK = 64
D_FF = 2816
EPS = 1e-6

LANES = 128
SUBLANES = 8
MAIN_W = POOL_WIDTH + 2 * GLA_KW + 2 * GLA_VW
U_COL, VOG_COL, QK_COL, ZR_COL = 0, POOL_WIDTH, POOL_WIDTH + 2 * GLA_VW, MAIN_W
MAIN_COLS = MAIN_W + LANES
GAB_LO = MAIN_W + GLA_GATE_RANK
IN_DIM = GAB_LO + 2 * D_MODEL
TAIL_ROWS = 16
POOL_PAD = SUBLANES

PROMPT_TILE = 512
FFN_TILE = 512
FFN_CHUNK = 256
FFN_OUT_GROUP = 4
SAMPLE_BATCH_BLOCK = 16
SAMPLE_ROWS = SUBLANES
SAMPLE_GATE_SLICES = 8
SAMPLE_EARLY_GATE_SLICES = 2
STAGE_ROWS = 256
STAGE_SLOTS = 8
FFN_IN_CAST_ROWS = 32
FFN_OUT_CAST_ROWS = 128
V7X_VMEM_BYTES = 64 * 1024 * 1024
VMEM_RESERVE = 4 * 1024 * 1024
VMEM_LIMIT = V7X_VMEM_BYTES - VMEM_RESERVE


def _dot(a, b):
    return jnp.dot(a, b, preferred_element_type=F32)


def _dot_nt(a, b):
    return lax.dot_general(a, b, (((1,), (1,)), ((), ())), preferred_element_type=F32)


def _dot_tn(a, b):
    return lax.dot_general(a, b, (((0,), (0,)), ((), ())), preferred_element_type=F32)


def _rms(x, g):
    return x * lax.rsqrt(jnp.mean(x * x, axis=-1, keepdims=True) + EPS) * g


def _rms_split(x, g):
    r = lax.rsqrt(jnp.mean(x * x, axis=-1, keepdims=True) + EPS)
    return (x * g).astype(BF16), r


def _sigmoid(x):
    return 0.5 * jnp.tanh(0.5 * x) + 0.5


def _silu(x):
    half = 0.5 * x
    return half * jnp.tanh(half) + half


def _log_sigmoid(x):
    return jnp.minimum(x, 0.0) - jnp.log(1.0 + jnp.exp(-jnp.abs(x)))


def _split_bf16(x):
    hi = x.astype(BF16)
    lo = (x - hi.astype(F32)).astype(BF16)
    return hi, lo


class _Weights:
    def __init__(self, gmix, bgk, pscale, gnorm, wpg, wmain, wgab, wgk, wpp, wgp, wout):
        self.gmix, self.bgk, self.pscale, self.gnorm, self.wpg = gmix, bgk, pscale, gnorm, wpg
        self.wmain, self.wgab, self.wgk = wmain, wgab, wgk
        self.wpp, self.wgp, self.wout = wpp, wgp, wout


def _in_proj(x, w):
    h, r = _rms_split(x, w.gmix[...])
    qkz = _dot(h, w.wmain[:, QK_COL:MAIN_COLS])
    q = qkz[:, :GLA_KW] * (r * (GLA_DK ** -0.5))
    k = qkz[:, GLA_KW:2 * GLA_KW] * r
    zr = qkz[:, 2 * GLA_KW:] * r
    z = _dot(zr.astype(BF16), w.wgk[...]) + w.bgk[...]
    u = _dot(h, w.wmain[:, U_COL:VOG_COL]) * r
    vog = _dot(h, w.wmain[:, VOG_COL:QK_COL]) * r
    v = vog[:, :GLA_VW]
    og = vog[:, GLA_VW:]
    return (h, r), u, q, k, v, og, z


def _gate_proj(xn, w, lo, hi):
    h, r = xn
    return _dot(h, w.wgab[:, lo:hi]) * r


def _chunk_cumsum_wide(g, chunk):
    n = g.shape[0] // chunk
    r = lax.broadcasted_iota(jnp.int32, (chunk, chunk), 0)
    c = lax.broadcasted_iota(jnp.int32, (chunk, chunk), 1)
    tri = jnp.where(c <= r, 1.0, 0.0).astype(BF16)
    hi, lo = _split_bf16(jnp.concatenate([g[j * chunk:(j + 1) * chunk] for j in range(n)], axis=1))
    wide = _dot(tri, hi) + _dot(tri, lo)
    width = g.shape[1]
    parts = [wide[:, j * width:(j + 1) * width] for j in range(n)]
    return jnp.concatenate(parts, axis=0), [p[chunk - 1:chunk, :] for p in parts]


def _chunk_cumsum(g, chunk):
    m = g.shape[0]
    r = lax.broadcasted_iota(jnp.int32, (m, m), 0)
    c = lax.broadcasted_iota(jnp.int32, (m, m), 1)
    tri = jnp.where((r // chunk == c // chunk) & (c <= r), 1.0, 0.0).astype(BF16)
    hi, lo = _split_bf16(g)
    return _dot(tri, hi) + _dot(tri, lo)


def _head_lane_mask(width, per_head):
    lane = lax.broadcasted_iota(jnp.int32, (1, width), 1)
    return [(lane // per_head) == h for h in range(GLA_HEADS)]


def _block_diag_rows(x_bf, per_head):
    r = x_bf.shape[0]
    zero = jnp.zeros((r, per_head), x_bf.dtype)
    rows = []
    for h in range(GLA_HEADS):
        rows.append(jnp.concatenate(
            [x_bf[:, h * per_head:(h + 1) * per_head] if hh == h else zero for hh in range(GLA_HEADS)], axis=1))
    return jnp.concatenate(rows, axis=0)


def _gla_post(o, og, w):
    parts = []
    for h in range(GLA_HEADS):
        oh = o[:, h * GLA_DV:(h + 1) * GLA_DV]
        parts.append(oh * lax.rsqrt(jnp.mean(oh * oh, axis=-1, keepdims=True) + EPS) * w.gnorm[...])
    on = jnp.concatenate(parts, axis=1)
    on = on * _silu(og)
    return _dot(on.astype(BF16), w.wgp[...])


def _pool_post(pooled, w):
    pb = pooled.astype(BF16)
    mixed = []
    for p in range(len(POOL_WINDOWS) // 2):
        w_pair = _pair_block_diag(w.wpg[0, 2 * p].astype(BF16), w.wpg[0, 2 * p + 1].astype(BF16))
        mixed.append(_dot(pb[:, 2 * p * POOL_GROUP_DIM:(2 * p + 2) * POOL_GROUP_DIM], w_pair))
    mixed = jnp.concatenate(mixed, axis=1)
    return _dot((mixed * w.pscale[...]).astype(BF16), w.wpp[...])


def _merge(x, y_a, y_b, sa, sb, w):
    merged = sa * y_a + sb * y_b
    return x + _dot(merged.astype(BF16), w.wout[...])


def _decay_columns(decay_row):
    return jnp.transpose(jnp.broadcast_to(decay_row, (LANES, decay_row.shape[1])))


def _pair_block_diag(a, b):
    zero = jnp.zeros(a.shape, a.dtype)
    return jnp.concatenate([jnp.concatenate([a, zero], axis=1), jnp.concatenate([zero, b], axis=1)], axis=0)


def _state_update(s_heads, kd_bf, v_bf, decay_row):
    dcol = _decay_columns(decay_row)
    out = []
    for p in range(GLA_HEADS // 2):
        upd = _dot_tn(kd_bf[:, 2 * p * GLA_DK:(2 * p + 2) * GLA_DK], v_bf[:, 2 * p * GLA_DV:(2 * p + 2) * GLA_DV])
        for j in range(2):
            h = 2 * p + j
            rows = slice(h * GLA_DK, (h + 1) * GLA_DK)
            out.append(dcol[rows] * s_heads[h] + upd[j * GLA_DK:(j + 1) * GLA_DK, j * GLA_DV:(j + 1) * GLA_DV])
    return out


def _stage_weights(wint_hbm, wpp_hbm, wgp_hbm, wout_hbm, wgk_ref, w, stage, zr_stage, sem, zr_sem):
    plan = []
    for r in range(0, MAIN_W, STAGE_ROWS):
        if r < POOL_WIDTH:
            col = U_COL + r
        elif r < POOL_WIDTH + 2 * GLA_KW:
            col = QK_COL + r - POOL_WIDTH
        else:
            col = VOG_COL + r - (POOL_WIDTH + 2 * GLA_KW)
        plan.append((wint_hbm, r, w.wmain, col, True))
    for r in range(0, 2 * D_MODEL, STAGE_ROWS):
        plan.append((wint_hbm, GAB_LO + r, w.wgab, r, True))
    for src, dst, n_rows in ((wpp_hbm, w.wpp, POOL_WIDTH), (wgp_hbm, w.wgp, GLA_VW), (wout_hbm, w.wout, D_MODEL)):
        for r in range(0, n_rows, STAGE_ROWS):
            plan.append((src, r, dst, r, False))

    def copy(j):
        src, r0 = plan[j][0], plan[j][1]
        return pltpu.make_async_copy(src.at[0, pl.ds(r0, STAGE_ROWS), :], stage.at[j % STAGE_SLOTS], sem.at[j % STAGE_SLOTS])

    def zr_copy():
        return pltpu.make_async_copy(
            wint_hbm.at[0, pl.ds(MAIN_W, GLA_GATE_RANK), :], zr_stage.at[pl.ds(0, GLA_GATE_RANK), :], zr_sem.at[0])

    for j in range(STAGE_SLOTS):
        copy(j).start()
    zr_copy().start()

    w.wgk[...] = jnp.zeros(w.wgk.shape, BF16)
    w.wgk[0:GLA_GATE_RANK, :] = wgk_ref[0].astype(BF16)
    zr_stage[GLA_GATE_RANK:, :] = jnp.zeros((LANES - GLA_GATE_RANK, D_MODEL), F32)

    for j in range(len(plan)):
        copy(j).wait()
        _, _, dst, d0, transposed = plan[j]
        slab = stage[j % STAGE_SLOTS]
        if transposed:
            dst[:, d0:d0 + STAGE_ROWS] = jnp.transpose(slab.astype(BF16))
        else:
            dst[d0:d0 + STAGE_ROWS, :] = slab.astype(BF16)
        if j + STAGE_SLOTS < len(plan):
            copy(j + STAGE_SLOTS).start()
    zr_copy().wait()
    w.wmain[:, ZR_COL:MAIN_COLS] = jnp.transpose(zr_stage[...]).astype(BF16)


def _prompt_tile(b_first, b_idx, t_idx, n_t, x_ref, meta_ref, w, x2_ref, pbuf_ref, sout_ref,
                 ext_ref, lvl_ref, s_ref, meta_tail_ref, meta_s_ref):
    tile = x_ref.shape[1]
    n_chunks = tile // GLA_CHUNK

    @pl.when(b_first)
    def _():
        _, u, _, k, v, _, z = _in_proj(meta_ref[...], w)
        meta_tail_ref[...] = u
        g = _log_sigmoid(z) * (1.0 / GLA_TAU)
        b = _chunk_cumsum(g, N_META)
        b_last = b[N_META - 1:N_META, :]
        kd = k * jnp.exp(b_last - b)
        zero_s = [jnp.zeros((GLA_DK, GLA_DV), F32)] * GLA_HEADS
        s_new = _state_update(zero_s, kd.astype(BF16), v.astype(BF16), jnp.exp(b_last))
        for hd in range(GLA_HEADS):
            meta_s_ref[hd] = s_new[hd]

    @pl.when(t_idx == 0)
    def _():
        ext_ref[0:POOL_PAD, :] = jnp.zeros((POOL_PAD, POOL_WIDTH), F32)
        lvl_ref[:, 0:POOL_PAD, :] = jnp.zeros((lvl_ref.shape[0], POOL_PAD, POOL_WIDTH), F32)
        ext_ref[POOL_PAD:POOL_PAD + TAIL_ROWS, :] = meta_tail_ref[...]
        s_ref[...] = meta_s_ref[...]

    x = x_ref[0]
    xn = _rms_split(x, w.gmix[...])
    h, r = xn
    gate_cols = 2 * D_MODEL // n_chunks
    gate_parts = [None] * n_chunks

    def gate_slice(c):
        gate_parts[c] = _sigmoid(_gate_proj(xn, w, c * gate_cols, (c + 1) * gate_cols))

    qkz = _dot(h, w.wmain[:, QK_COL:MAIN_COLS])
    u = _dot(h, w.wmain[:, U_COL:VOG_COL]) * r
    v = _dot(h, w.wmain[:, VOG_COL:VOG_COL + GLA_VW]) * r
    q = qkz[:, :GLA_KW] * (r * (GLA_DK ** -0.5))
    k = qkz[:, GLA_KW:2 * GLA_KW] * r
    zr = qkz[:, 2 * GLA_KW:] * r
    z = _dot(zr.astype(BF16), w.wgk[...]) + w.bgk[...]
    gate_slice(0)
    g = _log_sigmoid(z) * (1.0 / GLA_TAU)
    b, b_last_rows = _chunk_cumsum_wide(g, GLA_CHUNK)
    gate_slice(1)
    og_parts = []

    base = POOL_PAD + TAIL_ROWS
    span = TAIL_ROWS + tile
    ext_ref[base:base + tile, :] = u
    cur = ext_ref[POOL_PAD:POOL_PAD + span, :]
    pooled = []
    for gi, win in enumerate(POOL_WINDOWS):
        shift = win // 2
        lo = gi * POOL_GROUP_DIM
        prev_ref = ext_ref if gi == 0 else lvl_ref.at[gi - 1]
        cur = cur[:, (POOL_GROUP_DIM if gi else 0):] + prev_ref[POOL_PAD - shift:POOL_PAD - shift + span, lo:]
        pooled.append(cur[TAIL_ROWS:, 0:POOL_GROUP_DIM] * (1.0 / win) - u[:, lo:lo + POOL_GROUP_DIM])
        if gi + 1 < len(POOL_WINDOWS):
            lvl_ref[gi, POOL_PAD:POOL_PAD + span, lo:] = cur
    y_a = _pool_post(jnp.concatenate(pooled, axis=1), w)
    ext_ref[POOL_PAD:base, :] = ext_ref[POOL_PAD + tile:base + tile, :]

    b_last = jnp.concatenate([jnp.broadcast_to(r, (GLA_CHUNK, GLA_KW)) for r in b_last_rows], axis=0)
    qe = (q * jnp.exp(b)).astype(BF16)
    ke = k * jnp.exp(-b)
    kd = (k * jnp.exp(b_last - b)).astype(BF16)
    v_bf = v.astype(BF16)

    k_masks = _head_lane_mask(GLA_KW, GLA_DK)
    row_i = lax.broadcasted_iota(jnp.int32, (GLA_CHUNK, GLA_KW), 0)
    col_j = lax.broadcasted_iota(jnp.int32, (GLA_CHUNK, GLA_KW), 1) % GLA_CHUNK
    causal = col_j <= row_i

    s_heads = [s_ref[hd] for hd in range(GLA_HEADS)]
    o_chunks = []
    for c in range(n_chunks):
        if c + 2 < n_chunks:
            gate_slice(c + 2)
        else:
            og_lo = VOG_COL + GLA_VW + len(og_parts) * (GLA_VW // 2)
            og_parts.append(_dot(h, w.wmain[:, og_lo:og_lo + GLA_VW // 2]) * r)
        rows = slice(c * GLA_CHUNK, (c + 1) * GLA_CHUNK)
        ke_c = ke[rows]
        ke_bd = jnp.concatenate([jnp.where(k_masks[hd], ke_c, 0.0) for hd in range(GLA_HEADS)], axis=0).astype(BF16)
        att = jnp.where(causal, _dot_nt(qe[rows], ke_bd), 0.0).astype(BF16)
        o_pairs = []
        for p in range(GLA_HEADS // 2):
            h0, h1 = 2 * p, 2 * p + 1
            lanes_k = slice(h0 * GLA_DK, (h1 + 1) * GLA_DK)
            lanes_j = slice(h0 * GLA_CHUNK, (h1 + 1) * GLA_CHUNK)
            v0 = v_bf[rows, h0 * GLA_DV:(h0 + 1) * GLA_DV]
            v1 = v_bf[rows, h1 * GLA_DV:(h1 + 1) * GLA_DV]
            rhs = jnp.concatenate([_pair_block_diag(s_heads[h0].astype(BF16), s_heads[h1].astype(BF16)),
                                   _pair_block_diag(v0, v1)], axis=0)
            o_pairs.append(_dot(jnp.concatenate([qe[rows, lanes_k], att[:, lanes_j]], axis=1), rhs))
        o_chunks.append(jnp.concatenate(o_pairs, axis=1))
        s_heads = _state_update(s_heads, kd[rows], v_bf[rows], jnp.exp(b_last_rows[c]))
    for hd in range(GLA_HEADS):
        s_ref[hd] = s_heads[hd]

    y_b = _gla_post(jnp.concatenate(o_chunks, axis=0), jnp.concatenate(og_parts, axis=1), w)
    sg = jnp.concatenate(gate_parts, axis=1)
    x2_ref[0] = _merge(x, y_a, y_b, sg[:, :D_MODEL], sg[:, D_MODEL:], w)

    @pl.when(t_idx == n_t - 1)
    def _():
        sout_ref[0, 0] = s_ref[...]

    for bb in range(pbuf_ref.shape[2]):
        @pl.when((t_idx == n_t - 1) & (b_idx == bb))
        def _():
            for r in range(POOL_BUF):
                row = base - POOL_BUF + r
                pbuf_ref[0, r, bb:bb + 1, :] = ext_ref[row:row + 1, :]


def _sample_block(x_ref, pool_ref, sin_ref, w, x2_ref, pbuf_ref, sout_ref,
                  xs_ref, us_ref, pooled_ref, qm_ref, kdx_ref, rhs_ref, oi_ref):
    nb, seq, _ = x_ref.shape
    rows_pb = SAMPLE_ROWS
    m = nb * rows_pb

    xs_ref[...] = jnp.zeros(xs_ref.shape, F32)
    xs_ref[:, 0:seq, :] = x_ref[...]
    x = xs_ref[...].reshape(m, D_MODEL)
    xn, u, q, k, v, og, z = _in_proj(x, w)

    gate_cols = 2 * D_MODEL // SAMPLE_GATE_SLICES
    gate_parts = [_sigmoid(_gate_proj(xn, w, c * gate_cols, (c + 1) * gate_cols))
                  for c in range(SAMPLE_EARLY_GATE_SLICES)]

    pooled_ref[...] = jnp.zeros(pooled_ref.shape, F32)
    for gi, win in enumerate(POOL_WINDOWS):
        cols = slice(gi * POOL_GROUP_DIM, (gi + 1) * POOL_GROUP_DIM)
        us_ref[gi] = u[:, cols]
        tok = [us_ref[gi, pl.ds(t, nb, stride=rows_pb), :] for t in range(seq)]
        hist = [pool_ref[0, r, :, cols] for r in range(POOL_BUF)] + tok
        for t in range(seq):
            acc = tok[t]
            for n in range(1, win):
                acc = acc + hist[POOL_BUF + t - n]
            pooled_ref[gi, pl.ds(t, nb, stride=rows_pb), :] = acc * (1.0 / win) - tok[t]
        for r in range(POOL_BUF):
            pbuf_ref[0, r, :, cols] = hist[seq + r]

    r8 = lax.broadcasted_iota(jnp.int32, (m, 1), 0) % rows_pb
    g = jnp.where(r8 < seq, _log_sigmoid(z) * (1.0 / GLA_TAU), 0.0)
    b = _chunk_cumsum(g, rows_pb)
    b3 = b.reshape(nb, rows_pb, GLA_KW)
    b_last = jnp.broadcast_to(b3[:, seq - 1:seq, :], b3.shape).reshape(m, GLA_KW)
    qe = q * jnp.exp(b)
    ke = k * jnp.exp(-b)
    kd = k * jnp.exp(b_last - b)
    decay = jnp.exp(b_last)
    v_bf = v.astype(BF16)

    k_masks = _head_lane_mask(GLA_KW, GLA_DK)
    ke_bd = jnp.concatenate([jnp.where(k_masks[hd], ke, 0.0) for hd in range(GLA_HEADS)], axis=0).astype(BF16)
    row_i = lax.broadcasted_iota(jnp.int32, (m, GLA_HEADS * m), 0)
    col_j = lax.broadcasted_iota(jnp.int32, (m, GLA_HEADS * m), 1) % m
    keep = (row_i // rows_pb == col_j // rows_pb) & (col_j <= row_i)
    att = jnp.where(keep, _dot_nt(qe.astype(BF16), ke_bd), 0.0).astype(BF16)
    o_intra = _dot(att, _block_diag_rows(v_bf, GLA_DV))

    qe3 = qe.reshape(nb, rows_pb, GLA_KW)
    qm_ref[...] = jnp.concatenate([jnp.where(k_masks[hd], qe3, 0.0) for hd in range(GLA_HEADS)], axis=1).astype(BF16)
    d_hi = decay.astype(BF16).astype(F32)
    d_lo = decay - d_hi
    kdx = jnp.where(r8 == seq, d_hi, jnp.where(r8 == seq + 1, d_lo, kd))
    kdx_ref[...] = kdx.reshape(nb, rows_pb, GLA_KW)
    ones_rows = jnp.where((r8 == seq) | (r8 == seq + 1), 1.0, 0.0) + jnp.zeros((m, GLA_DV), F32)
    rhs = jnp.concatenate(
        [piece for hd in range(GLA_HEADS) for piece in (v[:, hd * GLA_DV:(hd + 1) * GLA_DV], ones_rows)], axis=1)
    rhs_ref[...] = rhs.reshape(nb, rows_pb, 2 * GLA_VW)

    def per_batch(i):
        s_all = sin_ref[0, i]
        s_flat = s_all.reshape(GLA_KW, GLA_DV).astype(BF16)
        oi_ref[i] = _dot(qm_ref[i], s_flat)
        kdt = jnp.transpose(kdx_ref[i]).astype(BF16)
        rhs_i = rhs_ref[i].astype(BF16)
        for hd in range(GLA_HEADS):
            r = _dot(kdt[hd * GLA_DK:(hd + 1) * GLA_DK], rhs_i[:, hd * 2 * GLA_DV:(hd + 1) * 2 * GLA_DV])
            sout_ref[0, i, hd] = r[:, GLA_DV:] * s_all[hd] + r[:, :GLA_DV]

    late = SAMPLE_GATE_SLICES - SAMPLE_EARLY_GATE_SLICES
    slice_at = {(j * nb) // late: SAMPLE_EARLY_GATE_SLICES + j for j in range(late)}
    for i in range(nb):
        if i in slice_at:
            c = slice_at[i]
            gate_parts.append(_sigmoid(_gate_proj(xn, w, c * gate_cols, (c + 1) * gate_cols)))
        per_batch(i)

    oi = oi_ref[...]
    o_inter = jnp.concatenate([oi[:, hd * rows_pb:(hd + 1) * rows_pb, :] for hd in range(GLA_HEADS)], axis=2)
    o = o_intra + o_inter.reshape(m, GLA_VW)
    y_a = _pool_post(jnp.concatenate([pooled_ref[gi] for gi in range(len(POOL_WINDOWS))], axis=1), w)
    y_b = _gla_post(o, og, w)
    sg = jnp.concatenate(gate_parts, axis=1)
    x2 = _merge(x, y_a, y_b, sg[:, :D_MODEL], sg[:, D_MODEL:], w).reshape(nb, rows_pb, D_MODEL)
    for bi in range(nb):
        x2_ref[bi * seq:(bi + 1) * seq, :] = x2[bi, 0:seq, :]


def _mixer_kernel(n_s, n_t,
                  xs_in, pool_in, s_in, xp_in, meta_ref, gmix_ref, bgk_ref, pscale_ref, gnorm_ref, wgk_ref, wpg_ref,
                  wint_hbm, wpp_hbm, wgp_hbm, wout_hbm, wfi_in, wfo_in,
                  x2s_out, pools_out, ss_out, x2p_out, poolp_out, sp_out, wfi_out, wfo_out,
                  wmain_s, wgab_s, wgk_s, wpp_s, wgp_s, wout_s, stage, zr_stage, sem, zr_sem,
                  ext_ref, lvl_ref, s_ref, meta_tail_ref, meta_s_ref,
                  xs_ref, us_ref, pooled_ref, qm_ref, kdx_ref, rhs_ref, oi_ref):
    i = pl.program_id(0)
    w = _Weights(gmix_ref, bgk_ref, pscale_ref, gnorm_ref, wpg_ref,
                 wmain_s, wgab_s, wgk_s, wpp_s, wgp_s, wout_s)

    @pl.when(i == 0)
    def _():
        _stage_weights(wint_hbm, wpp_hbm, wgp_hbm, wout_hbm, wgk_ref, w, stage, zr_stage, sem, zr_sem)

    wfi_bf = wfi_in[0].astype(BF16)
    for kc in range(wfi_out.shape[0]):
        cols = slice(kc * FFN_CHUNK, (kc + 1) * FFN_CHUNK)
        wfi_out[kc] = jnp.concatenate([wfi_bf[:, cols], wfi_bf[:, D_FF + kc * FFN_CHUNK:D_FF + (kc + 1) * FFN_CHUNK]],
                                      axis=1)
    wfo_out[...] = wfo_in[0].astype(BF16)

    @pl.when(i < n_s)
    def _():
        _sample_block(xs_in, pool_in, s_in, w, x2s_out, pools_out, ss_out,
                      xs_ref, us_ref, pooled_ref, qm_ref, kdx_ref, rhs_ref, oi_ref)

    @pl.when(i >= n_s)
    def _():
        t_idx = (i - n_s) % n_t
        _prompt_tile(i == n_s, (i - n_s) // n_t, t_idx, n_t, xp_in, meta_ref, w, x2p_out, poolp_out, sp_out,
                     ext_ref, lvl_ref, s_ref, meta_tail_ref, meta_s_ref)


def _ffn_weight_copies(wi_hbm, wo_hbm, wi_s, wo_s, sem):
    copies = []
    for c in range(D_FF // FFN_CHUNK):
        rows = pl.ds(c * FFN_CHUNK, FFN_CHUNK)
        copies.append((
            pltpu.make_async_copy(wi_hbm.at[c], wi_s.at[c], sem.at[2 * c]),
            pltpu.make_async_copy(wo_hbm.at[rows, :], wo_s.at[rows, :], sem.at[2 * c + 1])))
    return copies


def _ffn_tile(x, gffn_ref, wi_s, wo_s, gfin_ref, before_chunk=None):
    n_chunks = D_FF // FFN_CHUNK
    h, r = _rms_split(x, gffn_ref[...])
    acc = x
    group = []
    for c in range(n_chunks):
        if before_chunk is not None:
            before_chunk(c)
        gate_up = _dot(h, wi_s[c]) * r
        group.append((_silu(gate_up[:, :FFN_CHUNK]) * gate_up[:, FFN_CHUNK:]).astype(BF16))
        if len(group) == FFN_OUT_GROUP or c + 1 == n_chunks:
            first = (c + 1 - len(group)) * FFN_CHUNK
            acc = acc + _dot(jnp.concatenate(group, axis=1), wo_s[first:(c + 1) * FFN_CHUNK, :])
            group = []
    return _rms(acc, gfin_ref[...])


def _ffn_kernel(seq, xp_ref, xs_ref, gffn_ref, wi_hbm, wo_hbm, gfin_ref, yp_ref, ys_ref, wi_s, wo_s, sem):
    i = pl.program_id(0)

    @pl.when(i == 0)
    def _():
        copies = _ffn_weight_copies(wi_hbm, wo_hbm, wi_s, wo_s, sem)
        for chunk_copies in copies:
            for cp in chunk_copies:
                cp.start()

        def wait_chunk(c):
            for cp in copies[c]:
                cp.wait()

        y = _ffn_tile(xs_ref[...], gffn_ref, wi_s, wo_s, gfin_ref, before_chunk=wait_chunk)
        for bi in range(ys_ref.shape[0]):
            ys_ref[bi] = y[bi * seq:(bi + 1) * seq, :]

    @pl.when(i > 0)
    def _():
        yp_ref[0] = _ffn_tile(xp_ref[0], gffn_ref, wi_s, wo_s, gfin_ref)


def _const_spec(shape):
    zeros = (0,) * len(shape)
    return pl.BlockSpec(shape, lambda *_: zeros, pipeline_mode=pl.Buffered(1))


def kernel(x_prompt, x_sample, state_pool, state_gla, meta_tokens, g_mix, w_in, w_gk_up, b_gk, w_pool_group,
           pool_scale, w_pool_proj, g_gla_norm, w_gla_proj, w_out, g_ffn, w_ffn_in, w_ffn_out, g_final):
    depth = w_in.shape[0]
    assert depth == 1, "single-layer trunk only"
    bp, tp, d = x_prompt.shape
    bs, ts, _ = x_sample.shape
    nbb = SAMPLE_BATCH_BLOCK
    assert d == D_MODEL and w_in.shape == (1, D_MODEL, IN_DIM) and meta_tokens.shape == (N_META, D_MODEL)
    assert tp % PROMPT_TILE == 0 and tp % FFN_TILE == 0 and PROMPT_TILE % GLA_CHUNK == 0
    assert bs % nbb == 0 and ts + 2 <= SAMPLE_ROWS and bs * ts == FFN_TILE
    n_t = tp // PROMPT_TILE
    n_p = bp * n_t
    n_s = bs // nbb
    n_fi = D_MODEL // FFN_IN_CAST_ROWS
    n_fo = D_FF // FFN_OUT_CAST_ROWS
    assert n_fi <= n_p and n_fo <= n_p

    def s_idx(i):
        return jnp.minimum(i, n_s - 1)

    def p_idx(i):
        return jnp.maximum(i - n_s, 0)

    w_in_t = jnp.transpose(w_in, (0, 2, 1))
    pool_hist = jnp.transpose(state_pool, (0, 2, 1, 3))
    small = (meta_tokens, g_mix, b_gk, pool_scale, g_gla_norm, w_gk_up, w_pool_group)
    hbm = pl.BlockSpec(memory_space=pl.ANY)
    in_specs = (
        [pl.BlockSpec((nbb, ts, d), lambda i: (s_idx(i), 0, 0)),
         pl.BlockSpec((1, POOL_BUF, nbb, POOL_WIDTH), lambda i: (0, 0, s_idx(i), 0)),
         pl.BlockSpec((1, nbb, GLA_HEADS, GLA_DK, GLA_DV), lambda i: (0, s_idx(i), 0, 0, 0)),
         pl.BlockSpec((1, PROMPT_TILE, d), lambda i: (p_idx(i) // n_t, p_idx(i) % n_t, 0))]
        + [_const_spec(a.shape) for a in small]
        + [hbm, hbm, hbm, hbm,
           pl.BlockSpec((1, FFN_IN_CAST_ROWS, 2 * D_FF), lambda i: (0, jnp.minimum(p_idx(i), n_fi - 1), 0)),
           pl.BlockSpec((1, FFN_OUT_CAST_ROWS, d), lambda i: (0, jnp.minimum(p_idx(i), n_fo - 1), 0))])
    out_specs = [
        pl.BlockSpec((nbb * ts, d), lambda i: (s_idx(i), 0)),
        pl.BlockSpec((1, POOL_BUF, nbb, POOL_WIDTH), lambda i: (0, 0, s_idx(i), 0)),
        pl.BlockSpec((1, nbb, GLA_HEADS, GLA_DK, GLA_DV), lambda i: (0, s_idx(i), 0, 0, 0)),
        pl.BlockSpec((1, PROMPT_TILE, d), lambda i: (p_idx(i) // n_t, p_idx(i) % n_t, 0)),
        pl.BlockSpec((1, POOL_BUF, bp, POOL_WIDTH), lambda i: (0, 0, 0, 0)),
        pl.BlockSpec((1, 1, GLA_HEADS, GLA_DK, GLA_DV), lambda i: (0, p_idx(i) // n_t, 0, 0, 0)),
        pl.BlockSpec((D_FF // FFN_CHUNK, FFN_IN_CAST_ROWS, 2 * FFN_CHUNK),
                     lambda i: (0, jnp.minimum(p_idx(i), n_fi - 1), 0)),
        pl.BlockSpec((FFN_OUT_CAST_ROWS, d), lambda i: (jnp.minimum(p_idx(i), n_fo - 1), 0))]
    out_shape = [
        jax.ShapeDtypeStruct((bs * ts, d), F32),
        jax.ShapeDtypeStruct((1, POOL_BUF, bs, POOL_WIDTH), F32),
        jax.ShapeDtypeStruct(state_gla.shape, F32),
        jax.ShapeDtypeStruct(x_prompt.shape, F32),
        jax.ShapeDtypeStruct((1, POOL_BUF, bp, POOL_WIDTH), F32),
        jax.ShapeDtypeStruct((1, bp, GLA_HEADS, GLA_DK, GLA_DV), F32),
        jax.ShapeDtypeStruct((D_FF // FFN_CHUNK, D_MODEL, 2 * FFN_CHUNK), BF16),
        jax.ShapeDtypeStruct((D_FF, D_MODEL), BF16)]
    scratch_shapes = [
        pltpu.VMEM((D_MODEL, MAIN_COLS), BF16),
        pltpu.VMEM((D_MODEL, 2 * D_MODEL), BF16), pltpu.VMEM((LANES, GLA_KW), BF16),
        pltpu.VMEM((POOL_WIDTH, D_MODEL), BF16), pltpu.VMEM((GLA_VW, D_MODEL), BF16),
        pltpu.VMEM((D_MODEL, D_MODEL), BF16),
        pltpu.VMEM((STAGE_SLOTS, STAGE_ROWS, D_MODEL), F32), pltpu.VMEM((LANES, D_MODEL), F32),
        pltpu.SemaphoreType.DMA((STAGE_SLOTS,)), pltpu.SemaphoreType.DMA((1,)),
        pltpu.VMEM((POOL_PAD + TAIL_ROWS + PROMPT_TILE, POOL_WIDTH), F32),
        pltpu.VMEM((len(POOL_WINDOWS) - 1, POOL_PAD + TAIL_ROWS + PROMPT_TILE, POOL_WIDTH), F32),
        pltpu.VMEM((GLA_HEADS, GLA_DK, GLA_DV), F32),
        pltpu.VMEM((N_META, POOL_WIDTH), F32),
        pltpu.VMEM((GLA_HEADS, GLA_DK, GLA_DV), F32),
        pltpu.VMEM((nbb, SAMPLE_ROWS, D_MODEL), F32),
        pltpu.VMEM((len(POOL_WINDOWS), nbb * SAMPLE_ROWS, POOL_GROUP_DIM), F32),
        pltpu.VMEM((len(POOL_WINDOWS), nbb * SAMPLE_ROWS, POOL_GROUP_DIM), F32),
        pltpu.VMEM((nbb, GLA_HEADS * SAMPLE_ROWS, GLA_KW), BF16),
        pltpu.VMEM((nbb, SAMPLE_ROWS, GLA_KW), F32),
        pltpu.VMEM((nbb, SAMPLE_ROWS, 2 * GLA_VW), F32),
        pltpu.VMEM((nbb, GLA_HEADS * SAMPLE_ROWS, GLA_DV), F32)]

    def mixer(*refs):
        _mixer_kernel(n_s, n_t, *refs)

    x2_s, pool_s, gla_s, x2_p, pool_p, gla_p, wfi_bf, wfo_bf = pl.pallas_call(
        mixer,
        grid=(n_s + n_p,),
        in_specs=in_specs,
        out_specs=out_specs,
        out_shape=out_shape,
        scratch_shapes=scratch_shapes,
        compiler_params=pltpu.CompilerParams(dimension_semantics=("arbitrary",), vmem_limit_bytes=VMEM_LIMIT),
        name="mixer",
    )(x_sample, pool_hist, state_gla, x_prompt, *small, w_in_t, w_pool_proj, w_gla_proj, w_out, w_ffn_in, w_ffn_out)

    def ffn(*refs):
        _ffn_kernel(ts, *refs)

    n_tf = tp // FFN_TILE

    def prompt_tile_index(i):
        tile_id = jnp.maximum(i - 1, 0)
        return (tile_id // n_tf, tile_id % n_tf, 0)

    ffn_args = (g_ffn, wfi_bf, wfo_bf, g_final.reshape(1, D_MODEL))
    y_prompt, y_sample = pl.pallas_call(
        ffn,
        grid=(bp * n_tf + 1,),
        in_specs=[pl.BlockSpec((1, FFN_TILE, d), prompt_tile_index),
                  _const_spec(x2_s.shape), _const_spec(g_ffn.shape), hbm, hbm, _const_spec((1, D_MODEL))],
        out_specs=[pl.BlockSpec((1, FFN_TILE, d), prompt_tile_index),
                   pl.BlockSpec(x_sample.shape, lambda i: (0, 0, 0))],
        out_shape=[jax.ShapeDtypeStruct(x_prompt.shape, F32), jax.ShapeDtypeStruct(x_sample.shape, F32)],
        scratch_shapes=[pltpu.VMEM(wfi_bf.shape, BF16), pltpu.VMEM(wfo_bf.shape, BF16),
                        pltpu.SemaphoreType.DMA((2 * (D_FF // FFN_CHUNK),))],
        compiler_params=pltpu.CompilerParams(dimension_semantics=("arbitrary",), vmem_limit_bytes=VMEM_LIMIT),
        name="ffn",
    )(x2_p, x2_s, *ffn_args)
    pool_p = jnp.transpose(pool_p, (0, 2, 1, 3))
    pool_s = jnp.transpose(pool_s, (0, 2, 1, 3))
    return y_prompt, y_sample, pool_p, gla_p, pool_s, gla_s
```

```python
import jax
import jax.numpy as jnp
from jax import lax
from jax.experimental import pallas as pl
from jax.experimental.pallas import tpu as pltpu

F32 = jnp.float32
BF16 = jnp.bfloat16

D_MODEL = 1024
N_META = 16
POOL_WIDTH = 512
POOL_WINDOWS = (2, 4, 8, 16)
POOL_GROUP_DIM = 128
POOL_BUF = 15
GLA_HEADS = 4
GLA_DV = 128
GLA_DK = 64
GLA_KW = GLA_HEADS * GLA_DK
GLA_VW = GLA_HEADS * GLA_DV
GLA_GATE_RANK = 16
GLA_TAU = 16.0
GLA_CHUNK = 64
D_FF = 2816
EPS = 1e-6

LANES = 128
SUBLANES = 8
MAIN_W = POOL_WIDTH + 2 * GLA_KW + 2 * GLA_VW
U_COL, VOG_COL, QK_COL, ZR_COL = 0, POOL_WIDTH, POOL_WIDTH + 2 * GLA_VW, MAIN_W
MAIN_COLS = MAIN_W + LANES
GAB_LO = MAIN_W + GLA_GATE_RANK
IN_DIM = GAB_LO + 2 * D_MODEL
TAIL_ROWS = 16
POOL_PAD = SUBLANES

PROMPT_TILE = 512
FFN_TILE = 512
FFN_CHUNK = 256
FFN_OUT_GROUP = 4
SAMPLE_BATCH_BLOCK = 16
SAMPLE_ROWS = SUBLANES
SAMPLE_GATE_SLICES = 8
SAMPLE_EARLY_GATE_SLICES = 2
STAGE_ROWS = 256
STAGE_SLOTS = 8
FFN_IN_CAST_ROWS = 32
FFN_OUT_CAST_ROWS = 128
V7X_VMEM_BYTES = 64 * 1024 * 1024
VMEM_RESERVE = 4 * 1024 * 1024
VMEM_LIMIT = V7X_VMEM_BYTES - VMEM_RESERVE


def _dot(a, b):
    return jnp.dot(a, b, preferred_element_type=F32)


def _dot_nt(a, b):
    return lax.dot_general(a, b, (((1,), (1,)), ((), ())), preferred_element_type=F32)


def _dot_tn(a, b):
    return lax.dot_general(a, b, (((0,), (0,)), ((), ())), preferred_element_type=F32)


def _rms(x, g):
    return x * lax.rsqrt(jnp.mean(x * x, axis=-1, keepdims=True) + EPS) * g


def _rms_split(x, g):
    r = lax.rsqrt(jnp.mean(x * x, axis=-1, keepdims=True) + EPS)
    return (x * g).astype(BF16), r


def _sigmoid(x):
    return 0.5 * jnp.tanh(0.5 * x) + 0.5


def _silu(x):
    half = 0.5 * x
    return half * jnp.tanh(half) + half


def _log_sigmoid(x):
    return jnp.minimum(x, 0.0) - jnp.log(1.0 + jnp.exp(-jnp.abs(x)))


def _split_bf16(x):
    hi = x.astype(BF16)
    lo = (x - hi.astype(F32)).astype(BF16)
    return hi, lo


class _Weights:
    def __init__(self, gmix, bgk, pscale, gnorm, wpg, wmain, wgab, wgk, wpp, wgp, wout):
        self.gmix, self.bgk, self.pscale, self.gnorm, self.wpg = gmix, bgk, pscale, gnorm, wpg
        self.wmain, self.wgab, self.wgk = wmain, wgab, wgk
        self.wpp, self.wgp, self.wout = wpp, wgp, wout


def _in_proj(x, w):
    h, r = _rms_split(x, w.gmix[...])
    qkz = _dot(h, w.wmain[:, QK_COL:MAIN_COLS])
    q = qkz[:, :GLA_KW] * (r * (GLA_DK ** -0.5))
    k = qkz[:, GLA_KW:2 * GLA_KW] * r
    zr = qkz[:, 2 * GLA_KW:] * r
    z = _dot(zr.astype(BF16), w.wgk[...]) + w.bgk[...]
    u = _dot(h, w.wmain[:, U_COL:VOG_COL]) * r
    vog = _dot(h, w.wmain[:, VOG_COL:QK_COL]) * r
    v = vog[:, :GLA_VW]
    og = vog[:, GLA_VW:]
    return (h, r), u, q, k, v, og, z


def _gate_proj(xn, w, lo, hi):
    h, r = xn
    return _dot(h, w.wgab[:, lo:hi]) * r


def _chunk_cumsum_wide(g, chunk):
    n = g.shape[0] // chunk
    r = lax.broadcasted_iota(jnp.int32, (chunk, chunk), 0)
    c = lax.broadcasted_iota(jnp.int32, (chunk, chunk), 1)
    tri = jnp.where(c <= r, 1.0, 0.0).astype(BF16)
    hi, lo = _split_bf16(jnp.concatenate([g[j * chunk:(j + 1) * chunk] for j in range(n)], axis=1))
    wide = _dot(tri, hi) + _dot(tri, lo)
    width = g.shape[1]
    parts = [wide[:, j * width:(j + 1) * width] for j in range(n)]
    return jnp.concatenate(parts, axis=0), [p[chunk - 1:chunk, :] for p in parts]


def _chunk_cumsum(g, chunk):
    m = g.shape[0]
    r = lax.broadcasted_iota(jnp.int32, (m, m), 0)
    c = lax.broadcasted_iota(jnp.int32, (m, m), 1)
    tri = jnp.where((r // chunk == c // chunk) & (c <= r), 1.0, 0.0).astype(BF16)
    hi, lo = _split_bf16(g)
    return _dot(tri, hi) + _dot(tri, lo)


def _head_lane_mask(width, per_head):
    lane = lax.broadcasted_iota(jnp.int32, (1, width), 1)
    return [(lane // per_head) == h for h in range(GLA_HEADS)]


def _block_diag_rows(x_bf, per_head):
    r = x_bf.shape[0]
    zero = jnp.zeros((r, per_head), x_bf.dtype)
    rows = []
    for h in range(GLA_HEADS):
        rows.append(jnp.concatenate(
            [x_bf[:, h * per_head:(h + 1) * per_head] if hh == h else zero for hh in range(GLA_HEADS)], axis=1))
    return jnp.concatenate(rows, axis=0)


def _gla_post(o, og, w):
    parts = []
    for h in range(GLA_HEADS):
        oh = o[:, h * GLA_DV:(h + 1) * GLA_DV]
        parts.append(oh * lax.rsqrt(jnp.mean(oh * oh, axis=-1, keepdims=True) + EPS) * w.gnorm[...])
    on = jnp.concatenate(parts, axis=1)
    on = on * _silu(og)
    return _dot(on.astype(BF16), w.wgp[...])


def _pool_post(pooled, w):
    pb = pooled.astype(BF16)
    mixed = []
    for p in range(len(POOL_WINDOWS) // 2):
        w_pair = _pair_block_diag(w.wpg[0, 2 * p].astype(BF16), w.wpg[0, 2 * p + 1].astype(BF16))
        mixed.append(_dot(pb[:, 2 * p * POOL_GROUP_DIM:(2 * p + 2) * POOL_GROUP_DIM], w_pair))
    mixed = jnp.concatenate(mixed, axis=1)
    return _dot((mixed * w.pscale[...]).astype(BF16), w.wpp[...])


def _merge(x, y_a, y_b, sa, sb, w):
    merged = sa * y_a + sb * y_b
    return x + _dot(merged.astype(BF16), w.wout[...])


def _decay_columns(decay_row):
    return jnp.transpose(jnp.broadcast_to(decay_row, (LANES, decay_row.shape[1])))


def _pair_block_diag(a, b):
    zero = jnp.zeros(a.shape, a.dtype)
    return jnp.concatenate([jnp.concatenate([a, zero], axis=1), jnp.concatenate([zero, b], axis=1)], axis=0)


def _state_update(s_heads, kd_bf, v_bf, decay_row):
    dcol = _decay_columns(decay_row)
    out = []
    for p in range(GLA_HEADS // 2):
        upd = _dot_tn(kd_bf[:, 2 * p * GLA_DK:(2 * p + 2) * GLA_DK], v_bf[:, 2 * p * GLA_DV:(2 * p + 2) * GLA_DV])
        for j in range(2):
            h = 2 * p + j
            rows = slice(h * GLA_DK, (h + 1) * GLA_DK)
            out.append(dcol[rows] * s_heads[h] + upd[j * GLA_DK:(j + 1) * GLA_DK, j * GLA_DV:(j + 1) * GLA_DV])
    return out


def _stage_weights(wint_hbm, wpp_hbm, wgp_hbm, wout_hbm, wgk_ref, w, stage, zr_stage, sem, zr_sem):
    plan = []
    for r in range(0, MAIN_W, STAGE_ROWS):
        if r < POOL_WIDTH:
            col = U_COL + r
        elif r < POOL_WIDTH + 2 * GLA_KW:
            col = QK_COL + r - POOL_WIDTH
        else:
            col = VOG_COL + r - (POOL_WIDTH + 2 * GLA_KW)
        plan.append((wint_hbm, r, w.wmain, col, True))
    for r in range(0, 2 * D_MODEL, STAGE_ROWS):
        plan.append((wint_hbm, GAB_LO + r, w.wgab, r, True))
    for src, dst, n_rows in ((wpp_hbm, w.wpp, POOL_WIDTH), (wgp_hbm, w.wgp, GLA_VW), (wout_hbm, w.wout, D_MODEL)):
        for r in range(0, n_rows, STAGE_ROWS):
            plan.append((src, r, dst, r, False))

    def copy(j):
        src, r0 = plan[j][0], plan[j][1]
        return pltpu.make_async_copy(src.at[0, pl.ds(r0, STAGE_ROWS), :], stage.at[j % STAGE_SLOTS], sem.at[j % STAGE_SLOTS])

    def zr_copy():
        return pltpu.make_async_copy(
            wint_hbm.at[0, pl.ds(MAIN_W, GLA_GATE_RANK), :], zr_stage.at[pl.ds(0, GLA_GATE_RANK), :], zr_sem.at[0])

    for j in range(STAGE_SLOTS):
        copy(j).start()
    zr_copy().start()

    w.wgk[...] = jnp.zeros(w.wgk.shape, BF16)
    w.wgk[0:GLA_GATE_RANK, :] = wgk_ref[0].astype(BF16)
    zr_stage[GLA_GATE_RANK:, :] = jnp.zeros((LANES - GLA_GATE_RANK, D_MODEL), F32)

    for j in range(len(plan)):
        copy(j).wait()
        _, _, dst, d0, transposed = plan[j]
        slab = stage[j % STAGE_SLOTS]
        if transposed:
            dst[:, d0:d0 + STAGE_ROWS] = jnp.transpose(slab.astype(BF16))
        else:
            dst[d0:d0 + STAGE_ROWS, :] = slab.astype(BF16)
        if j + STAGE_SLOTS < len(plan):
            copy(j + STAGE_SLOTS).start()
    zr_copy().wait()
    w.wmain[:, ZR_COL:MAIN_COLS] = jnp.transpose(zr_stage[...]).astype(BF16)


def _prompt_tile(b_first, b_idx, t_idx, n_t, x_ref, meta_ref, w, x2_ref, pbuf_ref, sout_ref,
                 ext_ref, lvl_ref, s_ref, meta_tail_ref, meta_s_ref):
    tile = x_ref.shape[1]
    n_chunks = tile // GLA_CHUNK

    @pl.when(b_first)
    def _():
        _, u, _, k, v, _, z = _in_proj(meta_ref[...], w)
        meta_tail_ref[...] = u
        g = _log_sigmoid(z) * (1.0 / GLA_TAU)
        b = _chunk_cumsum(g, N_META)
        b_last = b[N_META - 1:N_META, :]
        kd = k * jnp.exp(b_last - b)
        zero_s = [jnp.zeros((GLA_DK, GLA_DV), F32)] * GLA_HEADS
        s_new = _state_update(zero_s, kd.astype(BF16), v.astype(BF16), jnp.exp(b_last))
        for hd in range(GLA_HEADS):
            meta_s_ref[hd] = s_new[hd]

    @pl.when(t_idx == 0)
    def _():
        ext_ref[0:POOL_PAD, :] = jnp.zeros((POOL_PAD, POOL_WIDTH), F32)
        lvl_ref[:, 0:POOL_PAD, :] = jnp.zeros((lvl_ref.shape[0], POOL_PAD, POOL_WIDTH), F32)
        ext_ref[POOL_PAD:POOL_PAD + TAIL_ROWS, :] = meta_tail_ref[...]
        s_ref[...] = meta_s_ref[...]

    x = x_ref[0]
    xn = _rms_split(x, w.gmix[...])
    h, r = xn
    gate_cols = 2 * D_MODEL // n_chunks
    gate_parts = [None] * n_chunks

    def gate_slice(c):
        gate_parts[c] = _sigmoid(_gate_proj(xn, w, c * gate_cols, (c + 1) * gate_cols))

    qkz = _dot(h, w.wmain[:, QK_COL:MAIN_COLS])
    u = _dot(h, w.wmain[:, U_COL:VOG_COL]) * r
    v = _dot(h, w.wmain[:, VOG_COL:VOG_COL + GLA_VW]) * r
    q = qkz[:, :GLA_KW] * (r * (GLA_DK ** -0.5))
    k = qkz[:, GLA_KW:2 * GLA_KW] * r
    zr = qkz[:, 2 * GLA_KW:] * r
    z = _dot(zr.astype(BF16), w.wgk[...]) + w.bgk[...]
    gate_slice(0)
    g = _log_sigmoid(z) * (1.0 / GLA_TAU)
    b, b_last_rows = _chunk_cumsum_wide(g, GLA_CHUNK)
    gate_slice(1)
    og_parts = []

    base = POOL_PAD + TAIL_ROWS
    span = TAIL_ROWS + tile
    ext_ref[base:base + tile, :] = u
    cur = ext_ref[POOL_PAD:POOL_PAD + span, :]
    pooled = []
    for gi, win in enumerate(POOL_WINDOWS):
        shift = win // 2
        lo = gi * POOL_GROUP_DIM
        prev_ref = ext_ref if gi == 0 else lvl_ref.at[gi - 1]
        cur = cur[:, (POOL_GROUP_DIM if gi else 0):] + prev_ref[POOL_PAD - shift:POOL_PAD - shift + span, lo:]
        pooled.append(cur[TAIL_ROWS:, 0:POOL_GROUP_DIM] * (1.0 / win) - u[:, lo:lo + POOL_GROUP_DIM])
        if gi + 1 < len(POOL_WINDOWS):
            lvl_ref[gi, POOL_PAD:POOL_PAD + span, lo:] = cur
    y_a = _pool_post(jnp.concatenate(pooled, axis=1), w)
    ext_ref[POOL_PAD:base, :] = ext_ref[POOL_PAD + tile:base + tile, :]

    b_last = jnp.concatenate([jnp.broadcast_to(r, (GLA_CHUNK, GLA_KW)) for r in b_last_rows], axis=0)
    qe = (q * jnp.exp(b)).astype(BF16)
    ke = k * jnp.exp(-b)
    kd = (k * jnp.exp(b_last - b)).astype(BF16)
    v_bf = v.astype(BF16)

    k_masks = _head_lane_mask(GLA_KW, GLA_DK)
    row_i = lax.broadcasted_iota(jnp.int32, (GLA_CHUNK, GLA_KW), 0)
    col_j = lax.broadcasted_iota(jnp.int32, (GLA_CHUNK, GLA_KW), 1) % GLA_CHUNK
    causal = col_j <= row_i

    s_heads = [s_ref[hd] for hd in range(GLA_HEADS)]
    o_chunks = []
    for c in range(n_chunks):
        if c + 2 < n_chunks:
            gate_slice(c + 2)
        else:
            og_lo = VOG_COL + GLA_VW + len(og_parts) * (GLA_VW // 2)
            og_parts.append(_dot(h, w.wmain[:, og_lo:og_lo + GLA_VW // 2]) * r)
        rows = slice(c * GLA_CHUNK, (c + 1) * GLA_CHUNK)
        ke_c = ke[rows]
        ke_bd = jnp.concatenate([jnp.where(k_masks[hd], ke_c, 0.0) for hd in range(GLA_HEADS)], axis=0).astype(BF16)
        att = jnp.where(causal, _dot_nt(qe[rows], ke_bd), 0.0).astype(BF16)
        o_pairs = []
        for p in range(GLA_HEADS // 2):
            h0, h1 = 2 * p, 2 * p + 1
            lanes_k = slice(h0 * GLA_DK, (h1 + 1) * GLA_DK)
            lanes_j = slice(h0 * GLA_CHUNK, (h1 + 1) * GLA_CHUNK)
            v0 = v_bf[rows, h0 * GLA_DV:(h0 + 1) * GLA_DV]
            v1 = v_bf[rows, h1 * GLA_DV:(h1 + 1) * GLA_DV]
            rhs = jnp.concatenate([_pair_block_diag(s_heads[h0].astype(BF16), s_heads[h1].astype(BF16)),
                                   _pair_block_diag(v0, v1)], axis=0)
            o_pairs.append(_dot(jnp.concatenate([qe[rows, lanes_k], att[:, lanes_j]], axis=1), rhs))
        o_chunks.append(jnp.concatenate(o_pairs, axis=1))
        s_heads = _state_update(s_heads, kd[rows], v_bf[rows], jnp.exp(b_last_rows[c]))
    for hd in range(GLA_HEADS):
        s_ref[hd] = s_heads[hd]

    y_b = _gla_post(jnp.concatenate(o_chunks, axis=0), jnp.concatenate(og_parts, axis=1), w)
    sg = jnp.concatenate(gate_parts, axis=1)
    x2_ref[0] = _merge(x, y_a, y_b, sg[:, :D_MODEL], sg[:, D_MODEL:], w)

    @pl.when(t_idx == n_t - 1)
    def _():
        sout_ref[0, 0] = s_ref[...]

    for bb in range(pbuf_ref.shape[2]):
        @pl.when((t_idx == n_t - 1) & (b_idx == bb))
        def _():
            for r in range(POOL_BUF):
                row = base - POOL_BUF + r
                pbuf_ref[0, r, bb:bb + 1, :] = ext_ref[row:row + 1, :]


def _sample_block(x_ref, pool_ref, sin_ref, w, x2_ref, pbuf_ref, sout_ref,
                  xs_ref, us_ref, pooled_ref, qm_ref, kdx_ref, rhs_ref, oi_ref):
    nb, seq, _ = x_ref.shape
    rows_pb = SAMPLE_ROWS
    m = nb * rows_pb

    xs_ref[:, seq:, :] = jnp.zeros((nb, rows_pb - seq, D_MODEL), F32)
    xs_ref[:, 0:seq, :] = x_ref[...]
    x = xs_ref[...].reshape(m, D_MODEL)
    xn, u, q, k, v, og, z = _in_proj(x, w)

    gate_cols = 2 * D_MODEL // SAMPLE_GATE_SLICES
    gate_parts = [_sigmoid(_gate_proj(xn, w, c * gate_cols, (c + 1) * gate_cols))
                  for c in range(SAMPLE_EARLY_GATE_SLICES)]

    pooled_ref[...] = jnp.zeros(pooled_ref.shape, F32)
    for gi, win in enumerate(POOL_WINDOWS):
        cols = slice(gi * POOL_GROUP_DIM, (gi + 1) * POOL_GROUP_DIM)
        us_ref[gi] = u[:, cols]
        tok = [us_ref[gi, pl.ds(t, nb, stride=rows_pb), :] for t in range(seq)]
        hist = [pool_ref[0, r, :, cols] for r in range(POOL_BUF)] + tok
        for t in range(seq):
            acc = tok[t]
            for n in range(1, win):
                acc = acc + hist[POOL_BUF + t - n]
            pooled_ref[gi, pl.ds(t, nb, stride=rows_pb), :] = acc * (1.0 / win) - tok[t]
        for r in range(POOL_BUF):
            pbuf_ref[0, r, :, cols] = hist[seq + r]

    r8 = lax.broadcasted_iota(jnp.int32, (m, 1), 0) % rows_pb
    g = jnp.where(r8 < seq, _log_sigmoid(z) * (1.0 / GLA_TAU), 0.0)
    b = _chunk_cumsum(g, rows_pb)
    b3 = b.reshape(nb, rows_pb, GLA_KW)
    b_last = jnp.broadcast_to(b3[:, seq - 1:seq, :], b3.shape).reshape(m, GLA_KW)
    qe = q * jnp.exp(b)
    ke = k * jnp.exp(-b)
    kd = k * jnp.exp(b_last - b)
    decay = jnp.exp(b_last)
    v_bf = v.astype(BF16)

    k_masks = _head_lane_mask(GLA_KW, GLA_DK)
    ke_bd = jnp.concatenate([jnp.where(k_masks[hd], ke, 0.0) for hd in range(GLA_HEADS)], axis=0).astype(BF16)
    row_i = lax.broadcasted_iota(jnp.int32, (m, GLA_HEADS * m), 0)
    col_j = lax.broadcasted_iota(jnp.int32, (m, GLA_HEADS * m), 1) % m
    keep = (row_i // rows_pb == col_j // rows_pb) & (col_j <= row_i)
    att = jnp.where(keep, _dot_nt(qe.astype(BF16), ke_bd), 0.0).astype(BF16)
    o_intra = _dot(att, _block_diag_rows(v_bf, GLA_DV))

    qe3 = qe.reshape(nb, rows_pb, GLA_KW)
    qm_ref[...] = jnp.concatenate([jnp.where(k_masks[hd], qe3, 0.0) for hd in range(GLA_HEADS)], axis=1).astype(BF16)
    d_hi = decay.astype(BF16).astype(F32)
    d_lo = decay - d_hi
    kdx = jnp.where(r8 == seq, d_hi, jnp.where(r8 == seq + 1, d_lo, kd))
    kdx_ref[...] = kdx.reshape(nb, rows_pb, GLA_KW)
    ones_rows = jnp.where((r8 == seq) | (r8 == seq + 1), 1.0, 0.0) + jnp.zeros((m, GLA_DV), F32)
    rhs = jnp.concatenate(
        [piece for hd in range(GLA_HEADS) for piece in (v[:, hd * GLA_DV:(hd + 1) * GLA_DV], ones_rows)], axis=1)
    rhs_ref[...] = rhs.reshape(nb, rows_pb, 2 * GLA_VW)

    def per_batch(i):
        s_all = sin_ref[0, i]
        s_flat = s_all.reshape(GLA_KW, GLA_DV).astype(BF16)
        oi_ref[i] = _dot(qm_ref[i], s_flat)
        kdt = jnp.transpose(kdx_ref[i]).astype(BF16)
        rhs_i = rhs_ref[i].astype(BF16)
        for hd in range(GLA_HEADS):
            r = _dot(kdt[hd * GLA_DK:(hd + 1) * GLA_DK], rhs_i[:, hd * 2 * GLA_DV:(hd + 1) * 2 * GLA_DV])
            sout_ref[0, i, hd] = r[:, GLA_DV:] * s_all[hd] + r[:, :GLA_DV]

    late = SAMPLE_GATE_SLICES - SAMPLE_EARLY_GATE_SLICES
    slice_at = {(j * nb) // late: SAMPLE_EARLY_GATE_SLICES + j for j in range(late)}
    for i in range(nb):
        if i in slice_at:
            c = slice_at[i]
            gate_parts.append(_sigmoid(_gate_proj(xn, w, c * gate_cols, (c + 1) * gate_cols)))
        per_batch(i)

    oi = oi_ref[...]
    o_inter = jnp.concatenate([oi[:, hd * rows_pb:(hd + 1) * rows_pb, :] for hd in range(GLA_HEADS)], axis=2)
    o = o_intra + o_inter.reshape(m, GLA_VW)
    y_a = _pool_post(jnp.concatenate([pooled_ref[gi] for gi in range(len(POOL_WINDOWS))], axis=1), w)
    y_b = _gla_post(o, og, w)
    sg = jnp.concatenate(gate_parts, axis=1)
    x2 = _merge(x, y_a, y_b, sg[:, :D_MODEL], sg[:, D_MODEL:], w).reshape(nb, rows_pb, D_MODEL)
    for bi in range(nb):
        x2_ref[bi * seq:(bi + 1) * seq, :] = x2[bi, 0:seq, :]


def _mixer_kernel(n_s, n_t,
                  xs_in, pool_in, s_in, xp_in, meta_ref, gmix_ref, bgk_ref, pscale_ref, gnorm_ref, wgk_ref, wpg_ref,
                  wint_hbm, wpp_hbm, wgp_hbm, wout_hbm, wfi_in, wfo_in,
                  x2s_out, pools_out, ss_out, x2p_out, poolp_out, sp_out, wfi_out, wfo_out,
                  wmain_s, wgab_s, wgk_s, wpp_s, wgp_s, wout_s, stage, zr_stage, sem, zr_sem,
                  ext_ref, lvl_ref, s_ref, meta_tail_ref, meta_s_ref,
                  xs_ref, us_ref, pooled_ref, qm_ref, kdx_ref, rhs_ref, oi_ref):
    i = pl.program_id(0)
    w = _Weights(gmix_ref, bgk_ref, pscale_ref, gnorm_ref, wpg_ref,
                 wmain_s, wgab_s, wgk_s, wpp_s, wgp_s, wout_s)

    @pl.when(i == 0)
    def _():
        _stage_weights(wint_hbm, wpp_hbm, wgp_hbm, wout_hbm, wgk_ref, w, stage, zr_stage, sem, zr_sem)

    wfi_bf = wfi_in[0].astype(BF16)
    for kc in range(wfi_out.shape[0]):
        cols = slice(kc * FFN_CHUNK, (kc + 1) * FFN_CHUNK)
        wfi_out[kc] = jnp.concatenate([wfi_bf[:, cols], wfi_bf[:, D_FF + kc * FFN_CHUNK:D_FF + (kc + 1) * FFN_CHUNK]],
                                      axis=1)
    wfo_out[...] = wfo_in[0].astype(BF16)

    @pl.when(i < n_s)
    def _():
        _sample_block(xs_in, pool_in, s_in, w, x2s_out, pools_out, ss_out,
                      xs_ref, us_ref, pooled_ref, qm_ref, kdx_ref, rhs_ref, oi_ref)

    @pl.when(i >= n_s)
    def _():
        t_idx = (i - n_s) % n_t
        _prompt_tile(i == n_s, (i - n_s) // n_t, t_idx, n_t, xp_in, meta_ref, w, x2p_out, poolp_out, sp_out,
                     ext_ref, lvl_ref, s_ref, meta_tail_ref, meta_s_ref)


def _ffn_weight_copies(wi_hbm, wo_hbm, wi_s, wo_s, sem):
    copies = []
    for c in range(D_FF // FFN_CHUNK):
        rows = pl.ds(c * FFN_CHUNK, FFN_CHUNK)
        copies.append((
            pltpu.make_async_copy(wi_hbm.at[c], wi_s.at[c], sem.at[2 * c]),
            pltpu.make_async_copy(wo_hbm.at[rows, :], wo_s.at[rows, :], sem.at[2 * c + 1])))
    return copies


def _ffn_tile(x, gffn_ref, wi_s, wo_s, gfin_ref, before_chunk=None):
    n_chunks = D_FF // FFN_CHUNK
    h, r = _rms_split(x, gffn_ref[...])
    acc = x
    group = []
    for c in range(n_chunks):
        if before_chunk is not None:
            before_chunk(c)
        gate_up = _dot(h, wi_s[c]) * r
        group.append((_silu(gate_up[:, :FFN_CHUNK]) * gate_up[:, FFN_CHUNK:]).astype(BF16))
        if len(group) == FFN_OUT_GROUP or c + 1 == n_chunks:
            first = (c + 1 - len(group)) * FFN_CHUNK
            acc = acc + _dot(jnp.concatenate(group, axis=1), wo_s[first:(c + 1) * FFN_CHUNK, :])
            group = []
    return _rms(acc, gfin_ref[...])


def _ffn_kernel(seq, xp_ref, xs_ref, gffn_ref, wi_hbm, wo_hbm, gfin_ref, yp_ref, ys_ref, wi_s, wo_s, sem):
    i = pl.program_id(0)

    @pl.when(i == 0)
    def _():
        copies = _ffn_weight_copies(wi_hbm, wo_hbm, wi_s, wo_s, sem)
        for chunk_copies in copies:
            for cp in chunk_copies:
                cp.start()

        def wait_chunk(c):
            for cp in copies[c]:
                cp.wait()

        y = _ffn_tile(xs_ref[...], gffn_ref, wi_s, wo_s, gfin_ref, before_chunk=wait_chunk)
        for bi in range(ys_ref.shape[0]):
            ys_ref[bi] = y[bi * seq:(bi + 1) * seq, :]

    @pl.when(i > 0)
    def _():
        yp_ref[0] = _ffn_tile(xp_ref[0], gffn_ref, wi_s, wo_s, gfin_ref)


def _const_spec(shape):
    zeros = (0,) * len(shape)
    return pl.BlockSpec(shape, lambda *_: zeros, pipeline_mode=pl.Buffered(1))


def kernel(x_prompt, x_sample, state_pool, state_gla, meta_tokens, g_mix, w_in, w_gk_up, b_gk, w_pool_group,
           pool_scale, w_pool_proj, g_gla_norm, w_gla_proj, w_out, g_ffn, w_ffn_in, w_ffn_out, g_final):
    depth = w_in.shape[0]
    assert depth == 1, "single-layer trunk only"
    bp, tp, d = x_prompt.shape
    bs, ts, _ = x_sample.shape
    nbb = SAMPLE_BATCH_BLOCK
    assert d == D_MODEL and w_in.shape == (1, D_MODEL, IN_DIM) and meta_tokens.shape == (N_META, D_MODEL)
    assert tp % PROMPT_TILE == 0 and tp % FFN_TILE == 0 and PROMPT_TILE % GLA_CHUNK == 0
    assert bs % nbb == 0 and ts + 2 <= SAMPLE_ROWS and bs * ts == FFN_TILE
    n_t = tp // PROMPT_TILE
    n_p = bp * n_t
    n_s = bs // nbb
    n_fi = D_MODEL // FFN_IN_CAST_ROWS
    n_fo = D_FF // FFN_OUT_CAST_ROWS
    assert n_fi <= n_p and n_fo <= n_p

    def s_idx(i):
        return jnp.minimum(i, n_s - 1)

    def p_idx(i):
        return jnp.maximum(i - n_s, 0)

    w_in_t = jnp.transpose(w_in, (0, 2, 1))
    pool_hist = jnp.transpose(state_pool, (0, 2, 1, 3))
    small = (meta_tokens, g_mix, b_gk, pool_scale, g_gla_norm, w_gk_up, w_pool_group)
    hbm = pl.BlockSpec(memory_space=pl.ANY)
    in_specs = (
        [pl.BlockSpec((nbb, ts, d), lambda i: (s_idx(i), 0, 0)),
         pl.BlockSpec((1, POOL_BUF, nbb, POOL_WIDTH), lambda i: (0, 0, s_idx(i), 0)),
         pl.BlockSpec((1, nbb, GLA_HEADS, GLA_DK, GLA_DV), lambda i: (0, s_idx(i), 0, 0, 0)),
         pl.BlockSpec((1, PROMPT_TILE, d), lambda i: (p_idx(i) // n_t, p_idx(i) % n_t, 0))]
        + [_const_spec(a.shape) for a in small]
        + [hbm, hbm, hbm, hbm,
           pl.BlockSpec((1, FFN_IN_CAST_ROWS, 2 * D_FF), lambda i: (0, jnp.minimum(p_idx(i), n_fi - 1), 0)),
           pl.BlockSpec((1, FFN_OUT_CAST_ROWS, d), lambda i: (0, jnp.minimum(p_idx(i), n_fo - 1), 0))])
    out_specs = [
        pl.BlockSpec((nbb * ts, d), lambda i: (s_idx(i), 0)),
        pl.BlockSpec((1, POOL_BUF, nbb, POOL_WIDTH), lambda i: (0, 0, s_idx(i), 0)),
        pl.BlockSpec((1, nbb, GLA_HEADS, GLA_DK, GLA_DV), lambda i: (0, s_idx(i), 0, 0, 0)),
        pl.BlockSpec((1, PROMPT_TILE, d), lambda i: (p_idx(i) // n_t, p_idx(i) % n_t, 0)),
        pl.BlockSpec((1, POOL_BUF, bp, POOL_WIDTH), lambda i: (0, 0, 0, 0)),
        pl.BlockSpec((1, 1, GLA_HEADS, GLA_DK, GLA_DV), lambda i: (0, p_idx(i) // n_t, 0, 0, 0)),
        pl.BlockSpec((D_FF // FFN_CHUNK, FFN_IN_CAST_ROWS, 2 * FFN_CHUNK),
                     lambda i: (0, jnp.minimum(p_idx(i), n_fi - 1), 0)),
        pl.BlockSpec((FFN_OUT_CAST_ROWS, d), lambda i: (jnp.minimum(p_idx(i), n_fo - 1), 0))]
    out_shape = [
        jax.ShapeDtypeStruct((bs * ts, d), F32),
        jax.ShapeDtypeStruct((1, POOL_BUF, bs, POOL_WIDTH), F32),
        jax.ShapeDtypeStruct(state_gla.shape, F32),
        jax.ShapeDtypeStruct(x_prompt.shape, F32),
        jax.ShapeDtypeStruct((1, POOL_BUF, bp, POOL_WIDTH), F32),
        jax.ShapeDtypeStruct((1, bp, GLA_HEADS, GLA_DK, GLA_DV), F32),
        jax.ShapeDtypeStruct((D_FF // FFN_CHUNK, D_MODEL, 2 * FFN_CHUNK), BF16),
        jax.ShapeDtypeStruct((D_FF, D_MODEL), BF16)]
    scratch_shapes = [
        pltpu.VMEM((D_MODEL, MAIN_COLS), BF16),
        pltpu.VMEM((D_MODEL, 2 * D_MODEL), BF16), pltpu.VMEM((LANES, GLA_KW), BF16),
        pltpu.VMEM((POOL_WIDTH, D_MODEL), BF16), pltpu.VMEM((GLA_VW, D_MODEL), BF16),
        pltpu.VMEM((D_MODEL, D_MODEL), BF16),
        pltpu.VMEM((STAGE_SLOTS, STAGE_ROWS, D_MODEL), F32), pltpu.VMEM((LANES, D_MODEL), F32),
        pltpu.SemaphoreType.DMA((STAGE_SLOTS,)), pltpu.SemaphoreType.DMA((1,)),
        pltpu.VMEM((POOL_PAD + TAIL_ROWS + PROMPT_TILE, POOL_WIDTH), F32),
        pltpu.VMEM((len(POOL_WINDOWS) - 1, POOL_PAD + TAIL_ROWS + PROMPT_TILE, POOL_WIDTH), F32),
        pltpu.VMEM((GLA_HEADS, GLA_DK, GLA_DV), F32),
        pltpu.VMEM((N_META, POOL_WIDTH), F32),
        pltpu.VMEM((GLA_HEADS, GLA_DK, GLA_DV), F32),
        pltpu.VMEM((nbb, SAMPLE_ROWS, D_MODEL), F32),
        pltpu.VMEM((len(POOL_WINDOWS), nbb * SAMPLE_ROWS, POOL_GROUP_DIM), F32),
        pltpu.VMEM((len(POOL_WINDOWS), nbb * SAMPLE_ROWS, POOL_GROUP_DIM), F32),
        pltpu.VMEM((nbb, GLA_HEADS * SAMPLE_ROWS, GLA_KW), BF16),
        pltpu.VMEM((nbb, SAMPLE_ROWS, GLA_KW), F32),
        pltpu.VMEM((nbb, SAMPLE_ROWS, 2 * GLA_VW), F32),
        pltpu.VMEM((nbb, GLA_HEADS * SAMPLE_ROWS, GLA_DV), F32)]

    def mixer(*refs):
        _mixer_kernel(n_s, n_t, *refs)

    x2_s, pool_s, gla_s, x2_p, pool_p, gla_p, wfi_bf, wfo_bf = pl.pallas_call(
        mixer,
        grid=(n_s + n_p,),
        in_specs=in_specs,
        out_specs=out_specs,
        out_shape=out_shape,
        scratch_shapes=scratch_shapes,
        compiler_params=pltpu.CompilerParams(dimension_semantics=("arbitrary",), vmem_limit_bytes=VMEM_LIMIT),
        name="mixer",
    )(x_sample, pool_hist, state_gla, x_prompt, *small, w_in_t, w_pool_proj, w_gla_proj, w_out, w_ffn_in, w_ffn_out)

    def ffn(*refs):
        _ffn_kernel(ts, *refs)

    n_tf = tp // FFN_TILE

    def prompt_tile_index(i):
        tile_id = jnp.maximum(i - 1, 0)
        return (tile_id // n_tf, tile_id % n_tf, 0)

    ffn_args = (g_ffn, wfi_bf, wfo_bf, g_final.reshape(1, D_MODEL))
    y_prompt, y_sample = pl.pallas_call(
        ffn,
        grid=(bp * n_tf + 1,),
        in_specs=[pl.BlockSpec((1, FFN_TILE, d), prompt_tile_index),
                  _const_spec(x2_s.shape), _const_spec(g_ffn.shape), hbm, hbm, _const_spec((1, D_MODEL))],
        out_specs=[pl.BlockSpec((1, FFN_TILE, d), prompt_tile_index),
                   pl.BlockSpec(x_sample.shape, lambda i: (0, 0, 0))],
        out_shape=[jax.ShapeDtypeStruct(x_prompt.shape, F32), jax.ShapeDtypeStruct(x_sample.shape, F32)],
        scratch_shapes=[pltpu.VMEM(wfi_bf.shape, BF16), pltpu.VMEM(wfo_bf.shape, BF16),
                        pltpu.SemaphoreType.DMA((2 * (D_FF // FFN_CHUNK),))],
        compiler_params=pltpu.CompilerParams(dimension_semantics=("arbitrary",), vmem_limit_bytes=VMEM_LIMIT),
        name="ffn",
    )(x2_p, x2_s, *ffn_args)
    pool_p = jnp.transpose(pool_p, (0, 2, 1, 3))
    pool_s = jnp.transpose(pool_s, (0, 2, 1, 3))
    return y_prompt, y_sample, pool_p, gla_p, pool_s, gla_s
```

```python
import jax
import jax.numpy as jnp
from jax import lax
from jax.experimental import pallas as pl
from jax.experimental.pallas import tpu as pltpu

F32 = jnp.float32
BF16 = jnp.bfloat16

D_MODEL = 1024
N_META = 16
POOL_WIDTH = 512
POOL_WINDOWS = (2, 4, 8, 16)
POOL_GROUP_DIM = 128
POOL_BUF = 15
GLA_HEADS = 4
GLA_DV = 128
GLA_DK = 64
GLA_KW = GLA_HEADS * GLA_DK
GLA_VW = GLA_HEADS * GLA_DV
GLA_GATE_RANK = 16
GLA_TAU = 16.0
GLA_CHUNK = 64
D_FF = 2816
EPS = 1e-6

LANES = 128
SUBLANES = 8
MAIN_W = POOL_WIDTH + 2 * GLA_KW + 2 * GLA_VW
U_COL, VOG_COL, QK_COL, ZR_COL = 0, POOL_WIDTH, POOL_WIDTH + 2 * GLA_VW, MAIN_W
MAIN_COLS = MAIN_W + LANES
GAB_LO = MAIN_W + GLA_GATE_RANK
IN_DIM = GAB_LO + 2 * D_MODEL
TAIL_ROWS = 16
POOL_PAD = SUBLANES

PROMPT_TILE = 512
FFN_TILE = 512
FFN_CHUNK = 256
FFN_OUT_GROUP = 4
SAMPLE_BATCH_BLOCK = 32
SAMPLE_ATT_ROWS = 128
STATE_GROUP = 8
SAMPLE_ROWS = SUBLANES
SAMPLE_GATE_SLICES = 8
SAMPLE_EARLY_GATE_SLICES = 2
STAGE_ROWS = 256
STAGE_SLOTS = 4
FFN_IN_CAST_ROWS = 32
FFN_OUT_CAST_ROWS = 128
V7X_VMEM_BYTES = 64 * 1024 * 1024
VMEM_RESERVE = 4 * 1024 * 1024
VMEM_LIMIT = V7X_VMEM_BYTES - VMEM_RESERVE


def _dot(a, b):
    return jnp.dot(a, b, preferred_element_type=F32)


def _dot_nt(a, b):
    return lax.dot_general(a, b, (((1,), (1,)), ((), ())), preferred_element_type=F32)


def _dot_tn(a, b):
    return lax.dot_general(a, b, (((0,), (0,)), ((), ())), preferred_element_type=F32)


def _rms(x, g):
    return x * lax.rsqrt(jnp.mean(x * x, axis=-1, keepdims=True) + EPS) * g


def _rms_split(x, g):
    r = lax.rsqrt(jnp.mean(x * x, axis=-1, keepdims=True) + EPS)
    return (x * g).astype(BF16), r


def _sigmoid(x):
    return 0.5 * jnp.tanh(0.5 * x) + 0.5


def _silu(x):
    half = 0.5 * x
    return half * jnp.tanh(half) + half


def _log_sigmoid(x):
    return jnp.minimum(x, 0.0) - jnp.log(1.0 + jnp.exp(-jnp.abs(x)))


def _split_bf16(x):
    hi = x.astype(BF16)
    lo = (x - hi.astype(F32)).astype(BF16)
    return hi, lo


class _Weights:
    def __init__(self, gmix, bgk, pscale, gnorm, wpg, wmain, wgab, wgk, wpp, wgp, wout):
        self.gmix, self.bgk, self.pscale, self.gnorm, self.wpg = gmix, bgk, pscale, gnorm, wpg
        self.wmain, self.wgab, self.wgk = wmain, wgab, wgk
        self.wpp, self.wgp, self.wout = wpp, wgp, wout


def _in_proj(x, w):
    h, r = _rms_split(x, w.gmix[...])
    qkz = _dot(h, w.wmain[:, QK_COL:MAIN_COLS])
    q = qkz[:, :GLA_KW] * (r * (GLA_DK ** -0.5))
    k = qkz[:, GLA_KW:2 * GLA_KW] * r
    zr = qkz[:, 2 * GLA_KW:] * r
    z = _dot(zr.astype(BF16), w.wgk[...]) + w.bgk[...]
    u = _dot(h, w.wmain[:, U_COL:VOG_COL]) * r
    vog = _dot(h, w.wmain[:, VOG_COL:QK_COL]) * r
    v = vog[:, :GLA_VW]
    og = vog[:, GLA_VW:]
    return (h, r), u, q, k, v, og, z


def _gate_proj(xn, w, lo, hi):
    h, r = xn
    return _dot(h, w.wgab[:, lo:hi]) * r


def _chunk_cumsum_wide(g, chunk):
    n = g.shape[0] // chunk
    r = lax.broadcasted_iota(jnp.int32, (chunk, chunk), 0)
    c = lax.broadcasted_iota(jnp.int32, (chunk, chunk), 1)
    tri = jnp.where(c <= r, 1.0, 0.0).astype(BF16)
    hi, lo = _split_bf16(jnp.concatenate([g[j * chunk:(j + 1) * chunk] for j in range(n)], axis=1))
    wide = _dot(tri, hi) + _dot(tri, lo)
    width = g.shape[1]
    parts = [wide[:, j * width:(j + 1) * width] for j in range(n)]
    return jnp.concatenate(parts, axis=0), [p[chunk - 1:chunk, :] for p in parts]


def _chunk_cumsum(g, chunk):
    m = g.shape[0]
    r = lax.broadcasted_iota(jnp.int32, (m, m), 0)
    c = lax.broadcasted_iota(jnp.int32, (m, m), 1)
    tri = jnp.where((r // chunk == c // chunk) & (c <= r), 1.0, 0.0).astype(BF16)
    hi, lo = _split_bf16(g)
    return _dot(tri, hi) + _dot(tri, lo)


def _head_lane_mask(width, per_head):
    lane = lax.broadcasted_iota(jnp.int32, (1, width), 1)
    return [(lane // per_head) == h for h in range(GLA_HEADS)]


def _block_diag_rows(x_bf, per_head):
    r = x_bf.shape[0]
    zero = jnp.zeros((r, per_head), x_bf.dtype)
    rows = []
    for h in range(GLA_HEADS):
        rows.append(jnp.concatenate(
            [x_bf[:, h * per_head:(h + 1) * per_head] if hh == h else zero for hh in range(GLA_HEADS)], axis=1))
    return jnp.concatenate(rows, axis=0)


def _gla_post(o, og, w):
    parts = []
    for h in range(GLA_HEADS):
        oh = o[:, h * GLA_DV:(h + 1) * GLA_DV]
        parts.append(oh * lax.rsqrt(jnp.mean(oh * oh, axis=-1, keepdims=True) + EPS) * w.gnorm[...])
    on = jnp.concatenate(parts, axis=1)
    on = on * _silu(og)
    return _dot(on.astype(BF16), w.wgp[...])


def _pool_post(pooled, w):
    pb = pooled.astype(BF16)
    mixed = []
    for p in range(len(POOL_WINDOWS) // 2):
        w_pair = _pair_block_diag(w.wpg[0, 2 * p].astype(BF16), w.wpg[0, 2 * p + 1].astype(BF16))
        mixed.append(_dot(pb[:, 2 * p * POOL_GROUP_DIM:(2 * p + 2) * POOL_GROUP_DIM], w_pair))
    mixed = jnp.concatenate(mixed, axis=1)
    return _dot((mixed * w.pscale[...]).astype(BF16), w.wpp[...])


def _merge(x, y_a, y_b, sa, sb, w):
    merged = sa * y_a + sb * y_b
    return x + _dot(merged.astype(BF16), w.wout[...])


def _decay_columns(decay_row):
    return jnp.transpose(jnp.broadcast_to(decay_row, (LANES, decay_row.shape[1])))


def _pair_block_diag(a, b):
    zero = jnp.zeros(a.shape, a.dtype)
    return jnp.concatenate([jnp.concatenate([a, zero], axis=1), jnp.concatenate([zero, b], axis=1)], axis=0)


def _state_update(s_heads, kd_bf, v_bf, decay_row):
    dcol = _decay_columns(decay_row)
    out = []
    for p in range(GLA_HEADS // 2):
        upd = _dot_tn(kd_bf[:, 2 * p * GLA_DK:(2 * p + 2) * GLA_DK], v_bf[:, 2 * p * GLA_DV:(2 * p + 2) * GLA_DV])
        for j in range(2):
            h = 2 * p + j
            rows = slice(h * GLA_DK, (h + 1) * GLA_DK)
            out.append(dcol[rows] * s_heads[h] + upd[j * GLA_DK:(j + 1) * GLA_DK, j * GLA_DV:(j + 1) * GLA_DV])
    return out


def _stage_weights(wint_hbm, wpp_hbm, wgp_hbm, wout_hbm, wgk_ref, w, stage, zr_stage, sem, zr_sem):
    plan = []
    for r in range(0, MAIN_W, STAGE_ROWS):
        if r < POOL_WIDTH:
            col = U_COL + r
        elif r < POOL_WIDTH + 2 * GLA_KW:
            col = QK_COL + r - POOL_WIDTH
        else:
            col = VOG_COL + r - (POOL_WIDTH + 2 * GLA_KW)
        plan.append((wint_hbm, r, w.wmain, col, True))
    for r in range(0, 2 * D_MODEL, STAGE_ROWS):
        plan.append((wint_hbm, GAB_LO + r, w.wgab, r, True))
    for src, dst, n_rows in ((wpp_hbm, w.wpp, POOL_WIDTH), (wgp_hbm, w.wgp, GLA_VW), (wout_hbm, w.wout, D_MODEL)):
        for r in range(0, n_rows, STAGE_ROWS):
            plan.append((src, r, dst, r, False))

    def copy(j):
        src, r0 = plan[j][0], plan[j][1]
        return pltpu.make_async_copy(src.at[0, pl.ds(r0, STAGE_ROWS), :], stage.at[j % STAGE_SLOTS], sem.at[j % STAGE_SLOTS])

    def zr_copy():
        return pltpu.make_async_copy(
            wint_hbm.at[0, pl.ds(MAIN_W, GLA_GATE_RANK), :], zr_stage.at[pl.ds(0, GLA_GATE_RANK), :], zr_sem.at[0])

    for j in range(STAGE_SLOTS):
        copy(j).start()
    zr_copy().start()

    w.wgk[...] = jnp.zeros(w.wgk.shape, BF16)
    w.wgk[0:GLA_GATE_RANK, :] = wgk_ref[0].astype(BF16)
    zr_stage[GLA_GATE_RANK:, :] = jnp.zeros((LANES - GLA_GATE_RANK, D_MODEL), F32)

    for j in range(len(plan)):
        copy(j).wait()
        _, _, dst, d0, transposed = plan[j]
        slab = stage[j % STAGE_SLOTS]
        if transposed:
            dst[:, d0:d0 + STAGE_ROWS] = jnp.transpose(slab.astype(BF16))
        else:
            dst[d0:d0 + STAGE_ROWS, :] = slab.astype(BF16)
        if j + STAGE_SLOTS < len(plan):
            copy(j + STAGE_SLOTS).start()
    zr_copy().wait()
    w.wmain[:, ZR_COL:MAIN_COLS] = jnp.transpose(zr_stage[...]).astype(BF16)


def _prompt_tile(b_first, b_idx, t_idx, n_t, x_ref, meta_ref, w, x2_ref, pbuf_ref, sout_ref,
                 ext_ref, lvl_ref, s_ref, meta_tail_ref, meta_s_ref):
    tile = x_ref.shape[1]
    n_chunks = tile // GLA_CHUNK

    @pl.when(b_first)
    def _():
        _, u, _, k, v, _, z = _in_proj(meta_ref[...], w)
        meta_tail_ref[...] = u
        g = _log_sigmoid(z) * (1.0 / GLA_TAU)
        b = _chunk_cumsum(g, N_META)
        b_last = b[N_META - 1:N_META, :]
        kd = k * jnp.exp(b_last - b)
        zero_s = [jnp.zeros((GLA_DK, GLA_DV), F32)] * GLA_HEADS
        s_new = _state_update(zero_s, kd.astype(BF16), v.astype(BF16), jnp.exp(b_last))
        for hd in range(GLA_HEADS):
            meta_s_ref[hd] = s_new[hd]

    @pl.when(t_idx == 0)
    def _():
        ext_ref[0:POOL_PAD, :] = jnp.zeros((POOL_PAD, POOL_WIDTH), F32)
        lvl_ref[:, 0:POOL_PAD, :] = jnp.zeros((lvl_ref.shape[0], POOL_PAD, POOL_WIDTH), F32)
        ext_ref[POOL_PAD:POOL_PAD + TAIL_ROWS, :] = meta_tail_ref[...]
        s_ref[...] = meta_s_ref[...]

    x = x_ref[0]
    xn = _rms_split(x, w.gmix[...])
    h, r = xn
    gate_cols = 2 * D_MODEL // n_chunks
    gate_parts = [None] * n_chunks

    def gate_slice(c):
        gate_parts[c] = _sigmoid(_gate_proj(xn, w, c * gate_cols, (c + 1) * gate_cols))

    qkz = _dot(h, w.wmain[:, QK_COL:MAIN_COLS])
    u = _dot(h, w.wmain[:, U_COL:VOG_COL]) * r
    v = _dot(h, w.wmain[:, VOG_COL:VOG_COL + GLA_VW]) * r
    q = qkz[:, :GLA_KW] * (r * (GLA_DK ** -0.5))
    k = qkz[:, GLA_KW:2 * GLA_KW] * r
    zr = qkz[:, 2 * GLA_KW:] * r
    z = _dot(zr.astype(BF16), w.wgk[...]) + w.bgk[...]
    gate_slice(0)
    g = _log_sigmoid(z) * (1.0 / GLA_TAU)
    b, b_last_rows = _chunk_cumsum_wide(g, GLA_CHUNK)
    gate_slice(1)
    og_parts = []

    base = POOL_PAD + TAIL_ROWS
    span = TAIL_ROWS + tile
    ext_ref[base:base + tile, :] = u
    cur = ext_ref[POOL_PAD:POOL_PAD + span, :]
    pooled = []
    for gi, win in enumerate(POOL_WINDOWS):
        shift = win // 2
        lo = gi * POOL_GROUP_DIM
        prev_ref = ext_ref if gi == 0 else lvl_ref.at[gi - 1]
        cur = cur[:, (POOL_GROUP_DIM if gi else 0):] + prev_ref[POOL_PAD - shift:POOL_PAD - shift + span, lo:]
        pooled.append(cur[TAIL_ROWS:, 0:POOL_GROUP_DIM] * (1.0 / win) - u[:, lo:lo + POOL_GROUP_DIM])
        if gi + 1 < len(POOL_WINDOWS):
            lvl_ref[gi, POOL_PAD:POOL_PAD + span, lo:] = cur
    y_a = _pool_post(jnp.concatenate(pooled, axis=1), w)
    ext_ref[POOL_PAD:base, :] = ext_ref[POOL_PAD + tile:base + tile, :]

    b_last = jnp.concatenate([jnp.broadcast_to(r, (GLA_CHUNK, GLA_KW)) for r in b_last_rows], axis=0)
    qe = (q * jnp.exp(b)).astype(BF16)
    ke = k * jnp.exp(-b)
    kd = (k * jnp.exp(b_last - b)).astype(BF16)
    v_bf = v.astype(BF16)

    k_masks = _head_lane_mask(GLA_KW, GLA_DK)
    row_i = lax.broadcasted_iota(jnp.int32, (GLA_CHUNK, GLA_KW), 0)
    col_j = lax.broadcasted_iota(jnp.int32, (GLA_CHUNK, GLA_KW), 1) % GLA_CHUNK
    causal = col_j <= row_i

    s_heads = [s_ref[hd] for hd in range(GLA_HEADS)]
    o_chunks = []
    for c in range(n_chunks):
        if c + 2 < n_chunks:
            gate_slice(c + 2)
        else:
            og_lo = VOG_COL + GLA_VW + len(og_parts) * (GLA_VW // 2)
            og_parts.append(_dot(h, w.wmain[:, og_lo:og_lo + GLA_VW // 2]) * r)
        rows = slice(c * GLA_CHUNK, (c + 1) * GLA_CHUNK)
        ke_c = ke[rows]
        ke_bd = jnp.concatenate([jnp.where(k_masks[hd], ke_c, 0.0) for hd in range(GLA_HEADS)], axis=0).astype(BF16)
        att = jnp.where(causal, _dot_nt(qe[rows], ke_bd), 0.0).astype(BF16)
        o_pairs = []
        for p in range(GLA_HEADS // 2):
            h0, h1 = 2 * p, 2 * p + 1
            lanes_k = slice(h0 * GLA_DK, (h1 + 1) * GLA_DK)
            lanes_j = slice(h0 * GLA_CHUNK, (h1 + 1) * GLA_CHUNK)
            v0 = v_bf[rows, h0 * GLA_DV:(h0 + 1) * GLA_DV]
            v1 = v_bf[rows, h1 * GLA_DV:(h1 + 1) * GLA_DV]
            rhs = jnp.concatenate([_pair_block_diag(s_heads[h0].astype(BF16), s_heads[h1].astype(BF16)),
                                   _pair_block_diag(v0, v1)], axis=0)
            o_pairs.append(_dot(jnp.concatenate([qe[rows, lanes_k], att[:, lanes_j]], axis=1), rhs))
        o_chunks.append(jnp.concatenate(o_pairs, axis=1))
        s_heads = _state_update(s_heads, kd[rows], v_bf[rows], jnp.exp(b_last_rows[c]))
    for hd in range(GLA_HEADS):
        s_ref[hd] = s_heads[hd]

    y_b = _gla_post(jnp.concatenate(o_chunks, axis=0), jnp.concatenate(og_parts, axis=1), w)
    sg = jnp.concatenate(gate_parts, axis=1)
    x2_ref[0] = _merge(x, y_a, y_b, sg[:, :D_MODEL], sg[:, D_MODEL:], w)

    @pl.when(t_idx == n_t - 1)
    def _():
        sout_ref[0, 0] = s_ref[...]

    for bb in range(pbuf_ref.shape[2]):
        @pl.when((t_idx == n_t - 1) & (b_idx == bb))
        def _():
            for r in range(POOL_BUF):
                row = base - POOL_BUF + r
                pbuf_ref[0, r, bb:bb + 1, :] = ext_ref[row:row + 1, :]


def _sample_block(first_batch, x_ref, pool_ref, sin_hbm, w, x2_ref, pbuf_ref, sout_hbm,
                  xs_ref, us_ref, pooled_ref, qm_ref, kdx_ref, rhs_ref, oi_ref,
                  sbuf_in, sbuf_out, in_sem, out_sem):
    nb, seq, _ = x_ref.shape
    rows_pb = SAMPLE_ROWS
    m = nb * rows_pb
    n_groups = nb // STATE_GROUP

    def in_copy(g):
        return pltpu.make_async_copy(sin_hbm.at[0, pl.ds(first_batch + g * STATE_GROUP, STATE_GROUP)],
                                     sbuf_in.at[g], in_sem.at[g])

    def out_copy(g):
        return pltpu.make_async_copy(sbuf_out.at[g],
                                     sout_hbm.at[0, pl.ds(first_batch + g * STATE_GROUP, STATE_GROUP)], out_sem.at[g])

    for g in range(n_groups):
        in_copy(g).start()

    xs_ref[:, seq:, :] = jnp.zeros((nb, rows_pb - seq, D_MODEL), F32)
    xs_ref[:, 0:seq, :] = x_ref[...]
    x = xs_ref[...].reshape(m, D_MODEL)
    xn, u, q, k, v, og, z = _in_proj(x, w)

    gate_cols = 2 * D_MODEL // SAMPLE_GATE_SLICES
    gate_parts = [_sigmoid(_gate_proj(xn, w, c * gate_cols, (c + 1) * gate_cols))
                  for c in range(SAMPLE_EARLY_GATE_SLICES)]

    pooled_ref[...] = jnp.zeros(pooled_ref.shape, F32)
    for gi, win in enumerate(POOL_WINDOWS):
        cols = slice(gi * POOL_GROUP_DIM, (gi + 1) * POOL_GROUP_DIM)
        us_ref[gi] = u[:, cols]
        tok = [us_ref[gi, pl.ds(t, nb, stride=rows_pb), :] for t in range(seq)]
        hist = [pool_ref[0, r, :, cols] for r in range(POOL_BUF)] + tok
        for t in range(seq):
            acc = tok[t]
            for n in range(1, win):
                acc = acc + hist[POOL_BUF + t - n]
            pooled_ref[gi, pl.ds(t, nb, stride=rows_pb), :] = acc * (1.0 / win) - tok[t]
        for r in range(POOL_BUF):
            pbuf_ref[0, r, :, cols] = hist[seq + r]

    r8 = lax.broadcasted_iota(jnp.int32, (m, 1), 0) % rows_pb
    g = jnp.where(r8 < seq, _log_sigmoid(z) * (1.0 / GLA_TAU), 0.0)
    b = _chunk_cumsum(g, rows_pb)
    b3 = b.reshape(nb, rows_pb, GLA_KW)
    b_last = jnp.broadcast_to(b3[:, seq - 1:seq, :], b3.shape).reshape(m, GLA_KW)
    qe = q * jnp.exp(b)
    ke = k * jnp.exp(-b)
    kd = k * jnp.exp(b_last - b)
    decay = jnp.exp(b_last)
    v_bf = v.astype(BF16)

    k_masks = _head_lane_mask(GLA_KW, GLA_DK)
    ma = SAMPLE_ATT_ROWS
    row_i = lax.broadcasted_iota(jnp.int32, (ma, GLA_HEADS * ma), 0)
    col_j = lax.broadcasted_iota(jnp.int32, (ma, GLA_HEADS * ma), 1) % ma
    keep = (row_i // rows_pb == col_j // rows_pb) & (col_j <= row_i)
    qe_bf = qe.astype(BF16)
    o_intra = []
    for a0 in range(0, m, ma):
        rows = slice(a0, a0 + ma)
        ke_bd = jnp.concatenate([jnp.where(k_masks[hd], ke[rows], 0.0) for hd in range(GLA_HEADS)],
                                axis=0).astype(BF16)
        att = jnp.where(keep, _dot_nt(qe_bf[rows], ke_bd), 0.0).astype(BF16)
        o_intra.append(_dot(att, _block_diag_rows(v_bf[rows], GLA_DV)))
    o_intra = jnp.concatenate(o_intra, axis=0)

    qe3 = qe.reshape(nb, rows_pb, GLA_KW)
    qm_ref[...] = jnp.concatenate([jnp.where(k_masks[hd], qe3, 0.0) for hd in range(GLA_HEADS)], axis=1).astype(BF16)
    d_hi = decay.astype(BF16).astype(F32)
    d_lo = decay - d_hi
    kdx = jnp.where(r8 == seq, d_hi, jnp.where(r8 == seq + 1, d_lo, kd))
    kdx_ref[...] = kdx.reshape(nb, rows_pb, GLA_KW)
    ones_rows = jnp.where((r8 == seq) | (r8 == seq + 1), 1.0, 0.0) + jnp.zeros((m, GLA_DV), F32)
    rhs = jnp.concatenate(
        [piece for hd in range(GLA_HEADS) for piece in (v[:, hd * GLA_DV:(hd + 1) * GLA_DV], ones_rows)], axis=1)
    rhs_ref[...] = rhs.reshape(nb, rows_pb, 2 * GLA_VW)

    def per_batch(i):
        g, j = divmod(i, STATE_GROUP)
        s_all = sbuf_in[g, j]
        s_flat = s_all.reshape(GLA_KW, GLA_DV).astype(BF16)
        oi_ref[i] = _dot(qm_ref[i], s_flat)
        kdt = jnp.transpose(kdx_ref[i]).astype(BF16)
        rhs_i = rhs_ref[i].astype(BF16)
        for hd in range(GLA_HEADS):
            r = _dot(kdt[hd * GLA_DK:(hd + 1) * GLA_DK], rhs_i[:, hd * 2 * GLA_DV:(hd + 1) * 2 * GLA_DV])
            sbuf_out[g, j, hd] = r[:, GLA_DV:] * s_all[hd] + r[:, :GLA_DV]

    late = SAMPLE_GATE_SLICES - SAMPLE_EARLY_GATE_SLICES
    slice_at = {(j * nb) // late: SAMPLE_EARLY_GATE_SLICES + j for j in range(late)}
    for g in range(n_groups):
        in_copy(g).wait()
    for i in range(nb):
        if i in slice_at:
            c = slice_at[i]
            gate_parts.append(_sigmoid(_gate_proj(xn, w, c * gate_cols, (c + 1) * gate_cols)))
        per_batch(i)
        if i % STATE_GROUP == STATE_GROUP - 1:
            out_copy(i // STATE_GROUP).start()
    for g in range(n_groups):
        out_copy(g).wait()

    oi = oi_ref[...]
    o_inter = jnp.concatenate([oi[:, hd * rows_pb:(hd + 1) * rows_pb, :] for hd in range(GLA_HEADS)], axis=2)
    o = o_intra + o_inter.reshape(m, GLA_VW)
    y_a = _pool_post(jnp.concatenate([pooled_ref[gi] for gi in range(len(POOL_WINDOWS))], axis=1), w)
    y_b = _gla_post(o, og, w)
    sg = jnp.concatenate(gate_parts, axis=1)
    x2 = _merge(x, y_a, y_b, sg[:, :D_MODEL], sg[:, D_MODEL:], w).reshape(nb, rows_pb, D_MODEL)
    for bi in range(nb):
        x2_ref[bi * seq:(bi + 1) * seq, :] = x2[bi, 0:seq, :]


def _mixer_kernel(n_s, n_t,
                  xs_in, pool_in, s_in, xp_in, meta_ref, gmix_ref, bgk_ref, pscale_ref, gnorm_ref, wgk_ref, wpg_ref,
                  wint_hbm, wpp_hbm, wgp_hbm, wout_hbm, wfi_in, wfo_in,
                  x2s_out, pools_out, ss_out, x2p_out, poolp_out, sp_out, wfi_out, wfo_out,
                  wmain_s, wgab_s, wgk_s, wpp_s, wgp_s, wout_s, stage, zr_stage, sem, zr_sem,
                  ext_ref, lvl_ref, s_ref, meta_tail_ref, meta_s_ref,
                  xs_ref, us_ref, pooled_ref, qm_ref, kdx_ref, rhs_ref, oi_ref, sbuf_in, sbuf_out, in_sem, out_sem):
    i = pl.program_id(0)
    w = _Weights(gmix_ref, bgk_ref, pscale_ref, gnorm_ref, wpg_ref,
                 wmain_s, wgab_s, wgk_s, wpp_s, wgp_s, wout_s)

    @pl.when(i == 0)
    def _():
        _stage_weights(wint_hbm, wpp_hbm, wgp_hbm, wout_hbm, wgk_ref, w, stage, zr_stage, sem, zr_sem)

    wfi_bf = wfi_in[0].astype(BF16)
    for kc in range(wfi_out.shape[0]):
        cols = slice(kc * FFN_CHUNK, (kc + 1) * FFN_CHUNK)
        wfi_out[kc] = jnp.concatenate([wfi_bf[:, cols], wfi_bf[:, D_FF + kc * FFN_CHUNK:D_FF + (kc + 1) * FFN_CHUNK]],
                                      axis=1)
    wfo_out[...] = wfo_in[0].astype(BF16)

    @pl.when(i < n_s)
    def _():
        _sample_block(i * xs_in.shape[0], xs_in, pool_in, s_in, w, x2s_out, pools_out, ss_out,
                      xs_ref, us_ref, pooled_ref, qm_ref, kdx_ref, rhs_ref, oi_ref,
                      sbuf_in, sbuf_out, in_sem, out_sem)

    @pl.when(i >= n_s)
    def _():
        t_idx = (i - n_s) % n_t
        _prompt_tile(i == n_s, (i - n_s) // n_t, t_idx, n_t, xp_in, meta_ref, w, x2p_out, poolp_out, sp_out,
                     ext_ref, lvl_ref, s_ref, meta_tail_ref, meta_s_ref)


def _ffn_weight_copies(wi_hbm, wo_hbm, wi_s, wo_s, sem):
    copies = []
    for c in range(D_FF // FFN_CHUNK):
        rows = pl.ds(c * FFN_CHUNK, FFN_CHUNK)
        copies.append((
            pltpu.make_async_copy(wi_hbm.at[c], wi_s.at[c], sem.at[2 * c]),
            pltpu.make_async_copy(wo_hbm.at[rows, :], wo_s.at[rows, :], sem.at[2 * c + 1])))
    return copies


def _ffn_tile(x, gffn_ref, wi_s, wo_s, gfin_ref, before_chunk=None):
    n_chunks = D_FF // FFN_CHUNK
    h, r = _rms_split(x, gffn_ref[...])
    acc = x
    group = []
    for c in range(n_chunks):
        if before_chunk is not None:
            before_chunk(c)
        gate_up = _dot(h, wi_s[c]) * r
        group.append((_silu(gate_up[:, :FFN_CHUNK]) * gate_up[:, FFN_CHUNK:]).astype(BF16))
        if len(group) == FFN_OUT_GROUP or c + 1 == n_chunks:
            first = (c + 1 - len(group)) * FFN_CHUNK
            acc = acc + _dot(jnp.concatenate(group, axis=1), wo_s[first:(c + 1) * FFN_CHUNK, :])
            group = []
    return _rms(acc, gfin_ref[...])


def _ffn_kernel(seq, xp_ref, xs_ref, gffn_ref, wi_hbm, wo_hbm, gfin_ref, yp_ref, ys_ref, wi_s, wo_s, sem):
    i = pl.program_id(0)

    @pl.when(i == 0)
    def _():
        copies = _ffn_weight_copies(wi_hbm, wo_hbm, wi_s, wo_s, sem)
        for chunk_copies in copies:
            for cp in chunk_copies:
                cp.start()

        def wait_chunk(c):
            for cp in copies[c]:
                cp.wait()

        y = _ffn_tile(xs_ref[...], gffn_ref, wi_s, wo_s, gfin_ref, before_chunk=wait_chunk)
        for bi in range(ys_ref.shape[0]):
            ys_ref[bi] = y[bi * seq:(bi + 1) * seq, :]

    @pl.when(i > 0)
    def _():
        yp_ref[0] = _ffn_tile(xp_ref[0], gffn_ref, wi_s, wo_s, gfin_ref)


def _const_spec(shape):
    zeros = (0,) * len(shape)
    return pl.BlockSpec(shape, lambda *_: zeros, pipeline_mode=pl.Buffered(1))


def kernel(x_prompt, x_sample, state_pool, state_gla, meta_tokens, g_mix, w_in, w_gk_up, b_gk, w_pool_group,
           pool_scale, w_pool_proj, g_gla_norm, w_gla_proj, w_out, g_ffn, w_ffn_in, w_ffn_out, g_final):
    depth = w_in.shape[0]
    assert depth == 1, "single-layer trunk only"
    bp, tp, d = x_prompt.shape
    bs, ts, _ = x_sample.shape
    nbb = SAMPLE_BATCH_BLOCK
    assert d == D_MODEL and w_in.shape == (1, D_MODEL, IN_DIM) and meta_tokens.shape == (N_META, D_MODEL)
    assert tp % PROMPT_TILE == 0 and tp % FFN_TILE == 0 and PROMPT_TILE % GLA_CHUNK == 0
    assert bs % nbb == 0 and ts + 2 <= SAMPLE_ROWS and bs * ts == FFN_TILE
    assert nbb % STATE_GROUP == 0
    assert (nbb * SAMPLE_ROWS) % SAMPLE_ATT_ROWS == 0 and SAMPLE_ATT_ROWS % SAMPLE_ROWS == 0
    n_t = tp // PROMPT_TILE
    n_p = bp * n_t
    n_s = bs // nbb
    n_fi = D_MODEL // FFN_IN_CAST_ROWS
    n_fo = D_FF // FFN_OUT_CAST_ROWS
    assert n_fi <= n_p and n_fo <= n_p

    def s_idx(i):
        return jnp.minimum(i, n_s - 1)

    def p_idx(i):
        return jnp.maximum(i - n_s, 0)

    w_in_t = jnp.transpose(w_in, (0, 2, 1))
    pool_hist = jnp.transpose(state_pool, (0, 2, 1, 3))
    small = (meta_tokens, g_mix, b_gk, pool_scale, g_gla_norm, w_gk_up, w_pool_group)
    hbm = pl.BlockSpec(memory_space=pl.ANY)
    in_specs = (
        [pl.BlockSpec((nbb, ts, d), lambda i: (s_idx(i), 0, 0)),
         pl.BlockSpec((1, POOL_BUF, nbb, POOL_WIDTH), lambda i: (0, 0, s_idx(i), 0)),
         hbm,
         pl.BlockSpec((1, PROMPT_TILE, d), lambda i: (p_idx(i) // n_t, p_idx(i) % n_t, 0))]
        + [_const_spec(a.shape) for a in small]
        + [hbm, hbm, hbm, hbm,
           pl.BlockSpec((1, FFN_IN_CAST_ROWS, 2 * D_FF), lambda i: (0, jnp.minimum(p_idx(i), n_fi - 1), 0)),
           pl.BlockSpec((1, FFN_OUT_CAST_ROWS, d), lambda i: (0, jnp.minimum(p_idx(i), n_fo - 1), 0))])
    out_specs = [
        pl.BlockSpec((nbb * ts, d), lambda i: (s_idx(i), 0)),
        pl.BlockSpec((1, POOL_BUF, nbb, POOL_WIDTH), lambda i: (0, 0, s_idx(i), 0)),
        hbm,
        pl.BlockSpec((1, PROMPT_TILE, d), lambda i: (p_idx(i) // n_t, p_idx(i) % n_t, 0)),
        pl.BlockSpec((1, POOL_BUF, bp, POOL_WIDTH), lambda i: (0, 0, 0, 0)),
        pl.BlockSpec((1, 1, GLA_HEADS, GLA_DK, GLA_DV), lambda i: (0, p_idx(i) // n_t, 0, 0, 0)),
        pl.BlockSpec((D_FF // FFN_CHUNK, FFN_IN_CAST_ROWS, 2 * FFN_CHUNK),
                     lambda i: (0, jnp.minimum(p_idx(i), n_fi - 1), 0)),
        pl.BlockSpec((FFN_OUT_CAST_ROWS, d), lambda i: (jnp.minimum(p_idx(i), n_fo - 1), 0))]
    out_shape = [
        jax.ShapeDtypeStruct((bs * ts, d), F32),
        jax.ShapeDtypeStruct((1, POOL_BUF, bs, POOL_WIDTH), F32),
        jax.ShapeDtypeStruct(state_gla.shape, F32),
        jax.ShapeDtypeStruct(x_prompt.shape, F32),
        jax.ShapeDtypeStruct((1, POOL_BUF, bp, POOL_WIDTH), F32),
        jax.ShapeDtypeStruct((1, bp, GLA_HEADS, GLA_DK, GLA_DV), F32),
        jax.ShapeDtypeStruct((D_FF // FFN_CHUNK, D_MODEL, 2 * FFN_CHUNK), BF16),
        jax.ShapeDtypeStruct((D_FF, D_MODEL), BF16)]
    scratch_shapes = [
        pltpu.VMEM((D_MODEL, MAIN_COLS), BF16),
        pltpu.VMEM((D_MODEL, 2 * D_MODEL), BF16), pltpu.VMEM((LANES, GLA_KW), BF16),
        pltpu.VMEM((POOL_WIDTH, D_MODEL), BF16), pltpu.VMEM((GLA_VW, D_MODEL), BF16),
        pltpu.VMEM((D_MODEL, D_MODEL), BF16),
        pltpu.VMEM((STAGE_SLOTS, STAGE_ROWS, D_MODEL), F32), pltpu.VMEM((LANES, D_MODEL), F32),
        pltpu.SemaphoreType.DMA((STAGE_SLOTS,)), pltpu.SemaphoreType.DMA((1,)),
        pltpu.VMEM((POOL_PAD + TAIL_ROWS + PROMPT_TILE, POOL_WIDTH), F32),
        pltpu.VMEM((len(POOL_WINDOWS) - 1, POOL_PAD + TAIL_ROWS + PROMPT_TILE, POOL_WIDTH), F32),
        pltpu.VMEM((GLA_HEADS, GLA_DK, GLA_DV), F32),
        pltpu.VMEM((N_META, POOL_WIDTH), F32),
        pltpu.VMEM((GLA_HEADS, GLA_DK, GLA_DV), F32),
        pltpu.VMEM((nbb, SAMPLE_ROWS, D_MODEL), F32),
        pltpu.VMEM((len(POOL_WINDOWS), nbb * SAMPLE_ROWS, POOL_GROUP_DIM), F32),
        pltpu.VMEM((len(POOL_WINDOWS), nbb * SAMPLE_ROWS, POOL_GROUP_DIM), F32),
        pltpu.VMEM((nbb, GLA_HEADS * SAMPLE_ROWS, GLA_KW), BF16),
        pltpu.VMEM((nbb, SAMPLE_ROWS, GLA_KW), F32),
        pltpu.VMEM((nbb, SAMPLE_ROWS, 2 * GLA_VW), F32),
        pltpu.VMEM((nbb, GLA_HEADS * SAMPLE_ROWS, GLA_DV), F32),
        pltpu.VMEM((nbb // STATE_GROUP, STATE_GROUP, GLA_HEADS, GLA_DK, GLA_DV), F32),
        pltpu.VMEM((nbb // STATE_GROUP, STATE_GROUP, GLA_HEADS, GLA_DK, GLA_DV), F32),
        pltpu.SemaphoreType.DMA((nbb // STATE_GROUP,)), pltpu.SemaphoreType.DMA((nbb // STATE_GROUP,))]

    def mixer(*refs):
        _mixer_kernel(n_s, n_t, *refs)

    x2_s, pool_s, gla_s, x2_p, pool_p, gla_p, wfi_bf, wfo_bf = pl.pallas_call(
        mixer,
        grid=(n_s + n_p,),
        in_specs=in_specs,
        out_specs=out_specs,
        out_shape=out_shape,
        scratch_shapes=scratch_shapes,
        compiler_params=pltpu.CompilerParams(dimension_semantics=("arbitrary",), vmem_limit_bytes=VMEM_LIMIT),
        name="mixer",
    )(x_sample, pool_hist, state_gla, x_prompt, *small, w_in_t, w_pool_proj, w_gla_proj, w_out, w_ffn_in, w_ffn_out)

    def ffn(*refs):
        _ffn_kernel(ts, *refs)

    n_tf = tp // FFN_TILE

    def prompt_tile_index(i):
        tile_id = jnp.maximum(i - 1, 0)
        return (tile_id // n_tf, tile_id % n_tf, 0)

    ffn_args = (g_ffn, wfi_bf, wfo_bf, g_final.reshape(1, D_MODEL))
    y_prompt, y_sample = pl.pallas_call(
        ffn,
        grid=(bp * n_tf + 1,),
        in_specs=[pl.BlockSpec((1, FFN_TILE, d), prompt_tile_index),
                  _const_spec(x2_s.shape), _const_spec(g_ffn.shape), hbm, hbm, _const_spec((1, D_MODEL))],
        out_specs=[pl.BlockSpec((1, FFN_TILE, d), prompt_tile_index),
                   pl.BlockSpec(x_sample.shape, lambda i: (0, 0, 0))],
        out_shape=[jax.ShapeDtypeStruct(x_prompt.shape, F32), jax.ShapeDtypeStruct(x_sample.shape, F32)],
        scratch_shapes=[pltpu.VMEM(wfi_bf.shape, BF16), pltpu.VMEM(wfo_bf.shape, BF16),
                        pltpu.SemaphoreType.DMA((2 * (D_FF // FFN_CHUNK),))],
        compiler_params=pltpu.CompilerParams(dimension_semantics=("arbitrary",), vmem_limit_bytes=VMEM_LIMIT),
        name="ffn",
    )(x2_p, x2_s, *ffn_args)
    pool_p = jnp.transpose(pool_p, (0, 2, 1, 3))
    pool_s = jnp.transpose(pool_s, (0, 2, 1, 3))
    return y_prompt, y_sample, pool_p, gla_p, pool_s, gla_s
```

```python
import jax
import jax.numpy as jnp
from jax import lax
from jax.experimental import pallas as pl
from jax.experimental.pallas import tpu as pltpu

F32 = jnp.float32
BF16 = jnp.bfloat16

D_MODEL = 1024
N_META = 16
POOL_WIDTH = 512
POOL_WINDOWS = (2, 4, 8, 16)
POOL_GROUP_DIM = 128
POOL_BUF = 15
GLA_HEADS = 4
GLA_DV = 128
GLA_DK = 64
GLA_KW = GLA_HEADS * GLA_DK
GLA_VW = GLA_HEADS * GLA_DV
GLA_GATE_RANK = 16
GLA_TAU = 16.0
GLA_CHUNK = 64
D_FF = 2816
EPS = 1e-6

LANES = 128
SUBLANES = 8
MAIN_W = POOL_WIDTH + 2 * GLA_KW + 2 * GLA_VW
U_COL, VOG_COL, QK_COL, ZR_COL = 0, POOL_WIDTH, POOL_WIDTH + 2 * GLA_VW, MAIN_W
MAIN_COLS = MAIN_W + LANES
GAB_LO = MAIN_W + GLA_GATE_RANK
IN_DIM = GAB_LO + 2 * D_MODEL
TAIL_ROWS = 16
POOL_PAD = SUBLANES

PROMPT_TILE = 512
FFN_TILE = 512
FFN_CHUNK = 256
FFN_OUT_GROUP = 4
FFN_TAIL_ROWS = 256
SAMPLE_BATCH_BLOCK = 16
SAMPLE_ROWS = SUBLANES
SAMPLE_GATE_SLICES = 8
SAMPLE_EARLY_GATE_SLICES = 2
STAGE_ROWS = 256
STAGE_SLOTS = 8
FFN_IN_CAST_ROWS = 32
FFN_OUT_CAST_ROWS = 128
V7X_VMEM_BYTES = 64 * 1024 * 1024
VMEM_RESERVE = 4 * 1024 * 1024
VMEM_LIMIT = V7X_VMEM_BYTES - VMEM_RESERVE


def _dot(a, b):
    return jnp.dot(a, b, preferred_element_type=F32)


def _dot_nt(a, b):
    return lax.dot_general(a, b, (((1,), (1,)), ((), ())), preferred_element_type=F32)


def _dot_tn(a, b):
    return lax.dot_general(a, b, (((0,), (0,)), ((), ())), preferred_element_type=F32)


def _rms(x, g):
    return x * lax.rsqrt(jnp.mean(x * x, axis=-1, keepdims=True) + EPS) * g


def _rms_split(x, g):
    r = lax.rsqrt(jnp.mean(x * x, axis=-1, keepdims=True) + EPS)
    return (x * g).astype(BF16), r


def _sigmoid(x):
    return 0.5 * jnp.tanh(0.5 * x) + 0.5


def _silu(x):
    half = 0.5 * x
    return half * jnp.tanh(half) + half


def _log_sigmoid(x):
    return jnp.minimum(x, 0.0) - jnp.log(1.0 + jnp.exp(-jnp.abs(x)))


def _split_bf16(x):
    hi = x.astype(BF16)
    lo = (x - hi.astype(F32)).astype(BF16)
    return hi, lo


class _Weights:
    def __init__(self, gmix, bgk, pscale, gnorm, wpg, wmain, wgab, wgk, wpp, wgp, wout):
        self.gmix, self.bgk, self.pscale, self.gnorm, self.wpg = gmix, bgk, pscale, gnorm, wpg
        self.wmain, self.wgab, self.wgk = wmain, wgab, wgk
        self.wpp, self.wgp, self.wout = wpp, wgp, wout


def _in_proj(x, w):
    h, r = _rms_split(x, w.gmix[...])
    qkz = _dot(h, w.wmain[:, QK_COL:MAIN_COLS])
    q = qkz[:, :GLA_KW] * (r * (GLA_DK ** -0.5))
    k = qkz[:, GLA_KW:2 * GLA_KW] * r
    zr = qkz[:, 2 * GLA_KW:] * r
    z = _dot(zr.astype(BF16), w.wgk[...]) + w.bgk[...]
    u = _dot(h, w.wmain[:, U_COL:VOG_COL]) * r
    vog = _dot(h, w.wmain[:, VOG_COL:QK_COL]) * r
    v = vog[:, :GLA_VW]
    og = vog[:, GLA_VW:]
    return (h, r), u, q, k, v, og, z


def _gate_proj(xn, w, lo, hi):
    h, r = xn
    return _dot(h, w.wgab[:, lo:hi]) * r


def _chunk_cumsum_wide(g, chunk):
    n = g.shape[0] // chunk
    r = lax.broadcasted_iota(jnp.int32, (chunk, chunk), 0)
    c = lax.broadcasted_iota(jnp.int32, (chunk, chunk), 1)
    tri = jnp.where(c <= r, 1.0, 0.0).astype(BF16)
    hi, lo = _split_bf16(jnp.concatenate([g[j * chunk:(j + 1) * chunk] for j in range(n)], axis=1))
    wide = _dot(tri, hi) + _dot(tri, lo)
    width = g.shape[1]
    parts = [wide[:, j * width:(j + 1) * width] for j in range(n)]
    return jnp.concatenate(parts, axis=0), [p[chunk - 1:chunk, :] for p in parts]


def _chunk_cumsum(g, chunk):
    m = g.shape[0]
    r = lax.broadcasted_iota(jnp.int32, (m, m), 0)
    c = lax.broadcasted_iota(jnp.int32, (m, m), 1)
    tri = jnp.where((r // chunk == c // chunk) & (c <= r), 1.0, 0.0).astype(BF16)
    hi, lo = _split_bf16(g)
    return _dot(tri, hi) + _dot(tri, lo)


def _head_lane_mask(width, per_head):
    lane = lax.broadcasted_iota(jnp.int32, (1, width), 1)
    return [(lane // per_head) == h for h in range(GLA_HEADS)]


def _block_diag_rows(x_bf, per_head):
    r = x_bf.shape[0]
    zero = jnp.zeros((r, per_head), x_bf.dtype)
    rows = []
    for h in range(GLA_HEADS):
        rows.append(jnp.concatenate(
            [x_bf[:, h * per_head:(h + 1) * per_head] if hh == h else zero for hh in range(GLA_HEADS)], axis=1))
    return jnp.concatenate(rows, axis=0)


def _gla_post(o, og, w):
    parts = []
    for h in range(GLA_HEADS):
        oh = o[:, h * GLA_DV:(h + 1) * GLA_DV]
        parts.append(oh * lax.rsqrt(jnp.mean(oh * oh, axis=-1, keepdims=True) + EPS) * w.gnorm[...])
    on = jnp.concatenate(parts, axis=1)
    on = on * _silu(og)
    return _dot(on.astype(BF16), w.wgp[...])


def _pool_post(pooled, w):
    pb = pooled.astype(BF16)
    mixed = []
    for p in range(len(POOL_WINDOWS) // 2):
        w_pair = _pair_block_diag(w.wpg[0, 2 * p].astype(BF16), w.wpg[0, 2 * p + 1].astype(BF16))
        mixed.append(_dot(pb[:, 2 * p * POOL_GROUP_DIM:(2 * p + 2) * POOL_GROUP_DIM], w_pair))
    mixed = jnp.concatenate(mixed, axis=1)
    return _dot((mixed * w.pscale[...]).astype(BF16), w.wpp[...])


def _merge(x, y_a, y_b, sa, sb, w):
    merged = sa * y_a + sb * y_b
    return x + _dot(merged.astype(BF16), w.wout[...])


def _decay_columns(decay_row):
    return jnp.transpose(jnp.broadcast_to(decay_row, (LANES, decay_row.shape[1])))


def _pair_block_diag(a, b):
    zero = jnp.zeros(a.shape, a.dtype)
    return jnp.concatenate([jnp.concatenate([a, zero], axis=1), jnp.concatenate([zero, b], axis=1)], axis=0)


def _state_update(s_heads, kd_bf, v_bf, decay_row):
    dcol = _decay_columns(decay_row)
    out = []
    for p in range(GLA_HEADS // 2):
        upd = _dot_tn(kd_bf[:, 2 * p * GLA_DK:(2 * p + 2) * GLA_DK], v_bf[:, 2 * p * GLA_DV:(2 * p + 2) * GLA_DV])
        for j in range(2):
            h = 2 * p + j
            rows = slice(h * GLA_DK, (h + 1) * GLA_DK)
            out.append(dcol[rows] * s_heads[h] + upd[j * GLA_DK:(j + 1) * GLA_DK, j * GLA_DV:(j + 1) * GLA_DV])
    return out


def _stage_weights(wint_hbm, wpp_hbm, wgp_hbm, wout_hbm, wgk_ref, w, stage, zr_stage, sem, zr_sem):
    plan = []
    for r in range(0, MAIN_W, STAGE_ROWS):
        if r < POOL_WIDTH:
            col = U_COL + r
        elif r < POOL_WIDTH + 2 * GLA_KW:
            col = QK_COL + r - POOL_WIDTH
        else:
            col = VOG_COL + r - (POOL_WIDTH + 2 * GLA_KW)
        plan.append((wint_hbm, r, w.wmain, col, True))
    for r in range(0, 2 * D_MODEL, STAGE_ROWS):
        plan.append((wint_hbm, GAB_LO + r, w.wgab, r, True))
    for src, dst, n_rows in ((wpp_hbm, w.wpp, POOL_WIDTH), (wgp_hbm, w.wgp, GLA_VW), (wout_hbm, w.wout, D_MODEL)):
        for r in range(0, n_rows, STAGE_ROWS):
            plan.append((src, r, dst, r, False))

    def copy(j):
        src, r0 = plan[j][0], plan[j][1]
        return pltpu.make_async_copy(src.at[0, pl.ds(r0, STAGE_ROWS), :], stage.at[j % STAGE_SLOTS], sem.at[j % STAGE_SLOTS])

    def zr_copy():
        return pltpu.make_async_copy(
            wint_hbm.at[0, pl.ds(MAIN_W, GLA_GATE_RANK), :], zr_stage.at[pl.ds(0, GLA_GATE_RANK), :], zr_sem.at[0])

    for j in range(STAGE_SLOTS):
        copy(j).start()
    zr_copy().start()

    w.wgk[...] = jnp.zeros(w.wgk.shape, BF16)
    w.wgk[0:GLA_GATE_RANK, :] = wgk_ref[0].astype(BF16)
    zr_stage[GLA_GATE_RANK:, :] = jnp.zeros((LANES - GLA_GATE_RANK, D_MODEL), F32)

    for j in range(len(plan)):
        copy(j).wait()
        _, _, dst, d0, transposed = plan[j]
        slab = stage[j % STAGE_SLOTS]
        if transposed:
            dst[:, d0:d0 + STAGE_ROWS] = jnp.transpose(slab.astype(BF16))
        else:
            dst[d0:d0 + STAGE_ROWS, :] = slab.astype(BF16)
        if j + STAGE_SLOTS < len(plan):
            copy(j + STAGE_SLOTS).start()
    zr_copy().wait()
    w.wmain[:, ZR_COL:MAIN_COLS] = jnp.transpose(zr_stage[...]).astype(BF16)


def _prompt_tile(b_first, b_idx, t_idx, n_t, x_ref, meta_ref, w, x2_ref, pbuf_ref, sout_ref,
                 ext_ref, lvl_ref, s_ref, meta_tail_ref, meta_s_ref):
    tile = x_ref.shape[1]
    n_chunks = tile // GLA_CHUNK

    @pl.when(b_first)
    def _():
        _, u, _, k, v, _, z = _in_proj(meta_ref[...], w)
        meta_tail_ref[...] = u
        g = _log_sigmoid(z) * (1.0 / GLA_TAU)
        b = _chunk_cumsum(g, N_META)
        b_last = b[N_META - 1:N_META, :]
        kd = k * jnp.exp(b_last - b)
        zero_s = [jnp.zeros((GLA_DK, GLA_DV), F32)] * GLA_HEADS
        s_new = _state_update(zero_s, kd.astype(BF16), v.astype(BF16), jnp.exp(b_last))
        for hd in range(GLA_HEADS):
            meta_s_ref[hd] = s_new[hd]

    @pl.when(t_idx == 0)
    def _():
        ext_ref[0:POOL_PAD, :] = jnp.zeros((POOL_PAD, POOL_WIDTH), F32)
        lvl_ref[:, 0:POOL_PAD, :] = jnp.zeros((lvl_ref.shape[0], POOL_PAD, POOL_WIDTH), F32)
        ext_ref[POOL_PAD:POOL_PAD + TAIL_ROWS, :] = meta_tail_ref[...]
        s_ref[...] = meta_s_ref[...]

    x = x_ref[0]
    xn = _rms_split(x, w.gmix[...])
    h, r = xn
    gate_cols = 2 * D_MODEL // n_chunks
    gate_parts = [None] * n_chunks

    def gate_slice(c):
        gate_parts[c] = _sigmoid(_gate_proj(xn, w, c * gate_cols, (c + 1) * gate_cols))

    qkz = _dot(h, w.wmain[:, QK_COL:MAIN_COLS])
    u = _dot(h, w.wmain[:, U_COL:VOG_COL]) * r
    v = _dot(h, w.wmain[:, VOG_COL:VOG_COL + GLA_VW]) * r
    q = qkz[:, :GLA_KW] * (r * (GLA_DK ** -0.5))
    k = qkz[:, GLA_KW:2 * GLA_KW] * r
    zr = qkz[:, 2 * GLA_KW:] * r
    z = _dot(zr.astype(BF16), w.wgk[...]) + w.bgk[...]
    gate_slice(0)
    g = _log_sigmoid(z) * (1.0 / GLA_TAU)
    b, b_last_rows = _chunk_cumsum_wide(g, GLA_CHUNK)
    gate_slice(1)
    og_parts = []

    base = POOL_PAD + TAIL_ROWS
    span = TAIL_ROWS + tile
    ext_ref[base:base + tile, :] = u
    cur = ext_ref[POOL_PAD:POOL_PAD + span, :]
    pooled = []
    for gi, win in enumerate(POOL_WINDOWS):
        shift = win // 2
        lo = gi * POOL_GROUP_DIM
        prev_ref = ext_ref if gi == 0 else lvl_ref.at[gi - 1]
        cur = cur[:, (POOL_GROUP_DIM if gi else 0):] + prev_ref[POOL_PAD - shift:POOL_PAD - shift + span, lo:]
        pooled.append(cur[TAIL_ROWS:, 0:POOL_GROUP_DIM] * (1.0 / win) - u[:, lo:lo + POOL_GROUP_DIM])
        if gi + 1 < len(POOL_WINDOWS):
            lvl_ref[gi, POOL_PAD:POOL_PAD + span, lo:] = cur
    y_a = _pool_post(jnp.concatenate(pooled, axis=1), w)
    ext_ref[POOL_PAD:base, :] = ext_ref[POOL_PAD + tile:base + tile, :]

    b_last = jnp.concatenate([jnp.broadcast_to(r, (GLA_CHUNK, GLA_KW)) for r in b_last_rows], axis=0)
    qe = (q * jnp.exp(b)).astype(BF16)
    ke = k * jnp.exp(-b)
    kd = (k * jnp.exp(b_last - b)).astype(BF16)
    v_bf = v.astype(BF16)

    k_masks = _head_lane_mask(GLA_KW, GLA_DK)
    row_i = lax.broadcasted_iota(jnp.int32, (GLA_CHUNK, GLA_KW), 0)
    col_j = lax.broadcasted_iota(jnp.int32, (GLA_CHUNK, GLA_KW), 1) % GLA_CHUNK
    causal = col_j <= row_i

    s_heads = [s_ref[hd] for hd in range(GLA_HEADS)]
    o_chunks = []
    for c in range(n_chunks):
        if c + 2 < n_chunks:
            gate_slice(c + 2)
        else:
            og_lo = VOG_COL + GLA_VW + len(og_parts) * (GLA_VW // 2)
            og_parts.append(_dot(h, w.wmain[:, og_lo:og_lo + GLA_VW // 2]) * r)
        rows = slice(c * GLA_CHUNK, (c + 1) * GLA_CHUNK)
        ke_c = ke[rows]
        ke_bd = jnp.concatenate([jnp.where(k_masks[hd], ke_c, 0.0) for hd in range(GLA_HEADS)], axis=0).astype(BF16)
        att = jnp.where(causal, _dot_nt(qe[rows], ke_bd), 0.0).astype(BF16)
        o_pairs = []
        for p in range(GLA_HEADS // 2):
            h0, h1 = 2 * p, 2 * p + 1
            lanes_k = slice(h0 * GLA_DK, (h1 + 1) * GLA_DK)
            lanes_j = slice(h0 * GLA_CHUNK, (h1 + 1) * GLA_CHUNK)
            v0 = v_bf[rows, h0 * GLA_DV:(h0 + 1) * GLA_DV]
            v1 = v_bf[rows, h1 * GLA_DV:(h1 + 1) * GLA_DV]
            rhs = jnp.concatenate([_pair_block_diag(s_heads[h0].astype(BF16), s_heads[h1].astype(BF16)),
                                   _pair_block_diag(v0, v1)], axis=0)
            o_pairs.append(_dot(jnp.concatenate([qe[rows, lanes_k], att[:, lanes_j]], axis=1), rhs))
        o_chunks.append(jnp.concatenate(o_pairs, axis=1))
        s_heads = _state_update(s_heads, kd[rows], v_bf[rows], jnp.exp(b_last_rows[c]))
    for hd in range(GLA_HEADS):
        s_ref[hd] = s_heads[hd]

    y_b = _gla_post(jnp.concatenate(o_chunks, axis=0), jnp.concatenate(og_parts, axis=1), w)
    sg = jnp.concatenate(gate_parts, axis=1)
    x2_ref[0] = _merge(x, y_a, y_b, sg[:, :D_MODEL], sg[:, D_MODEL:], w)

    @pl.when(t_idx == n_t - 1)
    def _():
        sout_ref[0, 0] = s_ref[...]

    for bb in range(pbuf_ref.shape[2]):
        @pl.when((t_idx == n_t - 1) & (b_idx == bb))
        def _():
            for r in range(POOL_BUF):
                row = base - POOL_BUF + r
                pbuf_ref[0, r, bb:bb + 1, :] = ext_ref[row:row + 1, :]


def _sample_block(x_ref, pool_ref, sin_ref, w, x2_ref, pbuf_ref, sout_ref,
                  xs_ref, us_ref, pooled_ref, qm_ref, kdx_ref, rhs_ref, oi_ref):
    nb, seq, _ = x_ref.shape
    rows_pb = SAMPLE_ROWS
    m = nb * rows_pb

    xs_ref[:, seq:, :] = jnp.zeros((nb, rows_pb - seq, D_MODEL), F32)
    xs_ref[:, 0:seq, :] = x_ref[...]
    x = xs_ref[...].reshape(m, D_MODEL)
    xn, u, q, k, v, og, z = _in_proj(x, w)

    gate_cols = 2 * D_MODEL // SAMPLE_GATE_SLICES
    gate_parts = [_sigmoid(_gate_proj(xn, w, c * gate_cols, (c + 1) * gate_cols))
                  for c in range(SAMPLE_EARLY_GATE_SLICES)]

    pooled_ref[...] = jnp.zeros(pooled_ref.shape, F32)
    for gi, win in enumerate(POOL_WINDOWS):
        cols = slice(gi * POOL_GROUP_DIM, (gi + 1) * POOL_GROUP_DIM)
        us_ref[gi] = u[:, cols]
        tok = [us_ref[gi, pl.ds(t, nb, stride=rows_pb), :] for t in range(seq)]
        hist = [pool_ref[0, r, :, cols] for r in range(POOL_BUF)] + tok
        for t in range(seq):
            acc = tok[t]
            for n in range(1, win):
                acc = acc + hist[POOL_BUF + t - n]
            pooled_ref[gi, pl.ds(t, nb, stride=rows_pb), :] = acc * (1.0 / win) - tok[t]
        for r in range(POOL_BUF):
            pbuf_ref[0, r, :, cols] = hist[seq + r]

    r8 = lax.broadcasted_iota(jnp.int32, (m, 1), 0) % rows_pb
    g = jnp.where(r8 < seq, _log_sigmoid(z) * (1.0 / GLA_TAU), 0.0)
    b = _chunk_cumsum(g, rows_pb)
    b3 = b.reshape(nb, rows_pb, GLA_KW)
    b_last = jnp.broadcast_to(b3[:, seq - 1:seq, :], b3.shape).reshape(m, GLA_KW)
    qe = q * jnp.exp(b)
    ke = k * jnp.exp(-b)
    kd = k * jnp.exp(b_last - b)
    decay = jnp.exp(b_last)
    v_bf = v.astype(BF16)

    k_masks = _head_lane_mask(GLA_KW, GLA_DK)
    ke_bd = jnp.concatenate([jnp.where(k_masks[hd], ke, 0.0) for hd in range(GLA_HEADS)], axis=0).astype(BF16)
    row_i = lax.broadcasted_iota(jnp.int32, (m, GLA_HEADS * m), 0)
    col_j = lax.broadcasted_iota(jnp.int32, (m, GLA_HEADS * m), 1) % m
    keep = (row_i // rows_pb == col_j // rows_pb) & (col_j <= row_i)
    att = jnp.where(keep, _dot_nt(qe.astype(BF16), ke_bd), 0.0).astype(BF16)
    o_intra = _dot(att, _block_diag_rows(v_bf, GLA_DV))

    qe3 = qe.reshape(nb, rows_pb, GLA_KW)
    qm_ref[...] = jnp.concatenate([jnp.where(k_masks[hd], qe3, 0.0) for hd in range(GLA_HEADS)], axis=1).astype(BF16)
    d_hi = decay.astype(BF16).astype(F32)
    d_lo = decay - d_hi
    kdx = jnp.where(r8 == seq, d_hi, jnp.where(r8 == seq + 1, d_lo, kd))
    kdx_ref[...] = kdx.reshape(nb, rows_pb, GLA_KW)
    ones_rows = jnp.where((r8 == seq) | (r8 == seq + 1), 1.0, 0.0) + jnp.zeros((m, GLA_DV), F32)
    rhs = jnp.concatenate(
        [piece for hd in range(GLA_HEADS) for piece in (v[:, hd * GLA_DV:(hd + 1) * GLA_DV], ones_rows)], axis=1)
    rhs_ref[...] = rhs.reshape(nb, rows_pb, 2 * GLA_VW)

    def per_batch(i):
        s_all = sin_ref[0, i]
        s_flat = s_all.reshape(GLA_KW, GLA_DV).astype(BF16)
        oi_ref[i] = _dot(qm_ref[i], s_flat)
        kdt = jnp.transpose(kdx_ref[i]).astype(BF16)
        rhs_i = rhs_ref[i].astype(BF16)
        for hd in range(GLA_HEADS):
            r = _dot(kdt[hd * GLA_DK:(hd + 1) * GLA_DK], rhs_i[:, hd * 2 * GLA_DV:(hd + 1) * 2 * GLA_DV])
            sout_ref[0, i, hd] = r[:, GLA_DV:] * s_all[hd] + r[:, :GLA_DV]

    late = SAMPLE_GATE_SLICES - SAMPLE_EARLY_GATE_SLICES
    slice_at = {(j * nb) // late: SAMPLE_EARLY_GATE_SLICES + j for j in range(late)}
    for i in range(nb):
        if i in slice_at:
            c = slice_at[i]
            gate_parts.append(_sigmoid(_gate_proj(xn, w, c * gate_cols, (c + 1) * gate_cols)))
        per_batch(i)

    oi = oi_ref[...]
    o_inter = jnp.concatenate([oi[:, hd * rows_pb:(hd + 1) * rows_pb, :] for hd in range(GLA_HEADS)], axis=2)
    o = o_intra + o_inter.reshape(m, GLA_VW)
    y_a = _pool_post(jnp.concatenate([pooled_ref[gi] for gi in range(len(POOL_WINDOWS))], axis=1), w)
    y_b = _gla_post(o, og, w)
    sg = jnp.concatenate(gate_parts, axis=1)
    x2 = _merge(x, y_a, y_b, sg[:, :D_MODEL], sg[:, D_MODEL:], w).reshape(nb, rows_pb, D_MODEL)
    for bi in range(nb):
        x2_ref[bi * seq:(bi + 1) * seq, :] = x2[bi, 0:seq, :]


def _mixer_kernel(n_s, n_t,
                  xs_in, pool_in, s_in, xp_in, meta_ref, gmix_ref, bgk_ref, pscale_ref, gnorm_ref, wgk_ref, wpg_ref,
                  wint_hbm, wpp_hbm, wgp_hbm, wout_hbm, wfi_in, wfo_in,
                  x2s_out, pools_out, ss_out, x2p_out, poolp_out, sp_out, wfi_out, wfo_out,
                  wmain_s, wgab_s, wgk_s, wpp_s, wgp_s, wout_s, stage, zr_stage, sem, zr_sem,
                  ext_ref, lvl_ref, s_ref, meta_tail_ref, meta_s_ref,
                  xs_ref, us_ref, pooled_ref, qm_ref, kdx_ref, rhs_ref, oi_ref):
    i = pl.program_id(0)
    w = _Weights(gmix_ref, bgk_ref, pscale_ref, gnorm_ref, wpg_ref,
                 wmain_s, wgab_s, wgk_s, wpp_s, wgp_s, wout_s)

    @pl.when(i == 0)
    def _():
        _stage_weights(wint_hbm, wpp_hbm, wgp_hbm, wout_hbm, wgk_ref, w, stage, zr_stage, sem, zr_sem)

    wfi_bf = wfi_in[0].astype(BF16)
    for kc in range(wfi_out.shape[0]):
        cols = slice(kc * FFN_CHUNK, (kc + 1) * FFN_CHUNK)
        wfi_out[kc] = jnp.concatenate([wfi_bf[:, cols], wfi_bf[:, D_FF + kc * FFN_CHUNK:D_FF + (kc + 1) * FFN_CHUNK]],
                                      axis=1)
    wfo_out[...] = wfo_in[0].astype(BF16)

    @pl.when(i < n_s)
    def _():
        _sample_block(xs_in, pool_in, s_in, w, x2s_out, pools_out, ss_out,
                      xs_ref, us_ref, pooled_ref, qm_ref, kdx_ref, rhs_ref, oi_ref)

    @pl.when(i >= n_s)
    def _():
        t_idx = (i - n_s) % n_t
        _prompt_tile(i == n_s, (i - n_s) // n_t, t_idx, n_t, xp_in, meta_ref, w, x2p_out, poolp_out, sp_out,
                     ext_ref, lvl_ref, s_ref, meta_tail_ref, meta_s_ref)


def _ffn_weight_copies(wi_hbm, wo_hbm, wi_s, wo_s, sem):
    copies = []
    for c in range(D_FF // FFN_CHUNK):
        rows = pl.ds(c * FFN_CHUNK, FFN_CHUNK)
        copies.append((
            pltpu.make_async_copy(wi_hbm.at[c], wi_s.at[c], sem.at[2 * c]),
            pltpu.make_async_copy(wo_hbm.at[rows, :], wo_s.at[rows, :], sem.at[2 * c + 1])))
    return copies


def _ffn_tile(x, gffn_ref, wi_s, wo_s, gfin_ref, before_chunk=None):
    n_chunks = D_FF // FFN_CHUNK
    h, r = _rms_split(x, gffn_ref[...])
    acc = x
    group = []
    for c in range(n_chunks):
        if before_chunk is not None:
            before_chunk(c)
        gate_up = _dot(h, wi_s[c]) * r
        group.append((_silu(gate_up[:, :FFN_CHUNK]) * gate_up[:, FFN_CHUNK:]).astype(BF16))
        if c + 1 == n_chunks:
            first = (c + 1 - len(group)) * FFN_CHUNK
            act = jnp.concatenate(group, axis=1)
            out = []
            for r0 in range(0, x.shape[0], FFN_TAIL_ROWS):
                rows = slice(r0, r0 + FFN_TAIL_ROWS)
                out.append(_rms(acc[rows] + _dot(act[rows], wo_s[first:(c + 1) * FFN_CHUNK, :]), gfin_ref[...]))
            return jnp.concatenate(out, axis=0)
        if len(group) == FFN_OUT_GROUP:
            first = (c + 1 - len(group)) * FFN_CHUNK
            acc = acc + _dot(jnp.concatenate(group, axis=1), wo_s[first:(c + 1) * FFN_CHUNK, :])
            group = []


def _ffn_kernel(seq, xp_ref, xs_ref, gffn_ref, wi_hbm, wo_hbm, gfin_ref, yp_ref, ys_ref, wi_s, wo_s, sem):
    i = pl.program_id(0)

    @pl.when(i == 0)
    def _():
        copies = _ffn_weight_copies(wi_hbm, wo_hbm, wi_s, wo_s, sem)
        for chunk_copies in copies:
            for cp in chunk_copies:
                cp.start()

        def wait_chunk(c):
            for cp in copies[c]:
                cp.wait()

        y = _ffn_tile(xs_ref[...], gffn_ref, wi_s, wo_s, gfin_ref, before_chunk=wait_chunk)
        for bi in range(ys_ref.shape[0]):
            ys_ref[bi] = y[bi * seq:(bi + 1) * seq, :]

    @pl.when(i > 0)
    def _():
        yp_ref[0] = _ffn_tile(xp_ref[0], gffn_ref, wi_s, wo_s, gfin_ref)


def _const_spec(shape):
    zeros = (0,) * len(shape)
    return pl.BlockSpec(shape, lambda *_: zeros, pipeline_mode=pl.Buffered(1))


def kernel(x_prompt, x_sample, state_pool, state_gla, meta_tokens, g_mix, w_in, w_gk_up, b_gk, w_pool_group,
           pool_scale, w_pool_proj, g_gla_norm, w_gla_proj, w_out, g_ffn, w_ffn_in, w_ffn_out, g_final):
    depth = w_in.shape[0]
    assert depth == 1, "single-layer trunk only"
    bp, tp, d = x_prompt.shape
    bs, ts, _ = x_sample.shape
    nbb = SAMPLE_BATCH_BLOCK
    assert d == D_MODEL and w_in.shape == (1, D_MODEL, IN_DIM) and meta_tokens.shape == (N_META, D_MODEL)
    assert tp % PROMPT_TILE == 0 and tp % FFN_TILE == 0 and PROMPT_TILE % GLA_CHUNK == 0
    assert bs % nbb == 0 and ts + 2 <= SAMPLE_ROWS and bs * ts == FFN_TILE
    n_t = tp // PROMPT_TILE
    n_p = bp * n_t
    n_s = bs // nbb
    n_fi = D_MODEL // FFN_IN_CAST_ROWS
    n_fo = D_FF // FFN_OUT_CAST_ROWS
    assert n_fi <= n_p and n_fo <= n_p

    def s_idx(i):
        return jnp.minimum(i, n_s - 1)

    def p_idx(i):
        return jnp.maximum(i - n_s, 0)

    w_in_t = jnp.transpose(w_in, (0, 2, 1))
    pool_hist = jnp.transpose(state_pool, (0, 2, 1, 3))
    small = (meta_tokens, g_mix, b_gk, pool_scale, g_gla_norm, w_gk_up, w_pool_group)
    hbm = pl.BlockSpec(memory_space=pl.ANY)
    in_specs = (
        [pl.BlockSpec((nbb, ts, d), lambda i: (s_idx(i), 0, 0)),
         pl.BlockSpec((1, POOL_BUF, nbb, POOL_WIDTH), lambda i: (0, 0, s_idx(i), 0)),
         pl.BlockSpec((1, nbb, GLA_HEADS, GLA_DK, GLA_DV), lambda i: (0, s_idx(i), 0, 0, 0)),
         pl.BlockSpec((1, PROMPT_TILE, d), lambda i: (p_idx(i) // n_t, p_idx(i) % n_t, 0))]
        + [_const_spec(a.shape) for a in small]
        + [hbm, hbm, hbm, hbm,
           pl.BlockSpec((1, FFN_IN_CAST_ROWS, 2 * D_FF), lambda i: (0, jnp.minimum(p_idx(i), n_fi - 1), 0)),
           pl.BlockSpec((1, FFN_OUT_CAST_ROWS, d), lambda i: (0, jnp.minimum(p_idx(i), n_fo - 1), 0))])
    out_specs = [
        pl.BlockSpec((nbb * ts, d), lambda i: (s_idx(i), 0)),
        pl.BlockSpec((1, POOL_BUF, nbb, POOL_WIDTH), lambda i: (0, 0, s_idx(i), 0)),
        pl.BlockSpec((1, nbb, GLA_HEADS, GLA_DK, GLA_DV), lambda i: (0, s_idx(i), 0, 0, 0)),
        pl.BlockSpec((1, PROMPT_TILE, d), lambda i: (p_idx(i) // n_t, p_idx(i) % n_t, 0)),
        pl.BlockSpec((1, POOL_BUF, bp, POOL_WIDTH), lambda i: (0, 0, 0, 0)),
        pl.BlockSpec((1, 1, GLA_HEADS, GLA_DK, GLA_DV), lambda i: (0, p_idx(i) // n_t, 0, 0, 0)),
        pl.BlockSpec((D_FF // FFN_CHUNK, FFN_IN_CAST_ROWS, 2 * FFN_CHUNK),
                     lambda i: (0, jnp.minimum(p_idx(i), n_fi - 1), 0)),
        pl.BlockSpec((FFN_OUT_CAST_ROWS, d), lambda i: (jnp.minimum(p_idx(i), n_fo - 1), 0))]
    out_shape = [
        jax.ShapeDtypeStruct((bs * ts, d), F32),
        jax.ShapeDtypeStruct((1, POOL_BUF, bs, POOL_WIDTH), F32),
        jax.ShapeDtypeStruct(state_gla.shape, F32),
        jax.ShapeDtypeStruct(x_prompt.shape, F32),
        jax.ShapeDtypeStruct((1, POOL_BUF, bp, POOL_WIDTH), F32),
        jax.ShapeDtypeStruct((1, bp, GLA_HEADS, GLA_DK, GLA_DV), F32),
        jax.ShapeDtypeStruct((D_FF // FFN_CHUNK, D_MODEL, 2 * FFN_CHUNK), BF16),
        jax.ShapeDtypeStruct((D_FF, D_MODEL), BF16)]
    scratch_shapes = [
        pltpu.VMEM((D_MODEL, MAIN_COLS), BF16),
        pltpu.VMEM((D_MODEL, 2 * D_MODEL), BF16), pltpu.VMEM((LANES, GLA_KW), BF16),
        pltpu.VMEM((POOL_WIDTH, D_MODEL), BF16), pltpu.VMEM((GLA_VW, D_MODEL), BF16),
        pltpu.VMEM((D_MODEL, D_MODEL), BF16),
        pltpu.VMEM((STAGE_SLOTS, STAGE_ROWS, D_MODEL), F32), pltpu.VMEM((LANES, D_MODEL), F32),
        pltpu.SemaphoreType.DMA((STAGE_SLOTS,)), pltpu.SemaphoreType.DMA((1,)),
        pltpu.VMEM((POOL_PAD + TAIL_ROWS + PROMPT_TILE, POOL_WIDTH), F32),
        pltpu.VMEM((len(POOL_WINDOWS) - 1, POOL_PAD + TAIL_ROWS + PROMPT_TILE, POOL_WIDTH), F32),
        pltpu.VMEM((GLA_HEADS, GLA_DK, GLA_DV), F32),
        pltpu.VMEM((N_META, POOL_WIDTH), F32),
        pltpu.VMEM((GLA_HEADS, GLA_DK, GLA_DV), F32),
        pltpu.VMEM((nbb, SAMPLE_ROWS, D_MODEL), F32),
        pltpu.VMEM((len(POOL_WINDOWS), nbb * SAMPLE_ROWS, POOL_GROUP_DIM), F32),
        pltpu.VMEM((len(POOL_WINDOWS), nbb * SAMPLE_ROWS, POOL_GROUP_DIM), F32),
        pltpu.VMEM((nbb, GLA_HEADS * SAMPLE_ROWS, GLA_KW), BF16),
        pltpu.VMEM((nbb, SAMPLE_ROWS, GLA_KW), F32),
        pltpu.VMEM((nbb, SAMPLE_ROWS, 2 * GLA_VW), F32),
        pltpu.VMEM((nbb, GLA_HEADS * SAMPLE_ROWS, GLA_DV), F32)]

    def mixer(*refs):
        _mixer_kernel(n_s, n_t, *refs)

    x2_s, pool_s, gla_s, x2_p, pool_p, gla_p, wfi_bf, wfo_bf = pl.pallas_call(
        mixer,
        grid=(n_s + n_p,),
        in_specs=in_specs,
        out_specs=out_specs,
        out_shape=out_shape,
        scratch_shapes=scratch_shapes,
        compiler_params=pltpu.CompilerParams(dimension_semantics=("arbitrary",), vmem_limit_bytes=VMEM_LIMIT),
        name="mixer",
    )(x_sample, pool_hist, state_gla, x_prompt, *small, w_in_t, w_pool_proj, w_gla_proj, w_out, w_ffn_in, w_ffn_out)

    def ffn(*refs):
        _ffn_kernel(ts, *refs)

    n_tf = tp // FFN_TILE

    def prompt_tile_index(i):
        tile_id = jnp.maximum(i - 1, 0)
        return (tile_id // n_tf, tile_id % n_tf, 0)

    ffn_args = (g_ffn, wfi_bf, wfo_bf, g_final.reshape(1, D_MODEL))
    y_prompt, y_sample = pl.pallas_call(
        ffn,
        grid=(bp * n_tf + 1,),
        in_specs=[pl.BlockSpec((1, FFN_TILE, d), prompt_tile_index),
                  _const_spec(x2_s.shape), _const_spec(g_ffn.shape), hbm, hbm, _const_spec((1, D_MODEL))],
        out_specs=[pl.BlockSpec((1, FFN_TILE, d), prompt_tile_index),
                   pl.BlockSpec(x_sample.shape, lambda i: (0, 0, 0))],
        out_shape=[jax.ShapeDtypeStruct(x_prompt.shape, F32), jax.ShapeDtypeStruct(x_sample.shape, F32)],
        scratch_shapes=[pltpu.VMEM(wfi_bf.shape, BF16), pltpu.VMEM(wfo_bf.shape, BF16),
                        pltpu.SemaphoreType.DMA((2 * (D_FF // FFN_CHUNK),))],
        compiler_params=pltpu.CompilerParams(dimension_semantics=("arbitrary",), vmem_limit_bytes=VMEM_LIMIT),
        name="ffn",
    )(x2_p, x2_s, *ffn_args)
    pool_p = jnp.transpose(pool_p, (0, 2, 1, 3))
    pool_s = jnp.transpose(pool_s, (0, 2, 1, 3))
    return y_prompt, y_sample, pool_p, gla_p, pool_s, gla_s
```

```python
import jax
import jax.numpy as jnp
from jax import lax
from jax.experimental import pallas as pl
from jax.experimental.pallas import tpu as pltpu

F32 = jnp.float32
BF16 = jnp.bfloat16

D_MODEL = 1024
N_META = 16
POOL_WIDTH = 512
POOL_WINDOWS = (2, 4, 8, 16)
POOL_GROUP_DIM = 128
POOL_BUF = 15
GLA_HEADS = 4
GLA_DV = 128
GLA_DK = 64
GLA_KW = GLA_HEADS * GLA_DK
GLA_VW = GLA_HEADS * GLA_DV
GLA_GATE_RANK = 16
GLA_TAU = 16.0
GLA_CHUNK = 64
D_FF = 2816
EPS = 1e-6

LANES = 128
SUBLANES = 8
MAIN_W = POOL_WIDTH + 2 * GLA_KW + 2 * GLA_VW
U_COL, VOG_COL, QK_COL, ZR_COL = 0, POOL_WIDTH, POOL_WIDTH + 2 * GLA_VW, MAIN_W
MAIN_COLS = MAIN_W + LANES
GAB_LO = MAIN_W + GLA_GATE_RANK
IN_DIM = GAB_LO + 2 * D_MODEL
TAIL_ROWS = 16
POOL_PAD = SUBLANES

PROMPT_TILE = 512
FFN_TILE = 512
FFN_CHUNK = 256
FFN_OUT_GROUP = 4
FFN_TAIL_ROWS = 256
SAMPLE_BATCH_BLOCK = 16
SAMPLE_ROWS = SUBLANES
SAMPLE_GATE_SLICES = 8
SAMPLE_EARLY_GATE_SLICES = 2
STAGE_ROWS = 256
STAGE_SLOTS = 8
FFN_IN_CAST_ROWS = 32
FFN_OUT_CAST_ROWS = 128
V7X_VMEM_BYTES = 64 * 1024 * 1024
VMEM_RESERVE = 4 * 1024 * 1024
VMEM_LIMIT = V7X_VMEM_BYTES - VMEM_RESERVE


def _dot(a, b):
    return jnp.dot(a, b, preferred_element_type=F32)


def _dot_nt(a, b):
    return lax.dot_general(a, b, (((1,), (1,)), ((), ())), preferred_element_type=F32)


def _dot_tn(a, b):
    return lax.dot_general(a, b, (((0,), (0,)), ((), ())), preferred_element_type=F32)


def _rms(x, g):
    return x * lax.rsqrt(jnp.mean(x * x, axis=-1, keepdims=True) + EPS) * g


def _rms_split(x, g):
    r = lax.rsqrt(jnp.mean(x * x, axis=-1, keepdims=True) + EPS)
    return (x * g).astype(BF16), r


def _sigmoid(x):
    return 0.5 * jnp.tanh(0.5 * x) + 0.5


def _silu(x):
    half = 0.5 * x
    return half * jnp.tanh(half) + half


def _log_sigmoid(x):
    return jnp.minimum(x, 0.0) - jnp.log(1.0 + jnp.exp(-jnp.abs(x)))


def _split_bf16(x):
    hi = x.astype(BF16)
    lo = (x - hi.astype(F32)).astype(BF16)
    return hi, lo


class _Weights:
    def __init__(self, gmix, bgk, pscale, gnorm, wpg, wmain, wgab, wgk, wpp, wgp, wout):
        self.gmix, self.bgk, self.pscale, self.gnorm, self.wpg = gmix, bgk, pscale, gnorm, wpg
        self.wmain, self.wgab, self.wgk = wmain, wgab, wgk
        self.wpp, self.wgp, self.wout = wpp, wgp, wout


def _in_proj(x, w):
    h, r = _rms_split(x, w.gmix[...])
    qkz = _dot(h, w.wmain[:, QK_COL:MAIN_COLS])
    q = qkz[:, :GLA_KW] * (r * (GLA_DK ** -0.5))
    k = qkz[:, GLA_KW:2 * GLA_KW] * r
    zr = qkz[:, 2 * GLA_KW:] * r
    z = _dot(zr.astype(BF16), w.wgk[...]) + w.bgk[...]
    u = _dot(h, w.wmain[:, U_COL:VOG_COL]) * r
    vog = _dot(h, w.wmain[:, VOG_COL:QK_COL]) * r
    v = vog[:, :GLA_VW]
    og = vog[:, GLA_VW:]
    return (h, r), u, q, k, v, og, z


def _gate_proj(xn, w, lo, hi):
    h, r = xn
    return _dot(h, w.wgab[:, lo:hi]) * r


def _chunk_cumsum_wide(g, chunk):
    n = g.shape[0] // chunk
    r = lax.broadcasted_iota(jnp.int32, (chunk, chunk), 0)
    c = lax.broadcasted_iota(jnp.int32, (chunk, chunk), 1)
    tri = jnp.where(c <= r, 1.0, 0.0).astype(BF16)
    hi, lo = _split_bf16(jnp.concatenate([g[j * chunk:(j + 1) * chunk] for j in range(n)], axis=1))
    wide = _dot(tri, hi) + _dot(tri, lo)
    width = g.shape[1]
    parts = [wide[:, j * width:(j + 1) * width] for j in range(n)]
    return jnp.concatenate(parts, axis=0), [p[chunk - 1:chunk, :] for p in parts]


def _chunk_cumsum(g, chunk):
    m = g.shape[0]
    r = lax.broadcasted_iota(jnp.int32, (m, m), 0)
    c = lax.broadcasted_iota(jnp.int32, (m, m), 1)
    tri = jnp.where((r // chunk == c // chunk) & (c <= r), 1.0, 0.0).astype(BF16)
    hi, lo = _split_bf16(g)
    return _dot(tri, hi) + _dot(tri, lo)


def _head_lane_mask(width, per_head):
    lane = lax.broadcasted_iota(jnp.int32, (1, width), 1)
    return [(lane // per_head) == h for h in range(GLA_HEADS)]


def _block_diag_rows(x_bf, per_head):
    r = x_bf.shape[0]
    zero = jnp.zeros((r, per_head), x_bf.dtype)
    rows = []
    for h in range(GLA_HEADS):
        rows.append(jnp.concatenate(
            [x_bf[:, h * per_head:(h + 1) * per_head] if hh == h else zero for hh in range(GLA_HEADS)], axis=1))
    return jnp.concatenate(rows, axis=0)


def _gla_post(o, og, w):
    parts = []
    for h in range(GLA_HEADS):
        oh = o[:, h * GLA_DV:(h + 1) * GLA_DV]
        parts.append(oh * lax.rsqrt(jnp.mean(oh * oh, axis=-1, keepdims=True) + EPS) * w.gnorm[...])
    on = jnp.concatenate(parts, axis=1)
    on = on * _silu(og)
    return _dot(on.astype(BF16), w.wgp[...])


def _pool_post(pooled, w):
    pb = pooled.astype(BF16)
    mixed = []
    for p in range(len(POOL_WINDOWS) // 2):
        w_pair = _pair_block_diag(w.wpg[0, 2 * p].astype(BF16), w.wpg[0, 2 * p + 1].astype(BF16))
        mixed.append(_dot(pb[:, 2 * p * POOL_GROUP_DIM:(2 * p + 2) * POOL_GROUP_DIM], w_pair))
    mixed = jnp.concatenate(mixed, axis=1)
    return _dot((mixed * w.pscale[...]).astype(BF16), w.wpp[...])


def _merge(x, y_a, y_b, sa, sb, w):
    merged = sa * y_a + sb * y_b
    return x + _dot(merged.astype(BF16), w.wout[...])


def _decay_columns(decay_row):
    return jnp.transpose(jnp.broadcast_to(decay_row, (LANES, decay_row.shape[1])))


def _pair_block_diag(a, b):
    zero = jnp.zeros(a.shape, a.dtype)
    return jnp.concatenate([jnp.concatenate([a, zero], axis=1), jnp.concatenate([zero, b], axis=1)], axis=0)


def _state_update(s_heads, kd_bf, v_bf, decay_row):
    dcol = _decay_columns(decay_row)
    out = []
    for p in range(GLA_HEADS // 2):
        upd = _dot_tn(kd_bf[:, 2 * p * GLA_DK:(2 * p + 2) * GLA_DK], v_bf[:, 2 * p * GLA_DV:(2 * p + 2) * GLA_DV])
        for j in range(2):
            h = 2 * p + j
            rows = slice(h * GLA_DK, (h + 1) * GLA_DK)
            out.append(dcol[rows] * s_heads[h] + upd[j * GLA_DK:(j + 1) * GLA_DK, j * GLA_DV:(j + 1) * GLA_DV])
    return out


def _stage_weights(wint_hbm, wpp_hbm, wgp_hbm, wout_hbm, wgk_ref, w, stage, zr_stage, sem, zr_sem):
    plan = []
    for r in range(0, MAIN_W, STAGE_ROWS):
        if r < POOL_WIDTH:
            col = U_COL + r
        elif r < POOL_WIDTH + 2 * GLA_KW:
            col = QK_COL + r - POOL_WIDTH
        else:
            col = VOG_COL + r - (POOL_WIDTH + 2 * GLA_KW)
        plan.append((wint_hbm, r, w.wmain, col, True))
    for r in range(0, 2 * D_MODEL, STAGE_ROWS):
        plan.append((wint_hbm, GAB_LO + r, w.wgab, r, True))
    for src, dst, n_rows in ((wpp_hbm, w.wpp, POOL_WIDTH), (wgp_hbm, w.wgp, GLA_VW), (wout_hbm, w.wout, D_MODEL)):
        for r in range(0, n_rows, STAGE_ROWS):
            plan.append((src, r, dst, r, False))

    def copy(j):
        src, r0 = plan[j][0], plan[j][1]
        return pltpu.make_async_copy(src.at[0, pl.ds(r0, STAGE_ROWS), :], stage.at[j % STAGE_SLOTS], sem.at[j % STAGE_SLOTS])

    def zr_copy():
        return pltpu.make_async_copy(
            wint_hbm.at[0, pl.ds(MAIN_W, GLA_GATE_RANK), :], zr_stage.at[pl.ds(0, GLA_GATE_RANK), :], zr_sem.at[0])

    for j in range(STAGE_SLOTS):
        copy(j).start()
    zr_copy().start()

    w.wgk[...] = jnp.zeros(w.wgk.shape, BF16)
    w.wgk[0:GLA_GATE_RANK, :] = wgk_ref[0].astype(BF16)
    zr_stage[GLA_GATE_RANK:, :] = jnp.zeros((LANES - GLA_GATE_RANK, D_MODEL), F32)

    for j in range(len(plan)):
        copy(j).wait()
        _, _, dst, d0, transposed = plan[j]
        slab = stage[j % STAGE_SLOTS]
        if transposed:
            dst[:, d0:d0 + STAGE_ROWS] = jnp.transpose(slab.astype(BF16))
        else:
            dst[d0:d0 + STAGE_ROWS, :] = slab.astype(BF16)
        if j + STAGE_SLOTS < len(plan):
            copy(j + STAGE_SLOTS).start()
    zr_copy().wait()
    w.wmain[:, ZR_COL:MAIN_COLS] = jnp.transpose(zr_stage[...]).astype(BF16)


def _prompt_tile(b_first, b_idx, t_idx, n_t, x_ref, meta_ref, w, x2_ref, pbuf_ref, sout_ref,
                 ext_ref, lvl_ref, s_ref, meta_tail_ref, meta_s_ref):
    tile = x_ref.shape[1]
    n_chunks = tile // GLA_CHUNK

    @pl.when(b_first)
    def _():
        _, u, _, k, v, _, z = _in_proj(meta_ref[...], w)
        meta_tail_ref[...] = u
        g = _log_sigmoid(z) * (1.0 / GLA_TAU)
        b = _chunk_cumsum(g, N_META)
        b_last = b[N_META - 1:N_META, :]
        kd = k * jnp.exp(b_last - b)
        zero_s = [jnp.zeros((GLA_DK, GLA_DV), F32)] * GLA_HEADS
        s_new = _state_update(zero_s, kd.astype(BF16), v.astype(BF16), jnp.exp(b_last))
        for hd in range(GLA_HEADS):
            meta_s_ref[hd] = s_new[hd]

    @pl.when(t_idx == 0)
    def _():
        ext_ref[0:POOL_PAD, :] = jnp.zeros((POOL_PAD, POOL_WIDTH), F32)
        lvl_ref[:, 0:POOL_PAD, :] = jnp.zeros((lvl_ref.shape[0], POOL_PAD, POOL_WIDTH), F32)
        ext_ref[POOL_PAD:POOL_PAD + TAIL_ROWS, :] = meta_tail_ref[...]
        s_ref[...] = meta_s_ref[...]

    x = x_ref[0]
    xn = _rms_split(x, w.gmix[...])
    h, r = xn
    gate_cols = 2 * D_MODEL // n_chunks
    gate_parts = [None] * n_chunks

    def gate_slice(c):
        gate_parts[c] = _sigmoid(_gate_proj(xn, w, c * gate_cols, (c + 1) * gate_cols))

    qkz = _dot(h, w.wmain[:, QK_COL:MAIN_COLS])
    u = _dot(h, w.wmain[:, U_COL:VOG_COL]) * r
    v = _dot(h, w.wmain[:, VOG_COL:VOG_COL + GLA_VW]) * r
    q = qkz[:, :GLA_KW] * (r * (GLA_DK ** -0.5))
    k = qkz[:, GLA_KW:2 * GLA_KW] * r
    zr = qkz[:, 2 * GLA_KW:] * r
    z = _dot(zr.astype(BF16), w.wgk[...]) + w.bgk[...]
    gate_slice(0)
    g = _log_sigmoid(z) * (1.0 / GLA_TAU)
    b, b_last_rows = _chunk_cumsum_wide(g, GLA_CHUNK)
    gate_slice(1)
    og_parts = []

    base = POOL_PAD + TAIL_ROWS
    span = TAIL_ROWS + tile
    ext_ref[base:base + tile, :] = u
    cur = ext_ref[POOL_PAD:POOL_PAD + span, :]
    pooled = []
    for gi, win in enumerate(POOL_WINDOWS):
        shift = win // 2
        lo = gi * POOL_GROUP_DIM
        prev_ref = ext_ref if gi == 0 else lvl_ref.at[gi - 1]
        cur = cur[:, (POOL_GROUP_DIM if gi else 0):] + prev_ref[POOL_PAD - shift:POOL_PAD - shift + span, lo:]
        pooled.append(cur[TAIL_ROWS:, 0:POOL_GROUP_DIM] * (1.0 / win) - u[:, lo:lo + POOL_GROUP_DIM])
        if gi + 1 < len(POOL_WINDOWS):
            lvl_ref[gi, POOL_PAD:POOL_PAD + span, lo:] = cur
    y_a = _pool_post(jnp.concatenate(pooled, axis=1), w)
    ext_ref[POOL_PAD:base, :] = ext_ref[POOL_PAD + tile:base + tile, :]

    b_last = jnp.concatenate([jnp.broadcast_to(r, (GLA_CHUNK, GLA_KW)) for r in b_last_rows], axis=0)
    qe = (q * jnp.exp(b)).astype(BF16)
    ke = k * jnp.exp(-b)
    kd = (k * jnp.exp(b_last - b)).astype(BF16)
    v_bf = v.astype(BF16)

    k_masks = _head_lane_mask(GLA_KW, GLA_DK)
    row_i = lax.broadcasted_iota(jnp.int32, (GLA_CHUNK, GLA_KW), 0)
    col_j = lax.broadcasted_iota(jnp.int32, (GLA_CHUNK, GLA_KW), 1) % GLA_CHUNK
    causal = col_j <= row_i

    s_heads = [s_ref[hd] for hd in range(GLA_HEADS)]
    o_chunks = []
    for c in range(n_chunks):
        if c + 2 < n_chunks:
            gate_slice(c + 2)
        else:
            og_lo = VOG_COL + GLA_VW + len(og_parts) * (GLA_VW // 2)
            og_parts.append(_dot(h, w.wmain[:, og_lo:og_lo + GLA_VW // 2]) * r)
        rows = slice(c * GLA_CHUNK, (c + 1) * GLA_CHUNK)
        ke_c = ke[rows]
        ke_bd = jnp.concatenate([jnp.where(k_masks[hd], ke_c, 0.0) for hd in range(GLA_HEADS)], axis=0).astype(BF16)
        att = jnp.where(causal, _dot_nt(qe[rows], ke_bd), 0.0).astype(BF16)
        o_pairs = []
        for p in range(GLA_HEADS // 2):
            h0, h1 = 2 * p, 2 * p + 1
            lanes_k = slice(h0 * GLA_DK, (h1 + 1) * GLA_DK)
            lanes_j = slice(h0 * GLA_CHUNK, (h1 + 1) * GLA_CHUNK)
            v0 = v_bf[rows, h0 * GLA_DV:(h0 + 1) * GLA_DV]
            v1 = v_bf[rows, h1 * GLA_DV:(h1 + 1) * GLA_DV]
            rhs = jnp.concatenate([_pair_block_diag(s_heads[h0].astype(BF16), s_heads[h1].astype(BF16)),
                                   _pair_block_diag(v0, v1)], axis=0)
            o_pairs.append(_dot(jnp.concatenate([qe[rows, lanes_k], att[:, lanes_j]], axis=1), rhs))
        o_chunks.append(jnp.concatenate(o_pairs, axis=1))
        s_heads = _state_update(s_heads, kd[rows], v_bf[rows], jnp.exp(b_last_rows[c]))
    for hd in range(GLA_HEADS):
        s_ref[hd] = s_heads[hd]

    y_b = _gla_post(jnp.concatenate(o_chunks, axis=0), jnp.concatenate(og_parts, axis=1), w)
    sg = jnp.concatenate(gate_parts, axis=1)
    x2_ref[0] = _merge(x, y_a, y_b, sg[:, :D_MODEL], sg[:, D_MODEL:], w)

    @pl.when(t_idx == n_t - 1)
    def _():
        sout_ref[0, 0] = s_ref[...]

    for bb in range(pbuf_ref.shape[2]):
        @pl.when((t_idx == n_t - 1) & (b_idx == bb))
        def _():
            for r in range(POOL_BUF):
                row = base - POOL_BUF + r
                pbuf_ref[0, r, bb:bb + 1, :] = ext_ref[row:row + 1, :]


def _sample_block(x_ref, pool_ref, sin_ref, w, x2_ref, pbuf_ref, sout_ref,
                  xs_ref, us_ref, pooled_ref, qm_ref, kdx_ref, rhs_ref, oi_ref):
    nb, seq, _ = x_ref.shape
    rows_pb = SAMPLE_ROWS
    m = nb * rows_pb

    xs_ref[:, seq:, :] = jnp.zeros((nb, rows_pb - seq, D_MODEL), F32)
    xs_ref[:, 0:seq, :] = x_ref[...]
    x = xs_ref[...].reshape(m, D_MODEL)
    xn, u, q, k, v, og, z = _in_proj(x, w)

    gate_cols = 2 * D_MODEL // SAMPLE_GATE_SLICES
    gate_parts = [_sigmoid(_gate_proj(xn, w, c * gate_cols, (c + 1) * gate_cols))
                  for c in range(SAMPLE_EARLY_GATE_SLICES)]

    pooled_ref[...] = jnp.zeros(pooled_ref.shape, F32)
    for gi, win in enumerate(POOL_WINDOWS):
        cols = slice(gi * POOL_GROUP_DIM, (gi + 1) * POOL_GROUP_DIM)
        us_ref[gi] = u[:, cols]
        tok = [us_ref[gi, pl.ds(t, nb, stride=rows_pb), :] for t in range(seq)]
        hist = [pool_ref[0, r, :, cols] for r in range(POOL_BUF)] + tok
        for t in range(seq):
            acc = tok[t]
            for n in range(1, win):
                acc = acc + hist[POOL_BUF + t - n]
            pooled_ref[gi, pl.ds(t, nb, stride=rows_pb), :] = acc * (1.0 / win) - tok[t]
        for r in range(POOL_BUF):
            pbuf_ref[0, r, :, cols] = hist[seq + r]

    r8 = lax.broadcasted_iota(jnp.int32, (m, 1), 0) % rows_pb
    g = jnp.where(r8 < seq, _log_sigmoid(z) * (1.0 / GLA_TAU), 0.0)
    b = _chunk_cumsum(g, rows_pb)
    b3 = b.reshape(nb, rows_pb, GLA_KW)
    b_last = jnp.broadcast_to(b3[:, seq - 1:seq, :], b3.shape).reshape(m, GLA_KW)
    qe = q * jnp.exp(b)
    ke = k * jnp.exp(-b)
    kd = k * jnp.exp(b_last - b)
    decay = jnp.exp(b_last)
    v_bf = v.astype(BF16)

    k_masks = _head_lane_mask(GLA_KW, GLA_DK)
    ke_bd = jnp.concatenate([jnp.where(k_masks[hd], ke, 0.0) for hd in range(GLA_HEADS)], axis=0).astype(BF16)
    row_i = lax.broadcasted_iota(jnp.int32, (m, GLA_HEADS * m), 0)
    col_j = lax.broadcasted_iota(jnp.int32, (m, GLA_HEADS * m), 1) % m
    keep = (row_i // rows_pb == col_j // rows_pb) & (col_j <= row_i)
    att = jnp.where(keep, _dot_nt(qe.astype(BF16), ke_bd), 0.0).astype(BF16)
    o_intra = _dot(att, _block_diag_rows(v_bf, GLA_DV))

    qe3 = qe.reshape(nb, rows_pb, GLA_KW)
    qm_ref[...] = jnp.concatenate([jnp.where(k_masks[hd], qe3, 0.0) for hd in range(GLA_HEADS)], axis=1).astype(BF16)
    d_hi = decay.astype(BF16).astype(F32)
    d_lo = decay - d_hi
    kdx = jnp.where(r8 == seq, d_hi, jnp.where(r8 == seq + 1, d_lo, kd))
    kdx_ref[...] = kdx.reshape(nb, rows_pb, GLA_KW)
    ones_rows = jnp.where((r8 == seq) | (r8 == seq + 1), 1.0, 0.0) + jnp.zeros((m, GLA_DV), F32)
    rhs = jnp.concatenate(
        [piece for hd in range(GLA_HEADS) for piece in (v[:, hd * GLA_DV:(hd + 1) * GLA_DV], ones_rows)], axis=1)
    rhs_ref[...] = rhs.reshape(nb, rows_pb, 2 * GLA_VW)

    def per_batch(i):
        s_all = sin_ref[0, i]
        s_flat = s_all.reshape(GLA_KW, GLA_DV).astype(BF16)
        oi_ref[i] = _dot(qm_ref[i], s_flat)
        kdt = jnp.transpose(kdx_ref[i]).astype(BF16)
        rhs_i = rhs_ref[i].astype(BF16)
        for hd in range(GLA_HEADS):
            r = _dot(kdt[hd * GLA_DK:(hd + 1) * GLA_DK], rhs_i[:, hd * 2 * GLA_DV:(hd + 1) * 2 * GLA_DV])
            sout_ref[0, i, hd] = r[:, GLA_DV:] * s_all[hd] + r[:, :GLA_DV]

    late = SAMPLE_GATE_SLICES - SAMPLE_EARLY_GATE_SLICES
    slice_at = {(j * nb) // late: SAMPLE_EARLY_GATE_SLICES + j for j in range(late)}
    for i in range(nb):
        if i in slice_at:
            c = slice_at[i]
            gate_parts.append(_sigmoid(_gate_proj(xn, w, c * gate_cols, (c + 1) * gate_cols)))
        per_batch(i)

    oi = oi_ref[...]
    o_inter = jnp.concatenate([oi[:, hd * rows_pb:(hd + 1) * rows_pb, :] for hd in range(GLA_HEADS)], axis=2)
    o = o_intra + o_inter.reshape(m, GLA_VW)
    y_a = _pool_post(jnp.concatenate([pooled_ref[gi] for gi in range(len(POOL_WINDOWS))], axis=1), w)
    y_b = _gla_post(o, og, w)
    sg = jnp.concatenate(gate_parts, axis=1)
    x2 = _merge(x, y_a, y_b, sg[:, :D_MODEL], sg[:, D_MODEL:], w).reshape(nb, rows_pb, D_MODEL)
    for bi in range(nb):
        x2_ref[bi * seq:(bi + 1) * seq, :] = x2[bi, 0:seq, :]


def _mixer_kernel(n_s, n_t,
                  xs_in, pool_in, s_in, xp_in, meta_ref, gmix_ref, bgk_ref, pscale_ref, gnorm_ref, wgk_ref, wpg_ref,
                  wint_hbm, wpp_hbm, wgp_hbm, wout_hbm, wfi_in, wfo_in,
                  x2s_out, pools_out, ss_out, x2p_out, poolp_out, sp_out, wfi_out, wfo_out,
                  wmain_s, wgab_s, wgk_s, wpp_s, wgp_s, wout_s, stage, zr_stage, sem, zr_sem,
                  ext_ref, lvl_ref, s_ref, meta_tail_ref, meta_s_ref,
                  xs_ref, us_ref, pooled_ref, qm_ref, kdx_ref, rhs_ref, oi_ref):
    i = pl.program_id(0)
    w = _Weights(gmix_ref, bgk_ref, pscale_ref, gnorm_ref, wpg_ref,
                 wmain_s, wgab_s, wgk_s, wpp_s, wgp_s, wout_s)

    @pl.when(i == 0)
    def _():
        _stage_weights(wint_hbm, wpp_hbm, wgp_hbm, wout_hbm, wgk_ref, w, stage, zr_stage, sem, zr_sem)

    wfi_bf = wfi_in[0].astype(BF16)
    for kc in range(wfi_out.shape[0]):
        cols = slice(kc * FFN_CHUNK, (kc + 1) * FFN_CHUNK)
        wfi_out[kc] = jnp.concatenate([wfi_bf[:, cols], wfi_bf[:, D_FF + kc * FFN_CHUNK:D_FF + (kc + 1) * FFN_CHUNK]],
                                      axis=1)
    wfo_out[...] = wfo_in[0].astype(BF16)

    @pl.when(i < n_s)
    def _():
        _sample_block(xs_in, pool_in, s_in, w, x2s_out, pools_out, ss_out,
                      xs_ref, us_ref, pooled_ref, qm_ref, kdx_ref, rhs_ref, oi_ref)

    @pl.when(i >= n_s)
    def _():
        t_idx = (i - n_s) % n_t
        _prompt_tile(i == n_s, (i - n_s) // n_t, t_idx, n_t, xp_in, meta_ref, w, x2p_out, poolp_out, sp_out,
                     ext_ref, lvl_ref, s_ref, meta_tail_ref, meta_s_ref)


def _ffn_weight_copies(wi_hbm, wo_hbm, wi_s, wo_s, sem):
    copies = []
    for c in range(D_FF // FFN_CHUNK):
        rows = pl.ds(c * FFN_CHUNK, FFN_CHUNK)
        copies.append((
            pltpu.make_async_copy(wi_hbm.at[c], wi_s.at[c], sem.at[2 * c]),
            pltpu.make_async_copy(wo_hbm.at[rows, :], wo_s.at[rows, :], sem.at[2 * c + 1])))
    return copies


def _ffn_tile(x, gffn_ref, wi_s, wo_s, gfin_ref, before_chunk=None):
    n_chunks = D_FF // FFN_CHUNK
    h, r = _rms_split(x, gffn_ref[...])
    acc = x
    group = []
    for c in range(n_chunks):
        if before_chunk is not None:
            before_chunk(c)
        gate_up = _dot(h, wi_s[c]) * r
        group.append((_silu(gate_up[:, :FFN_CHUNK]) * gate_up[:, FFN_CHUNK:]).astype(BF16))
        if c + 1 == n_chunks:
            first = (c + 1 - len(group)) * FFN_CHUNK
            act = jnp.concatenate(group, axis=1)
            out = []
            for r0 in range(0, x.shape[0], FFN_TAIL_ROWS):
                rows = slice(r0, r0 + FFN_TAIL_ROWS)
                out.append(_rms(acc[rows] + _dot(act[rows], wo_s[first:(c + 1) * FFN_CHUNK, :]), gfin_ref[...]))
            return jnp.concatenate(out, axis=0)
        if len(group) == FFN_OUT_GROUP:
            first = (c + 1 - len(group)) * FFN_CHUNK
            acc = acc + _dot(jnp.concatenate(group, axis=1), wo_s[first:(c + 1) * FFN_CHUNK, :])
            group = []


def _ffn_kernel(seq, xp_ref, xs_ref, gffn_ref, wi_hbm, wo_hbm, gfin_ref, yp_ref, ys_ref, wi_s, wo_s, sem):
    i = pl.program_id(0)

    @pl.when(i == 0)
    def _():
        copies = _ffn_weight_copies(wi_hbm, wo_hbm, wi_s, wo_s, sem)
        for chunk_copies in copies:
            for cp in chunk_copies:
                cp.start()

        run_at = {0: 1, 1: 2, 3: 4, 7: len(copies) - 7}

        def wait_chunk(c):
            for cc in range(c, c + run_at.get(c, 0)):
                for cp in copies[cc]:
                    cp.wait()

        y = _ffn_tile(xs_ref[...], gffn_ref, wi_s, wo_s, gfin_ref, before_chunk=wait_chunk)
        for bi in range(ys_ref.shape[0]):
            ys_ref[bi] = y[bi * seq:(bi + 1) * seq, :]

    @pl.when(i > 0)
    def _():
        yp_ref[0] = _ffn_tile(xp_ref[0], gffn_ref, wi_s, wo_s, gfin_ref)


def _const_spec(shape):
    zeros = (0,) * len(shape)
    return pl.BlockSpec(shape, lambda *_: zeros, pipeline_mode=pl.Buffered(1))


def kernel(x_prompt, x_sample, state_pool, state_gla, meta_tokens, g_mix, w_in, w_gk_up, b_gk, w_pool_group,
           pool_scale, w_pool_proj, g_gla_norm, w_gla_proj, w_out, g_ffn, w_ffn_in, w_ffn_out, g_final):
    depth = w_in.shape[0]
    assert depth == 1, "single-layer trunk only"
    bp, tp, d = x_prompt.shape
    bs, ts, _ = x_sample.shape
    nbb = SAMPLE_BATCH_BLOCK
    assert d == D_MODEL and w_in.shape == (1, D_MODEL, IN_DIM) and meta_tokens.shape == (N_META, D_MODEL)
    assert tp % PROMPT_TILE == 0 and tp % FFN_TILE == 0 and PROMPT_TILE % GLA_CHUNK == 0
    assert bs % nbb == 0 and ts + 2 <= SAMPLE_ROWS and bs * ts == FFN_TILE
    n_t = tp // PROMPT_TILE
    n_p = bp * n_t
    n_s = bs // nbb
    n_fi = D_MODEL // FFN_IN_CAST_ROWS
    n_fo = D_FF // FFN_OUT_CAST_ROWS
    assert n_fi <= n_p and n_fo <= n_p

    def s_idx(i):
        return jnp.minimum(i, n_s - 1)

    def p_idx(i):
        return jnp.maximum(i - n_s, 0)

    w_in_t = jnp.transpose(w_in, (0, 2, 1))
    pool_hist = jnp.transpose(state_pool, (0, 2, 1, 3))
    small = (meta_tokens, g_mix, b_gk, pool_scale, g_gla_norm, w_gk_up, w_pool_group)
    hbm = pl.BlockSpec(memory_space=pl.ANY)
    in_specs = (
        [pl.BlockSpec((nbb, ts, d), lambda i: (s_idx(i), 0, 0)),
         pl.BlockSpec((1, POOL_BUF, nbb, POOL_WIDTH), lambda i: (0, 0, s_idx(i), 0)),
         pl.BlockSpec((1, nbb, GLA_HEADS, GLA_DK, GLA_DV), lambda i: (0, s_idx(i), 0, 0, 0)),
         pl.BlockSpec((1, PROMPT_TILE, d), lambda i: (p_idx(i) // n_t, p_idx(i) % n_t, 0))]
        + [_const_spec(a.shape) for a in small]
        + [hbm, hbm, hbm, hbm,
           pl.BlockSpec((1, FFN_IN_CAST_ROWS, 2 * D_FF), lambda i: (0, jnp.minimum(p_idx(i), n_fi - 1), 0)),
           pl.BlockSpec((1, FFN_OUT_CAST_ROWS, d), lambda i: (0, jnp.minimum(p_idx(i), n_fo - 1), 0))])
    out_specs = [
        pl.BlockSpec((nbb * ts, d), lambda i: (s_idx(i), 0)),
        pl.BlockSpec((1, POOL_BUF, nbb, POOL_WIDTH), lambda i: (0, 0, s_idx(i), 0)),
        pl.BlockSpec((1, nbb, GLA_HEADS, GLA_DK, GLA_DV), lambda i: (0, s_idx(i), 0, 0, 0)),
        pl.BlockSpec((1, PROMPT_TILE, d), lambda i: (p_idx(i) // n_t, p_idx(i) % n_t, 0)),
        pl.BlockSpec((1, POOL_BUF, bp, POOL_WIDTH), lambda i: (0, 0, 0, 0)),
        pl.BlockSpec((1, 1, GLA_HEADS, GLA_DK, GLA_DV), lambda i: (0, p_idx(i) // n_t, 0, 0, 0)),
        pl.BlockSpec((D_FF // FFN_CHUNK, FFN_IN_CAST_ROWS, 2 * FFN_CHUNK),
                     lambda i: (0, jnp.minimum(p_idx(i), n_fi - 1), 0)),
        pl.BlockSpec((FFN_OUT_CAST_ROWS, d), lambda i: (jnp.minimum(p_idx(i), n_fo - 1), 0))]
    out_shape = [
        jax.ShapeDtypeStruct((bs * ts, d), F32),
        jax.ShapeDtypeStruct((1, POOL_BUF, bs, POOL_WIDTH), F32),
        jax.ShapeDtypeStruct(state_gla.shape, F32),
        jax.ShapeDtypeStruct(x_prompt.shape, F32),
        jax.ShapeDtypeStruct((1, POOL_BUF, bp, POOL_WIDTH), F32),
        jax.ShapeDtypeStruct((1, bp, GLA_HEADS, GLA_DK, GLA_DV), F32),
        jax.ShapeDtypeStruct((D_FF // FFN_CHUNK, D_MODEL, 2 * FFN_CHUNK), BF16),
        jax.ShapeDtypeStruct((D_FF, D_MODEL), BF16)]
    scratch_shapes = [
        pltpu.VMEM((D_MODEL, MAIN_COLS), BF16),
        pltpu.VMEM((D_MODEL, 2 * D_MODEL), BF16), pltpu.VMEM((LANES, GLA_KW), BF16),
        pltpu.VMEM((POOL_WIDTH, D_MODEL), BF16), pltpu.VMEM((GLA_VW, D_MODEL), BF16),
        pltpu.VMEM((D_MODEL, D_MODEL), BF16),
        pltpu.VMEM((STAGE_SLOTS, STAGE_ROWS, D_MODEL), F32), pltpu.VMEM((LANES, D_MODEL), F32),
        pltpu.SemaphoreType.DMA((STAGE_SLOTS,)), pltpu.SemaphoreType.DMA((1,)),
        pltpu.VMEM((POOL_PAD + TAIL_ROWS + PROMPT_TILE, POOL_WIDTH), F32),
        pltpu.VMEM((len(POOL_WINDOWS) - 1, POOL_PAD + TAIL_ROWS + PROMPT_TILE, POOL_WIDTH), F32),
        pltpu.VMEM((GLA_HEADS, GLA_DK, GLA_DV), F32),
        pltpu.VMEM((N_META, POOL_WIDTH), F32),
        pltpu.VMEM((GLA_HEADS, GLA_DK, GLA_DV), F32),
        pltpu.VMEM((nbb, SAMPLE_ROWS, D_MODEL), F32),
        pltpu.VMEM((len(POOL_WINDOWS), nbb * SAMPLE_ROWS, POOL_GROUP_DIM), F32),
        pltpu.VMEM((len(POOL_WINDOWS), nbb * SAMPLE_ROWS, POOL_GROUP_DIM), F32),
        pltpu.VMEM((nbb, GLA_HEADS * SAMPLE_ROWS, GLA_KW), BF16),
        pltpu.VMEM((nbb, SAMPLE_ROWS, GLA_KW), F32),
        pltpu.VMEM((nbb, SAMPLE_ROWS, 2 * GLA_VW), F32),
        pltpu.VMEM((nbb, GLA_HEADS * SAMPLE_ROWS, GLA_DV), F32)]

    def mixer(*refs):
        _mixer_kernel(n_s, n_t, *refs)

    x2_s, pool_s, gla_s, x2_p, pool_p, gla_p, wfi_bf, wfo_bf = pl.pallas_call(
        mixer,
        grid=(n_s + n_p,),
        in_specs=in_specs,
        out_specs=out_specs,
        out_shape=out_shape,
        scratch_shapes=scratch_shapes,
        compiler_params=pltpu.CompilerParams(dimension_semantics=("arbitrary",), vmem_limit_bytes=VMEM_LIMIT),
        name="mixer",
    )(x_sample, pool_hist, state_gla, x_prompt, *small, w_in_t, w_pool_proj, w_gla_proj, w_out, w_ffn_in, w_ffn_out)

    def ffn(*refs):
        _ffn_kernel(ts, *refs)

    n_tf = tp // FFN_TILE

    def prompt_tile_index(i):
        tile_id = jnp.maximum(i - 1, 0)
        return (tile_id // n_tf, tile_id % n_tf, 0)

    ffn_args = (g_ffn, wfi_bf, wfo_bf, g_final.reshape(1, D_MODEL))
    y_prompt, y_sample = pl.pallas_call(
        ffn,
        grid=(bp * n_tf + 1,),
        in_specs=[pl.BlockSpec((1, FFN_TILE, d), prompt_tile_index),
                  _const_spec(x2_s.shape), _const_spec(g_ffn.shape), hbm, hbm, _const_spec((1, D_MODEL))],
        out_specs=[pl.BlockSpec((1, FFN_TILE, d), prompt_tile_index),
                   pl.BlockSpec(x_sample.shape, lambda i: (0, 0, 0))],
        out_shape=[jax.ShapeDtypeStruct(x_prompt.shape, F32), jax.ShapeDtypeStruct(x_sample.shape, F32)],
        scratch_shapes=[pltpu.VMEM(wfi_bf.shape, BF16), pltpu.VMEM(wfo_bf.shape, BF16),
                        pltpu.SemaphoreType.DMA((2 * (D_FF // FFN_CHUNK),))],
        compiler_params=pltpu.CompilerParams(dimension_semantics=("arbitrary",), vmem_limit_bytes=VMEM_LIMIT),
        name="ffn",
    )(x2_p, x2_s, *ffn_args)
    pool_p = jnp.transpose(pool_p, (0, 2, 1, 3))
    pool_s = jnp.transpose(pool_s, (0, 2, 1, 3))
    return y_prompt, y_sample, pool_p, gla_p, pool_s, gla_s
```

```python
import jax
import jax.numpy as jnp
from jax import lax
from jax.experimental import pallas as pl
from jax.experimental.pallas import tpu as pltpu

F32 = jnp.float32
BF16 = jnp.bfloat16

D_MODEL = 1024
N_META = 16
POOL_WIDTH = 512
POOL_WINDOWS = (2, 4, 8, 16)
POOL_GROUP_DIM = 128
POOL_BUF = 15
GLA_HEADS = 4
GLA_DV = 128
GLA_DK = 64
GLA_KW = GLA_HEADS * GLA_DK
GLA_VW = GLA_HEADS * GLA_DV
GLA_GATE_RANK = 16
GLA_TAU = 16.0
GLA_CHUNK = 64
D_FF = 2816
EPS = 1e-6

LANES = 128
SUBLANES = 8
MAIN_W = POOL_WIDTH + 2 * GLA_KW + 2 * GLA_VW
U_COL, VOG_COL, QK_COL, ZR_COL = 0, POOL_WIDTH, POOL_WIDTH + 2 * GLA_VW, MAIN_W
MAIN_COLS = MAIN_W + LANES
GAB_LO = MAIN_W + GLA_GATE_RANK
IN_DIM = GAB_LO + 2 * D_MODEL
TAIL_ROWS = 16
POOL_PAD = SUBLANES

PROMPT_TILE = 512
FFN_TILE = 512
FFN_CHUNK = 256
FFN_OUT_GROUP = 4
FFN_TAIL_ROWS = 256
SAMPLE_BATCH_BLOCK = 16
SAMPLE_ROWS = SUBLANES
SAMPLE_GATE_SLICES = 8
SAMPLE_EARLY_GATE_SLICES = 2
STAGE_ROWS = 256
STAGE_SLOTS = 8
FFN_IN_CAST_ROWS = 32
FFN_OUT_CAST_ROWS = 128
V7X_VMEM_BYTES = 64 * 1024 * 1024
VMEM_RESERVE = 4 * 1024 * 1024
VMEM_LIMIT = V7X_VMEM_BYTES - VMEM_RESERVE


def _dot(a, b):
    return jnp.dot(a, b, preferred_element_type=F32)


def _dot_nt(a, b):
    return lax.dot_general(a, b, (((1,), (1,)), ((), ())), preferred_element_type=F32)


def _dot_tn(a, b):
    return lax.dot_general(a, b, (((0,), (0,)), ((), ())), preferred_element_type=F32)


def _rms(x, g):
    return x * lax.rsqrt(jnp.mean(x * x, axis=-1, keepdims=True) + EPS) * g


def _rms_split(x, g):
    r = lax.rsqrt(jnp.mean(x * x, axis=-1, keepdims=True) + EPS)
    return (x * g).astype(BF16), r


def _sigmoid(x):
    return 0.5 * jnp.tanh(0.5 * x) + 0.5


def _silu(x):
    half = 0.5 * x
    return half * jnp.tanh(half) + half


def _log_sigmoid(x):
    return jnp.minimum(x, 0.0) - jnp.log(1.0 + jnp.exp(-jnp.abs(x)))


def _split_bf16(x):
    hi = x.astype(BF16)
    lo = (x - hi.astype(F32)).astype(BF16)
    return hi, lo


class _Weights:
    def __init__(self, gmix, bgk, pscale, gnorm, wpg, wmain, wgab, wgk, wpp, wgp, wout):
        self.gmix, self.bgk, self.pscale, self.gnorm, self.wpg = gmix, bgk, pscale, gnorm, wpg
        self.wmain, self.wgab, self.wgk = wmain, wgab, wgk
        self.wpp, self.wgp, self.wout = wpp, wgp, wout


def _in_proj(x, w):
    h, r = _rms_split(x, w.gmix[...])
    qkz = _dot(h, w.wmain[:, QK_COL:MAIN_COLS])
    q = qkz[:, :GLA_KW] * (r * (GLA_DK ** -0.5))
    k = qkz[:, GLA_KW:2 * GLA_KW] * r
    zr = qkz[:, 2 * GLA_KW:] * r
    z = _dot(zr.astype(BF16), w.wgk[...]) + w.bgk[...]
    u = _dot(h, w.wmain[:, U_COL:VOG_COL]) * r
    vog = _dot(h, w.wmain[:, VOG_COL:QK_COL]) * r
    v = vog[:, :GLA_VW]
    og = vog[:, GLA_VW:]
    return (h, r), u, q, k, v, og, z


def _gate_proj(xn, w, lo, hi):
    h, r = xn
    return _dot(h, w.wgab[:, lo:hi]) * r


def _chunk_cumsum_wide(g, chunk):
    n = g.shape[0] // chunk
    r = lax.broadcasted_iota(jnp.int32, (chunk, chunk), 0)
    c = lax.broadcasted_iota(jnp.int32, (chunk, chunk), 1)
    tri = jnp.where(c <= r, 1.0, 0.0).astype(BF16)
    hi, lo = _split_bf16(jnp.concatenate([g[j * chunk:(j + 1) * chunk] for j in range(n)], axis=1))
    wide = _dot(tri, hi) + _dot(tri, lo)
    width = g.shape[1]
    parts = [wide[:, j * width:(j + 1) * width] for j in range(n)]
    return jnp.concatenate(parts, axis=0), [p[chunk - 1:chunk, :] for p in parts]


def _chunk_cumsum(g, chunk):
    m = g.shape[0]
    r = lax.broadcasted_iota(jnp.int32, (m, m), 0)
    c = lax.broadcasted_iota(jnp.int32, (m, m), 1)
    tri = jnp.where((r // chunk == c // chunk) & (c <= r), 1.0, 0.0).astype(BF16)
    hi, lo = _split_bf16(g)
    return _dot(tri, hi) + _dot(tri, lo)


def _head_lane_mask(width, per_head):
    lane = lax.broadcasted_iota(jnp.int32, (1, width), 1)
    return [(lane // per_head) == h for h in range(GLA_HEADS)]


def _block_diag_rows(x_bf, per_head):
    r = x_bf.shape[0]
    zero = jnp.zeros((r, per_head), x_bf.dtype)
    rows = []
    for h in range(GLA_HEADS):
        rows.append(jnp.concatenate(
            [x_bf[:, h * per_head:(h + 1) * per_head] if hh == h else zero for hh in range(GLA_HEADS)], axis=1))
    return jnp.concatenate(rows, axis=0)


def _gla_post(o, og, w):
    parts = []
    for h in range(GLA_HEADS):
        oh = o[:, h * GLA_DV:(h + 1) * GLA_DV]
        parts.append(oh * lax.rsqrt(jnp.mean(oh * oh, axis=-1, keepdims=True) + EPS) * w.gnorm[...])
    on = jnp.concatenate(parts, axis=1)
    on = on * _silu(og)
    return _dot(on.astype(BF16), w.wgp[...])


def _pool_post(pooled, w):
    pb = pooled.astype(BF16)
    mixed = []
    for p in range(len(POOL_WINDOWS) // 2):
        w_pair = _pair_block_diag(w.wpg[0, 2 * p].astype(BF16), w.wpg[0, 2 * p + 1].astype(BF16))
        mixed.append(_dot(pb[:, 2 * p * POOL_GROUP_DIM:(2 * p + 2) * POOL_GROUP_DIM], w_pair))
    mixed = jnp.concatenate(mixed, axis=1)
    return _dot((mixed * w.pscale[...]).astype(BF16), w.wpp[...])


def _merge(x, y_a, y_b, sa, sb, w):
    merged = sa * y_a + sb * y_b
    return x + _dot(merged.astype(BF16), w.wout[...])


def _decay_columns(decay_row):
    return jnp.transpose(jnp.broadcast_to(decay_row, (LANES, decay_row.shape[1])))


def _pair_block_diag(a, b):
    zero = jnp.zeros(a.shape, a.dtype)
    return jnp.concatenate([jnp.concatenate([a, zero], axis=1), jnp.concatenate([zero, b], axis=1)], axis=0)


def _state_update(s_heads, kd_bf, v_bf, decay_row):
    dcol = _decay_columns(decay_row)
    out = []
    for p in range(GLA_HEADS // 2):
        upd = _dot_tn(kd_bf[:, 2 * p * GLA_DK:(2 * p + 2) * GLA_DK], v_bf[:, 2 * p * GLA_DV:(2 * p + 2) * GLA_DV])
        for j in range(2):
            h = 2 * p + j
            rows = slice(h * GLA_DK, (h + 1) * GLA_DK)
            out.append(dcol[rows] * s_heads[h] + upd[j * GLA_DK:(j + 1) * GLA_DK, j * GLA_DV:(j + 1) * GLA_DV])
    return out


def _stage_weights(wint_hbm, wpp_hbm, wgp_hbm, wout_hbm, wgk_ref, w, stage, zr_stage, sem, zr_sem):
    plan = []
    for r in range(0, MAIN_W, STAGE_ROWS):
        if r < POOL_WIDTH:
            col = U_COL + r
        elif r < POOL_WIDTH + 2 * GLA_KW:
            col = QK_COL + r - POOL_WIDTH
        else:
            col = VOG_COL + r - (POOL_WIDTH + 2 * GLA_KW)
        plan.append((wint_hbm, r, w.wmain, col, True))
    for r in range(0, 2 * D_MODEL, STAGE_ROWS):
        plan.append((wint_hbm, GAB_LO + r, w.wgab, r, True))
    for src, dst, n_rows in ((wpp_hbm, w.wpp, POOL_WIDTH), (wgp_hbm, w.wgp, GLA_VW), (wout_hbm, w.wout, D_MODEL)):
        for r in range(0, n_rows, STAGE_ROWS):
            plan.append((src, r, dst, r, False))

    def copy(j):
        src, r0 = plan[j][0], plan[j][1]
        return pltpu.make_async_copy(src.at[0, pl.ds(r0, STAGE_ROWS), :], stage.at[j % STAGE_SLOTS], sem.at[j % STAGE_SLOTS])

    def zr_copy():
        return pltpu.make_async_copy(
            wint_hbm.at[0, pl.ds(MAIN_W, GLA_GATE_RANK), :], zr_stage.at[pl.ds(0, GLA_GATE_RANK), :], zr_sem.at[0])

    for j in range(STAGE_SLOTS):
        copy(j).start()
    zr_copy().start()

    w.wgk[...] = jnp.zeros(w.wgk.shape, BF16)
    w.wgk[0:GLA_GATE_RANK, :] = wgk_ref[0].astype(BF16)
    zr_stage[GLA_GATE_RANK:, :] = jnp.zeros((LANES - GLA_GATE_RANK, D_MODEL), F32)

    for j in range(len(plan)):
        copy(j).wait()
        _, _, dst, d0, transposed = plan[j]
        slab = stage[j % STAGE_SLOTS]
        if transposed:
            dst[:, d0:d0 + STAGE_ROWS] = jnp.transpose(slab.astype(BF16))
        else:
            dst[d0:d0 + STAGE_ROWS, :] = slab.astype(BF16)
        if j + STAGE_SLOTS < len(plan):
            copy(j + STAGE_SLOTS).start()
    zr_copy().wait()
    w.wmain[:, ZR_COL:MAIN_COLS] = jnp.transpose(zr_stage[...]).astype(BF16)


def _prompt_tile(b_first, b_idx, t_idx, n_t, x_ref, meta_ref, w, x2_ref, pbuf_ref, sout_ref,
                 ext_ref, lvl_ref, s_ref, meta_tail_ref, meta_s_ref):
    tile = x_ref.shape[1]
    n_chunks = tile // GLA_CHUNK

    @pl.when(b_first)
    def _():
        _, u, _, k, v, _, z = _in_proj(meta_ref[...], w)
        meta_tail_ref[...] = u
        g = _log_sigmoid(z) * (1.0 / GLA_TAU)
        b = _chunk_cumsum(g, N_META)
        b_last = b[N_META - 1:N_META, :]
        kd = k * jnp.exp(b_last - b)
        zero_s = [jnp.zeros((GLA_DK, GLA_DV), F32)] * GLA_HEADS
        s_new = _state_update(zero_s, kd.astype(BF16), v.astype(BF16), jnp.exp(b_last))
        for hd in range(GLA_HEADS):
            meta_s_ref[hd] = s_new[hd]

    @pl.when(t_idx == 0)
    def _():
        ext_ref[0:POOL_PAD, :] = jnp.zeros((POOL_PAD, POOL_WIDTH), F32)
        lvl_ref[:, 0:POOL_PAD, :] = jnp.zeros((lvl_ref.shape[0], POOL_PAD, POOL_WIDTH), F32)
        ext_ref[POOL_PAD:POOL_PAD + TAIL_ROWS, :] = meta_tail_ref[...]
        s_ref[...] = meta_s_ref[...]

    x = x_ref[0]
    xn = _rms_split(x, w.gmix[...])
    h, r = xn
    gate_cols = 2 * D_MODEL // n_chunks
    gate_parts = [None] * n_chunks

    def gate_slice(c):
        gate_parts[c] = _sigmoid(_gate_proj(xn, w, c * gate_cols, (c + 1) * gate_cols))

    qkz = _dot(h, w.wmain[:, QK_COL:MAIN_COLS])
    u = _dot(h, w.wmain[:, U_COL:VOG_COL]) * r
    v = _dot(h, w.wmain[:, VOG_COL:VOG_COL + GLA_VW]) * r
    q = qkz[:, :GLA_KW] * (r * (GLA_DK ** -0.5))
    k = qkz[:, GLA_KW:2 * GLA_KW] * r
    zr = qkz[:, 2 * GLA_KW:] * r
    z = _dot(zr.astype(BF16), w.wgk[...]) + w.bgk[...]
    gate_slice(0)
    g = _log_sigmoid(z) * (1.0 / GLA_TAU)
    b, b_last_rows = _chunk_cumsum_wide(g, GLA_CHUNK)
    gate_slice(1)
    og_parts = []

    base = POOL_PAD + TAIL_ROWS
    span = TAIL_ROWS + tile
    ext_ref[base:base + tile, :] = u
    cur = ext_ref[POOL_PAD:POOL_PAD + span, :]
    pooled = []
    for gi, win in enumerate(POOL_WINDOWS):
        shift = win // 2
        lo = gi * POOL_GROUP_DIM
        prev_ref = ext_ref if gi == 0 else lvl_ref.at[gi - 1]
        cur = cur[:, (POOL_GROUP_DIM if gi else 0):] + prev_ref[POOL_PAD - shift:POOL_PAD - shift + span, lo:]
        pooled.append(cur[TAIL_ROWS:, 0:POOL_GROUP_DIM] * (1.0 / win) - u[:, lo:lo + POOL_GROUP_DIM])
        if gi + 1 < len(POOL_WINDOWS):
            lvl_ref[gi, POOL_PAD:POOL_PAD + span, lo:] = cur
    y_a = _pool_post(jnp.concatenate(pooled, axis=1), w)
    ext_ref[POOL_PAD:base, :] = ext_ref[POOL_PAD + tile:base + tile, :]

    b_last = jnp.concatenate([jnp.broadcast_to(r, (GLA_CHUNK, GLA_KW)) for r in b_last_rows], axis=0)
    qe = (q * jnp.exp(b)).astype(BF16)
    ke = k * jnp.exp(-b)
    kd = (k * jnp.exp(b_last - b)).astype(BF16)
    v_bf = v.astype(BF16)

    k_masks = _head_lane_mask(GLA_KW, GLA_DK)
    row_i = lax.broadcasted_iota(jnp.int32, (GLA_CHUNK, GLA_KW), 0)
    col_j = lax.broadcasted_iota(jnp.int32, (GLA_CHUNK, GLA_KW), 1) % GLA_CHUNK
    causal = col_j <= row_i

    s_heads = [s_ref[hd] for hd in range(GLA_HEADS)]
    o_chunks = []
    for c in range(n_chunks):
        rows = slice(c * GLA_CHUNK, (c + 1) * GLA_CHUNK)
        ke_c = ke[rows]
        ke_bd = jnp.concatenate([jnp.where(k_masks[hd], ke_c, 0.0) for hd in range(GLA_HEADS)], axis=0).astype(BF16)
        att = jnp.where(causal, _dot_nt(qe[rows], ke_bd), 0.0).astype(BF16)
        if c + 2 < n_chunks:
            gate_slice(c + 2)
        else:
            og_lo = VOG_COL + GLA_VW + len(og_parts) * (GLA_VW // 2)
            og_parts.append(_dot(h, w.wmain[:, og_lo:og_lo + GLA_VW // 2]) * r)
        o_pairs = []
        for p in range(GLA_HEADS // 2):
            h0, h1 = 2 * p, 2 * p + 1
            lanes_k = slice(h0 * GLA_DK, (h1 + 1) * GLA_DK)
            lanes_j = slice(h0 * GLA_CHUNK, (h1 + 1) * GLA_CHUNK)
            v0 = v_bf[rows, h0 * GLA_DV:(h0 + 1) * GLA_DV]
            v1 = v_bf[rows, h1 * GLA_DV:(h1 + 1) * GLA_DV]
            rhs = jnp.concatenate([_pair_block_diag(s_heads[h0].astype(BF16), s_heads[h1].astype(BF16)),
                                   _pair_block_diag(v0, v1)], axis=0)
            o_pairs.append(_dot(jnp.concatenate([qe[rows, lanes_k], att[:, lanes_j]], axis=1), rhs))
        o_chunks.append(jnp.concatenate(o_pairs, axis=1))
        s_heads = _state_update(s_heads, kd[rows], v_bf[rows], jnp.exp(b_last_rows[c]))
    for hd in range(GLA_HEADS):
        s_ref[hd] = s_heads[hd]

    y_b = _gla_post(jnp.concatenate(o_chunks, axis=0), jnp.concatenate(og_parts, axis=1), w)
    sg = jnp.concatenate(gate_parts, axis=1)
    x2_ref[0] = _merge(x, y_a, y_b, sg[:, :D_MODEL], sg[:, D_MODEL:], w)

    @pl.when(t_idx == n_t - 1)
    def _():
        sout_ref[0, 0] = s_ref[...]

    for bb in range(pbuf_ref.shape[2]):
        @pl.when((t_idx == n_t - 1) & (b_idx == bb))
        def _():
            for r in range(POOL_BUF):
                row = base - POOL_BUF + r
                pbuf_ref[0, r, bb:bb + 1, :] = ext_ref[row:row + 1, :]


def _sample_block(x_ref, pool_ref, sin_ref, w, x2_ref, pbuf_ref, sout_ref,
                  xs_ref, us_ref, pooled_ref, qm_ref, kdx_ref, rhs_ref, oi_ref):
    nb, seq, _ = x_ref.shape
    rows_pb = SAMPLE_ROWS
    m = nb * rows_pb

    xs_ref[:, seq:, :] = jnp.zeros((nb, rows_pb - seq, D_MODEL), F32)
    xs_ref[:, 0:seq, :] = x_ref[...]
    x = xs_ref[...].reshape(m, D_MODEL)
    xn, u, q, k, v, og, z = _in_proj(x, w)

    gate_cols = 2 * D_MODEL // SAMPLE_GATE_SLICES
    gate_parts = [_sigmoid(_gate_proj(xn, w, c * gate_cols, (c + 1) * gate_cols))
                  for c in range(SAMPLE_EARLY_GATE_SLICES)]

    pooled_ref[...] = jnp.zeros(pooled_ref.shape, F32)
    for gi, win in enumerate(POOL_WINDOWS):
        cols = slice(gi * POOL_GROUP_DIM, (gi + 1) * POOL_GROUP_DIM)
        us_ref[gi] = u[:, cols]
        tok = [us_ref[gi, pl.ds(t, nb, stride=rows_pb), :] for t in range(seq)]
        hist = [pool_ref[0, r, :, cols] for r in range(POOL_BUF)] + tok
        for t in range(seq):
            acc = tok[t]
            for n in range(1, win):
                acc = acc + hist[POOL_BUF + t - n]
            pooled_ref[gi, pl.ds(t, nb, stride=rows_pb), :] = acc * (1.0 / win) - tok[t]
        for r in range(POOL_BUF):
            pbuf_ref[0, r, :, cols] = hist[seq + r]

    r8 = lax.broadcasted_iota(jnp.int32, (m, 1), 0) % rows_pb
    g = jnp.where(r8 < seq, _log_sigmoid(z) * (1.0 / GLA_TAU), 0.0)
    b = _chunk_cumsum(g, rows_pb)
    b3 = b.reshape(nb, rows_pb, GLA_KW)
    b_last = jnp.broadcast_to(b3[:, seq - 1:seq, :], b3.shape).reshape(m, GLA_KW)
    qe = q * jnp.exp(b)
    ke = k * jnp.exp(-b)
    kd = k * jnp.exp(b_last - b)
    decay = jnp.exp(b_last)
    v_bf = v.astype(BF16)

    k_masks = _head_lane_mask(GLA_KW, GLA_DK)
    ke_bd = jnp.concatenate([jnp.where(k_masks[hd], ke, 0.0) for hd in range(GLA_HEADS)], axis=0).astype(BF16)
    row_i = lax.broadcasted_iota(jnp.int32, (m, GLA_HEADS * m), 0)
    col_j = lax.broadcasted_iota(jnp.int32, (m, GLA_HEADS * m), 1) % m
    keep = (row_i // rows_pb == col_j // rows_pb) & (col_j <= row_i)
    att = jnp.where(keep, _dot_nt(qe.astype(BF16), ke_bd), 0.0).astype(BF16)
    o_intra = _dot(att, _block_diag_rows(v_bf, GLA_DV))

    qe3 = qe.reshape(nb, rows_pb, GLA_KW)
    qm_ref[...] = jnp.concatenate([jnp.where(k_masks[hd], qe3, 0.0) for hd in range(GLA_HEADS)], axis=1).astype(BF16)
    d_hi = decay.astype(BF16).astype(F32)
    d_lo = decay - d_hi
    kdx = jnp.where(r8 == seq, d_hi, jnp.where(r8 == seq + 1, d_lo, kd))
    kdx_ref[...] = kdx.reshape(nb, rows_pb, GLA_KW)
    ones_rows = jnp.where((r8 == seq) | (r8 == seq + 1), 1.0, 0.0) + jnp.zeros((m, GLA_DV), F32)
    rhs = jnp.concatenate(
        [piece for hd in range(GLA_HEADS) for piece in (v[:, hd * GLA_DV:(hd + 1) * GLA_DV], ones_rows)], axis=1)
    rhs_ref[...] = rhs.reshape(nb, rows_pb, 2 * GLA_VW)

    def per_batch(i):
        s_all = sin_ref[0, i]
        s_flat = s_all.reshape(GLA_KW, GLA_DV).astype(BF16)
        oi_ref[i] = _dot(qm_ref[i], s_flat)
        kdt = jnp.transpose(kdx_ref[i]).astype(BF16)
        rhs_i = rhs_ref[i].astype(BF16)
        for hd in range(GLA_HEADS):
            r = _dot(kdt[hd * GLA_DK:(hd + 1) * GLA_DK], rhs_i[:, hd * 2 * GLA_DV:(hd + 1) * 2 * GLA_DV])
            sout_ref[0, i, hd] = r[:, GLA_DV:] * s_all[hd] + r[:, :GLA_DV]

    late = SAMPLE_GATE_SLICES - SAMPLE_EARLY_GATE_SLICES
    slice_at = {(j * nb) // late: SAMPLE_EARLY_GATE_SLICES + j for j in range(late)}
    for i in range(nb):
        if i in slice_at:
            c = slice_at[i]
            gate_parts.append(_sigmoid(_gate_proj(xn, w, c * gate_cols, (c + 1) * gate_cols)))
        per_batch(i)

    oi = oi_ref[...]
    o_inter = jnp.concatenate([oi[:, hd * rows_pb:(hd + 1) * rows_pb, :] for hd in range(GLA_HEADS)], axis=2)
    o = o_intra + o_inter.reshape(m, GLA_VW)
    y_a = _pool_post(jnp.concatenate([pooled_ref[gi] for gi in range(len(POOL_WINDOWS))], axis=1), w)
    y_b = _gla_post(o, og, w)
    sg = jnp.concatenate(gate_parts, axis=1)
    x2 = _merge(x, y_a, y_b, sg[:, :D_MODEL], sg[:, D_MODEL:], w).reshape(nb, rows_pb, D_MODEL)
    for bi in range(nb):
        x2_ref[bi * seq:(bi + 1) * seq, :] = x2[bi, 0:seq, :]


def _mixer_kernel(n_s, n_t,
                  xs_in, pool_in, s_in, xp_in, meta_ref, gmix_ref, bgk_ref, pscale_ref, gnorm_ref, wgk_ref, wpg_ref,
                  wint_hbm, wpp_hbm, wgp_hbm, wout_hbm, wfi_in, wfo_in,
                  x2s_out, pools_out, ss_out, x2p_out, poolp_out, sp_out, wfi_out, wfo_out,
                  wmain_s, wgab_s, wgk_s, wpp_s, wgp_s, wout_s, stage, zr_stage, sem, zr_sem,
                  ext_ref, lvl_ref, s_ref, meta_tail_ref, meta_s_ref,
                  xs_ref, us_ref, pooled_ref, qm_ref, kdx_ref, rhs_ref, oi_ref):
    i = pl.program_id(0)
    w = _Weights(gmix_ref, bgk_ref, pscale_ref, gnorm_ref, wpg_ref,
                 wmain_s, wgab_s, wgk_s, wpp_s, wgp_s, wout_s)

    @pl.when(i == 0)
    def _():
        _stage_weights(wint_hbm, wpp_hbm, wgp_hbm, wout_hbm, wgk_ref, w, stage, zr_stage, sem, zr_sem)

    wfi_bf = wfi_in[0].astype(BF16)
    for kc in range(wfi_out.shape[0]):
        cols = slice(kc * FFN_CHUNK, (kc + 1) * FFN_CHUNK)
        wfi_out[kc] = jnp.concatenate([wfi_bf[:, cols], wfi_bf[:, D_FF + kc * FFN_CHUNK:D_FF + (kc + 1) * FFN_CHUNK]],
                                      axis=1)
    wfo_out[...] = wfo_in[0].astype(BF16)

    @pl.when(i < n_s)
    def _():
        _sample_block(xs_in, pool_in, s_in, w, x2s_out, pools_out, ss_out,
                      xs_ref, us_ref, pooled_ref, qm_ref, kdx_ref, rhs_ref, oi_ref)

    @pl.when(i >= n_s)
    def _():
        t_idx = (i - n_s) % n_t
        _prompt_tile(i == n_s, (i - n_s) // n_t, t_idx, n_t, xp_in, meta_ref, w, x2p_out, poolp_out, sp_out,
                     ext_ref, lvl_ref, s_ref, meta_tail_ref, meta_s_ref)


def _ffn_weight_copies(wi_hbm, wo_hbm, wi_s, wo_s, sem):
    copies = []
    for c in range(D_FF // FFN_CHUNK):
        rows = pl.ds(c * FFN_CHUNK, FFN_CHUNK)
        copies.append((
            pltpu.make_async_copy(wi_hbm.at[c], wi_s.at[c], sem.at[2 * c]),
            pltpu.make_async_copy(wo_hbm.at[rows, :], wo_s.at[rows, :], sem.at[2 * c + 1])))
    return copies


def _ffn_tile(x, gffn_ref, wi_s, wo_s, gfin_ref, before_chunk=None):
    n_chunks = D_FF // FFN_CHUNK
    h, r = _rms_split(x, gffn_ref[...])
    acc = x
    group = []
    for c in range(n_chunks):
        if before_chunk is not None:
            before_chunk(c)
        gate_up = _dot(h, wi_s[c]) * r
        group.append((_silu(gate_up[:, :FFN_CHUNK]) * gate_up[:, FFN_CHUNK:]).astype(BF16))
        if c + 1 == n_chunks:
            first = (c + 1 - len(group)) * FFN_CHUNK
            act = jnp.concatenate(group, axis=1)
            out = []
            for r0 in range(0, x.shape[0], FFN_TAIL_ROWS):
                rows = slice(r0, r0 + FFN_TAIL_ROWS)
                out.append(_rms(acc[rows] + _dot(act[rows], wo_s[first:(c + 1) * FFN_CHUNK, :]), gfin_ref[...]))
            return jnp.concatenate(out, axis=0)
        if len(group) == FFN_OUT_GROUP:
            first = (c + 1 - len(group)) * FFN_CHUNK
            acc = acc + _dot(jnp.concatenate(group, axis=1), wo_s[first:(c + 1) * FFN_CHUNK, :])
            group = []


def _ffn_kernel(seq, xp_ref, xs_ref, gffn_ref, wi_hbm, wo_hbm, gfin_ref, yp_ref, ys_ref, wi_s, wo_s, sem):
    i = pl.program_id(0)

    @pl.when(i == 0)
    def _():
        copies = _ffn_weight_copies(wi_hbm, wo_hbm, wi_s, wo_s, sem)
        for chunk_copies in copies:
            for cp in chunk_copies:
                cp.start()

        run_at = {0: 1, 1: 2, 3: 4, 7: len(copies) - 7}

        def wait_chunk(c):
            for cc in range(c, c + run_at.get(c, 0)):
                for cp in copies[cc]:
                    cp.wait()

        y = _ffn_tile(xs_ref[...], gffn_ref, wi_s, wo_s, gfin_ref, before_chunk=wait_chunk)
        for bi in range(ys_ref.shape[0]):
            ys_ref[bi] = y[bi * seq:(bi + 1) * seq, :]

    @pl.when(i > 0)
    def _():
        yp_ref[0] = _ffn_tile(xp_ref[0], gffn_ref, wi_s, wo_s, gfin_ref)


def _const_spec(shape):
    zeros = (0,) * len(shape)
    return pl.BlockSpec(shape, lambda *_: zeros, pipeline_mode=pl.Buffered(1))


def kernel(x_prompt, x_sample, state_pool, state_gla, meta_tokens, g_mix, w_in, w_gk_up, b_gk, w_pool_group,
           pool_scale, w_pool_proj, g_gla_norm, w_gla_proj, w_out, g_ffn, w_ffn_in, w_ffn_out, g_final):
    depth = w_in.shape[0]
    assert depth == 1, "single-layer trunk only"
    bp, tp, d = x_prompt.shape
    bs, ts, _ = x_sample.shape
    nbb = SAMPLE_BATCH_BLOCK
    assert d == D_MODEL and w_in.shape == (1, D_MODEL, IN_DIM) and meta_tokens.shape == (N_META, D_MODEL)
    assert tp % PROMPT_TILE == 0 and tp % FFN_TILE == 0 and PROMPT_TILE % GLA_CHUNK == 0
    assert bs % nbb == 0 and ts + 2 <= SAMPLE_ROWS and bs * ts == FFN_TILE
    n_t = tp // PROMPT_TILE
    n_p = bp * n_t
    n_s = bs // nbb
    n_fi = D_MODEL // FFN_IN_CAST_ROWS
    n_fo = D_FF // FFN_OUT_CAST_ROWS
    assert n_fi <= n_p and n_fo <= n_p

    def s_idx(i):
        return jnp.minimum(i, n_s - 1)

    def p_idx(i):
        return jnp.maximum(i - n_s, 0)

    w_in_t = jnp.transpose(w_in, (0, 2, 1))
    pool_hist = jnp.transpose(state_pool, (0, 2, 1, 3))
    small = (meta_tokens, g_mix, b_gk, pool_scale, g_gla_norm, w_gk_up, w_pool_group)
    hbm = pl.BlockSpec(memory_space=pl.ANY)
    in_specs = (
        [pl.BlockSpec((nbb, ts, d), lambda i: (s_idx(i), 0, 0)),
         pl.BlockSpec((1, POOL_BUF, nbb, POOL_WIDTH), lambda i: (0, 0, s_idx(i), 0)),
         pl.BlockSpec((1, nbb, GLA_HEADS, GLA_DK, GLA_DV), lambda i: (0, s_idx(i), 0, 0, 0)),
         pl.BlockSpec((1, PROMPT_TILE, d), lambda i: (p_idx(i) // n_t, p_idx(i) % n_t, 0))]
        + [_const_spec(a.shape) for a in small]
        + [hbm, hbm, hbm, hbm,
           pl.BlockSpec((1, FFN_IN_CAST_ROWS, 2 * D_FF), lambda i: (0, jnp.minimum(p_idx(i), n_fi - 1), 0)),
           pl.BlockSpec((1, FFN_OUT_CAST_ROWS, d), lambda i: (0, jnp.minimum(p_idx(i), n_fo - 1), 0))])
    out_specs = [
        pl.BlockSpec((nbb * ts, d), lambda i: (s_idx(i), 0)),
        pl.BlockSpec((1, POOL_BUF, nbb, POOL_WIDTH), lambda i: (0, 0, s_idx(i), 0)),
        pl.BlockSpec((1, nbb, GLA_HEADS, GLA_DK, GLA_DV), lambda i: (0, s_idx(i), 0, 0, 0)),
        pl.BlockSpec((1, PROMPT_TILE, d), lambda i: (p_idx(i) // n_t, p_idx(i) % n_t, 0)),
        pl.BlockSpec((1, POOL_BUF, bp, POOL_WIDTH), lambda i: (0, 0, 0, 0)),
        pl.BlockSpec((1, 1, GLA_HEADS, GLA_DK, GLA_DV), lambda i: (0, p_idx(i) // n_t, 0, 0, 0)),
        pl.BlockSpec((D_FF // FFN_CHUNK, FFN_IN_CAST_ROWS, 2 * FFN_CHUNK),
                     lambda i: (0, jnp.minimum(p_idx(i), n_fi - 1), 0)),
        pl.BlockSpec((FFN_OUT_CAST_ROWS, d), lambda i: (jnp.minimum(p_idx(i), n_fo - 1), 0))]
    out_shape = [
        jax.ShapeDtypeStruct((bs * ts, d), F32),
        jax.ShapeDtypeStruct((1, POOL_BUF, bs, POOL_WIDTH), F32),
        jax.ShapeDtypeStruct(state_gla.shape, F32),
        jax.ShapeDtypeStruct(x_prompt.shape, F32),
        jax.ShapeDtypeStruct((1, POOL_BUF, bp, POOL_WIDTH), F32),
        jax.ShapeDtypeStruct((1, bp, GLA_HEADS, GLA_DK, GLA_DV), F32),
        jax.ShapeDtypeStruct((D_FF // FFN_CHUNK, D_MODEL, 2 * FFN_CHUNK), BF16),
        jax.ShapeDtypeStruct((D_FF, D_MODEL), BF16)]
    scratch_shapes = [
        pltpu.VMEM((D_MODEL, MAIN_COLS), BF16),
        pltpu.VMEM((D_MODEL, 2 * D_MODEL), BF16), pltpu.VMEM((LANES, GLA_KW), BF16),
        pltpu.VMEM((POOL_WIDTH, D_MODEL), BF16), pltpu.VMEM((GLA_VW, D_MODEL), BF16),
        pltpu.VMEM((D_MODEL, D_MODEL), BF16),
        pltpu.VMEM((STAGE_SLOTS, STAGE_ROWS, D_MODEL), F32), pltpu.VMEM((LANES, D_MODEL), F32),
        pltpu.SemaphoreType.DMA((STAGE_SLOTS,)), pltpu.SemaphoreType.DMA((1,)),
        pltpu.VMEM((POOL_PAD + TAIL_ROWS + PROMPT_TILE, POOL_WIDTH), F32),
        pltpu.VMEM((len(POOL_WINDOWS) - 1, POOL_PAD + TAIL_ROWS + PROMPT_TILE, POOL_WIDTH), F32),
        pltpu.VMEM((GLA_HEADS, GLA_DK, GLA_DV), F32),
        pltpu.VMEM((N_META, POOL_WIDTH), F32),
        pltpu.VMEM((GLA_HEADS, GLA_DK, GLA_DV), F32),
        pltpu.VMEM((nbb, SAMPLE_ROWS, D_MODEL), F32),
        pltpu.VMEM((len(POOL_WINDOWS), nbb * SAMPLE_ROWS, POOL_GROUP_DIM), F32),
        pltpu.VMEM((len(POOL_WINDOWS), nbb * SAMPLE_ROWS, POOL_GROUP_DIM), F32),
        pltpu.VMEM((nbb, GLA_HEADS * SAMPLE_ROWS, GLA_KW), BF16),
        pltpu.VMEM((nbb, SAMPLE_ROWS, GLA_KW), F32),
        pltpu.VMEM((nbb, SAMPLE_ROWS, 2 * GLA_VW), F32),
        pltpu.VMEM((nbb, GLA_HEADS * SAMPLE_ROWS, GLA_DV), F32)]

    def mixer(*refs):
        _mixer_kernel(n_s, n_t, *refs)

    x2_s, pool_s, gla_s, x2_p, pool_p, gla_p, wfi_bf, wfo_bf = pl.pallas_call(
        mixer,
        grid=(n_s + n_p,),
        in_specs=in_specs,
        out_specs=out_specs,
        out_shape=out_shape,
        scratch_shapes=scratch_shapes,
        compiler_params=pltpu.CompilerParams(dimension_semantics=("arbitrary",), vmem_limit_bytes=VMEM_LIMIT),
        name="mixer",
    )(x_sample, pool_hist, state_gla, x_prompt, *small, w_in_t, w_pool_proj, w_gla_proj, w_out, w_ffn_in, w_ffn_out)

    def ffn(*refs):
        _ffn_kernel(ts, *refs)

    n_tf = tp // FFN_TILE

    def prompt_tile_index(i):
        tile_id = jnp.maximum(i - 1, 0)
        return (tile_id // n_tf, tile_id % n_tf, 0)

    ffn_args = (g_ffn, wfi_bf, wfo_bf, g_final.reshape(1, D_MODEL))
    y_prompt, y_sample = pl.pallas_call(
        ffn,
        grid=(bp * n_tf + 1,),
        in_specs=[pl.BlockSpec((1, FFN_TILE, d), prompt_tile_index),
                  _const_spec(x2_s.shape), _const_spec(g_ffn.shape), hbm, hbm, _const_spec((1, D_MODEL))],
        out_specs=[pl.BlockSpec((1, FFN_TILE, d), prompt_tile_index),
                   pl.BlockSpec(x_sample.shape, lambda i: (0, 0, 0))],
        out_shape=[jax.ShapeDtypeStruct(x_prompt.shape, F32), jax.ShapeDtypeStruct(x_sample.shape, F32)],
        scratch_shapes=[pltpu.VMEM(wfi_bf.shape, BF16), pltpu.VMEM(wfo_bf.shape, BF16),
                        pltpu.SemaphoreType.DMA((2 * (D_FF // FFN_CHUNK),))],
        compiler_params=pltpu.CompilerParams(dimension_semantics=("arbitrary",), vmem_limit_bytes=VMEM_LIMIT),
        name="ffn",
    )(x2_p, x2_s, *ffn_args)
    pool_p = jnp.transpose(pool_p, (0, 2, 1, 3))
    pool_s = jnp.transpose(pool_s, (0, 2, 1, 3))
    return y_prompt, y_sample, pool_p, gla_p, pool_s, gla_s
```

```python
import jax
import jax.numpy as jnp
from jax import lax
from jax.experimental import pallas as pl
from jax.experimental.pallas import tpu as pltpu

F32 = jnp.float32
BF16 = jnp.bfloat16

D_MODEL = 1024
N_META = 16
POOL_WIDTH = 512
POOL_WINDOWS = (2, 4, 8, 16)
POOL_GROUP_DIM = 128
POOL_BUF = 15
GLA_HEADS = 4
GLA_DV = 128
GLA_DK = 64
GLA_KW = GLA_HEADS * GLA_DK
GLA_VW = GLA_HEADS * GLA_DV
GLA_GATE_RANK = 16
GLA_TAU = 16.0
GLA_CHUNK = 64
D_FF = 2816
EPS = 1e-6

LANES = 128
SUBLANES = 8
MAIN_W = POOL_WIDTH + 2 * GLA_KW + 2 * GLA_VW
U_COL, VOG_COL, QK_COL, ZR_COL = 0, POOL_WIDTH, POOL_WIDTH + 2 * GLA_VW, MAIN_W
MAIN_COLS = MAIN_W + LANES
GAB_LO = MAIN_W + GLA_GATE_RANK
IN_DIM = GAB_LO + 2 * D_MODEL
TAIL_ROWS = 16
POOL_PAD = SUBLANES

PROMPT_TILE = 512
FFN_TILE = 512
FFN_CHUNK = 256
FFN_OUT_GROUP = 4
FFN_TAIL_ROWS = 256
SAMPLE_BATCH_BLOCK = 16
SAMPLE_ROWS = SUBLANES
SAMPLE_GATE_SLICES = 8
SAMPLE_EARLY_GATE_SLICES = 2
STAGE_ROWS = 256
STAGE_SLOTS = 8
FFN_IN_CAST_ROWS = 32
FFN_OUT_CAST_ROWS = 128
V7X_VMEM_BYTES = 64 * 1024 * 1024
VMEM_RESERVE = 4 * 1024 * 1024
VMEM_LIMIT = V7X_VMEM_BYTES - VMEM_RESERVE


def _dot(a, b):
    return jnp.dot(a, b, preferred_element_type=F32)


def _dot_nt(a, b):
    return lax.dot_general(a, b, (((1,), (1,)), ((), ())), preferred_element_type=F32)


def _dot_tn(a, b):
    return lax.dot_general(a, b, (((0,), (0,)), ((), ())), preferred_element_type=F32)


def _rms(x, g):
    return x * lax.rsqrt(jnp.mean(x * x, axis=-1, keepdims=True) + EPS) * g


def _rms_split(x, g):
    r = lax.rsqrt(jnp.mean(x * x, axis=-1, keepdims=True) + EPS)
    return (x * g).astype(BF16), r


def _sigmoid(x):
    return 0.5 * jnp.tanh(0.5 * x) + 0.5


def _silu(x):
    half = 0.5 * x
    return half * jnp.tanh(half) + half


def _log_sigmoid(x):
    return jnp.minimum(x, 0.0) - jnp.log(1.0 + jnp.exp(-jnp.abs(x)))


def _split_bf16(x):
    hi = x.astype(BF16)
    lo = (x - hi.astype(F32)).astype(BF16)
    return hi, lo


class _Weights:
    def __init__(self, gmix, bgk, pscale, gnorm, wpg, wmain, wgab, wgk, wpp, wgp, wout):
        self.gmix, self.bgk, self.pscale, self.gnorm, self.wpg = gmix, bgk, pscale, gnorm, wpg
        self.wmain, self.wgab, self.wgk = wmain, wgab, wgk
        self.wpp, self.wgp, self.wout = wpp, wgp, wout


def _in_proj(x, w):
    h, r = _rms_split(x, w.gmix[...])
    qkz = _dot(h, w.wmain[:, QK_COL:MAIN_COLS])
    q = qkz[:, :GLA_KW] * (r * (GLA_DK ** -0.5))
    k = qkz[:, GLA_KW:2 * GLA_KW] * r
    zr = qkz[:, 2 * GLA_KW:] * r
    z = _dot(zr.astype(BF16), w.wgk[...]) + w.bgk[...]
    u = _dot(h, w.wmain[:, U_COL:VOG_COL]) * r
    vog = _dot(h, w.wmain[:, VOG_COL:QK_COL]) * r
    v = vog[:, :GLA_VW]
    og = vog[:, GLA_VW:]
    return (h, r), u, q, k, v, og, z


def _gate_proj(xn, w, lo, hi):
    h, r = xn
    return _dot(h, w.wgab[:, lo:hi]) * r


def _chunk_cumsum_wide(g, chunk):
    n = g.shape[0] // chunk
    r = lax.broadcasted_iota(jnp.int32, (chunk, chunk), 0)
    c = lax.broadcasted_iota(jnp.int32, (chunk, chunk), 1)
    tri = jnp.where(c <= r, 1.0, 0.0).astype(BF16)
    hi, lo = _split_bf16(jnp.concatenate([g[j * chunk:(j + 1) * chunk] for j in range(n)], axis=1))
    wide = _dot(tri, hi) + _dot(tri, lo)
    width = g.shape[1]
    parts = [wide[:, j * width:(j + 1) * width] for j in range(n)]
    return jnp.concatenate(parts, axis=0), [p[chunk - 1:chunk, :] for p in parts]


def _chunk_cumsum(g, chunk):
    m = g.shape[0]
    r = lax.broadcasted_iota(jnp.int32, (m, m), 0)
    c = lax.broadcasted_iota(jnp.int32, (m, m), 1)
    tri = jnp.where((r // chunk == c // chunk) & (c <= r), 1.0, 0.0).astype(BF16)
    hi, lo = _split_bf16(g)
    return _dot(tri, hi) + _dot(tri, lo)


def _head_lane_mask(width, per_head):
    lane = lax.broadcasted_iota(jnp.int32, (1, width), 1)
    return [(lane // per_head) == h for h in range(GLA_HEADS)]


def _block_diag_rows(x_bf, per_head):
    r = x_bf.shape[0]
    zero = jnp.zeros((r, per_head), x_bf.dtype)
    rows = []
    for h in range(GLA_HEADS):
        rows.append(jnp.concatenate(
            [x_bf[:, h * per_head:(h + 1) * per_head] if hh == h else zero for hh in range(GLA_HEADS)], axis=1))
    return jnp.concatenate(rows, axis=0)


def _gla_post(o, og, w):
    parts = []
    for h in range(GLA_HEADS):
        oh = o[:, h * GLA_DV:(h + 1) * GLA_DV]
        parts.append(oh * lax.rsqrt(jnp.mean(oh * oh, axis=-1, keepdims=True) + EPS) * w.gnorm[...])
    on = jnp.concatenate(parts, axis=1)
    on = on * _silu(og)
    return _dot(on.astype(BF16), w.wgp[...])


def _pool_post(pooled, w):
    pb = pooled.astype(BF16)
    mixed = []
    for p in range(len(POOL_WINDOWS) // 2):
        w_pair = _pair_block_diag(w.wpg[0, 2 * p].astype(BF16), w.wpg[0, 2 * p + 1].astype(BF16))
        mixed.append(_dot(pb[:, 2 * p * POOL_GROUP_DIM:(2 * p + 2) * POOL_GROUP_DIM], w_pair))
    mixed = jnp.concatenate(mixed, axis=1)
    return _dot((mixed * w.pscale[...]).astype(BF16), w.wpp[...])


def _merge(x, y_a, y_b, sa, sb, w):
    merged = sa * y_a + sb * y_b
    return x + _dot(merged.astype(BF16), w.wout[...])


def _decay_columns(decay_row):
    return jnp.transpose(jnp.broadcast_to(decay_row, (LANES, decay_row.shape[1])))


def _pair_block_diag(a, b):
    zero = jnp.zeros(a.shape, a.dtype)
    return jnp.concatenate([jnp.concatenate([a, zero], axis=1), jnp.concatenate([zero, b], axis=1)], axis=0)


def _state_update(s_heads, kd_bf, v_bf, decay_row):
    dcol = _decay_columns(decay_row)
    out = []
    for p in range(GLA_HEADS // 2):
        upd = _dot_tn(kd_bf[:, 2 * p * GLA_DK:(2 * p + 2) * GLA_DK], v_bf[:, 2 * p * GLA_DV:(2 * p + 2) * GLA_DV])
        for j in range(2):
            h = 2 * p + j
            rows = slice(h * GLA_DK, (h + 1) * GLA_DK)
            out.append(dcol[rows] * s_heads[h] + upd[j * GLA_DK:(j + 1) * GLA_DK, j * GLA_DV:(j + 1) * GLA_DV])
    return out


def _stage_weights(wint_hbm, wpp_hbm, wgp_hbm, wout_hbm, wgk_ref, w, stage, zr_stage, sem, zr_sem):
    plan = []
    for r in range(0, MAIN_W, STAGE_ROWS):
        if r < POOL_WIDTH:
            col = U_COL + r
        elif r < POOL_WIDTH + 2 * GLA_KW:
            col = QK_COL + r - POOL_WIDTH
        else:
            col = VOG_COL + r - (POOL_WIDTH + 2 * GLA_KW)
        plan.append((wint_hbm, r, w.wmain, col, True))
    for r in range(0, 2 * D_MODEL, STAGE_ROWS):
        plan.append((wint_hbm, GAB_LO + r, w.wgab, r, True))
    for src, dst, n_rows in ((wpp_hbm, w.wpp, POOL_WIDTH), (wgp_hbm, w.wgp, GLA_VW), (wout_hbm, w.wout, D_MODEL)):
        for r in range(0, n_rows, STAGE_ROWS):
            plan.append((src, r, dst, r, False))

    def copy(j):
        src, r0 = plan[j][0], plan[j][1]
        return pltpu.make_async_copy(src.at[0, pl.ds(r0, STAGE_ROWS), :], stage.at[j % STAGE_SLOTS], sem.at[j % STAGE_SLOTS])

    def zr_copy():
        return pltpu.make_async_copy(
            wint_hbm.at[0, pl.ds(MAIN_W, GLA_GATE_RANK), :], zr_stage.at[pl.ds(0, GLA_GATE_RANK), :], zr_sem.at[0])

    for j in range(STAGE_SLOTS):
        copy(j).start()
    zr_copy().start()

    w.wgk[...] = jnp.zeros(w.wgk.shape, BF16)
    w.wgk[0:GLA_GATE_RANK, :] = wgk_ref[0].astype(BF16)
    zr_stage[GLA_GATE_RANK:, :] = jnp.zeros((LANES - GLA_GATE_RANK, D_MODEL), F32)

    for j in range(len(plan)):
        copy(j).wait()
        _, _, dst, d0, transposed = plan[j]
        slab = stage[j % STAGE_SLOTS]
        if transposed:
            dst[:, d0:d0 + STAGE_ROWS] = jnp.transpose(slab.astype(BF16))
        else:
            dst[d0:d0 + STAGE_ROWS, :] = slab.astype(BF16)
        if j + STAGE_SLOTS < len(plan):
            copy(j + STAGE_SLOTS).start()
    zr_copy().wait()
    w.wmain[:, ZR_COL:MAIN_COLS] = jnp.transpose(zr_stage[...]).astype(BF16)


def _prompt_tile(b_first, b_idx, t_idx, n_t, x_ref, meta_ref, w, x2_ref, pbuf_ref, sout_ref,
                 ext_ref, lvl_ref, s_ref, meta_tail_ref, meta_s_ref):
    tile = x_ref.shape[1]
    n_chunks = tile // GLA_CHUNK

    @pl.when(b_first)
    def _():
        _, u, _, k, v, _, z = _in_proj(meta_ref[...], w)
        meta_tail_ref[...] = u
        g = _log_sigmoid(z) * (1.0 / GLA_TAU)
        b = _chunk_cumsum(g, N_META)
        b_last = b[N_META - 1:N_META, :]
        kd = k * jnp.exp(b_last - b)
        zero_s = [jnp.zeros((GLA_DK, GLA_DV), F32)] * GLA_HEADS
        s_new = _state_update(zero_s, kd.astype(BF16), v.astype(BF16), jnp.exp(b_last))
        for hd in range(GLA_HEADS):
            meta_s_ref[hd] = s_new[hd]

    @pl.when(t_idx == 0)
    def _():
        ext_ref[0:POOL_PAD, :] = jnp.zeros((POOL_PAD, POOL_WIDTH), F32)
        lvl_ref[:, 0:POOL_PAD, :] = jnp.zeros((lvl_ref.shape[0], POOL_PAD, POOL_WIDTH), F32)
        ext_ref[POOL_PAD:POOL_PAD + TAIL_ROWS, :] = meta_tail_ref[...]
        s_ref[...] = meta_s_ref[...]

    x = x_ref[0]
    xn = _rms_split(x, w.gmix[...])
    h, r = xn
    gate_cols = 2 * D_MODEL // n_chunks
    gate_parts = [None] * n_chunks

    def gate_slice(c):
        gate_parts[c] = _sigmoid(_gate_proj(xn, w, c * gate_cols, (c + 1) * gate_cols))

    qkz = _dot(h, w.wmain[:, QK_COL:MAIN_COLS])
    u = _dot(h, w.wmain[:, U_COL:VOG_COL]) * r
    v = _dot(h, w.wmain[:, VOG_COL:VOG_COL + GLA_VW]) * r
    q = qkz[:, :GLA_KW] * (r * (GLA_DK ** -0.5))
    k = qkz[:, GLA_KW:2 * GLA_KW] * r
    zr = qkz[:, 2 * GLA_KW:] * r
    z = _dot(zr.astype(BF16), w.wgk[...]) + w.bgk[...]
    gate_slice(0)
    g = _log_sigmoid(z) * (1.0 / GLA_TAU)
    b, b_last_rows = _chunk_cumsum_wide(g, GLA_CHUNK)
    gate_slice(1)
    og_parts = []

    base = POOL_PAD + TAIL_ROWS
    span = TAIL_ROWS + tile
    ext_ref[base:base + tile, :] = u
    cur = ext_ref[POOL_PAD:POOL_PAD + span, :]
    pooled = []
    for gi, win in enumerate(POOL_WINDOWS):
        shift = win // 2
        lo = gi * POOL_GROUP_DIM
        prev_ref = ext_ref if gi == 0 else lvl_ref.at[gi - 1]
        cur = cur[:, (POOL_GROUP_DIM if gi else 0):] + prev_ref[POOL_PAD - shift:POOL_PAD - shift + span, lo:]
        pooled.append(cur[TAIL_ROWS:, 0:POOL_GROUP_DIM] * (1.0 / win) - u[:, lo:lo + POOL_GROUP_DIM])
        if gi + 1 < len(POOL_WINDOWS):
            lvl_ref[gi, POOL_PAD:POOL_PAD + span, lo:] = cur
    y_a = _pool_post(jnp.concatenate(pooled, axis=1), w)
    ext_ref[POOL_PAD:base, :] = ext_ref[POOL_PAD + tile:base + tile, :]

    b_last = jnp.concatenate([jnp.broadcast_to(r, (GLA_CHUNK, GLA_KW)) for r in b_last_rows], axis=0)
    qe = (q * jnp.exp(b)).astype(BF16)
    ke = k * jnp.exp(-b)
    kd = (k * jnp.exp(b_last - b)).astype(BF16)
    v_bf = v.astype(BF16)

    k_masks = _head_lane_mask(GLA_KW, GLA_DK)
    row_i = lax.broadcasted_iota(jnp.int32, (GLA_CHUNK, GLA_KW), 0)
    col_j = lax.broadcasted_iota(jnp.int32, (GLA_CHUNK, GLA_KW), 1) % GLA_CHUNK
    causal = col_j <= row_i

    s_heads = [s_ref[hd] for hd in range(GLA_HEADS)]
    o_chunks = []
    for c in range(n_chunks):
        if c + 2 < n_chunks:
            gate_slice(c + 2)
        else:
            og_lo = VOG_COL + GLA_VW + len(og_parts) * (GLA_VW // 2)
            og_parts.append(_dot(h, w.wmain[:, og_lo:og_lo + GLA_VW // 2]) * r)
        rows = slice(c * GLA_CHUNK, (c + 1) * GLA_CHUNK)
        ke_c = ke[rows]
        ke_bd = jnp.concatenate([jnp.where(k_masks[hd], ke_c, 0.0) for hd in range(GLA_HEADS)], axis=0).astype(BF16)
        att = jnp.where(causal, _dot_nt(qe[rows], ke_bd), 0.0).astype(BF16)
        o_pairs = []
        for p in range(GLA_HEADS // 2):
            h0, h1 = 2 * p, 2 * p + 1
            lanes_k = slice(h0 * GLA_DK, (h1 + 1) * GLA_DK)
            lanes_j = slice(h0 * GLA_CHUNK, (h1 + 1) * GLA_CHUNK)
            v0 = v_bf[rows, h0 * GLA_DV:(h0 + 1) * GLA_DV]
            v1 = v_bf[rows, h1 * GLA_DV:(h1 + 1) * GLA_DV]
            rhs = jnp.concatenate([_pair_block_diag(s_heads[h0].astype(BF16), s_heads[h1].astype(BF16)),
                                   _pair_block_diag(v0, v1)], axis=0)
            o_pairs.append(_dot(jnp.concatenate([qe[rows, lanes_k], att[:, lanes_j]], axis=1), rhs))
        o_chunks.append(jnp.concatenate(o_pairs, axis=1))
        s_heads = _state_update(s_heads, kd[rows], v_bf[rows], jnp.exp(b_last_rows[c]))
    for hd in range(GLA_HEADS):
        s_ref[hd] = s_heads[hd]

    y_b = _gla_post(jnp.concatenate(o_chunks, axis=0), jnp.concatenate(og_parts, axis=1), w)
    sg = jnp.concatenate(gate_parts, axis=1)
    x2_ref[0] = _merge(x, y_a, y_b, sg[:, :D_MODEL], sg[:, D_MODEL:], w)

    @pl.when(t_idx == n_t - 1)
    def _():
        sout_ref[0, 0] = s_ref[...]

    for bb in range(pbuf_ref.shape[2]):
        @pl.when((t_idx == n_t - 1) & (b_idx == bb))
        def _():
            for r in range(POOL_BUF):
                row = base - POOL_BUF + r
                pbuf_ref[0, r, bb:bb + 1, :] = ext_ref[row:row + 1, :]


def _sample_block(x_ref, pool_ref, sin_ref, w, x2_ref, pbuf_ref, sout_ref,
                  xs_ref, us_ref, pooled_ref, qm_ref, kdx_ref, rhs_ref, oi_ref):
    nb, seq, _ = x_ref.shape
    rows_pb = SAMPLE_ROWS
    m = nb * rows_pb

    xs_ref[:, seq:, :] = jnp.zeros((nb, rows_pb - seq, D_MODEL), F32)
    xs_ref[:, 0:seq, :] = x_ref[...]
    x = xs_ref[...].reshape(m, D_MODEL)
    xn, u, q, k, v, og, z = _in_proj(x, w)

    gate_cols = 2 * D_MODEL // SAMPLE_GATE_SLICES
    gate_parts = [_sigmoid(_gate_proj(xn, w, c * gate_cols, (c + 1) * gate_cols))
                  for c in range(SAMPLE_EARLY_GATE_SLICES)]

    pooled_ref[...] = jnp.zeros(pooled_ref.shape, F32)
    for gi, win in enumerate(POOL_WINDOWS):
        cols = slice(gi * POOL_GROUP_DIM, (gi + 1) * POOL_GROUP_DIM)
        us_ref[gi] = u[:, cols]
        tok = [us_ref[gi, pl.ds(t, nb, stride=rows_pb), :] for t in range(seq)]
        hist = [pool_ref[0, r, :, cols] for r in range(POOL_BUF)] + tok
        for t in range(seq):
            acc = tok[t]
            for n in range(1, win):
                acc = acc + hist[POOL_BUF + t - n]
            pooled_ref[gi, pl.ds(t, nb, stride=rows_pb), :] = acc * (1.0 / win) - tok[t]
        for r in range(POOL_BUF):
            pbuf_ref[0, r, :, cols] = hist[seq + r]

    r8 = lax.broadcasted_iota(jnp.int32, (m, 1), 0) % rows_pb
    g = jnp.where(r8 < seq, _log_sigmoid(z) * (1.0 / GLA_TAU), 0.0)
    b = _chunk_cumsum(g, rows_pb)
    b3 = b.reshape(nb, rows_pb, GLA_KW)
    b_last = jnp.broadcast_to(b3[:, seq - 1:seq, :], b3.shape).reshape(m, GLA_KW)
    qe = q * jnp.exp(b)
    ke = k * jnp.exp(-b)
    kd = k * jnp.exp(b_last - b)
    decay = jnp.exp(b_last)
    v_bf = v.astype(BF16)

    k_masks = _head_lane_mask(GLA_KW, GLA_DK)
    ke_bd = jnp.concatenate([jnp.where(k_masks[hd], ke, 0.0) for hd in range(GLA_HEADS)], axis=0).astype(BF16)
    row_i = lax.broadcasted_iota(jnp.int32, (m, GLA_HEADS * m), 0)
    col_j = lax.broadcasted_iota(jnp.int32, (m, GLA_HEADS * m), 1) % m
    keep = (row_i // rows_pb == col_j // rows_pb) & (col_j <= row_i)
    att = jnp.where(keep, _dot_nt(qe.astype(BF16), ke_bd), 0.0).astype(BF16)
    o_intra = _dot(att, _block_diag_rows(v_bf, GLA_DV))

    qe3 = qe.reshape(nb, rows_pb, GLA_KW)
    qm_ref[...] = jnp.concatenate([jnp.where(k_masks[hd], qe3, 0.0) for hd in range(GLA_HEADS)], axis=1).astype(BF16)
    d_hi = decay.astype(BF16).astype(F32)
    d_lo = decay - d_hi
    kdx = jnp.where(r8 == seq, d_hi, jnp.where(r8 == seq + 1, d_lo, kd))
    kdx_ref[...] = kdx.reshape(nb, rows_pb, GLA_KW)
    ones_rows = jnp.where((r8 == seq) | (r8 == seq + 1), 1.0, 0.0) + jnp.zeros((m, GLA_DV), F32)
    rhs = jnp.concatenate(
        [piece for hd in range(GLA_HEADS) for piece in (v[:, hd * GLA_DV:(hd + 1) * GLA_DV], ones_rows)], axis=1)
    rhs_ref[...] = rhs.reshape(nb, rows_pb, 2 * GLA_VW)

    def per_batch(i):
        s_all = sin_ref[0, i]
        s_flat = s_all.reshape(GLA_KW, GLA_DV).astype(BF16)
        oi_ref[i] = _dot(qm_ref[i], s_flat)
        kdt = jnp.transpose(kdx_ref[i]).astype(BF16)
        rhs_i = rhs_ref[i].astype(BF16)
        for hd in range(GLA_HEADS):
            r = _dot(kdt[hd * GLA_DK:(hd + 1) * GLA_DK], rhs_i[:, hd * 2 * GLA_DV:(hd + 1) * 2 * GLA_DV])
            sout_ref[0, i, hd] = r[:, GLA_DV:] * s_all[hd] + r[:, :GLA_DV]

    late = SAMPLE_GATE_SLICES - SAMPLE_EARLY_GATE_SLICES
    slice_at = {(j * nb) // late: SAMPLE_EARLY_GATE_SLICES + j for j in range(late)}
    for i in range(nb):
        if i in slice_at:
            c = slice_at[i]
            gate_parts.append(_sigmoid(_gate_proj(xn, w, c * gate_cols, (c + 1) * gate_cols)))
        per_batch(i)

    oi = oi_ref[...]
    o_inter = jnp.concatenate([oi[:, hd * rows_pb:(hd + 1) * rows_pb, :] for hd in range(GLA_HEADS)], axis=2)
    o = o_intra + o_inter.reshape(m, GLA_VW)
    y_a = _pool_post(jnp.concatenate([pooled_ref[gi] for gi in range(len(POOL_WINDOWS))], axis=1), w)
    y_b = _gla_post(o, og, w)
    sg = jnp.concatenate(gate_parts, axis=1)
    x2 = _merge(x, y_a, y_b, sg[:, :D_MODEL], sg[:, D_MODEL:], w).reshape(nb, rows_pb, D_MODEL)
    for bi in range(nb):
        x2_ref[bi * seq:(bi + 1) * seq, :] = x2[bi, 0:seq, :]


def _mixer_kernel(n_s, n_t,
                  xs_in, pool_in, s_in, xp_in, meta_ref, gmix_ref, bgk_ref, pscale_ref, gnorm_ref, wgk_ref, wpg_ref,
                  wint_hbm, wpp_hbm, wgp_hbm, wout_hbm, wfi_in, wfo_in,
                  x2s_out, pools_out, ss_out, x2p_out, poolp_out, sp_out, wfi_out, wfo_out,
                  wmain_s, wgab_s, wgk_s, wpp_s, wgp_s, wout_s, stage, zr_stage, sem, zr_sem,
                  ext_ref, lvl_ref, s_ref, meta_tail_ref, meta_s_ref,
                  xs_ref, us_ref, pooled_ref, qm_ref, kdx_ref, rhs_ref, oi_ref):
    i = pl.program_id(0)
    w = _Weights(gmix_ref, bgk_ref, pscale_ref, gnorm_ref, wpg_ref,
                 wmain_s, wgab_s, wgk_s, wpp_s, wgp_s, wout_s)

    @pl.when(i == 0)
    def _():
        _stage_weights(wint_hbm, wpp_hbm, wgp_hbm, wout_hbm, wgk_ref, w, stage, zr_stage, sem, zr_sem)

    wfi_bf = wfi_in[0].astype(BF16)
    for kc in range(wfi_out.shape[0]):
        cols = slice(kc * FFN_CHUNK, (kc + 1) * FFN_CHUNK)
        wfi_out[kc] = jnp.concatenate([wfi_bf[:, cols], wfi_bf[:, D_FF + kc * FFN_CHUNK:D_FF + (kc + 1) * FFN_CHUNK]],
                                      axis=1)
    wfo_out[...] = wfo_in[0].astype(BF16)

    @pl.when(i < n_s)
    def _():
        _sample_block(xs_in, pool_in, s_in, w, x2s_out, pools_out, ss_out,
                      xs_ref, us_ref, pooled_ref, qm_ref, kdx_ref, rhs_ref, oi_ref)

    @pl.when(i >= n_s)
    def _():
        t_idx = (i - n_s) % n_t
        _prompt_tile(i == n_s, (i - n_s) // n_t, t_idx, n_t, xp_in, meta_ref, w, x2p_out, poolp_out, sp_out,
                     ext_ref, lvl_ref, s_ref, meta_tail_ref, meta_s_ref)


def _ffn_weight_copies(wi_hbm, wo_hbm, wi_s, wo_s, sem):
    copies = []
    for c in range(D_FF // FFN_CHUNK):
        rows = pl.ds(c * FFN_CHUNK, FFN_CHUNK)
        copies.append((
            pltpu.make_async_copy(wi_hbm.at[c], wi_s.at[c], sem.at[2 * c]),
            pltpu.make_async_copy(wo_hbm.at[rows, :], wo_s.at[rows, :], sem.at[2 * c + 1])))
    return copies


def _ffn_tile(x, gffn_ref, wi_s, wo_s, gfin_ref, before_chunk=None):
    n_chunks = D_FF // FFN_CHUNK
    h, r = _rms_split(x, gffn_ref[...])
    acc = x
    group = []
    for c in range(n_chunks):
        if before_chunk is not None:
            before_chunk(c)
        gate_up = _dot(h, wi_s[c]) * r
        group.append((_silu(gate_up[:, :FFN_CHUNK]) * gate_up[:, FFN_CHUNK:]).astype(BF16))
        if c + 1 == n_chunks:
            first = (c + 1 - len(group)) * FFN_CHUNK
            act = jnp.concatenate(group, axis=1)
            out = []
            for r0 in range(0, x.shape[0], FFN_TAIL_ROWS):
                rows = slice(r0, r0 + FFN_TAIL_ROWS)
                out.append(_rms(acc[rows] + _dot(act[rows], wo_s[first:(c + 1) * FFN_CHUNK, :]), gfin_ref[...]))
            return jnp.concatenate(out, axis=0)
        if len(group) == FFN_OUT_GROUP:
            first = (c + 1 - len(group)) * FFN_CHUNK
            acc = acc + _dot(jnp.concatenate(group, axis=1), wo_s[first:(c + 1) * FFN_CHUNK, :])
            group = []


def _ffn_kernel(seq, xp_ref, xs_ref, gffn_ref, wi_hbm, wo_hbm, gfin_ref, yp_ref, ys_ref, wi_s, wo_s, sem):
    i = pl.program_id(0)

    @pl.when(i == 0)
    def _():
        copies = _ffn_weight_copies(wi_hbm, wo_hbm, wi_s, wo_s, sem)
        for chunk_copies in copies:
            for cp in chunk_copies:
                cp.start()

        run_at = {0: 1, 1: 2, 3: 3, 6: len(copies) - 6}

        def wait_chunk(c):
            for cc in range(c, c + run_at.get(c, 0)):
                for cp in copies[cc]:
                    cp.wait()

        y = _ffn_tile(xs_ref[...], gffn_ref, wi_s, wo_s, gfin_ref, before_chunk=wait_chunk)
        for bi in range(ys_ref.shape[0]):
            ys_ref[bi] = y[bi * seq:(bi + 1) * seq, :]

    @pl.when(i > 0)
    def _():
        yp_ref[0] = _ffn_tile(xp_ref[0], gffn_ref, wi_s, wo_s, gfin_ref)


def _const_spec(shape):
    zeros = (0,) * len(shape)
    return pl.BlockSpec(shape, lambda *_: zeros, pipeline_mode=pl.Buffered(1))


def kernel(x_prompt, x_sample, state_pool, state_gla, meta_tokens, g_mix, w_in, w_gk_up, b_gk, w_pool_group,
           pool_scale, w_pool_proj, g_gla_norm, w_gla_proj, w_out, g_ffn, w_ffn_in, w_ffn_out, g_final):
    depth = w_in.shape[0]
    assert depth == 1, "single-layer trunk only"
    bp, tp, d = x_prompt.shape
    bs, ts, _ = x_sample.shape
    nbb = SAMPLE_BATCH_BLOCK
    assert d == D_MODEL and w_in.shape == (1, D_MODEL, IN_DIM) and meta_tokens.shape == (N_META, D_MODEL)
    assert tp % PROMPT_TILE == 0 and tp % FFN_TILE == 0 and PROMPT_TILE % GLA_CHUNK == 0
    assert bs % nbb == 0 and ts + 2 <= SAMPLE_ROWS and bs * ts == FFN_TILE
    n_t = tp // PROMPT_TILE
    n_p = bp * n_t
    n_s = bs // nbb
    n_fi = D_MODEL // FFN_IN_CAST_ROWS
    n_fo = D_FF // FFN_OUT_CAST_ROWS
    assert n_fi <= n_p and n_fo <= n_p

    def s_idx(i):
        return jnp.minimum(i, n_s - 1)

    def p_idx(i):
        return jnp.maximum(i - n_s, 0)

    w_in_t = jnp.transpose(w_in, (0, 2, 1))
    pool_hist = jnp.transpose(state_pool, (0, 2, 1, 3))
    small = (meta_tokens, g_mix, b_gk, pool_scale, g_gla_norm, w_gk_up, w_pool_group)
    hbm = pl.BlockSpec(memory_space=pl.ANY)
    in_specs = (
        [pl.BlockSpec((nbb, ts, d), lambda i: (s_idx(i), 0, 0)),
         pl.BlockSpec((1, POOL_BUF, nbb, POOL_WIDTH), lambda i: (0, 0, s_idx(i), 0)),
         pl.BlockSpec((1, nbb, GLA_HEADS, GLA_DK, GLA_DV), lambda i: (0, s_idx(i), 0, 0, 0)),
         pl.BlockSpec((1, PROMPT_TILE, d), lambda i: (p_idx(i) // n_t, p_idx(i) % n_t, 0))]
        + [_const_spec(a.shape) for a in small]
        + [hbm, hbm, hbm, hbm,
           pl.BlockSpec((1, FFN_IN_CAST_ROWS, 2 * D_FF), lambda i: (0, jnp.minimum(p_idx(i), n_fi - 1), 0)),
           pl.BlockSpec((1, FFN_OUT_CAST_ROWS, d), lambda i: (0, jnp.minimum(p_idx(i), n_fo - 1), 0))])
    out_specs = [
        pl.BlockSpec((nbb * ts, d), lambda i: (s_idx(i), 0)),
        pl.BlockSpec((1, POOL_BUF, nbb, POOL_WIDTH), lambda i: (0, 0, s_idx(i), 0)),
        pl.BlockSpec((1, nbb, GLA_HEADS, GLA_DK, GLA_DV), lambda i: (0, s_idx(i), 0, 0, 0)),
        pl.BlockSpec((1, PROMPT_TILE, d), lambda i: (p_idx(i) // n_t, p_idx(i) % n_t, 0)),
        pl.BlockSpec((1, POOL_BUF, bp, POOL_WIDTH), lambda i: (0, 0, 0, 0)),
        pl.BlockSpec((1, 1, GLA_HEADS, GLA_DK, GLA_DV), lambda i: (0, p_idx(i) // n_t, 0, 0, 0)),
        pl.BlockSpec((D_FF // FFN_CHUNK, FFN_IN_CAST_ROWS, 2 * FFN_CHUNK),
                     lambda i: (0, jnp.minimum(p_idx(i), n_fi - 1), 0)),
        pl.BlockSpec((FFN_OUT_CAST_ROWS, d), lambda i: (jnp.minimum(p_idx(i), n_fo - 1), 0))]
    out_shape = [
        jax.ShapeDtypeStruct((bs * ts, d), F32),
        jax.ShapeDtypeStruct((1, POOL_BUF, bs, POOL_WIDTH), F32),
        jax.ShapeDtypeStruct(state_gla.shape, F32),
        jax.ShapeDtypeStruct(x_prompt.shape, F32),
        jax.ShapeDtypeStruct((1, POOL_BUF, bp, POOL_WIDTH), F32),
        jax.ShapeDtypeStruct((1, bp, GLA_HEADS, GLA_DK, GLA_DV), F32),
        jax.ShapeDtypeStruct((D_FF // FFN_CHUNK, D_MODEL, 2 * FFN_CHUNK), BF16),
        jax.ShapeDtypeStruct((D_FF, D_MODEL), BF16)]
    scratch_shapes = [
        pltpu.VMEM((D_MODEL, MAIN_COLS), BF16),
        pltpu.VMEM((D_MODEL, 2 * D_MODEL), BF16), pltpu.VMEM((LANES, GLA_KW), BF16),
        pltpu.VMEM((POOL_WIDTH, D_MODEL), BF16), pltpu.VMEM((GLA_VW, D_MODEL), BF16),
        pltpu.VMEM((D_MODEL, D_MODEL), BF16),
        pltpu.VMEM((STAGE_SLOTS, STAGE_ROWS, D_MODEL), F32), pltpu.VMEM((LANES, D_MODEL), F32),
        pltpu.SemaphoreType.DMA((STAGE_SLOTS,)), pltpu.SemaphoreType.DMA((1,)),
        pltpu.VMEM((POOL_PAD + TAIL_ROWS + PROMPT_TILE, POOL_WIDTH), F32),
        pltpu.VMEM((len(POOL_WINDOWS) - 1, POOL_PAD + TAIL_ROWS + PROMPT_TILE, POOL_WIDTH), F32),
        pltpu.VMEM((GLA_HEADS, GLA_DK, GLA_DV), F32),
        pltpu.VMEM((N_META, POOL_WIDTH), F32),
        pltpu.VMEM((GLA_HEADS, GLA_DK, GLA_DV), F32),
        pltpu.VMEM((nbb, SAMPLE_ROWS, D_MODEL), F32),
        pltpu.VMEM((len(POOL_WINDOWS), nbb * SAMPLE_ROWS, POOL_GROUP_DIM), F32),
        pltpu.VMEM((len(POOL_WINDOWS), nbb * SAMPLE_ROWS, POOL_GROUP_DIM), F32),
        pltpu.VMEM((nbb, GLA_HEADS * SAMPLE_ROWS, GLA_KW), BF16),
        pltpu.VMEM((nbb, SAMPLE_ROWS, GLA_KW), F32),
        pltpu.VMEM((nbb, SAMPLE_ROWS, 2 * GLA_VW), F32),
        pltpu.VMEM((nbb, GLA_HEADS * SAMPLE_ROWS, GLA_DV), F32)]

    def mixer(*refs):
        _mixer_kernel(n_s, n_t, *refs)

    x2_s, pool_s, gla_s, x2_p, pool_p, gla_p, wfi_bf, wfo_bf = pl.pallas_call(
        mixer,
        grid=(n_s + n_p,),
        in_specs=in_specs,
        out_specs=out_specs,
        out_shape=out_shape,
        scratch_shapes=scratch_shapes,
        compiler_params=pltpu.CompilerParams(dimension_semantics=("arbitrary",), vmem_limit_bytes=VMEM_LIMIT),
        name="mixer",
    )(x_sample, pool_hist, state_gla, x_prompt, *small, w_in_t, w_pool_proj, w_gla_proj, w_out, w_ffn_in, w_ffn_out)

    def ffn(*refs):
        _ffn_kernel(ts, *refs)

    n_tf = tp // FFN_TILE

    def prompt_tile_index(i):
        tile_id = jnp.maximum(i - 1, 0)
        return (tile_id // n_tf, tile_id % n_tf, 0)

    ffn_args = (g_ffn, wfi_bf, wfo_bf, g_final.reshape(1, D_MODEL))
    y_prompt, y_sample = pl.pallas_call(
        ffn,
        grid=(bp * n_tf + 1,),
        in_specs=[pl.BlockSpec((1, FFN_TILE, d), prompt_tile_index),
                  _const_spec(x2_s.shape), _const_spec(g_ffn.shape), hbm, hbm, _const_spec((1, D_MODEL))],
        out_specs=[pl.BlockSpec((1, FFN_TILE, d), prompt_tile_index),
                   pl.BlockSpec(x_sample.shape, lambda i: (0, 0, 0))],
        out_shape=[jax.ShapeDtypeStruct(x_prompt.shape, F32), jax.ShapeDtypeStruct(x_sample.shape, F32)],
        scratch_shapes=[pltpu.VMEM(wfi_bf.shape, BF16), pltpu.VMEM(wfo_bf.shape, BF16),
                        pltpu.SemaphoreType.DMA((2 * (D_FF // FFN_CHUNK),))],
        compiler_params=pltpu.CompilerParams(dimension_semantics=("arbitrary",), vmem_limit_bytes=VMEM_LIMIT),
        name="ffn",
    )(x2_p, x2_s, *ffn_args)
    pool_p = jnp.transpose(pool_p, (0, 2, 1, 3))
    pool_s = jnp.transpose(pool_s, (0, 2, 1, 3))
    return y_prompt, y_sample, pool_p, gla_p, pool_s, gla_s
```
